```python
import math
import jax
import jax.numpy as jnp
from jax import lax
import numpy as np

D_MODEL = 2048
BATCH = 1
SEQ = 16384
DEPTH = 2
DEC_BATCH = 8
DEC_SEQ = 16
PAST_LEN = 2048

CHUNK = 64
N_EVEN = (DEPTH + 1) // 2
N_ODD = DEPTH // 2
A_HEADS = 16
A_HEAD_DIM = 64
A_WIDTH = A_HEADS * A_HEAD_DIM
A_PAST_CHUNKS = 8
A_WINDOW = A_PAST_CHUNKS * CHUNK
A_BAND = A_WINDOW + CHUNK
REL_CLIP = 128
B_HEADS = 16
B_HEAD_DIM = 64
B_WIDTH = B_HEADS * B_HEAD_DIM
B_GROUPS = 2
B_STATE = 128
B_CONV = 4
B_CONV_DIM = B_WIDTH + 2 * B_GROUPS * B_STATE
SSD_CHUNK = CHUNK
C_WIDTH = D_MODEL
C_GROUP_CH = 16
C_GROUPS = C_WIDTH // C_GROUP_CH
C_STATE = 64
D_FF = 5632
N_EXPERTS = 8
TOP_K = 2
D_FF_EXPERT = 2816
ALPHA = (2.0 * DEPTH) ** 0.25
BETA = (8.0 * DEPTH) ** -0.25
LN_EPS = 1e-5
RMS_EPS = 1e-5
NEG_INF = -1e30
IN0_SPLITS = (A_WIDTH, 2 * A_WIDTH, 3 * A_WIDTH, 3 * A_WIDTH + B_WIDTH, 3 * A_WIDTH + B_WIDTH + B_CONV_DIM)
IN0_WIDTH = 3 * A_WIDTH + B_WIDTH + B_CONV_DIM + B_HEADS

kernel_name = 'hybrid_streaming_encoder_step'


def layer_norm(x, g, b):
    xf = x.astype(jnp.float32)
    mu = jnp.mean(xf, axis=-1, keepdims=True)
    var = jnp.mean(jnp.square(xf - mu), axis=-1, keepdims=True)
    return ((xf - mu) * lax.rsqrt(var + LN_EPS) * g + b).astype(x.dtype)


def rms_norm(x, g):
    xf = x.astype(jnp.float32)
    return xf * lax.rsqrt(jnp.mean(jnp.square(xf), axis=-1, keepdims=True) + RMS_EPS) * g


def adaln(c, w, b):
    mod = jax.nn.silu(c) @ w + b
    shift, scale, gate = jnp.split(mod[:, None, :], 3, axis=-1)
    return shift, scale, gate


def swiglu(h, w_up, w_down):
    g, u = jnp.split(h @ w_up, 2, axis=-1)
    return (jax.nn.silu(g) * u) @ w_down


def band_allowed(q_pos, k_pos):
    qc = q_pos[:, None] // CHUNK
    kc = k_pos[None, :] // CHUNK
    return (kc <= qc) & (kc >= qc - A_PAST_CHUNKS) & (k_pos[None, :] >= 0)


def chunk_attend(q, k, v, q_pos, k_pos, rel_bias):
    s = jnp.einsum('bqhd,bkhd->bhqk', q, k).astype(jnp.float32) * (A_HEAD_DIM ** -0.5)
    rel = jnp.clip(q_pos[:, None] - k_pos[None, :], -REL_CLIP, REL_CLIP) + REL_CLIP
    s = s + rel_bias[:, rel].astype(jnp.float32)[None]
    s = jnp.where(band_allowed(q_pos, k_pos)[None, None], s, NEG_INF)
    p = jax.nn.softmax(s, axis=-1).astype(v.dtype)
    return jnp.einsum('bhqk,bkhd->bqhd', p, v)


def band_attention_prompt(q, k, v, rel_bias):
    bsz, L, H, dh = q.shape
    n_chunks = L // CHUNK
    pad = jnp.zeros((bsz, A_WINDOW, H, dh), k.dtype)
    kp = jnp.concatenate([pad, k], axis=1)
    vp = jnp.concatenate([pad, v], axis=1)

    def one_chunk(ci):
        start = ci * CHUNK
        qc = lax.dynamic_slice_in_dim(q, start, CHUNK, axis=1)
        kb = lax.dynamic_slice_in_dim(kp, start, A_BAND, axis=1)
        vb = lax.dynamic_slice_in_dim(vp, start, A_BAND, axis=1)
        q_pos = start + jnp.arange(CHUNK)
        k_pos = start - A_WINDOW + jnp.arange(A_BAND)
        return chunk_attend(qc, kb, vb, q_pos, k_pos, rel_bias)

    out = lax.map(one_chunk, jnp.arange(n_chunks))
    return jnp.transpose(out, (1, 0, 2, 3, 4)).reshape(bsz, L, H, dh)


def band_attention_sample(q, k, v, k_cache, v_cache, rel_bias):
    Lq = q.shape[1]
    W = k_cache.shape[1]
    kk = jnp.concatenate([k_cache.astype(k.dtype), k], axis=1)
    vv = jnp.concatenate([v_cache.astype(v.dtype), v], axis=1)
    q_pos = PAST_LEN + jnp.arange(Lq)
    k_pos = PAST_LEN - W + jnp.arange(W + Lq)
    return chunk_attend(q, kk, vv, q_pos, k_pos, rel_bias)


def causal_dwconv(x, buf, w, b):
    L = x.shape[1]
    xp = jnp.concatenate([buf.astype(x.dtype), x], axis=1)
    y = b + xp[:, 0:L] * w[0]
    for tap in range(1, B_CONV):
        y = y + xp[:, tap:tap + L] * w[tap]
    return y, xp[:, L:]


def ssd_scan(x, dt, A, Bm, Cm, h0, q_len):
    bsz, L, H, P = x.shape
    N = Bm.shape[-1]
    nc = L // q_len
    xs = x.astype(jnp.float32).reshape(bsz, nc, q_len, H, P)
    dts = dt.reshape(bsz, nc, q_len, H)
    Bs = Bm.astype(jnp.float32).reshape(bsz, nc, q_len, H, N)
    Cs = Cm.astype(jnp.float32).reshape(bsz, nc, q_len, H, N)
    acs = jnp.cumsum(dts * A, axis=2)
    diff = acs[:, :, :, None, :] - acs[:, :, None, :, :]
    tril = jnp.tril(jnp.ones((q_len, q_len), dtype=bool))[None, None, :, :, None]
    decay = jnp.exp(jnp.where(tril, diff, -jnp.inf))
    xdt = xs * dts[..., None]
    scores = jnp.einsum('bcihn,bcjhn->bcijh', Cs, Bs) * decay
    y_diag = jnp.einsum('bcijh,bcjhp->bcihp', scores, xdt)
    to_end = jnp.exp(acs[:, :, -1:, :] - acs)
    chunk_states = jnp.einsum('bcjhn,bcjhp->bchpn', Bs * to_end[..., None], xdt)
    chunk_decay = jnp.exp(acs[:, :, -1, :])

    def step(h, inp):
        s_c, d_c = inp
        return h * d_c[:, :, None, None] + s_c, h

    h_final, h_in = lax.scan(step, h0.astype(jnp.float32),
                             (jnp.moveaxis(chunk_states, 1, 0), jnp.moveaxis(chunk_decay, 1, 0)))
    h_in = jnp.moveaxis(h_in, 0, 1)
    y_off = jnp.einsum('bcihn,bchpn->bcihp', Cs, h_in) * jnp.exp(acs)[..., None]
    return (y_diag + y_off).reshape(bsz, L, H, P), h_final


def ssd_mixer(z, xbc, dt_raw, conv_buf, h0, conv_w, conv_b, dt_bias, a_log, d_skip, norm_g, q_len):
    bsz, L, _ = z.shape
    xbc, conv_new = causal_dwconv(xbc, conv_buf, conv_w, conv_b)
    xbc = jax.nn.silu(xbc)
    xs, Bm, Cm = jnp.split(xbc, [B_WIDTH, B_WIDTH + B_GROUPS * B_STATE], axis=-1)
    xs = xs.reshape(bsz, L, B_HEADS, B_HEAD_DIM)
    rep = B_HEADS // B_GROUPS
    Bm = jnp.repeat(Bm.reshape(bsz, L, B_GROUPS, B_STATE), rep, axis=2)
    Cm = jnp.repeat(Cm.reshape(bsz, L, B_GROUPS, B_STATE), rep, axis=2)
    dt = jax.nn.softplus(dt_raw.astype(jnp.float32) + dt_bias.astype(jnp.float32))
    A = -jnp.exp(a_log.astype(jnp.float32))
    y, h_final = ssd_scan(xs, dt, A, Bm, Cm, h0, q_len)
    y = y + d_skip.astype(jnp.float32)[:, None] * xs.astype(jnp.float32)
    y = y.reshape(bsz, L, B_WIDTH) * jax.nn.silu(z.astype(jnp.float32))
    return rms_norm(y, norm_g).astype(z.dtype), conv_new, h_final


def attn_ssd_mixer(h, k_cache, v_cache, conv_buf, ssm_h0, w_in, w_out, rel_bias,
                   conv_w, conv_b, dt_bias, a_log, d_skip, norm_g):
    bsz, L, _ = h.shape
    q, k, v, z, xbc, dt_raw = jnp.split(h @ w_in, IN0_SPLITS, axis=-1)
    q = q.reshape(bsz, L, A_HEADS, A_HEAD_DIM)
    k = k.reshape(bsz, L, A_HEADS, A_HEAD_DIM)
    v = v.reshape(bsz, L, A_HEADS, A_HEAD_DIM)
    if k_cache is None:
        att = band_attention_prompt(q, k, v, rel_bias)
        keep = min(A_WINDOW, L)
        k_new, v_new = k[:, L - keep:], v[:, L - keep:]
        q_len = SSD_CHUNK
    else:
        att = band_attention_sample(q, k, v, k_cache, v_cache, rel_bias)
        k_new, v_new = k, v
        q_len = L
    y_ssd, conv_new, h_final = ssd_mixer(z, xbc, dt_raw, conv_buf, ssm_h0, conv_w, conv_b,
                                         dt_bias, a_log, d_skip, norm_g, q_len)
    out = jnp.concatenate([att.reshape(bsz, L, A_WIDTH), y_ssd], axis=-1) @ w_out
    return out, k_new, v_new, conv_new, h_final


def complex_affine_combine(e1, e2):
    a1r, a1i, b1r, b1i = e1
    a2r, a2i, b2r, b2i = e2
    return (a2r * a1r - a2i * a1i, a2r * a1i + a2i * a1r,
            a2r * b1r - a2i * b1i + b2r, a2r * b1i + a2i * b1r + b2i)


def s5_scan(u, h0_re, h0_im, lam_re, lam_im, log_step, b_re, b_im, c_re, c_im, d_skip):
    bsz, L, _ = u.shape
    f32 = jnp.float32
    uf = u.astype(f32).reshape(bsz, L, C_GROUPS, C_GROUP_CH)
    lre, lim = lam_re.astype(f32), lam_im.astype(f32)
    step = jnp.exp(log_step.astype(f32))[:, None]
    mag = jnp.exp(lre * step)
    ang = lim * step
    ab_re, ab_im = mag * jnp.cos(ang), mag * jnp.sin(ang)
    den = jnp.square(lre) + jnp.square(lim)
    f_re = ((ab_re - 1.0) * lre + ab_im * lim) / den
    f_im = (ab_im * lre - (ab_re - 1.0) * lim) / den
    br, bi = b_re.astype(f32), b_im.astype(f32)
    bb_re = f_re[..., None] * br - f_im[..., None] * bi
    bb_im = f_re[..., None] * bi + f_im[..., None] * br
    bu_re = jnp.einsum('blgc,gpc->blgp', uf, bb_re)
    bu_im = jnp.einsum('blgc,gpc->blgp', uf, bb_im)
    h_re, h_im = h0_re.astype(f32), h0_im.astype(f32)
    bu_re = bu_re.at[:, 0].add(ab_re * h_re - ab_im * h_im)
    bu_im = bu_im.at[:, 0].add(ab_re * h_im + ab_im * h_re)
    a_re = jnp.broadcast_to(ab_re, bu_re.shape)
    a_im = jnp.broadcast_to(ab_im, bu_im.shape)
    _, _, s_re, s_im = lax.associative_scan(complex_affine_combine, (a_re, a_im, bu_re, bu_im), axis=1)
    y = (jnp.einsum('blgp,gcp->blgc', s_re, c_re.astype(f32))
         - jnp.einsum('blgp,gcp->blgc', s_im, c_im.astype(f32))
         + d_skip.astype(f32).reshape(C_GROUPS, C_GROUP_CH) * uf)
    return y.reshape(bsz, L, C_WIDTH), s_re[:, -1], s_im[:, -1]


def s5_glu_mixer(h, h0_re, h0_im, w_in, lam_re, lam_im, log_step, b_re, b_im, c_re, c_im, d_skip, glu_w):
    y, s_re, s_im = s5_scan(h @ w_in, h0_re, h0_im, lam_re, lam_im, log_step, b_re, b_im, c_re, c_im, d_skip)
    y = jax.nn.gelu(y).astype(h.dtype)
    val, gt = jnp.split(y @ glu_w, 2, axis=-1)
    return val * jax.nn.sigmoid(gt), s_re, s_im


def moe_swiglu(h, router_w, router_b, w_up, w_down):
    bsz, L, D = h.shape
    t = h.reshape(bsz * L, D)
    logits = (t @ router_w).astype(jnp.float32) + router_b.astype(jnp.float32)
    top_val, top_idx = lax.top_k(logits, TOP_K)
    top_w = jax.nn.softmax(top_val, axis=-1)
    gates = jnp.sum(jax.nn.one_hot(top_idx, N_EXPERTS, dtype=jnp.float32) * top_w[..., None], axis=1)
    out = jnp.zeros((bsz * L, D), jnp.float32)
    for e in range(N_EXPERTS):
        out = out + gates[:, e:e + 1] * swiglu(t, w_up[e], w_down[e]).astype(jnp.float32)
    return out.astype(h.dtype).reshape(bsz, L, D)


def run_trunk(x, c, k_cache, v_cache, conv_cache, ssm_cache, s5_re_cache, s5_im_cache,
              ada_w, ada_b, ln_g, ln_b,
              w_in0, w_out0, rel_bias, conv_w, conv_b, dt_bias, a_log, ssd_d, ssd_norm_g,
              ffn_w_up, ffn_w_down,
              w_in1, s5_lam_re, s5_lam_im, s5_log_step, s5_b_re, s5_b_im, s5_c_re, s5_c_im, s5_d, glu_w,
              router_w, router_b, moe_w_up, moe_w_down):
    prompt = k_cache is None
    bsz = x.shape[0]
    new_k, new_v, new_conv, new_ssm, new_re, new_im = [], [], [], [], [], []
    for layer in range(DEPTH):
        i = layer // 2
        shift, scale, gate = adaln(c, ada_w[layer, 0], ada_b[layer, 0])
        h = x * (1.0 + scale) + shift
        if layer % 2 == 0:
            if prompt:
                kc, vc = None, None
                conv_buf = jnp.zeros((bsz, B_CONV - 1, B_CONV_DIM), x.dtype)
                h0 = jnp.zeros((bsz, B_HEADS, B_HEAD_DIM, B_STATE), jnp.float32)
            else:
                kc, vc, conv_buf, h0 = k_cache[i], v_cache[i], conv_cache[i], ssm_cache[i]
            out, kn, vn, cn, hn = attn_ssd_mixer(h, kc, vc, conv_buf, h0, w_in0[i], w_out0[i], rel_bias[i],
                                                 conv_w[i], conv_b[i], dt_bias[i], a_log[i], ssd_d[i],
                                                 ssd_norm_g[i])
            new_k.append(kn)
            new_v.append(vn)
            new_conv.append(cn)
            new_ssm.append(hn)
        else:
            if prompt:
                h0_re = jnp.zeros((bsz, C_GROUPS, C_STATE), jnp.float32)
                h0_im = jnp.zeros((bsz, C_GROUPS, C_STATE), jnp.float32)
            else:
                h0_re, h0_im = s5_re_cache[i], s5_im_cache[i]
            out, sr, si = s5_glu_mixer(h, h0_re, h0_im, w_in1[i], s5_lam_re[i], s5_lam_im[i], s5_log_step[i],
                                       s5_b_re[i], s5_b_im[i], s5_c_re[i], s5_c_im[i], s5_d[i], glu_w[i])
            new_re.append(sr)
            new_im.append(si)
        x = layer_norm(ALPHA * x + (1.0 + gate) * out, ln_g[layer, 0], ln_b[layer, 0])
        shift, scale, gate = adaln(c, ada_w[layer, 1], ada_b[layer, 1])
        h = x * (1.0 + scale) + shift
        if layer % 2 == 0:
            out = swiglu(h, ffn_w_up[i], ffn_w_down[i])
        else:
            out = moe_swiglu(h, router_w[i], router_b[i], moe_w_up[i], moe_w_down[i])
        x = layer_norm(ALPHA * x + (1.0 + gate) * out, ln_g[layer, 1], ln_b[layer, 1])
    return (x, jnp.stack(new_k), jnp.stack(new_v), jnp.stack(new_conv), jnp.stack(new_ssm),
            jnp.stack(new_re), jnp.stack(new_im))


def setup_inputs(seed: int = 0) -> dict:
    key = jax.random.key(seed)
    ks = iter(jax.random.split(key, 48))
    f32 = jnp.float32

    def nrm(shape, s):
        return jax.random.normal(next(ks), shape, f32) * s

    keep = min(A_WINDOW, PAST_LEN)
    x_prompt = nrm((BATCH, SEQ, D_MODEL), 1.0)
    x_sample = nrm((DEC_BATCH, DEC_SEQ, D_MODEL), 1.0)
    cache_attn_k = nrm((N_EVEN, DEC_BATCH, keep, A_HEADS, A_HEAD_DIM), 1.0)
    cache_attn_v = nrm((N_EVEN, DEC_BATCH, keep, A_HEADS, A_HEAD_DIM), 1.0)
    state_ssd_conv = nrm((N_EVEN, DEC_BATCH, B_CONV - 1, B_CONV_DIM), 1.0)
    state_ssd = nrm((N_EVEN, DEC_BATCH, B_HEADS, B_HEAD_DIM, B_STATE), 0.1)
    state_s5_re = nrm((N_ODD, DEC_BATCH, C_GROUPS, C_STATE), 0.5)
    state_s5_im = nrm((N_ODD, DEC_BATCH, C_GROUPS, C_STATE), 0.5)
    c_prompt = nrm((BATCH, D_MODEL), 1.0)
    c_sample = nrm((DEC_BATCH, D_MODEL), 1.0)
    ada_w = nrm((DEPTH, 2, D_MODEL, 3 * D_MODEL), 0.1 * D_MODEL ** -0.5)
    ada_b = nrm((DEPTH, 2, 3 * D_MODEL), 0.02)
    ln_g = 1.0 + nrm((DEPTH, 2, D_MODEL), 0.02)
    ln_b = nrm((DEPTH, 2, D_MODEL), 0.02)
    w_in0 = nrm((N_EVEN, D_MODEL, IN0_WIDTH), D_MODEL ** -0.5)
    w_out0 = nrm((N_EVEN, A_WIDTH + B_WIDTH, D_MODEL), BETA * (A_WIDTH + B_WIDTH) ** -0.5)
    rel_bias = nrm((N_EVEN, A_HEADS, 2 * REL_CLIP + 1), 0.2)
    conv_w = nrm((N_EVEN, B_CONV, B_CONV_DIM), B_CONV ** -0.5)
    conv_b = nrm((N_EVEN, B_CONV_DIM), 0.02)
    dt0 = jnp.exp(jax.random.uniform(next(ks), (N_EVEN, B_HEADS), f32, math.log(1e-3), math.log(1e-1)))
    dt_bias = dt0 + jnp.log(-jnp.expm1(-dt0))
    a_log = jnp.log(jax.random.uniform(next(ks), (N_EVEN, B_HEADS), f32, 1.0, 16.0))
    ssd_d = 1.0 + nrm((N_EVEN, B_HEADS), 0.1)
    ssd_norm_g = 1.0 + nrm((N_EVEN, B_WIDTH), 0.02)
    ffn_w_up = nrm((N_EVEN, D_MODEL, 2 * D_FF), D_MODEL ** -0.5)
    ffn_w_down = nrm((N_EVEN, D_FF, D_MODEL), BETA * D_FF ** -0.5)
    w_in1 = nrm((N_ODD, D_MODEL, C_WIDTH), D_MODEL ** -0.5)
    s5_lam_re = -0.5 + nrm((N_ODD, C_GROUPS, C_STATE), 0.01)
    s5_lam_im = math.pi * jnp.arange(C_STATE, dtype=f32)[None, None, :] + nrm((N_ODD, C_GROUPS, C_STATE), 0.01)
    s5_log_step = jax.random.uniform(next(ks), (N_ODD, C_GROUPS), f32, math.log(1e-3), math.log(1e-1))
    s5_b_re = nrm((N_ODD, C_GROUPS, C_STATE, C_GROUP_CH), (2 * C_GROUP_CH) ** -0.5)
    s5_b_im = nrm((N_ODD, C_GROUPS, C_STATE, C_GROUP_CH), (2 * C_GROUP_CH) ** -0.5)
    s5_c_re = nrm((N_ODD, C_GROUPS, C_GROUP_CH, C_STATE), (2 * C_STATE) ** -0.5)
    s5_c_im = nrm((N_ODD, C_GROUPS, C_GROUP_CH, C_STATE), (2 * C_STATE) ** -0.5)
    s5_d = nrm((N_ODD, C_WIDTH), 1.0)
    glu_w = jnp.concatenate([nrm((N_ODD, C_WIDTH, D_MODEL), BETA * C_WIDTH ** -0.5),
                             nrm((N_ODD, C_WIDTH, D_MODEL), C_WIDTH ** -0.5)], axis=-1)
    router_w = nrm((N_ODD, D_MODEL, N_EXPERTS), D_MODEL ** -0.5)
    router_b = nrm((N_ODD, N_EXPERTS), 0.01)
    moe_w_up = nrm((N_ODD, N_EXPERTS, D_MODEL, 2 * D_FF_EXPERT), D_MODEL ** -0.5)
    moe_w_down = nrm((N_ODD, N_EXPERTS, D_FF_EXPERT, D_MODEL), BETA * D_FF_EXPERT ** -0.5)
    return {'x_prompt': x_prompt, 'x_sample': x_sample,
            'cache_attn_k': cache_attn_k, 'cache_attn_v': cache_attn_v,
            'state_ssd_conv': state_ssd_conv, 'state_ssd': state_ssd,
            'state_s5_re': state_s5_re, 'state_s5_im': state_s5_im,
            'c_prompt': c_prompt, 'c_sample': c_sample,
            'ada_w': ada_w, 'ada_b': ada_b, 'ln_g': ln_g, 'ln_b': ln_b,
            'w_in0': w_in0, 'w_out0': w_out0, 'rel_bias': rel_bias, 'conv_w': conv_w, 'conv_b': conv_b,
            'dt_bias': dt_bias, 'a_log': a_log, 'ssd_d': ssd_d, 'ssd_norm_g': ssd_norm_g,
            'ffn_w_up': ffn_w_up, 'ffn_w_down': ffn_w_down,
            'w_in1': w_in1, 's5_lam_re': s5_lam_re, 's5_lam_im': s5_lam_im, 's5_log_step': s5_log_step,
            's5_b_re': s5_b_re, 's5_b_im': s5_b_im, 's5_c_re': s5_c_re, 's5_c_im': s5_c_im, 's5_d': s5_d,
            'glu_w': glu_w, 'router_w': router_w, 'router_b': router_b,
            'moe_w_up': moe_w_up, 'moe_w_down': moe_w_down}


def reference(x_prompt, x_sample, cache_attn_k, cache_attn_v, state_ssd_conv, state_ssd, state_s5_re, state_s5_im,
              c_prompt, c_sample, ada_w, ada_b, ln_g, ln_b,
              w_in0, w_out0, rel_bias, conv_w, conv_b, dt_bias, a_log, ssd_d, ssd_norm_g,
              ffn_w_up, ffn_w_down,
              w_in1, s5_lam_re, s5_lam_im, s5_log_step, s5_b_re, s5_b_im, s5_c_re, s5_c_im, s5_d, glu_w,
              router_w, router_b, moe_w_up, moe_w_down):
    weights = (ada_w, ada_b, ln_g, ln_b,
               w_in0, w_out0, rel_bias, conv_w, conv_b, dt_bias, a_log, ssd_d, ssd_norm_g,
               ffn_w_up, ffn_w_down,
               w_in1, s5_lam_re, s5_lam_im, s5_log_step, s5_b_re, s5_b_im, s5_c_re, s5_c_im, s5_d, glu_w,
               router_w, router_b, moe_w_up, moe_w_down)
    y_prompt, k_p, v_p, conv_p, ssd_p, s5re_p, s5im_p = run_trunk(
        x_prompt, c_prompt, None, None, None, None, None, None, *weights)
    y_sample, k_s, v_s, conv_s, ssd_s, s5re_s, s5im_s = run_trunk(
        x_sample, c_sample, cache_attn_k, cache_attn_v, state_ssd_conv, state_ssd, state_s5_re, state_s5_im,
        *weights)
    return (y_prompt, y_sample, k_p, v_p, conv_p, ssd_p, s5re_p, s5im_p, k_s, v_s, conv_s, ssd_s, s5re_s, s5im_s)
```

```python
import functools
import math

import numpy as np
import jax
import jax.numpy as jnp
from jax import lax
from jax.experimental import pallas as pl
from jax.experimental.pallas import tpu as pltpu

F32 = jnp.float32
BF16 = jnp.bfloat16

D_MODEL = 2048
DEPTH = 2
PAST_LEN = 2048
CHUNK = 64
A_HEADS = 16
A_HEAD_DIM = 64
A_WIDTH = A_HEADS * A_HEAD_DIM
A_PAST_CHUNKS = 8
A_WINDOW = A_PAST_CHUNKS * CHUNK
REL_CLIP = 128
B_HEADS = 16
B_HEAD_DIM = 64
B_WIDTH = B_HEADS * B_HEAD_DIM
B_GROUPS = 2
B_STATE = 128
B_CONV = 4
B_CONV_DIM = B_WIDTH + 2 * B_GROUPS * B_STATE
C_GROUP_CH = 16
C_GROUPS = D_MODEL // C_GROUP_CH
C_STATE = 64
D_FF = 5632
N_EXPERTS = 8
D_FF_EXPERT = 2816
ALPHA = (2.0 * DEPTH) ** 0.25
LN_EPS = 1e-5
RMS_EPS = 1e-5
NEG_INF = -1e30
IN0_WIDTH = 3 * A_WIDTH + B_WIDTH + B_CONV_DIM + B_HEADS
IN0_PAD = 6144

LANES = 128
SUBLANES = 8
VMEM_LIMIT = 56 * 1024 * 1024

ATT_QBLOCK = 512
ATT_QSUB = 256
ATT_KSUB = ATT_QSUB + A_WINDOW
S5_JBLK = 16
S5_NJ = C_GROUPS // S5_JBLK
S5_JW = S5_JBLK * C_STATE


def _cparams(sem):
    return pltpu.CompilerParams(dimension_semantics=sem, vmem_limit_bytes=VMEM_LIMIT)


def _split_bf16(x):
    hi = x.astype(BF16)
    lo = (x - hi.astype(F32)).astype(BF16)
    return hi, lo


def _mxu(a, b, nt=False):
    if nt:
        return lax.dot_general(a, b, (((1,), (1,)), ((), ())), preferred_element_type=F32)
    return jnp.dot(a, b, preferred_element_type=F32)


def _dot1(a, b, nt=False):
    return _mxu(a.astype(BF16), b.astype(BF16), nt)


def _dot3(a, b, nt=False):
    ah, al = _split_bf16(a.astype(F32))
    bh, bl = _split_bf16(b.astype(F32))
    return _mxu(ah, bh, nt) + (_mxu(ah, bl, nt) + _mxu(al, bh, nt))


def _dot_sel(sel_bf16, x):
    x1 = x.astype(BF16)
    r1 = x - x1.astype(F32)
    x2 = r1.astype(BF16)
    x3 = (r1 - x2.astype(F32)).astype(BF16)
    return _mxu(sel_bf16, x1) + (_mxu(sel_bf16, x2) + _mxu(sel_bf16, x3))


def _dot_rsel(x, sel_bf16):
    x1 = x.astype(BF16)
    r1 = x - x1.astype(F32)
    x2 = r1.astype(BF16)
    x3 = (r1 - x2.astype(F32)).astype(BF16)
    return _mxu(x1, sel_bf16) + (_mxu(x2, sel_bf16) + _mxu(x3, sel_bf16))


def _silu(x):
    return x * jax.nn.sigmoid(x)


def _row_spec(arr, tm):
    d = arr.shape[-1]
    if arr.shape[0] == 1:
        return pl.BlockSpec((1, d), lambda i, *_: (0, 0))
    return pl.BlockSpec((tm, d), lambda i, *_: (i, 0))


def _adaln_kernel(c_ref, w_ref, b_ref, o_ref):
    c = c_ref[...]
    o_ref[0] = _dot3(_silu(c), w_ref[0]) + b_ref[0]


def adaln_all(c_rows, ada_w, ada_b):
    r, d = c_rows.shape
    n = ada_w.shape[-1]
    tn = 512
    w = ada_w.reshape(2 * DEPTH, d, n)
    b = ada_b.reshape(2 * DEPTH, 1, n)
    return pl.pallas_call(
        _adaln_kernel,
        grid=(2 * DEPTH, n // tn),
        in_specs=[pl.BlockSpec((r, d), lambda i, j: (0, 0)),
                  pl.BlockSpec((1, d, tn), lambda i, j: (i, 0, j)),
                  pl.BlockSpec((1, 1, tn), lambda i, j: (i, 0, j))],
        out_specs=pl.BlockSpec((1, r, tn), lambda i, j: (i, 0, j)),
        out_shape=jax.ShapeDtypeStruct((2 * DEPTH, r, n), F32),
        compiler_params=_cparams(("arbitrary", "arbitrary")),
        name="adaln",
    )(c_rows, w, b)


def _modulate_kernel(x_ref, sc_ref, sh_ref, o_ref):
    o_ref[...] = (x_ref[...] * (1.0 + sc_ref[...]) + sh_ref[...]).astype(o_ref.dtype)


def modulate(x, scale, shift, out_dtype):
    t, d = x.shape
    tm = min(t, 1024)
    return pl.pallas_call(
        _modulate_kernel,
        grid=(t // tm,),
        in_specs=[pl.BlockSpec((tm, d), lambda i: (i, 0)), _row_spec(scale, tm), _row_spec(shift, tm)],
        out_specs=pl.BlockSpec((tm, d), lambda i: (i, 0)),
        out_shape=jax.ShapeDtypeStruct((t, d), out_dtype),
        compiler_params=_cparams(("arbitrary",)),
        name="modulate",
    )(x, scale, shift)


def _mm_kernel(a_ref, w_ref, o_ref, *, dot, split_cols):
    res = dot(a_ref[...], w_ref[...]).astype(o_ref.dtype)
    if split_cols:
        for cb in range(o_ref.shape[0]):
            o_ref[cb] = res[:, cb * LANES:(cb + 1) * LANES]
    else:
        o_ref[...] = res


def matmul(a, w, dot, tn=1024, split_cols=False):
    t, k = a.shape
    n = w.shape[1]
    tm = min(t, 1024)
    if split_cols:
        out_spec = pl.BlockSpec((tn // LANES, tm, LANES), lambda i, j: (j, i, 0))
        out_shape = jax.ShapeDtypeStruct((n // LANES, t, LANES), F32)
    else:
        out_spec = pl.BlockSpec((tm, tn), lambda i, j: (i, j))
        out_shape = jax.ShapeDtypeStruct((t, n), F32)
    return pl.pallas_call(
        functools.partial(_mm_kernel, dot=dot, split_cols=split_cols),
        grid=(t // tm, n // tn),
        in_specs=[pl.BlockSpec((tm, k), lambda i, j: (i, 0)),
                  pl.BlockSpec((k, tn), lambda i, j: (0, j))],
        out_specs=out_spec,
        out_shape=out_shape,
        compiler_params=_cparams(("arbitrary", "arbitrary")),
        name="matmul",
    )(a, w)


def _ffn_up_kernel(*refs, dot, gated):
    if gated:
        a_ref, wg_ref, wu_ref, gates_ref, o_ref = refs
    else:
        a_ref, wg_ref, wu_ref, o_ref = refs
    a = a_ref[...]
    g = dot(a, wg_ref[0])
    u = dot(a, wu_ref[0])
    act = _silu(g) * u
    if gated:
        e = pl.program_id(1)
        gt = gates_ref[...]
        lane = lax.broadcasted_iota(jnp.int32, gt.shape, 1)
        act = act * jnp.sum(jnp.where(lane == e, gt, 0.0), axis=1, keepdims=True)
    o_ref[...] = act.astype(o_ref.dtype)


def ffn_up(a, w_up, dot, out_dtype, tm, tn, gates=None):
    t, d = a.shape
    e_n, _, f2 = w_up.shape
    f = f2 // 2
    nf = f // tn
    in_specs = [pl.BlockSpec((tm, d), lambda i, e, j: (i, 0)),
                pl.BlockSpec((1, d, tn), lambda i, e, j: (e, 0, j)),
                pl.BlockSpec((1, d, tn), lambda i, e, j: (e, 0, j + nf))]
    args = [a, w_up, w_up]
    if gates is not None:
        in_specs.append(pl.BlockSpec((tm, LANES), lambda i, e, j: (i, 0)))
        args.append(gates)
    return pl.pallas_call(
        functools.partial(_ffn_up_kernel, dot=dot, gated=gates is not None),
        grid=(t // tm, e_n, nf),
        in_specs=in_specs,
        out_specs=pl.BlockSpec((tm, tn), lambda i, e, j: (i, e * nf + j)),
        out_shape=jax.ShapeDtypeStruct((t, e_n * f), out_dtype),
        compiler_params=_cparams(("arbitrary", "arbitrary", "arbitrary")),
        name="ffn_up",
    )(*args)


def _mm_ln_kernel(*refs, n_a, nk, dot, glu, has_next):
    a_refs = refs[:n_a]
    w_refs = refs[n_a:2 * n_a]
    pos = 2 * n_a
    x_ref, gate_ref, g_ref, b_ref = refs[pos:pos + 4]
    pos += 4
    if has_next:
        sc_ref, sh_ref = refs[pos:pos + 2]
        pos += 2
    xo_ref = refs[pos]
    pos += 1
    if has_next:
        ho_ref = refs[pos]
        pos += 1
    acc_ref = refs[pos] if nk > 1 else None

    def load_a(a_ref):
        if len(a_ref.shape) == 3:
            return jnp.concatenate([a_ref[cb] for cb in range(a_ref.shape[0])], axis=1)
        return a_ref[...]

    part = dot(load_a(a_refs[0]), w_refs[0][...])
    for a_ref, w_ref in zip(a_refs[1:], w_refs[1:]):
        part = part + dot(load_a(a_ref), w_ref[...])

    def epilogue(acc):
        if glu:
            d = acc.shape[1] // 2
            out = acc[:, :d] * jax.nn.sigmoid(acc[:, d:])
        else:
            out = acc
        r = ALPHA * x_ref[...] + (1.0 + gate_ref[...]) * out
        mu = jnp.mean(r, axis=-1, keepdims=True)
        dev = r - mu
        var = jnp.mean(dev * dev, axis=-1, keepdims=True)
        xn = dev * lax.rsqrt(var + LN_EPS) * g_ref[...] + b_ref[...]
        xo_ref[...] = xn
        if has_next:
            ho_ref[...] = (xn * (1.0 + sc_ref[...]) + sh_ref[...]).astype(ho_ref.dtype)

    if nk == 1:
        epilogue(part)
    else:
        k = pl.program_id(1)

        @pl.when(k == 0)
        def _():
            acc_ref[...] = part

        @pl.when(k > 0)
        def _():
            acc_ref[...] += part

        @pl.when(k == nk - 1)
        def _():
            epilogue(acc_ref[...])


def matmul_ln(a_list, w, x, gate, ln_g, ln_b, dot, tm, tk=None, glu=False, nxt=None, next_dtype=None):
    t, d = x.shape
    n = w.shape[1]
    n_a = len(a_list)
    if n_a > 1:
        nk = 1
        in_specs = [pl.BlockSpec((tm, a.shape[1]), lambda i, k: (i, 0)) for a in a_list]
        off = 0
        for a in a_list:
            ka = a.shape[1]
            assert off % ka == 0
            in_specs.append(pl.BlockSpec((ka, n), functools.partial(lambda i, k, o: (o, 0), o=off // ka)))
            off += ka
        w_args = [w] * n_a
    else:
        a0 = a_list[0]
        split_cols = a0.ndim == 3
        ktot = a0.shape[0] * LANES if split_cols else a0.shape[1]
        tk = ktot if tk is None else tk
        nk = ktot // tk
        if split_cols:
            a_spec = pl.BlockSpec((tk // LANES, tm, LANES), lambda i, k: (k, i, 0))
        else:
            a_spec = pl.BlockSpec((tm, tk), lambda i, k: (i, k))
        in_specs = [a_spec, pl.BlockSpec((tk, n), lambda i, k: (k, 0))]
        w_args = [w]
    in_specs += [pl.BlockSpec((tm, d), lambda i, k: (i, 0)), _row_spec(gate, tm),
                 pl.BlockSpec((1, d), lambda i, k: (0, 0)), pl.BlockSpec((1, d), lambda i, k: (0, 0))]
    args = list(a_list) + w_args + [x, gate, ln_g.reshape(1, d), ln_b.reshape(1, d)]
    out_specs = [pl.BlockSpec((tm, d), lambda i, k: (i, 0))]
    out_shape = [jax.ShapeDtypeStruct((t, d), F32)]
    if nxt is not None:
        in_specs += [_row_spec(nxt[0], tm), _row_spec(nxt[1], tm)]
        args += [nxt[0], nxt[1]]
        out_specs.append(pl.BlockSpec((tm, d), lambda i, k: (i, 0)))
        out_shape.append(jax.ShapeDtypeStruct((t, d), next_dtype))
    scratch = [pltpu.VMEM((tm, n), F32)] if nk > 1 else []
    res = pl.pallas_call(
        functools.partial(_mm_ln_kernel, n_a=n_a, nk=nk, dot=dot, glu=glu, has_next=nxt is not None),
        grid=(t // tm, nk),
        in_specs=in_specs,
        out_specs=out_specs,
        out_shape=out_shape,
        scratch_shapes=scratch,
        compiler_params=_cparams(("arbitrary", "arbitrary")),
        name="matmul_ln",
    )(*args)
    return res if nxt is not None else (res[0], None)


def _softmax_pv(scores, values, dot):
    m = scores[0].max(axis=-1, keepdims=True)
    for s in scores[1:]:
        m = jnp.maximum(m, s.max(axis=-1, keepdims=True))
    num = None
    den = None
    for s, v in zip(scores, values):
        p = jnp.exp(s - m)
        l = p.sum(axis=-1, keepdims=True)
        o = dot(p, v)
        num = o if num is None else num + o
        den = l if den is None else den + l
    return num / den


def _attn_prompt_kernel(q_ref, kp_ref, kc_ref, vp_ref, vc_ref, bias_ref, o_ref, *, dot):
    i = pl.program_id(0)
    lo = lax.broadcasted_iota(jnp.int32, (1, LANES), 1) < A_HEAD_DIM
    q = q_ref[...] * (A_HEAD_DIM ** -0.5)
    k = jnp.concatenate([kp_ref[...], kc_ref[...]], axis=0)
    v = jnp.concatenate([vp_ref[...], vc_ref[...]], axis=0)
    kidx = lax.broadcasted_iota(jnp.int32, (1, ATT_KSUB), 1)
    for sub in range(ATT_QBLOCK // ATT_QSUB):
        r0 = sub * ATT_QSUB
        qs = q[r0:r0 + ATT_QSUB]
        ks = k[r0:r0 + ATT_KSUB]
        vs = v[r0:r0 + ATT_KSUB]
        before = jnp.where(jnp.logical_and(i == 0, kidx + r0 < ATT_QBLOCK), NEG_INF, 0.0)
        outs = []
        for hh in range(2):
            qm = jnp.where(lo if hh == 0 else jnp.logical_not(lo), qs, 0.0)
            s = dot(qm, ks, nt=True) + bias_ref[0, hh] + before
            outs.append(_softmax_pv([s], [vs], dot))
        o_ref[r0:r0 + ATT_QSUB, :] = jnp.where(lo, outs[0], outs[1]).astype(o_ref.dtype)


def _band_tables(rel_bias, lq, lk, q_pos0, k_pos0):
    q_pos = q_pos0 + np.arange(lq)
    k_pos = k_pos0 + np.arange(lk)
    rel = np.clip(q_pos[:, None] - k_pos[None, :], -REL_CLIP, REL_CLIP) + REL_CLIP
    qc = q_pos[:, None] // CHUNK
    kc = k_pos[None, :] // CHUNK
    allowed = (kc <= qc) & (kc >= qc - A_PAST_CHUNKS)
    tab = rel_bias[:, rel].astype(F32)
    return jnp.where(jnp.asarray(allowed)[None], tab, NEG_INF)


def attention_prompt(proj, rel_bias, dot, out_dtype):
    t = proj.shape[0]
    nb = t // ATT_QBLOCK
    npair = A_HEADS // 2
    tab = _band_tables(rel_bias, ATT_QSUB, ATT_KSUB, A_WINDOW, 0).reshape(npair, 2, ATT_QSUB, ATT_KSUB)
    blk = (ATT_QBLOCK, LANES)
    return pl.pallas_call(
        functools.partial(_attn_prompt_kernel, dot=dot),
        grid=(nb, npair),
        in_specs=[pl.BlockSpec(blk, lambda i, p: (i, p)),
                  pl.BlockSpec(blk, lambda i, p: (jnp.maximum(i - 1, 0), npair + p)),
                  pl.BlockSpec(blk, lambda i, p: (i, npair + p)),
                  pl.BlockSpec(blk, lambda i, p: (jnp.maximum(i - 1, 0), 2 * npair + p)),
                  pl.BlockSpec(blk, lambda i, p: (i, 2 * npair + p)),
                  pl.BlockSpec((1, 2, ATT_QSUB, ATT_KSUB), lambda i, p: (p, 0, 0, 0))],
        out_specs=pl.BlockSpec(blk, lambda i, p: (i, p)),
        out_shape=jax.ShapeDtypeStruct((t, A_WIDTH), out_dtype),
        compiler_params=_cparams(("arbitrary", "arbitrary")),
        name="attn_prompt",
    )(proj, proj, proj, proj, proj, tab)


def _attn_sample_kernel(q_ref, kn_ref, vn_ref, kc_ref, vc_ref, bias_ref, o_ref, *, dot, w):
    lo = lax.broadcasted_iota(jnp.int32, (1, LANES), 1) < A_HEAD_DIM
    lq = q_ref.shape[0]
    for p in range(A_HEADS // 2):
        cols = slice(p * LANES, (p + 1) * LANES)
        q = q_ref[:, cols] * (A_HEAD_DIM ** -0.5)
        kc = kc_ref[0, :, cols]
        vc = vc_ref[0, :, cols]
        kn = kn_ref[:, cols]
        vn = vn_ref[:, cols]
        outs = []
        for hh in range(2):
            qm = jnp.where(lo if hh == 0 else jnp.logical_not(lo), q, 0.0)
            bias = bias_ref[2 * p + hh]
            s_c = dot(qm, kc, nt=True) + bias[:, :w]
            s_n = dot(qm, kn, nt=True) + bias[:, w:w + lq]
            outs.append(_softmax_pv([s_c, s_n], [vc, vn], dot))
        o_ref[:, cols] = jnp.where(lo, outs[0], outs[1]).astype(o_ref.dtype)


def attention_sample(proj, k_cache, v_cache, rel_bias, bsz, lq, dot, out_dtype):
    w = k_cache.shape[1]
    tab = _band_tables(rel_bias, lq, w + lq, PAST_LEN, PAST_LEN - w)
    lkp = ((w + lq + LANES - 1) // LANES) * LANES
    tab = jnp.pad(tab, ((0, 0), (0, 0), (0, lkp - (w + lq))))
    kc = k_cache.reshape(bsz, w, A_WIDTH)
    vc = v_cache.reshape(bsz, w, A_WIDTH)
    return pl.pallas_call(
        functools.partial(_attn_sample_kernel, dot=dot, w=w),
        grid=(bsz,),
        in_specs=[pl.BlockSpec((lq, A_WIDTH), lambda b: (b, 0)),
                  pl.BlockSpec((lq, A_WIDTH), lambda b: (b, 1)),
                  pl.BlockSpec((lq, A_WIDTH), lambda b: (b, 2)),
                  pl.BlockSpec((1, w, A_WIDTH), lambda b: (b, 0, 0)),
                  pl.BlockSpec((1, w, A_WIDTH), lambda b: (b, 0, 0)),
                  pl.BlockSpec((A_HEADS, lq, lkp), lambda b: (0, 0, 0))],
        out_specs=pl.BlockSpec((lq, A_WIDTH), lambda b: (b, 0)),
        out_shape=jax.ShapeDtypeStruct((bsz * lq, A_WIDTH), out_dtype),
        compiler_params=_cparams(("arbitrary",)),
        name="attn_sample",
    )(proj, proj, proj, kc, vc, tab)


def _ssd_kernel(z_ref, x0_ref, x1_ref, x2_ref, dt_ref, cbuf_ref, h0_ref,
                cw_ref, cb_ref, dtb_ref, alog_ref, dsk_ref, ng_ref, exp_ref,
                y_ref, cnew_ref, hout_ref, xp_ref, st_ref, *, dot, q):
    c = pl.program_id(1)
    half = B_WIDTH // B_GROUPS

    @pl.when(c == 0)
    def _():
        xp_ref[0:SUBLANES, :] = cbuf_ref[0]
        st_ref[...] = h0_ref[0]

    xp_ref[SUBLANES:SUBLANES + q, 0:512] = x0_ref[...]
    xp_ref[SUBLANES:SUBLANES + q, 512:1024] = x1_ref[...]
    xp_ref[SUBLANES:SUBLANES + q, 1024:1536] = x2_ref[...]
    base = SUBLANES - (B_CONV - 1)
    conv = cb_ref[...] + xp_ref[base:base + q, :] * cw_ref[0:1, :]
    for tap in range(1, B_CONV):
        conv = conv + xp_ref[base + tap:base + tap + q, :] * cw_ref[tap:tap + 1, :]
    tail = xp_ref[q:q + SUBLANES, :]
    xp_ref[0:SUBLANES, :] = tail
    cnew_ref[0] = tail

    xbc = _silu(conv)
    xs = xbc[:, :B_WIDTH]
    bm = xbc[:, B_WIDTH:B_WIDTH + B_GROUPS * B_STATE]
    cm = xbc[:, B_WIDTH + B_GROUPS * B_STATE:]

    lane = lax.broadcasted_iota(jnp.int32, (1, LANES), 1)
    head_ok = lane < B_HEADS
    dt = jnp.where(head_ok, jax.nn.softplus(dt_ref[...] + dtb_ref[...]), 0.0)
    a_neg = -jnp.exp(alog_ref[...])
    row = lax.broadcasted_iota(jnp.int32, (q, q), 0)
    col = lax.broadcasted_iota(jnp.int32, (q, q), 1)
    tril = row >= col
    acs = _dot_sel(tril.astype(BF16), dt * a_neg)
    acs_t = acs.T
    eacs = jnp.exp(acs)
    to_end = jnp.exp(acs[q - 1:q, :] - acs)
    wide = _dot_rsel(jnp.concatenate([dt, eacs, to_end], axis=0), exp_ref[...])
    dt_w = wide[0:q]
    eacs_w = wide[q:2 * q]
    toend_w = wide[2 * q:3 * q]
    xdt = xs * dt_w
    xend = xdt * toend_w
    lo = lane < B_HEAD_DIM

    y_parts = []
    for g in range(B_GROUPS):
        bg = bm[:, g * B_STATE:(g + 1) * B_STATE]
        cg = cm[:, g * B_STATE:(g + 1) * B_STATE]
        gmat = dot(cg, bg, nt=True)
        y_off = dot(cg, st_ref[g]) * eacs_w[:, g * half:(g + 1) * half]
        for pr in range(half // LANES):
            cols = slice(g * half + pr * LANES, g * half + (pr + 1) * LANES)
            x_pair = xdt[:, cols]
            outs = []
            for hh in range(2):
                h = (g * half + pr * LANES) // B_HEAD_DIM + hh
                diff = acs[:, h:h + 1] - acs_t[h:h + 1, :]
                decay = jnp.exp(jnp.where(tril, diff, -jnp.inf))
                outs.append(dot(gmat * decay, x_pair))
            y_parts.append(jnp.where(lo, outs[0], outs[1]) + y_off[:, pr * LANES:(pr + 1) * LANES])
        st_new = dot(bg.T, xend[:, g * half:(g + 1) * half])
        st_ref[g] = st_ref[g] * eacs_w[q - 1:q, g * half:(g + 1) * half] + st_new
    hout_ref[0] = st_ref[...]

    y = jnp.concatenate(y_parts, axis=1) + dsk_ref[...] * xs
    y = y * _silu(z_ref[...])
    y = y * lax.rsqrt(jnp.mean(y * y, axis=-1, keepdims=True) + RMS_EPS) * ng_ref[...]
    y_ref[...] = y.astype(y_ref.dtype)


def ssd_mixer(proj, conv_buf, h0, conv_w, conv_b, dt_bias, a_log, d_skip, norm_g, bsz, seq, q, dot, out_dtype):
    t = bsz * seq
    nc = seq // q
    half = B_WIDTH // B_GROUPS
    cbuf = jnp.pad(conv_buf, ((0, 0), (SUBLANES - (B_CONV - 1), 0), (0, 0)))
    h0_t = jnp.transpose(h0.reshape(bsz, B_GROUPS, half, B_STATE), (0, 1, 3, 2))
    pad = lambda v: jnp.pad(v.reshape(1, -1), ((0, 0), (0, LANES - v.shape[-1])))
    expand = jnp.asarray(np.repeat(np.eye(LANES, B_HEADS, dtype=np.float32).T, B_HEAD_DIM, axis=0).T, BF16)
    dsk_w = jnp.repeat(d_skip, B_HEAD_DIM).reshape(1, B_WIDTH)
    zb, xb, db = 3 * A_WIDTH // B_WIDTH, (3 * A_WIDTH + B_WIDTH) // 512, (3 * A_WIDTH + B_WIDTH + B_CONV_DIM) // LANES
    const = lambda shp: pl.BlockSpec(shp, lambda b, c: (0,) * len(shp))
    y, cnew, hout = pl.pallas_call(
        functools.partial(_ssd_kernel, dot=dot, q=q),
        grid=(bsz, nc),
        in_specs=[pl.BlockSpec((q, B_WIDTH), lambda b, c: (b * nc + c, zb)),
                  pl.BlockSpec((q, 512), lambda b, c: (b * nc + c, xb)),
                  pl.BlockSpec((q, 512), lambda b, c: (b * nc + c, xb + 1)),
                  pl.BlockSpec((q, 512), lambda b, c: (b * nc + c, xb + 2)),
                  pl.BlockSpec((q, LANES), lambda b, c: (b * nc + c, db)),
                  pl.BlockSpec((1, SUBLANES, B_CONV_DIM), lambda b, c: (b, 0, 0)),
                  pl.BlockSpec((1, B_GROUPS, B_STATE, half), lambda b, c: (b, 0, 0, 0)),
                  const((B_CONV, B_CONV_DIM)), const((1, B_CONV_DIM)), const((1, LANES)), const((1, LANES)),
                  const((1, B_WIDTH)), const((1, B_WIDTH)), const((LANES, B_WIDTH))],
        out_specs=[pl.BlockSpec((q, B_WIDTH), lambda b, c: (b * nc + c, 0)),
                   pl.BlockSpec((1, SUBLANES, B_CONV_DIM), lambda b, c: (b, 0, 0)),
                   pl.BlockSpec((1, B_GROUPS, B_STATE, half), lambda b, c: (b, 0, 0, 0))],
        out_shape=[jax.ShapeDtypeStruct((t, B_WIDTH), out_dtype),
                   jax.ShapeDtypeStruct((bsz, SUBLANES, B_CONV_DIM), F32),
                   jax.ShapeDtypeStruct((bsz, B_GROUPS, B_STATE, half), F32)],
        scratch_shapes=[pltpu.VMEM((q + SUBLANES, B_CONV_DIM), F32),
                        pltpu.VMEM((B_GROUPS, B_STATE, half), F32)],
        compiler_params=_cparams(("arbitrary", "arbitrary")),
        name="ssd",
    )(proj, proj, proj, proj, proj, cbuf, h0_t,
      conv_w, conv_b.reshape(1, -1), pad(dt_bias), pad(a_log), dsk_w, norm_g.reshape(1, -1), expand)
    conv_new = cnew[:, SUBLANES - (B_CONV - 1):, :]
    h_final = jnp.transpose(hout, (0, 1, 3, 2)).reshape(bsz, B_HEADS, B_HEAD_DIM, B_STATE)
    return y, conv_new, h_final


def _s5_prep_kernel(lre_ref, lim_ref, ls_ref, bre_ref, bim_ref, are_ref, aim_ref, bbre_ref, bbim_ref):
    lre = lre_ref[...]
    lim = lim_ref[...]
    step = jnp.exp(ls_ref[...])
    mag = jnp.exp(lre * step)
    ang = lim * step
    ab_re = mag * jnp.cos(ang)
    ab_im = mag * jnp.sin(ang)
    den = lre * lre + lim * lim
    f_re = ((ab_re - 1.0) * lre + ab_im * lim) / den
    f_im = (ab_im * lre - (ab_re - 1.0) * lim) / den
    br = bre_ref[...]
    bi = bim_ref[...]
    are_ref[...] = ab_re
    aim_ref[...] = ab_im
    bbre_ref[...] = f_re * br - f_im * bi
    bbim_ref[...] = f_re * bi + f_im * br


def _cmul(ar, ai, br, bi):
    return ar * br - ai * bi, ar * bi + ai * br


def _gelu_tanh(x):
    return 0.5 * x * (1.0 + jnp.tanh(math.sqrt(2.0 / math.pi) * (x + 0.044715 * (x * x * x))))


def _s5_kernel(u_ref, bb_ref, ccre_ref, ccim_ref, a_ref, d_ref, h0_ref, y_ref, hout_ref,
               up_ref, s_ref, pw_ref, carry_ref, ys_ref, *, dot, seg):
    t = pl.program_id(2)
    w = S5_JW
    a_re = a_ref[0, 0:1, :]
    a_im = a_ref[0, 1:2, :]

    @pl.when(t == 0)
    def _():
        carry_ref[0:1, :] = h0_ref[0, 0]
        pw_ref[0:1, 0:w] = a_re
        pw_ref[0:1, w:2 * w] = a_im

        def power(i, carry):
            pr, pi = _cmul(pw_ref[pl.ds(i - 1, 1), 0:w], pw_ref[pl.ds(i - 1, 1), w:2 * w], a_re, a_im)
            pw_ref[pl.ds(i, 1), 0:w] = pr
            pw_ref[pl.ds(i, 1), w:2 * w] = pi
            return carry

        lax.fori_loop(1, seg, power, 0)

    ncb = u_ref.shape[0]

    def permute(i, carry):
        for cb in range(ncb):
            up_ref[cb, pl.ds(pl.multiple_of(i * SUBLANES, SUBLANES), SUBLANES), :] = (
                u_ref[cb, pl.ds(i, SUBLANES, stride=seg), :])
        return carry

    lax.fori_loop(0, seg, permute, 0)
    up = jnp.concatenate([up_ref[cb] for cb in range(ncb)], axis=1)
    s_ref[...] = dot(up, bb_ref[0])

    are_b = jnp.broadcast_to(a_re, (SUBLANES, w))
    aim_b = jnp.broadcast_to(a_im, (SUBLANES, w))

    def scan(i, st):
        sre, sim = st
        rows = pl.ds(pl.multiple_of(i * SUBLANES, SUBLANES), SUBLANES)
        pr, pi = _cmul(are_b, aim_b, sre, sim)
        nre = pr + s_ref[rows, 0:w]
        nim = pi + s_ref[rows, w:2 * w]
        s_ref[rows, 0:w] = nre
        s_ref[rows, w:2 * w] = nim
        return nre, nim

    zero = jnp.zeros((SUBLANES, w), F32)
    end_re, end_im = lax.fori_loop(0, seg, scan, (zero, zero))

    sg_re = pw_ref[seg - 1:seg, 0:w]
    sg_im = pw_ref[seg - 1:seg, w:2 * w]
    c_re = carry_ref[0:1, 0:w]
    c_im = carry_ref[0:1, w:2 * w]
    rows_re, rows_im = [], []
    for s in range(SUBLANES):
        rows_re.append(c_re)
        rows_im.append(c_im)
        pr, pi = _cmul(sg_re, sg_im, c_re, c_im)
        c_re = pr + end_re[s:s + 1, :]
        c_im = pi + end_im[s:s + 1, :]
    carry_ref[0:1, 0:w] = c_re
    carry_ref[0:1, w:2 * w] = c_im
    hout_ref[0, 0] = carry_ref[0:1, :]
    in_re = jnp.concatenate(rows_re, axis=0)
    in_im = jnp.concatenate(rows_im, axis=0)

    def fixup(i, carry):
        rows = pl.ds(pl.multiple_of(i * SUBLANES, SUBLANES), SUBLANES)
        p_re = jnp.broadcast_to(pw_ref[pl.ds(i, 1), 0:w], (SUBLANES, w))
        p_im = jnp.broadcast_to(pw_ref[pl.ds(i, 1), w:2 * w], (SUBLANES, w))
        pr, pi = _cmul(p_re, p_im, in_re, in_im)
        s_ref[rows, 0:w] = s_ref[rows, 0:w] + pr
        s_ref[rows, w:2 * w] = s_ref[rows, w:2 * w] + pi
        return carry

    lax.fori_loop(0, seg, fixup, 0)

    y = dot(s_ref[:, 0:w], ccre_ref[0]) - dot(s_ref[:, w:2 * w], ccim_ref[0]) + d_ref[...] * up
    y = _gelu_tanh(y)
    for cb in range(ncb):
        ys_ref[cb] = y[:, cb * LANES:(cb + 1) * LANES]

    def unpermute(i, carry):
        for cb in range(ncb):
            y_ref[cb, pl.ds(i, SUBLANES, stride=seg), :] = (
                ys_ref[cb, pl.ds(pl.multiple_of(i * SUBLANES, SUBLANES), SUBLANES), :])
        return carry

    lax.fori_loop(0, seg, unpermute, 0)


def _block_diag(m):
    nj, j, r, c = m.shape
    eye = jnp.eye(j, dtype=m.dtype)
    return (m[:, :, :, None, :] * eye[None, :, None, :, None]).reshape(nj, j * r, j * c)


def s5_mixer(u, h0_re, h0_im, lam_re, lam_im, log_step, b_re, b_im, c_re, c_im, d_skip,
             bsz, seq, tm, dot, w_dtype):
    t = bsz * seq
    g, p, ch = C_GROUPS, C_STATE, C_GROUP_CH
    rep = lambda v: jnp.repeat(v, ch, axis=-1)
    ls = jnp.broadcast_to(log_step[:, None], (g, p))
    pc = pl.BlockSpec((g, p * ch), lambda: (0, 0))
    shp = jax.ShapeDtypeStruct((g, p * ch), F32)
    a_re_x, a_im_x, bb_re, bb_im = pl.pallas_call(
        _s5_prep_kernel, in_specs=[pc] * 5, out_specs=[pc] * 4, out_shape=[shp] * 4, name="s5_prep",
    )(rep(lam_re), rep(lam_im), rep(ls), b_re.reshape(g, p * ch), b_im.reshape(g, p * ch))
    a_re = a_re_x[:, ::ch]
    a_im = a_im_x[:, ::ch]
    to_blk = lambda m: jnp.transpose(m.reshape(S5_NJ, S5_JBLK, p, ch), (0, 1, 3, 2))
    bb = jnp.concatenate([_block_diag(to_blk(bb_re)), _block_diag(to_blk(bb_im))], axis=-1).astype(w_dtype)
    cblk = lambda m: jnp.transpose(m.reshape(S5_NJ, S5_JBLK, ch, p), (0, 1, 3, 2))
    cc_re = _block_diag(cblk(c_re)).astype(w_dtype)
    cc_im = _block_diag(cblk(c_im)).astype(w_dtype)
    a_rows = jnp.stack([a_re.reshape(S5_NJ, S5_JW), a_im.reshape(S5_NJ, S5_JW)], axis=1)
    h0 = jnp.concatenate([h0_re.reshape(bsz, S5_NJ, 1, S5_JW), h0_im.reshape(bsz, S5_NJ, 1, S5_JW)], axis=-1)
    nt = seq // tm
    seg = tm // SUBLANES
    uw = S5_JBLK * ch
    ncb = uw // LANES
    y, hout = pl.pallas_call(
        functools.partial(_s5_kernel, dot=dot, seg=seg),
        grid=(bsz, S5_NJ, nt),
        in_specs=[pl.BlockSpec((ncb, tm, LANES), lambda b, j, i: (j, b * nt + i, 0)),
                  pl.BlockSpec((1, uw, 2 * S5_JW), lambda b, j, i: (j, 0, 0)),
                  pl.BlockSpec((1, S5_JW, uw), lambda b, j, i: (j, 0, 0)),
                  pl.BlockSpec((1, S5_JW, uw), lambda b, j, i: (j, 0, 0)),
                  pl.BlockSpec((1, 2, S5_JW), lambda b, j, i: (j, 0, 0)),
                  pl.BlockSpec((1, uw), lambda b, j, i: (0, j)),
                  pl.BlockSpec((1, 1, 1, 2 * S5_JW), lambda b, j, i: (b, j, 0, 0))],
        out_specs=[pl.BlockSpec((ncb, tm, LANES), lambda b, j, i: (j, b * nt + i, 0)),
                   pl.BlockSpec((1, 1, 1, 2 * S5_JW), lambda b, j, i: (b, j, 0, 0))],
        out_shape=[jax.ShapeDtypeStruct((g * ch // LANES, t, LANES), F32),
                   jax.ShapeDtypeStruct((bsz, S5_NJ, 1, 2 * S5_JW), F32)],
        scratch_shapes=[pltpu.VMEM((ncb, tm, LANES), F32),
                        pltpu.VMEM((tm, 2 * S5_JW), F32),
                        pltpu.VMEM((seg, 2 * S5_JW), F32),
                        pltpu.VMEM((SUBLANES, 2 * S5_JW), F32),
                        pltpu.VMEM((ncb, tm, LANES), F32)],
        compiler_params=_cparams(("arbitrary", "arbitrary", "arbitrary")),
        name="s5_scan",
    )(u, bb, cc_re, cc_im, a_rows, d_skip.reshape(1, -1), h0)
    s_re = hout[:, :, 0, :S5_JW].reshape(bsz, g, p)
    s_im = hout[:, :, 0, S5_JW:].reshape(bsz, g, p)
    return y, s_re, s_im


def _router_kernel(x_ref, sc_ref, sh_ref, w_ref, b_ref, o_ref):
    h = x_ref[...] * (1.0 + sc_ref[...]) + sh_ref[...]
    logits = _dot3(h, w_ref[...]) + b_ref[...]
    lane = lax.broadcasted_iota(jnp.int32, logits.shape, 1).astype(F32)
    logits = jnp.where(lane < N_EXPERTS, logits, -jnp.inf)
    m1 = logits.max(axis=-1, keepdims=True)
    i1 = jnp.min(jnp.where(logits == m1, lane, float(LANES)), axis=-1, keepdims=True)
    rest = jnp.where(lane == i1, -jnp.inf, logits)
    m2 = rest.max(axis=-1, keepdims=True)
    i2 = jnp.min(jnp.where(rest == m2, lane, float(LANES)), axis=-1, keepdims=True)
    e2 = jnp.exp(m2 - m1)
    w1 = 1.0 / (1.0 + e2)
    w2 = e2 / (1.0 + e2)
    o_ref[...] = jnp.where(lane == i1, w1, 0.0) + jnp.where(lane == i2, w2, 0.0)


def router(x, scale, shift, router_w, router_b):
    t, d = x.shape
    tm = min(t, 512)
    w = jnp.pad(router_w, ((0, 0), (0, LANES - N_EXPERTS)))
    b = jnp.pad(router_b.reshape(1, -1), ((0, 0), (0, LANES - N_EXPERTS)))
    return pl.pallas_call(
        _router_kernel,
        grid=(t // tm,),
        in_specs=[pl.BlockSpec((tm, d), lambda i: (i, 0)), _row_spec(scale, tm), _row_spec(shift, tm),
                  pl.BlockSpec((d, LANES), lambda i: (0, 0)), pl.BlockSpec((1, LANES), lambda i: (0, 0))],
        out_specs=pl.BlockSpec((tm, LANES), lambda i: (i, 0)),
        out_shape=jax.ShapeDtypeStruct((t, LANES), F32),
        compiler_params=_cparams(("arbitrary",)),
        name="router",
    )(x, scale, shift, w, b)


def _trunk(x, mods, caches, wts, precise):
    bsz, seq, d = x.shape
    t = bsz * seq
    prompt = caches is None
    dot = _dot3 if precise else _dot1
    act_dtype = F32 if precise else BF16
    tm = min(t, 512)
    x0 = x.reshape(t, d)

    sh, sc, gt = mods[0][0]
    h = modulate(x0, sc, sh, act_dtype)
    proj = matmul(h, wts["w_in0"], dot)
    k_all = proj[:, A_WIDTH:2 * A_WIDTH].reshape(bsz, seq, A_HEADS, A_HEAD_DIM)
    v_all = proj[:, 2 * A_WIDTH:3 * A_WIDTH].reshape(bsz, seq, A_HEADS, A_HEAD_DIM)
    if prompt:
        att = attention_prompt(proj, wts["rel_bias"], dot, act_dtype)
        keep = min(A_WINDOW, seq)
        k_new, v_new = k_all[:, seq - keep:], v_all[:, seq - keep:]
        conv_buf = jnp.zeros((bsz, B_CONV - 1, B_CONV_DIM), F32)
        ssm_h0 = jnp.zeros((bsz, B_HEADS, B_HEAD_DIM, B_STATE), F32)
        q_len = min(seq, 256)
    else:
        att = attention_sample(proj, caches["k"], caches["v"], wts["rel_bias"], bsz, seq, dot, act_dtype)
        k_new, v_new = k_all, v_all
        conv_buf, ssm_h0 = caches["conv"], caches["ssm"]
        q_len = seq
    y_ssd, conv_new, ssm_new = ssd_mixer(proj, conv_buf, ssm_h0, wts["conv_w"], wts["conv_b"], wts["dt_bias"],
                                         wts["a_log"], wts["ssd_d"], wts["ssd_norm_g"], bsz, seq, q_len,
                                         dot, act_dtype)
    x1, h1 = matmul_ln([att, y_ssd], wts["w_out0"], x0, gt, wts["ln_g"][0, 0], wts["ln_b"][0, 0], dot, tm,
                       nxt=(mods[0][1][1], mods[0][1][0]), next_dtype=act_dtype)
    act = ffn_up(h1, wts["ffn_w_up"], dot, act_dtype, tm=min(t, 1024), tn=512)
    x2, h2 = matmul_ln([act], wts["ffn_w_down"], x1, mods[0][1][2], wts["ln_g"][0, 1], wts["ln_b"][0, 1], dot, tm,
                       tk=1408, nxt=(mods[1][0][1], mods[1][0][0]), next_dtype=act_dtype)

    u = matmul(h2, wts["w_in1"], dot, split_cols=True)
    if prompt:
        s5_re0 = jnp.zeros((bsz, C_GROUPS, C_STATE), F32)
        s5_im0 = jnp.zeros((bsz, C_GROUPS, C_STATE), F32)
    else:
        s5_re0, s5_im0 = caches["s5_re"], caches["s5_im"]
    y5, s5_re, s5_im = s5_mixer(u, s5_re0, s5_im0, wts["s5_lam_re"], wts["s5_lam_im"], wts["s5_log_step"],
                                wts["s5_b_re"], wts["s5_b_im"], wts["s5_c_re"], wts["s5_c_im"], wts["s5_d"],
                                bsz, seq, min(seq, 512), dot, F32 if precise else BF16)
    x3, h3 = matmul_ln([y5], wts["glu_w"], x2, mods[1][0][2], wts["ln_g"][1, 0], wts["ln_b"][1, 0], dot,
                       tm, tk=512, glu=True, nxt=(mods[1][1][1], mods[1][1][0]), next_dtype=act_dtype)
    gates = router(x3, mods[1][1][1], mods[1][1][0], wts["router_w"], wts["router_b"])
    act = ffn_up(h3, wts["moe_w_up"], dot, act_dtype, tm=min(t, 2048), tn=256, gates=gates)
    x4, _ = matmul_ln([act], wts["moe_w_down"], x3, mods[1][1][2], wts["ln_g"][1, 1], wts["ln_b"][1, 1], dot, tm,
                      tk=1408)
    return (x4.reshape(bsz, seq, d), k_new[None], v_new[None], conv_new[None], ssm_new[None],
            s5_re[None], s5_im[None])


def kernel(x_prompt, x_sample, cache_attn_k, cache_attn_v, state_ssd_conv, state_ssd, state_s5_re, state_s5_im, c_prompt, c_sample, ada_w, ada_b, ln_g, ln_b, w_in0, w_out0, rel_bias, conv_w, conv_b, dt_bias, a_log, ssd_d, ssd_norm_g, ffn_w_up, ffn_w_down, w_in1, s5_lam_re, s5_lam_im, s5_log_step, s5_b_re, s5_b_im, s5_c_re, s5_c_im, s5_d, glu_w, router_w, router_b, moe_w_up, moe_w_down):
    d = D_MODEL
    bp, lp, _ = x_prompt.shape
    bs, ls, _ = x_sample.shape

    n_c = bp + bs
    rows = ((n_c + SUBLANES - 1) // SUBLANES) * SUBLANES
    c_all = jnp.pad(jnp.concatenate([c_prompt, c_sample], axis=0), ((0, rows - n_c), (0, 0)))
    mod = adaln_all(c_all, ada_w, ada_b).reshape(DEPTH, 2, rows, 3, d)

    def mods_for(r0, nb, per_row):
        out = []
        for layer in range(DEPTH):
            out.append([])
            for j in range(2):
                trip = []
                for part in range(3):
                    m = mod[layer, j, r0:r0 + nb, part]
                    trip.append(jnp.repeat(m, per_row, axis=0) if nb > 1 else m)
                out[-1].append(tuple(trip))
        return out

    assert bp == 1
    mods_p = mods_for(0, bp, lp)
    mods_s = mods_for(bp, bs, ls)

    in0_pad = ((0, 0), (0, IN0_PAD - IN0_WIDTH))
    shared = dict(ln_g=ln_g, ln_b=ln_b, rel_bias=rel_bias[0], conv_w=conv_w[0], conv_b=conv_b[0],
                  dt_bias=dt_bias[0], a_log=a_log[0], ssd_d=ssd_d[0], ssd_norm_g=ssd_norm_g[0],
                  s5_lam_re=s5_lam_re[0], s5_lam_im=s5_lam_im[0], s5_log_step=s5_log_step[0],
                  s5_b_re=s5_b_re[0], s5_b_im=s5_b_im[0], s5_c_re=s5_c_re[0], s5_c_im=s5_c_im[0], s5_d=s5_d[0],
                  router_w=router_w[0], router_b=router_b[0])
    big = dict(w_in0=jnp.pad(w_in0[0], in0_pad), w_out0=w_out0[0], ffn_w_up=ffn_w_up, ffn_w_down=ffn_w_down[0],
               w_in1=w_in1[0], glu_w=glu_w[0], moe_w_up=moe_w_up[0],
               moe_w_down=moe_w_down[0].reshape(N_EXPERTS * D_FF_EXPERT, d))
    wts_s = dict(shared, **big)
    wts_p = dict(shared, **{k: v.astype(BF16) for k, v in big.items()})

    y_p, k_p, v_p, conv_p, ssd_p, re_p, im_p = _trunk(x_prompt, mods_p, None, wts_p, precise=False)
    caches = dict(k=cache_attn_k[0], v=cache_attn_v[0], conv=state_ssd_conv[0], ssm=state_ssd[0],
                  s5_re=state_s5_re[0], s5_im=state_s5_im[0])
    y_s, k_s, v_s, conv_s, ssd_s, re_s, im_s = _trunk(x_sample, mods_s, caches, wts_s, precise=True)
    return (y_p, y_s, k_p, v_p, conv_p, ssd_p, re_p, im_p, k_s, v_s, conv_s, ssd_s, re_s, im_s)
```

```python
import functools
import math

import numpy as np
import jax
import jax.numpy as jnp
from jax import lax
from jax.experimental import pallas as pl
from jax.experimental.pallas import tpu as pltpu

F32 = jnp.float32
BF16 = jnp.bfloat16

D_MODEL = 2048
DEPTH = 2
PAST_LEN = 2048
CHUNK = 64
A_HEADS = 16
A_HEAD_DIM = 64
A_WIDTH = A_HEADS * A_HEAD_DIM
A_PAST_CHUNKS = 8
A_WINDOW = A_PAST_CHUNKS * CHUNK
REL_CLIP = 128
B_HEADS = 16
B_HEAD_DIM = 64
B_WIDTH = B_HEADS * B_HEAD_DIM
B_GROUPS = 2
B_STATE = 128
B_CONV = 4
B_CONV_DIM = B_WIDTH + 2 * B_GROUPS * B_STATE
C_GROUP_CH = 16
C_GROUPS = D_MODEL // C_GROUP_CH
C_STATE = 64
D_FF = 5632
N_EXPERTS = 8
TOP_K = 2
D_FF_EXPERT = 2816
ALPHA = (2.0 * DEPTH) ** 0.25
LN_EPS = 1e-5
RMS_EPS = 1e-5
NEG_INF = -1e30
IN0_WIDTH = 3 * A_WIDTH + B_WIDTH + B_CONV_DIM + B_HEADS
IN0_PAD = 6144

LANES = 128
SUBLANES = 8
VMEM_LIMIT = 56 * 1024 * 1024

LN_SUBROWS = 256
ATT_QBLOCK = 512
ATT_QSUB = 256
ATT_KSUB = ATT_QSUB + A_WINDOW
S5_JBLK = 16
S5_NJ = C_GROUPS // S5_JBLK
S5_JW = S5_JBLK * C_STATE


def _cparams(sem):
    return pltpu.CompilerParams(dimension_semantics=sem, vmem_limit_bytes=VMEM_LIMIT)


def _split_bf16(x):
    hi = x.astype(BF16)
    lo = (x - hi.astype(F32)).astype(BF16)
    return hi, lo


def _mxu(a, b, nt=False):
    if nt:
        return lax.dot_general(a, b, (((1,), (1,)), ((), ())), preferred_element_type=F32)
    return jnp.dot(a, b, preferred_element_type=F32)


def _dot1(a, b, nt=False):
    return _mxu(a.astype(BF16), b.astype(BF16), nt)


def _dot3(a, b, nt=False):
    ah, al = _split_bf16(a.astype(F32))
    bh, bl = _split_bf16(b.astype(F32))
    return _mxu(ah, bh, nt) + (_mxu(ah, bl, nt) + _mxu(al, bh, nt))


def _dot_sel(sel_bf16, x):
    x1 = x.astype(BF16)
    r1 = x - x1.astype(F32)
    x2 = r1.astype(BF16)
    x3 = (r1 - x2.astype(F32)).astype(BF16)
    return _mxu(sel_bf16, x1) + (_mxu(sel_bf16, x2) + _mxu(sel_bf16, x3))


def _dot_rsel(x, sel_bf16):
    x1 = x.astype(BF16)
    r1 = x - x1.astype(F32)
    x2 = r1.astype(BF16)
    x3 = (r1 - x2.astype(F32)).astype(BF16)
    return _mxu(x1, sel_bf16) + (_mxu(x2, sel_bf16) + _mxu(x3, sel_bf16))


def _silu(x):
    return x * jax.nn.sigmoid(x)


def _row_spec(arr, tm):
    d = arr.shape[-1]
    if arr.shape[0] == 1:
        return pl.BlockSpec((1, d), lambda i, *_: (0, 0))
    return pl.BlockSpec((tm, d), lambda i, *_: (i, 0))


def _adaln_kernel(c_ref, w_ref, b_ref, o_ref):
    c = c_ref[...]
    o_ref[0] = _dot3(_silu(c), w_ref[0]) + b_ref[0]


def adaln_all(c_rows, ada_w, ada_b):
    r, d = c_rows.shape
    n = ada_w.shape[-1]
    tn = 512
    w = ada_w.reshape(2 * DEPTH, d, n)
    b = ada_b.reshape(2 * DEPTH, 1, n)
    return pl.pallas_call(
        _adaln_kernel,
        grid=(2 * DEPTH, n // tn),
        in_specs=[pl.BlockSpec((r, d), lambda i, j: (0, 0)),
                  pl.BlockSpec((1, d, tn), lambda i, j: (i, 0, j)),
                  pl.BlockSpec((1, 1, tn), lambda i, j: (i, 0, j))],
        out_specs=pl.BlockSpec((1, r, tn), lambda i, j: (i, 0, j)),
        out_shape=jax.ShapeDtypeStruct((2 * DEPTH, r, n), F32),
        compiler_params=_cparams(("arbitrary", "arbitrary")),
        name="adaln",
    )(c_rows, w, b)


def _modulate_kernel(x_ref, sc_ref, sh_ref, o_ref):
    o_ref[...] = (x_ref[...] * (1.0 + sc_ref[...]) + sh_ref[...]).astype(o_ref.dtype)


def modulate(x, scale, shift, out_dtype):
    t, d = x.shape
    tm = min(t, 1024)
    return pl.pallas_call(
        _modulate_kernel,
        grid=(t // tm,),
        in_specs=[pl.BlockSpec((tm, d), lambda i: (i, 0)), _row_spec(scale, tm), _row_spec(shift, tm)],
        out_specs=pl.BlockSpec((tm, d), lambda i: (i, 0)),
        out_shape=jax.ShapeDtypeStruct((t, d), out_dtype),
        compiler_params=_cparams(("arbitrary",)),
        name="modulate",
    )(x, scale, shift)


def _mm_kernel(a_ref, w_ref, o_ref, *, dot, split_cols):
    res = dot(a_ref[...], w_ref[...]).astype(o_ref.dtype)
    if split_cols:
        for cb in range(o_ref.shape[0]):
            o_ref[cb] = res[:, cb * LANES:(cb + 1) * LANES]
    else:
        o_ref[...] = res


def matmul(a, w, dot, tn=1024, split_cols=False):
    t, k = a.shape
    n = w.shape[1]
    tm = min(t, 1024)
    if split_cols:
        out_spec = pl.BlockSpec((tn // LANES, tm, LANES), lambda i, j: (j, i, 0))
        out_shape = jax.ShapeDtypeStruct((n // LANES, t, LANES), F32)
    else:
        out_spec = pl.BlockSpec((tm, tn), lambda i, j: (i, j))
        out_shape = jax.ShapeDtypeStruct((t, n), F32)
    return pl.pallas_call(
        functools.partial(_mm_kernel, dot=dot, split_cols=split_cols),
        grid=(t // tm, n // tn),
        in_specs=[pl.BlockSpec((tm, k), lambda i, j: (i, 0)),
                  pl.BlockSpec((k, tn), lambda i, j: (0, j))],
        out_specs=out_spec,
        out_shape=out_shape,
        compiler_params=_cparams(("arbitrary", "arbitrary")),
        name="matmul",
    )(a, w)


def _ffn_up_kernel(*refs, dot, gated):
    if gated:
        a_ref, wg_ref, wu_ref, gates_ref, o_ref = refs
    else:
        a_ref, wg_ref, wu_ref, o_ref = refs
    a = a_ref[...]
    g = dot(a, wg_ref[0])
    u = dot(a, wu_ref[0])
    act = _silu(g) * u
    if gated:
        e = pl.program_id(1)
        gt = gates_ref[...]
        lane = lax.broadcasted_iota(jnp.int32, gt.shape, 1)
        act = act * jnp.sum(jnp.where(lane == e, gt, 0.0), axis=1, keepdims=True)
    o_ref[...] = act.astype(o_ref.dtype)


def ffn_up(a, w_up, dot, out_dtype, tm, tn, gates=None):
    t, d = a.shape
    e_n, _, f2 = w_up.shape
    f = f2 // 2
    nf = f // tn
    in_specs = [pl.BlockSpec((tm, d), lambda i, e, j: (i, 0)),
                pl.BlockSpec((1, d, tn), lambda i, e, j: (e, 0, j)),
                pl.BlockSpec((1, d, tn), lambda i, e, j: (e, 0, j + nf))]
    args = [a, w_up, w_up]
    if gates is not None:
        in_specs.append(pl.BlockSpec((tm, LANES), lambda i, e, j: (i, 0)))
        args.append(gates)
    return pl.pallas_call(
        functools.partial(_ffn_up_kernel, dot=dot, gated=gates is not None),
        grid=(t // tm, e_n, nf),
        in_specs=in_specs,
        out_specs=pl.BlockSpec((tm, tn), lambda i, e, j: (i, e * nf + j)),
        out_shape=jax.ShapeDtypeStruct((t, e_n * f), out_dtype),
        compiler_params=_cparams(("arbitrary", "arbitrary", "arbitrary")),
        name="ffn_up",
    )(*args)


def _mm_ln_kernel(*refs, n_a, nk, dot, glu, has_next):
    a_refs = refs[:n_a]
    w_refs = refs[n_a:2 * n_a]
    pos = 2 * n_a
    x_ref, gate_ref, g_ref, b_ref = refs[pos:pos + 4]
    pos += 4
    if has_next:
        sc_ref, sh_ref = refs[pos:pos + 2]
        pos += 2
    xo_ref = refs[pos]
    pos += 1
    if has_next:
        ho_ref = refs[pos]
        pos += 1
    acc_ref = refs[pos] if nk > 1 else None

    tm = x_ref.shape[0]
    sub = min(tm, max(LANES, min(LN_SUBROWS, tm // 2)))

    def load_a(a_ref, rows):
        if len(a_ref.shape) == 3:
            return jnp.concatenate([a_ref[cb, rows, :] for cb in range(a_ref.shape[0])], axis=1)
        return a_ref[rows, :]

    def product(rows):
        part = dot(load_a(a_refs[0], rows), w_refs[0][...])
        for a_ref, w_ref in zip(a_refs[1:], w_refs[1:]):
            part = part + dot(load_a(a_ref, rows), w_ref[...])
        return part

    def per_row(ref, rows):
        return ref[...] if ref.shape[0] == 1 else ref[rows, :]

    def epilogue(acc, rows):
        if glu:
            d = acc.shape[1] // 2
            out = acc[:, :d] * jax.nn.sigmoid(acc[:, d:])
        else:
            out = acc
        r = ALPHA * x_ref[rows, :] + (1.0 + per_row(gate_ref, rows)) * out
        mu = jnp.mean(r, axis=-1, keepdims=True)
        dev = r - mu
        var = jnp.mean(dev * dev, axis=-1, keepdims=True)
        xn = dev * lax.rsqrt(var + LN_EPS) * g_ref[...] + b_ref[...]
        xo_ref[rows, :] = xn
        if has_next:
            ho_ref[rows, :] = (xn * (1.0 + per_row(sc_ref, rows)) + per_row(sh_ref, rows)).astype(ho_ref.dtype)

    blocks = [slice(r0, r0 + sub) for r0 in range(0, tm, sub)]
    if nk == 1:
        for rows in blocks:
            epilogue(product(rows), rows)
    else:
        k = pl.program_id(1)
        part = product(slice(0, tm))

        @pl.when(k == 0)
        def _():
            acc_ref[...] = part

        @pl.when(k > 0)
        def _():
            acc_ref[...] += part

        @pl.when(k == nk - 1)
        def _():
            for rows in blocks:
                epilogue(acc_ref[rows, :], rows)


def matmul_ln(a_list, w, x, gate, ln_g, ln_b, dot, tm, tk=None, glu=False, nxt=None, next_dtype=None):
    t, d = x.shape
    n = w.shape[1]
    n_a = len(a_list)
    if n_a > 1:
        nk = 1
        in_specs = [pl.BlockSpec((tm, a.shape[1]), lambda i, k: (i, 0)) for a in a_list]
        off = 0
        for a in a_list:
            ka = a.shape[1]
            assert off % ka == 0
            in_specs.append(pl.BlockSpec((ka, n), functools.partial(lambda i, k, o: (o, 0), o=off // ka),
                                         pipeline_mode=pl.Buffered(1)))
            off += ka
        w_args = [w] * n_a
    else:
        a0 = a_list[0]
        split_cols = a0.ndim == 3
        ktot = a0.shape[0] * LANES if split_cols else a0.shape[1]
        tk = ktot if tk is None else tk
        nk = ktot // tk
        if split_cols:
            a_spec = pl.BlockSpec((tk // LANES, tm, LANES), lambda i, k: (k, i, 0))
        else:
            a_spec = pl.BlockSpec((tm, tk), lambda i, k: (i, k))
        w_mode = dict(pipeline_mode=pl.Buffered(1)) if nk == 1 else {}
        in_specs = [a_spec, pl.BlockSpec((tk, n), lambda i, k: (k, 0), **w_mode)]
        w_args = [w]
    in_specs += [pl.BlockSpec((tm, d), lambda i, k: (i, 0)), _row_spec(gate, tm),
                 pl.BlockSpec((1, d), lambda i, k: (0, 0)), pl.BlockSpec((1, d), lambda i, k: (0, 0))]
    args = list(a_list) + w_args + [x, gate, ln_g.reshape(1, d), ln_b.reshape(1, d)]
    out_specs = [pl.BlockSpec((tm, d), lambda i, k: (i, 0))]
    out_shape = [jax.ShapeDtypeStruct((t, d), F32)]
    if nxt is not None:
        in_specs += [_row_spec(nxt[0], tm), _row_spec(nxt[1], tm)]
        args += [nxt[0], nxt[1]]
        out_specs.append(pl.BlockSpec((tm, d), lambda i, k: (i, 0)))
        out_shape.append(jax.ShapeDtypeStruct((t, d), next_dtype))
    scratch = [pltpu.VMEM((tm, n), F32)] if nk > 1 else []
    res = pl.pallas_call(
        functools.partial(_mm_ln_kernel, n_a=n_a, nk=nk, dot=dot, glu=glu, has_next=nxt is not None),
        grid=(t // tm, nk),
        in_specs=in_specs,
        out_specs=out_specs,
        out_shape=out_shape,
        scratch_shapes=scratch,
        compiler_params=_cparams(("arbitrary", "arbitrary")),
        name="matmul_ln",
    )(*args)
    return res if nxt is not None else (res[0], None)


def _softmax_pv(scores, values, dot):
    m = scores[0].max(axis=-1, keepdims=True)
    for s in scores[1:]:
        m = jnp.maximum(m, s.max(axis=-1, keepdims=True))
    num = None
    den = None
    for s, v in zip(scores, values):
        p = jnp.exp(s - m)
        l = p.sum(axis=-1, keepdims=True)
        o = dot(p, v)
        num = o if num is None else num + o
        den = l if den is None else den + l
    return num / den


def _attn_prompt_kernel(q_ref, kp_ref, kc_ref, vp_ref, vc_ref, bias_ref, o_ref, *, dot):
    i = pl.program_id(0)
    lo = lax.broadcasted_iota(jnp.int32, (1, LANES), 1) < A_HEAD_DIM
    q = q_ref[...] * (A_HEAD_DIM ** -0.5)
    k = jnp.concatenate([kp_ref[...], kc_ref[...]], axis=0)
    v = jnp.concatenate([vp_ref[...], vc_ref[...]], axis=0)
    kidx = lax.broadcasted_iota(jnp.int32, (1, ATT_KSUB), 1)
    for sub in range(ATT_QBLOCK // ATT_QSUB):
        r0 = sub * ATT_QSUB
        qs = q[r0:r0 + ATT_QSUB]
        ks = k[r0:r0 + ATT_KSUB]
        vs = v[r0:r0 + ATT_KSUB]
        before = jnp.where(jnp.logical_and(i == 0, kidx + r0 < ATT_QBLOCK), NEG_INF, 0.0)
        outs = []
        for hh in range(2):
            qm = jnp.where(lo if hh == 0 else jnp.logical_not(lo), qs, 0.0)
            s = dot(qm, ks, nt=True) + bias_ref[0, hh] + before
            outs.append(_softmax_pv([s], [vs], dot))
        o_ref[r0:r0 + ATT_QSUB, :] = jnp.where(lo, outs[0], outs[1]).astype(o_ref.dtype)


def _band_table_kernel(base_ref, allow_ref, o_ref):
    lq, lk = o_ref.shape[1], o_ref.shape[2]
    wide = jnp.broadcast_to(base_ref[0], (lq, base_ref.shape[2]))
    toep = pltpu.roll(wide, 0, 1, stride=1, stride_axis=0)
    o_ref[0] = jnp.where(allow_ref[...] > 0.0, toep[:, :lk], NEG_INF)


def _band_tables(rel_bias, lq, lk, q_pos0, k_pos0):
    n_heads = rel_bias.shape[0]
    width = pl.next_power_of_2(lq + lk)
    c = np.arange(width)
    m = np.where(c < lk, c, c - width)
    idx = np.clip((q_pos0 - k_pos0) - m, -REL_CLIP, REL_CLIP) + REL_CLIP
    base = rel_bias[:, idx].astype(F32).reshape(n_heads, 1, width)
    q_pos = q_pos0 + np.arange(lq)
    k_pos = k_pos0 + np.arange(lk)
    qc = q_pos[:, None] // CHUNK
    kc = k_pos[None, :] // CHUNK
    allowed = ((kc <= qc) & (kc >= qc - A_PAST_CHUNKS) & (k_pos[None, :] >= 0)).astype(np.float32)
    return pl.pallas_call(
        _band_table_kernel,
        grid=(n_heads,),
        in_specs=[pl.BlockSpec((1, 1, width), lambda h: (h, 0, 0)),
                  pl.BlockSpec((lq, lk), lambda h: (0, 0))],
        out_specs=pl.BlockSpec((1, lq, lk), lambda h: (h, 0, 0)),
        out_shape=jax.ShapeDtypeStruct((n_heads, lq, lk), F32),
        compiler_params=_cparams(("arbitrary",)),
        name="band_table",
    )(base, jnp.asarray(allowed))


def attention_prompt(proj, rel_bias, dot, out_dtype):
    t = proj.shape[0]
    nb = t // ATT_QBLOCK
    npair = A_HEADS // 2
    tab = _band_tables(rel_bias, ATT_QSUB, ATT_KSUB, A_WINDOW, 0).reshape(npair, 2, ATT_QSUB, ATT_KSUB)
    blk = (ATT_QBLOCK, LANES)
    return pl.pallas_call(
        functools.partial(_attn_prompt_kernel, dot=dot),
        grid=(nb, npair),
        in_specs=[pl.BlockSpec(blk, lambda i, p: (i, p)),
                  pl.BlockSpec(blk, lambda i, p: (jnp.maximum(i - 1, 0), npair + p)),
                  pl.BlockSpec(blk, lambda i, p: (i, npair + p)),
                  pl.BlockSpec(blk, lambda i, p: (jnp.maximum(i - 1, 0), 2 * npair + p)),
                  pl.BlockSpec(blk, lambda i, p: (i, 2 * npair + p)),
                  pl.BlockSpec((1, 2, ATT_QSUB, ATT_KSUB), lambda i, p: (p, 0, 0, 0))],
        out_specs=pl.BlockSpec(blk, lambda i, p: (i, p)),
        out_shape=jax.ShapeDtypeStruct((t, A_WIDTH), out_dtype),
        compiler_params=_cparams(("arbitrary", "arbitrary")),
        name="attn_prompt",
    )(proj, proj, proj, proj, proj, tab)


def _attn_sample_kernel(q_ref, kn_ref, vn_ref, kc_ref, vc_ref, bias_ref, o_ref, *, dot, w):
    lo = lax.broadcasted_iota(jnp.int32, (1, LANES), 1) < A_HEAD_DIM
    lq = q_ref.shape[0]
    for p in range(A_HEADS // 2):
        cols = slice(p * LANES, (p + 1) * LANES)
        q = q_ref[:, cols] * (A_HEAD_DIM ** -0.5)
        kc = kc_ref[0, :, cols]
        vc = vc_ref[0, :, cols]
        kn = kn_ref[:, cols]
        vn = vn_ref[:, cols]
        outs = []
        for hh in range(2):
            qm = jnp.where(lo if hh == 0 else jnp.logical_not(lo), q, 0.0)
            bias = bias_ref[2 * p + hh]
            s_c = dot(qm, kc, nt=True) + bias[:, :w]
            s_n = dot(qm, kn, nt=True) + bias[:, w:w + lq]
            outs.append(_softmax_pv([s_c, s_n], [vc, vn], dot))
        o_ref[:, cols] = jnp.where(lo, outs[0], outs[1]).astype(o_ref.dtype)


def attention_sample(proj, k_cache, v_cache, rel_bias, bsz, lq, dot, out_dtype):
    w = k_cache.shape[1]
    lkp = ((w + lq + LANES - 1) // LANES) * LANES
    tab = _band_tables(rel_bias, lq, lkp, PAST_LEN, PAST_LEN - w)
    kc = k_cache.reshape(bsz, w, A_WIDTH)
    vc = v_cache.reshape(bsz, w, A_WIDTH)
    return pl.pallas_call(
        functools.partial(_attn_sample_kernel, dot=dot, w=w),
        grid=(bsz,),
        in_specs=[pl.BlockSpec((lq, A_WIDTH), lambda b: (b, 0)),
                  pl.BlockSpec((lq, A_WIDTH), lambda b: (b, 1)),
                  pl.BlockSpec((lq, A_WIDTH), lambda b: (b, 2)),
                  pl.BlockSpec((1, w, A_WIDTH), lambda b: (b, 0, 0)),
                  pl.BlockSpec((1, w, A_WIDTH), lambda b: (b, 0, 0)),
                  pl.BlockSpec((A_HEADS, lq, lkp), lambda b: (0, 0, 0))],
        out_specs=pl.BlockSpec((lq, A_WIDTH), lambda b: (b, 0)),
        out_shape=jax.ShapeDtypeStruct((bsz * lq, A_WIDTH), out_dtype),
        compiler_params=_cparams(("arbitrary",)),
        name="attn_sample",
    )(proj, proj, proj, kc, vc, tab)


def _ssd_kernel(z_ref, x0_ref, x1_ref, x2_ref, dt_ref, cbuf_ref, h0_ref,
                cw_ref, cb_ref, dtb_ref, alog_ref, dsk_ref, ng_ref, exp_ref,
                y_ref, cnew_ref, hout_ref, xp_ref, st_ref, *, dot, q):
    c = pl.program_id(1)
    half = B_WIDTH // B_GROUPS

    @pl.when(c == 0)
    def _():
        xp_ref[0:SUBLANES, :] = cbuf_ref[0]
        st_ref[...] = h0_ref[0]

    xp_ref[SUBLANES:SUBLANES + q, 0:512] = x0_ref[...]
    xp_ref[SUBLANES:SUBLANES + q, 512:1024] = x1_ref[...]
    xp_ref[SUBLANES:SUBLANES + q, 1024:1536] = x2_ref[...]
    base = SUBLANES - (B_CONV - 1)
    conv = cb_ref[...] + xp_ref[base:base + q, :] * cw_ref[0:1, :]
    for tap in range(1, B_CONV):
        conv = conv + xp_ref[base + tap:base + tap + q, :] * cw_ref[tap:tap + 1, :]
    tail = xp_ref[q:q + SUBLANES, :]
    xp_ref[0:SUBLANES, :] = tail
    cnew_ref[0] = tail

    xbc = _silu(conv)
    xs = xbc[:, :B_WIDTH]
    bm = xbc[:, B_WIDTH:B_WIDTH + B_GROUPS * B_STATE]
    cm = xbc[:, B_WIDTH + B_GROUPS * B_STATE:]

    lane = lax.broadcasted_iota(jnp.int32, (1, LANES), 1)
    head_ok = lane < B_HEADS
    dt = jnp.where(head_ok, jax.nn.softplus(dt_ref[...] + dtb_ref[...]), 0.0)
    a_neg = -jnp.exp(alog_ref[...])
    row = lax.broadcasted_iota(jnp.int32, (q, q), 0)
    col = lax.broadcasted_iota(jnp.int32, (q, q), 1)
    tril = row >= col
    acs = _dot_sel(tril.astype(BF16), dt * a_neg)
    acs_t = acs.T
    eacs = jnp.exp(acs)
    to_end = jnp.exp(acs[q - 1:q, :] - acs)
    wide = _dot_rsel(jnp.concatenate([dt, eacs, to_end], axis=0), exp_ref[...])
    dt_w = wide[0:q]
    eacs_w = wide[q:2 * q]
    toend_w = wide[2 * q:3 * q]
    xdt = xs * dt_w
    xend = xdt * toend_w
    lo = lane < B_HEAD_DIM

    y_parts = []
    for g in range(B_GROUPS):
        bg = bm[:, g * B_STATE:(g + 1) * B_STATE]
        cg = cm[:, g * B_STATE:(g + 1) * B_STATE]
        gmat = dot(cg, bg, nt=True)
        y_off = dot(cg, st_ref[g]) * eacs_w[:, g * half:(g + 1) * half]
        for pr in range(half // LANES):
            cols = slice(g * half + pr * LANES, g * half + (pr + 1) * LANES)
            x_pair = xdt[:, cols]
            outs = []
            for hh in range(2):
                h = (g * half + pr * LANES) // B_HEAD_DIM + hh
                diff = acs[:, h:h + 1] - acs_t[h:h + 1, :]
                decay = jnp.exp(jnp.where(tril, diff, -jnp.inf))
                outs.append(dot(gmat * decay, x_pair))
            y_parts.append(jnp.where(lo, outs[0], outs[1]) + y_off[:, pr * LANES:(pr + 1) * LANES])
        st_new = dot(bg.T, xend[:, g * half:(g + 1) * half])
        st_ref[g] = st_ref[g] * eacs_w[q - 1:q, g * half:(g + 1) * half] + st_new
    hout_ref[0] = st_ref[...]

    y = jnp.concatenate(y_parts, axis=1) + dsk_ref[...] * xs
    y = y * _silu(z_ref[...])
    y = y * lax.rsqrt(jnp.mean(y * y, axis=-1, keepdims=True) + RMS_EPS) * ng_ref[...]
    y_ref[...] = y.astype(y_ref.dtype)


def ssd_mixer(proj, conv_buf, h0, conv_w, conv_b, dt_bias, a_log, d_skip, norm_g, bsz, seq, q, dot, out_dtype):
    t = bsz * seq
    nc = seq // q
    half = B_WIDTH // B_GROUPS
    cbuf = jnp.pad(conv_buf, ((0, 0), (SUBLANES - (B_CONV - 1), 0), (0, 0)))
    h0_t = jnp.transpose(h0.reshape(bsz, B_GROUPS, half, B_STATE), (0, 1, 3, 2))
    pad = lambda v: jnp.pad(v.reshape(1, -1), ((0, 0), (0, LANES - v.shape[-1])))
    expand = jnp.asarray(np.repeat(np.eye(LANES, B_HEADS, dtype=np.float32).T, B_HEAD_DIM, axis=0).T, BF16)
    dsk_w = jnp.repeat(d_skip, B_HEAD_DIM).reshape(1, B_WIDTH)
    zb, xb, db = 3 * A_WIDTH // B_WIDTH, (3 * A_WIDTH + B_WIDTH) // 512, (3 * A_WIDTH + B_WIDTH + B_CONV_DIM) // LANES
    const = lambda shp: pl.BlockSpec(shp, lambda b, c: (0,) * len(shp))
    y, cnew, hout = pl.pallas_call(
        functools.partial(_ssd_kernel, dot=dot, q=q),
        grid=(bsz, nc),
        in_specs=[pl.BlockSpec((q, B_WIDTH), lambda b, c: (b * nc + c, zb)),
                  pl.BlockSpec((q, 512), lambda b, c: (b * nc + c, xb)),
                  pl.BlockSpec((q, 512), lambda b, c: (b * nc + c, xb + 1)),
                  pl.BlockSpec((q, 512), lambda b, c: (b * nc + c, xb + 2)),
                  pl.BlockSpec((q, LANES), lambda b, c: (b * nc + c, db)),
                  pl.BlockSpec((1, SUBLANES, B_CONV_DIM), lambda b, c: (b, 0, 0)),
                  pl.BlockSpec((1, B_GROUPS, B_STATE, half), lambda b, c: (b, 0, 0, 0)),
                  const((B_CONV, B_CONV_DIM)), const((1, B_CONV_DIM)), const((1, LANES)), const((1, LANES)),
                  const((1, B_WIDTH)), const((1, B_WIDTH)), const((LANES, B_WIDTH))],
        out_specs=[pl.BlockSpec((q, B_WIDTH), lambda b, c: (b * nc + c, 0)),
                   pl.BlockSpec((1, SUBLANES, B_CONV_DIM), lambda b, c: (b, 0, 0)),
                   pl.BlockSpec((1, B_GROUPS, B_STATE, half), lambda b, c: (b, 0, 0, 0))],
        out_shape=[jax.ShapeDtypeStruct((t, B_WIDTH), out_dtype),
                   jax.ShapeDtypeStruct((bsz, SUBLANES, B_CONV_DIM), F32),
                   jax.ShapeDtypeStruct((bsz, B_GROUPS, B_STATE, half), F32)],
        scratch_shapes=[pltpu.VMEM((q + SUBLANES, B_CONV_DIM), F32),
                        pltpu.VMEM((B_GROUPS, B_STATE, half), F32)],
        compiler_params=_cparams(("arbitrary", "arbitrary")),
        name="ssd",
    )(proj, proj, proj, proj, proj, cbuf, h0_t,
      conv_w, conv_b.reshape(1, -1), pad(dt_bias), pad(a_log), dsk_w, norm_g.reshape(1, -1), expand)
    conv_new = cnew[:, SUBLANES - (B_CONV - 1):, :]
    h_final = jnp.transpose(hout, (0, 1, 3, 2)).reshape(bsz, B_HEADS, B_HEAD_DIM, B_STATE)
    return y, conv_new, h_final


def _s5_prep_kernel(lre_ref, lim_ref, ls_ref, bre_ref, bim_ref, are_ref, aim_ref, bbre_ref, bbim_ref):
    lre = lre_ref[...]
    lim = lim_ref[...]
    step = jnp.exp(ls_ref[...])
    mag = jnp.exp(lre * step)
    ang = lim * step
    ab_re = mag * jnp.cos(ang)
    ab_im = mag * jnp.sin(ang)
    den = lre * lre + lim * lim
    f_re = ((ab_re - 1.0) * lre + ab_im * lim) / den
    f_im = (ab_im * lre - (ab_re - 1.0) * lim) / den
    br = bre_ref[...]
    bi = bim_ref[...]
    are_ref[...] = ab_re
    aim_ref[...] = ab_im
    bbre_ref[...] = f_re * br - f_im * bi
    bbim_ref[...] = f_re * bi + f_im * br


def _cmul(ar, ai, br, bi):
    return ar * br - ai * bi, ar * bi + ai * br


def _gelu_tanh(x):
    return 0.5 * x * (1.0 + jnp.tanh(math.sqrt(2.0 / math.pi) * (x + 0.044715 * (x * x * x))))


def _s5_kernel(u_ref, bb_ref, ccre_ref, ccim_ref, a_ref, d_ref, h0_ref, y_ref, hout_ref,
               s_ref, pw_ref, pwb_ref, carry_ref, *, dot, seg):
    t = pl.program_id(2)
    w = S5_JW
    unroll = min(seg, 4)
    a_re = a_ref[0, 0:1, :]
    a_im = a_ref[0, 1:2, :]

    @pl.when(t == 0)
    def _():
        carry_ref[0:1, :] = h0_ref[0, 0]
        pw_ref[0:1, 0:w] = a_re
        pw_ref[0:1, w:2 * w] = a_im

        def power(i, carry):
            pr, pi = _cmul(pw_ref[pl.ds(i - 1, 1), 0:w], pw_ref[pl.ds(i - 1, 1), w:2 * w], a_re, a_im)
            pw_ref[pl.ds(i, 1), 0:w] = pr
            pw_ref[pl.ds(i, 1), w:2 * w] = pi
            return carry

        lax.fori_loop(1, seg, power, 0)

        def spread(i, carry):
            rows = pl.ds(pl.multiple_of(i * SUBLANES, SUBLANES), SUBLANES)
            pwb_ref[rows, :] = jnp.broadcast_to(pw_ref[pl.ds(i, 1), :], (SUBLANES, 2 * w))
            return carry

        lax.fori_loop(0, seg, spread, 0)

    ncb = u_ref.shape[0]
    up = jnp.concatenate(
        [jnp.concatenate([u_ref[cb, pl.ds(i, SUBLANES, stride=seg), :] for i in range(seg)], axis=0)
         for cb in range(ncb)], axis=1)
    s_ref[...] = dot(up, bb_ref[0])

    are_b = jnp.broadcast_to(a_re, (SUBLANES, w))
    aim_b = jnp.broadcast_to(a_im, (SUBLANES, w))

    def scan(i, st):
        sre, sim = st
        rows = pl.ds(pl.multiple_of(i * SUBLANES, SUBLANES), SUBLANES)
        pr, pi = _cmul(are_b, aim_b, sre, sim)
        nre = pr + s_ref[rows, 0:w]
        nim = pi + s_ref[rows, w:2 * w]
        s_ref[rows, 0:w] = nre
        s_ref[rows, w:2 * w] = nim
        return nre, nim

    zero = jnp.zeros((SUBLANES, w), F32)
    end_re, end_im = lax.fori_loop(0, seg, scan, (zero, zero), unroll=unroll)

    sg_re = pw_ref[seg - 1:seg, 0:w]
    sg_im = pw_ref[seg - 1:seg, w:2 * w]
    c_re = carry_ref[0:1, 0:w]
    c_im = carry_ref[0:1, w:2 * w]
    rows_re, rows_im = [], []
    for s in range(SUBLANES):
        rows_re.append(c_re)
        rows_im.append(c_im)
        pr, pi = _cmul(sg_re, sg_im, c_re, c_im)
        c_re = pr + end_re[s:s + 1, :]
        c_im = pi + end_im[s:s + 1, :]
    carry_ref[0:1, 0:w] = c_re
    carry_ref[0:1, w:2 * w] = c_im
    hout_ref[0, 0] = carry_ref[0:1, :]
    in_re = jnp.concatenate(rows_re, axis=0)
    in_im = jnp.concatenate(rows_im, axis=0)

    def fixup(i, carry):
        rows = pl.ds(pl.multiple_of(i * SUBLANES, SUBLANES), SUBLANES)
        pr, pi = _cmul(pwb_ref[rows, 0:w], pwb_ref[rows, w:2 * w], in_re, in_im)
        s_ref[rows, 0:w] = s_ref[rows, 0:w] + pr
        s_ref[rows, w:2 * w] = s_ref[rows, w:2 * w] + pi
        return carry

    lax.fori_loop(0, seg, fixup, 0, unroll=unroll)

    y = dot(s_ref[:, 0:w], ccre_ref[0]) - dot(s_ref[:, w:2 * w], ccim_ref[0]) + d_ref[...] * up
    y = _gelu_tanh(y)
    for cb in range(ncb):
        for i in range(seg):
            y_ref[cb, pl.ds(i, SUBLANES, stride=seg), :] = (
                y[i * SUBLANES:(i + 1) * SUBLANES, cb * LANES:(cb + 1) * LANES])


def _block_diag(m):
    nj, j, r, c = m.shape
    eye = jnp.eye(j, dtype=m.dtype)
    return (m[:, :, :, None, :] * eye[None, :, None, :, None]).reshape(nj, j * r, j * c)


def s5_mixer(u, h0_re, h0_im, lam_re, lam_im, log_step, b_re, b_im, c_re, c_im, d_skip,
             bsz, seq, tm, dot, w_dtype):
    t = bsz * seq
    g, p, ch = C_GROUPS, C_STATE, C_GROUP_CH
    rep = lambda v: jnp.repeat(v, ch, axis=-1)
    ls = jnp.broadcast_to(log_step[:, None], (g, p))
    pc = pl.BlockSpec((g, p * ch), lambda: (0, 0))
    shp = jax.ShapeDtypeStruct((g, p * ch), F32)
    a_re_x, a_im_x, bb_re, bb_im = pl.pallas_call(
        _s5_prep_kernel, in_specs=[pc] * 5, out_specs=[pc] * 4, out_shape=[shp] * 4, name="s5_prep",
    )(rep(lam_re), rep(lam_im), rep(ls), b_re.reshape(g, p * ch), b_im.reshape(g, p * ch))
    a_re = a_re_x[:, ::ch]
    a_im = a_im_x[:, ::ch]
    to_blk = lambda m: jnp.transpose(m.reshape(S5_NJ, S5_JBLK, p, ch), (0, 1, 3, 2))
    bb = jnp.concatenate([_block_diag(to_blk(bb_re)), _block_diag(to_blk(bb_im))], axis=-1).astype(w_dtype)
    cblk = lambda m: jnp.transpose(m.reshape(S5_NJ, S5_JBLK, ch, p), (0, 1, 3, 2))
    cc_re = _block_diag(cblk(c_re)).astype(w_dtype)
    cc_im = _block_diag(cblk(c_im)).astype(w_dtype)
    a_rows = jnp.stack([a_re.reshape(S5_NJ, S5_JW), a_im.reshape(S5_NJ, S5_JW)], axis=1)
    h0 = jnp.concatenate([h0_re.reshape(bsz, S5_NJ, 1, S5_JW), h0_im.reshape(bsz, S5_NJ, 1, S5_JW)], axis=-1)
    nt = seq // tm
    seg = tm // SUBLANES
    uw = S5_JBLK * ch
    ncb = uw // LANES
    y, hout = pl.pallas_call(
        functools.partial(_s5_kernel, dot=dot, seg=seg),
        grid=(bsz, S5_NJ, nt),
        in_specs=[pl.BlockSpec((ncb, tm, LANES), lambda b, j, i: (j, b * nt + i, 0)),
                  pl.BlockSpec((1, uw, 2 * S5_JW), lambda b, j, i: (j, 0, 0)),
                  pl.BlockSpec((1, S5_JW, uw), lambda b, j, i: (j, 0, 0)),
                  pl.BlockSpec((1, S5_JW, uw), lambda b, j, i: (j, 0, 0)),
                  pl.BlockSpec((1, 2, S5_JW), lambda b, j, i: (j, 0, 0)),
                  pl.BlockSpec((1, uw), lambda b, j, i: (0, j)),
                  pl.BlockSpec((1, 1, 1, 2 * S5_JW), lambda b, j, i: (b, j, 0, 0))],
        out_specs=[pl.BlockSpec((ncb, tm, LANES), lambda b, j, i: (j, b * nt + i, 0)),
                   pl.BlockSpec((1, 1, 1, 2 * S5_JW), lambda b, j, i: (b, j, 0, 0))],
        out_shape=[jax.ShapeDtypeStruct((g * ch // LANES, t, LANES), F32),
                   jax.ShapeDtypeStruct((bsz, S5_NJ, 1, 2 * S5_JW), F32)],
        scratch_shapes=[pltpu.VMEM((tm, 2 * S5_JW), F32),
                        pltpu.VMEM((seg, 2 * S5_JW), F32),
                        pltpu.VMEM((tm, 2 * S5_JW), F32),
                        pltpu.VMEM((SUBLANES, 2 * S5_JW), F32)],
        compiler_params=_cparams(("arbitrary", "arbitrary", "arbitrary")),
        name="s5_scan",
    )(u, bb, cc_re, cc_im, a_rows, d_skip.reshape(1, -1), h0)
    s_re = hout[:, :, 0, :S5_JW].reshape(bsz, g, p)
    s_im = hout[:, :, 0, S5_JW:].reshape(bsz, g, p)
    return y, s_re, s_im


def _router_kernel(x_ref, sc_ref, sh_ref, w_ref, b_ref, o_ref, r_ref, cnt_ref, run_ref):
    h = x_ref[...] * (1.0 + sc_ref[...]) + sh_ref[...]
    logits = _dot3(h, w_ref[...]) + b_ref[...]
    lane = lax.broadcasted_iota(jnp.int32, logits.shape, 1).astype(F32)
    logits = jnp.where(lane < N_EXPERTS, logits, -jnp.inf)
    m1 = logits.max(axis=-1, keepdims=True)
    i1 = jnp.min(jnp.where(logits == m1, lane, float(LANES)), axis=-1, keepdims=True)
    rest = jnp.where(lane == i1, -jnp.inf, logits)
    m2 = rest.max(axis=-1, keepdims=True)
    i2 = jnp.min(jnp.where(rest == m2, lane, float(LANES)), axis=-1, keepdims=True)
    e2 = jnp.exp(m2 - m1)
    w1 = 1.0 / (1.0 + e2)
    w2 = e2 / (1.0 + e2)
    o_ref[...] = jnp.where(lane == i1, w1, 0.0) + jnp.where(lane == i2, w2, 0.0)

    step = pl.program_id(0)

    @pl.when(step == 0)
    def _():
        run_ref[...] = jnp.zeros_like(run_ref)

    tm = logits.shape[0]
    sel = jnp.where(jnp.logical_or(lane == i1, lane == i2), 1.0, 0.0)
    row = lax.broadcasted_iota(jnp.int32, (tm, tm), 0)
    col = lax.broadcasted_iota(jnp.int32, (tm, tm), 1)
    before = (row > col).astype(BF16)
    rank = _mxu(before, sel.astype(BF16)) + run_ref[0:1, :]
    pos1 = jnp.sum(jnp.where(lane == i1, rank, 0.0), axis=-1, keepdims=True)
    pos2 = jnp.sum(jnp.where(lane == i2, rank, 0.0), axis=-1, keepdims=True)
    run_ref[0:1, :] = run_ref[0:1, :] + jnp.sum(sel, axis=0, keepdims=True)
    cnt_ref[...] = run_ref[0:1, :]
    rec = jnp.zeros_like(logits)
    for k, val in enumerate((i1, i2, pos1, pos2, w1, w2)):
        rec = jnp.where(lane == float(k), val, rec)
    r_ref[...] = rec


ROUTE_E1, ROUTE_E2, ROUTE_POS1, ROUTE_POS2, ROUTE_W1, ROUTE_W2 = range(6)


def router(x, scale, shift, router_w, router_b):
    t, d = x.shape
    tm = min(t, 512)
    w = jnp.pad(router_w, ((0, 0), (0, LANES - N_EXPERTS)))
    b = jnp.pad(router_b.reshape(1, -1), ((0, 0), (0, LANES - N_EXPERTS)))
    return pl.pallas_call(
        _router_kernel,
        grid=(t // tm,),
        in_specs=[pl.BlockSpec((tm, d), lambda i: (i, 0)), _row_spec(scale, tm), _row_spec(shift, tm),
                  pl.BlockSpec((d, LANES), lambda i: (0, 0)), pl.BlockSpec((1, LANES), lambda i: (0, 0))],
        out_specs=[pl.BlockSpec((tm, LANES), lambda i: (i, 0)), pl.BlockSpec((tm, LANES), lambda i: (i, 0)),
                   pl.BlockSpec((1, LANES), lambda i: (0, 0))],
        out_shape=[jax.ShapeDtypeStruct((t, LANES), F32), jax.ShapeDtypeStruct((t, LANES), F32),
                   jax.ShapeDtypeStruct((1, LANES), F32)],
        scratch_shapes=[pltpu.VMEM((SUBLANES, LANES), F32)],
        compiler_params=_cparams(("arbitrary",)),
        name="router",
    )(x, scale, shift, w, b)


MOE_TILE = 256


def _dispatch_kernel(d1_ref, d2_ref, h_ref, xs_in_ref, xs_ref, sem):
    del xs_in_ref
    tm = h_ref.shape[0]

    def row_copy(t, dest):
        return pltpu.make_async_copy(h_ref.at[pl.ds(t, 1)], xs_ref.at[pl.ds(dest, 1)], sem)

    def issue(t, carry):
        row_copy(t, d1_ref[0, 0, t]).start()
        row_copy(t, d2_ref[0, 0, t]).start()
        return carry

    lax.fori_loop(0, tm, issue, 0)

    def drain(t, carry):
        row_copy(t, d1_ref[0, 0, t]).wait()
        row_copy(t, d2_ref[0, 0, t]).wait()
        return carry

    lax.fori_loop(0, tm, drain, 0)


def moe_dispatch(h, dest1, dest2, n_rows):
    t, d = h.shape
    tm = 512
    idx_spec = pl.BlockSpec((1, 1, tm), lambda i: (i, 0, 0), memory_space=pltpu.SMEM)
    return pl.pallas_call(
        _dispatch_kernel,
        grid=(t // tm,),
        in_specs=[idx_spec, idx_spec, pl.BlockSpec((tm, d), lambda i: (i, 0)),
                  pl.BlockSpec(memory_space=pl.ANY)],
        out_specs=pl.BlockSpec(memory_space=pl.ANY),
        out_shape=jax.ShapeDtypeStruct((n_rows, d), h.dtype),
        scratch_shapes=[pltpu.SemaphoreType.DMA(())],
        input_output_aliases={3: 0},
        compiler_params=_cparams(("arbitrary",)),
        name="moe_dispatch",
    )(dest1.reshape(t // tm, 1, tm), dest2.reshape(t // tm, 1, tm), h, jnp.zeros((n_rows, d), h.dtype))


def _expert_up_kernel(te_ref, nv_ref, x_ref, wg_ref, wu_ref, o_ref, *, dot):
    del te_ref

    @pl.when(pl.program_id(1) < nv_ref[0])
    def _():
        x = x_ref[...]
        g = dot(x, wg_ref[0])
        u = dot(x, wu_ref[0])
        o_ref[...] = (_silu(g) * u).astype(o_ref.dtype)

    @pl.when(pl.program_id(1) >= nv_ref[0])
    def _():
        o_ref[...] = jnp.zeros_like(o_ref)


def expert_up(xs, w_up, tile_expert, n_valid, dot, tn):
    r, d = xs.shape
    f = w_up.shape[2] // 2
    nf = f // tn
    nt = r // MOE_TILE
    grid_spec = pltpu.PrefetchScalarGridSpec(
        num_scalar_prefetch=2,
        grid=(nf, nt),
        in_specs=[pl.BlockSpec((MOE_TILE, d), lambda j, i, te, nv: (i, 0)),
                  pl.BlockSpec((1, d, tn), lambda j, i, te, nv: (te[i], 0, j)),
                  pl.BlockSpec((1, d, tn), lambda j, i, te, nv: (te[i], 0, j + nf))],
        out_specs=pl.BlockSpec((MOE_TILE, tn), lambda j, i, te, nv: (i, j)))
    return pl.pallas_call(
        functools.partial(_expert_up_kernel, dot=dot),
        grid_spec=grid_spec,
        out_shape=jax.ShapeDtypeStruct((r, f), BF16),
        compiler_params=_cparams(("arbitrary", "arbitrary")),
        name="expert_up",
    )(tile_expert, n_valid, xs, w_up, w_up)


def _expert_down_kernel(te_ref, nv_ref, a_ref, w_ref, o_ref, *, dot):
    del te_ref

    @pl.when(pl.program_id(0) < nv_ref[0])
    def _():
        o_ref[...] = dot(a_ref[...], w_ref[0])

    @pl.when(pl.program_id(0) >= nv_ref[0])
    def _():
        o_ref[...] = jnp.zeros_like(o_ref)


def expert_down(act, w_down, tile_expert, n_valid, dot):
    r, f = act.shape
    d = w_down.shape[2]
    grid_spec = pltpu.PrefetchScalarGridSpec(
        num_scalar_prefetch=2,
        grid=(r // MOE_TILE,),
        in_specs=[pl.BlockSpec((MOE_TILE, f), lambda i, te, nv: (i, 0)),
                  pl.BlockSpec((1, f, d), lambda i, te, nv: (te[i], 0, 0))],
        out_specs=pl.BlockSpec((MOE_TILE, d), lambda i, te, nv: (i, 0)))
    return pl.pallas_call(
        functools.partial(_expert_down_kernel, dot=dot),
        grid_spec=grid_spec,
        out_shape=jax.ShapeDtypeStruct((r, d), F32),
        compiler_params=_cparams(("arbitrary",)),
        name="expert_down",
    )(tile_expert, n_valid, act, w_down)


def _combine_ln_kernel(d1_ref, d2_ref, y_ref, rec_ref, x_ref, gate_ref, g_ref, b_ref, xo_ref, buf1, buf2, sem):
    tm = x_ref.shape[0]

    def row_copy(t, dest, buf):
        return pltpu.make_async_copy(y_ref.at[pl.ds(dest, 1)], buf.at[pl.ds(t, 1)], sem)

    def issue(t, carry):
        row_copy(t, d1_ref[0, 0, t], buf1).start()
        row_copy(t, d2_ref[0, 0, t], buf2).start()
        return carry

    lax.fori_loop(0, tm, issue, 0)

    def drain(t, carry):
        row_copy(t, d1_ref[0, 0, t], buf1).wait()
        row_copy(t, d2_ref[0, 0, t], buf2).wait()
        return carry

    lax.fori_loop(0, tm, drain, 0)

    rec = rec_ref[...]
    lane = lax.broadcasted_iota(jnp.int32, rec.shape, 1)
    w1 = jnp.sum(jnp.where(lane == ROUTE_W1, rec, 0.0), axis=-1, keepdims=True)
    w2 = jnp.sum(jnp.where(lane == ROUTE_W2, rec, 0.0), axis=-1, keepdims=True)
    out = w1 * buf1[...] + w2 * buf2[...]
    r = ALPHA * x_ref[...] + (1.0 + gate_ref[...]) * out
    mu = jnp.mean(r, axis=-1, keepdims=True)
    dev = r - mu
    var = jnp.mean(dev * dev, axis=-1, keepdims=True)
    xo_ref[...] = dev * lax.rsqrt(var + LN_EPS) * g_ref[...] + b_ref[...]


def moe_combine_ln(y_sorted, dest1, dest2, rec, x, gate, ln_g, ln_b):
    t, d = x.shape
    tm = 256
    idx_spec = pl.BlockSpec((1, 1, tm), lambda i: (i, 0, 0), memory_space=pltpu.SMEM)
    return pl.pallas_call(
        _combine_ln_kernel,
        grid=(t // tm,),
        in_specs=[idx_spec, idx_spec, pl.BlockSpec(memory_space=pl.ANY),
                  pl.BlockSpec((tm, LANES), lambda i: (i, 0)), pl.BlockSpec((tm, d), lambda i: (i, 0)),
                  _row_spec(gate, tm), pl.BlockSpec((1, d), lambda i: (0, 0)), pl.BlockSpec((1, d), lambda i: (0, 0))],
        out_specs=pl.BlockSpec((tm, d), lambda i: (i, 0)),
        out_shape=jax.ShapeDtypeStruct((t, d), F32),
        scratch_shapes=[pltpu.VMEM((tm, d), F32), pltpu.VMEM((tm, d), F32), pltpu.SemaphoreType.DMA(())],
        compiler_params=_cparams(("arbitrary",)),
        name="moe_combine_ln",
    )(dest1.reshape(t // tm, 1, tm), dest2.reshape(t // tm, 1, tm), y_sorted, rec, x, gate,
      ln_g.reshape(1, d), ln_b.reshape(1, d))


def moe_top2(h, x, rec, counts, gate, ln_g, ln_b, w_up, w_down, dot):
    t, d = h.shape
    n_tiles = (TOP_K * t) // MOE_TILE + N_EXPERTS
    n_rows = n_tiles * MOE_TILE
    cnt = counts[0, :N_EXPERTS].astype(jnp.int32)
    padded = ((cnt + MOE_TILE - 1) // MOE_TILE) * MOE_TILE
    ends = jnp.cumsum(padded)
    starts = ends - padded
    col = lambda k: rec[:, k].astype(jnp.int32)
    experts = jnp.arange(N_EXPERTS, dtype=jnp.int32)[None, :]
    start_of = lambda e: jnp.sum(jnp.where(e[:, None] == experts, starts[None, :], 0), axis=1)
    dest1 = start_of(col(ROUTE_E1)) + col(ROUTE_POS1)
    dest2 = start_of(col(ROUTE_E2)) + col(ROUTE_POS2)
    tile_start = jnp.arange(n_tiles, dtype=jnp.int32) * MOE_TILE
    tile_expert = jnp.minimum(jnp.searchsorted(ends, tile_start, side="right"), N_EXPERTS - 1).astype(jnp.int32)
    n_valid = (ends[-1:] // MOE_TILE).astype(jnp.int32)
    xs = moe_dispatch(h, dest1, dest2, n_rows)
    act = expert_up(xs, w_up, tile_expert, n_valid, dot, tn=D_FF_EXPERT // 2)
    y_sorted = expert_down(act, w_down, tile_expert, n_valid, dot)
    return moe_combine_ln(y_sorted, dest1, dest2, rec, x, gate, ln_g, ln_b)


def _trunk(x, mods, caches, wts, precise):
    bsz, seq, d = x.shape
    t = bsz * seq
    prompt = caches is None
    dot = _dot3 if precise else _dot1
    act_dtype = F32 if precise else BF16
    tm = min(t, 512)
    x0 = x.reshape(t, d)

    sh, sc, gt = mods[0][0]
    h = modulate(x0, sc, sh, act_dtype)
    proj = matmul(h, wts["w_in0"], dot)
    k_all = proj[:, A_WIDTH:2 * A_WIDTH].reshape(bsz, seq, A_HEADS, A_HEAD_DIM)
    v_all = proj[:, 2 * A_WIDTH:3 * A_WIDTH].reshape(bsz, seq, A_HEADS, A_HEAD_DIM)
    if prompt:
        att = attention_prompt(proj, wts["rel_bias"], dot, act_dtype)
        keep = min(A_WINDOW, seq)
        k_new, v_new = k_all[:, seq - keep:], v_all[:, seq - keep:]
        conv_buf = jnp.zeros((bsz, B_CONV - 1, B_CONV_DIM), F32)
        ssm_h0 = jnp.zeros((bsz, B_HEADS, B_HEAD_DIM, B_STATE), F32)
        q_len = min(seq, 256)
    else:
        att = attention_sample(proj, caches["k"], caches["v"], wts["rel_bias"], bsz, seq, dot, act_dtype)
        k_new, v_new = k_all, v_all
        conv_buf, ssm_h0 = caches["conv"], caches["ssm"]
        q_len = seq
    y_ssd, conv_new, ssm_new = ssd_mixer(proj, conv_buf, ssm_h0, wts["conv_w"], wts["conv_b"], wts["dt_bias"],
                                         wts["a_log"], wts["ssd_d"], wts["ssd_norm_g"], bsz, seq, q_len,
                                         dot, act_dtype)
    x1, h1 = matmul_ln([att, y_ssd], wts["w_out0"], x0, gt, wts["ln_g"][0, 0], wts["ln_b"][0, 0], dot, tm,
                       nxt=(mods[0][1][1], mods[0][1][0]), next_dtype=act_dtype)
    act = ffn_up(h1, wts["ffn_w_up"], dot, act_dtype, tm=min(t, 1024), tn=512)
    x2, h2 = matmul_ln([act], wts["ffn_w_down"], x1, mods[0][1][2], wts["ln_g"][0, 1], wts["ln_b"][0, 1], dot,
                       min(t, 256), tk=None if prompt else 1408, nxt=(mods[1][0][1], mods[1][0][0]), next_dtype=act_dtype)

    u = matmul(h2, wts["w_in1"], dot, split_cols=True)
    if prompt:
        s5_re0 = jnp.zeros((bsz, C_GROUPS, C_STATE), F32)
        s5_im0 = jnp.zeros((bsz, C_GROUPS, C_STATE), F32)
    else:
        s5_re0, s5_im0 = caches["s5_re"], caches["s5_im"]
    y5, s5_re, s5_im = s5_mixer(u, s5_re0, s5_im0, wts["s5_lam_re"], wts["s5_lam_im"], wts["s5_log_step"],
                                wts["s5_b_re"], wts["s5_b_im"], wts["s5_c_re"], wts["s5_c_im"], wts["s5_d"],
                                bsz, seq, min(seq, 512), dot, F32 if precise else BF16)
    x3, h3 = matmul_ln([y5], wts["glu_w"], x2, mods[1][0][2], wts["ln_g"][1, 0], wts["ln_b"][1, 0], dot,
                       min(t, 256), tk=None if prompt else 512, glu=True,
                       nxt=(mods[1][1][1], mods[1][1][0]), next_dtype=F32)
    gates, rec, counts = router(x3, mods[1][1][1], mods[1][1][0], wts["router_w"], wts["router_b"])
    if prompt:
        x4 = moe_top2(h3, x3, rec, counts, mods[1][1][2], wts["ln_g"][1, 1], wts["ln_b"][1, 1],
                      wts["moe_w_up"], wts["moe_w_down"], dot)
    else:
        act = ffn_up(h3, wts["moe_w_up"], dot, act_dtype, tm=min(t, 2048), tn=256, gates=gates)
        w_down = wts["moe_w_down"].reshape(N_EXPERTS * D_FF_EXPERT, d)
        x4, _ = matmul_ln([act], w_down, x3, mods[1][1][2], wts["ln_g"][1, 1], wts["ln_b"][1, 1], dot, tm, tk=1408)
    return (x4.reshape(bsz, seq, d), k_new[None], v_new[None], conv_new[None], ssm_new[None],
            s5_re[None], s5_im[None])


def kernel(x_prompt, x_sample, cache_attn_k, cache_attn_v, state_ssd_conv, state_ssd, state_s5_re, state_s5_im, c_prompt, c_sample, ada_w, ada_b, ln_g, ln_b, w_in0, w_out0, rel_bias, conv_w, conv_b, dt_bias, a_log, ssd_d, ssd_norm_g, ffn_w_up, ffn_w_down, w_in1, s5_lam_re, s5_lam_im, s5_log_step, s5_b_re, s5_b_im, s5_c_re, s5_c_im, s5_d, glu_w, router_w, router_b, moe_w_up, moe_w_down):
    d = D_MODEL
    bp, lp, _ = x_prompt.shape
    bs, ls, _ = x_sample.shape

    n_c = bp + bs
    rows = ((n_c + SUBLANES - 1) // SUBLANES) * SUBLANES
    c_all = jnp.pad(jnp.concatenate([c_prompt, c_sample], axis=0), ((0, rows - n_c), (0, 0)))
    mod = adaln_all(c_all, ada_w, ada_b).reshape(DEPTH, 2, rows, 3, d)

    def mods_for(r0, nb, per_row):
        out = []
        for layer in range(DEPTH):
            out.append([])
            for j in range(2):
                trip = []
                for part in range(3):
                    m = mod[layer, j, r0:r0 + nb, part]
                    trip.append(jnp.repeat(m, per_row, axis=0) if nb > 1 else m)
                out[-1].append(tuple(trip))
        return out

    assert bp == 1
    mods_p = mods_for(0, bp, lp)
    mods_s = mods_for(bp, bs, ls)

    in0_pad = ((0, 0), (0, IN0_PAD - IN0_WIDTH))
    shared = dict(ln_g=ln_g, ln_b=ln_b, rel_bias=rel_bias[0], conv_w=conv_w[0], conv_b=conv_b[0],
                  dt_bias=dt_bias[0], a_log=a_log[0], ssd_d=ssd_d[0], ssd_norm_g=ssd_norm_g[0],
                  s5_lam_re=s5_lam_re[0], s5_lam_im=s5_lam_im[0], s5_log_step=s5_log_step[0],
                  s5_b_re=s5_b_re[0], s5_b_im=s5_b_im[0], s5_c_re=s5_c_re[0], s5_c_im=s5_c_im[0], s5_d=s5_d[0],
                  router_w=router_w[0], router_b=router_b[0])
    big = dict(w_in0=jnp.pad(w_in0[0], in0_pad), w_out0=w_out0[0], ffn_w_up=ffn_w_up, ffn_w_down=ffn_w_down[0],
               w_in1=w_in1[0], glu_w=glu_w[0], moe_w_up=moe_w_up[0],
               moe_w_down=moe_w_down[0])
    wts_s = dict(shared, **big)
    wts_p = dict(shared, **{k: v.astype(BF16) for k, v in big.items()})

    y_p, k_p, v_p, conv_p, ssd_p, re_p, im_p = _trunk(x_prompt, mods_p, None, wts_p, precise=False)
    caches = dict(k=cache_attn_k[0], v=cache_attn_v[0], conv=state_ssd_conv[0], ssm=state_ssd[0],
                  s5_re=state_s5_re[0], s5_im=state_s5_im[0])
    y_s, k_s, v_s, conv_s, ssd_s, re_s, im_s = _trunk(x_sample, mods_s, caches, wts_s, precise=True)
    return (y_p, y_s, k_p, v_p, conv_p, ssd_p, re_p, im_p, k_s, v_s, conv_s, ssd_s, re_s, im_s)
```

```python
import functools
import math

import numpy as np
import jax
import jax.numpy as jnp
from jax import lax
from jax.experimental import pallas as pl
from jax.experimental.pallas import tpu as pltpu

F32 = jnp.float32
BF16 = jnp.bfloat16

D_MODEL = 2048
DEPTH = 2
PAST_LEN = 2048
CHUNK = 64
A_HEADS = 16
A_HEAD_DIM = 64
A_WIDTH = A_HEADS * A_HEAD_DIM
A_PAST_CHUNKS = 8
A_WINDOW = A_PAST_CHUNKS * CHUNK
REL_CLIP = 128
B_HEADS = 16
B_HEAD_DIM = 64
B_WIDTH = B_HEADS * B_HEAD_DIM
B_GROUPS = 2
B_STATE = 128
B_CONV = 4
B_CONV_DIM = B_WIDTH + 2 * B_GROUPS * B_STATE
C_GROUP_CH = 16
C_GROUPS = D_MODEL // C_GROUP_CH
C_STATE = 64
D_FF = 5632
N_EXPERTS = 8
TOP_K = 2
D_FF_EXPERT = 2816
ALPHA = (2.0 * DEPTH) ** 0.25
LN_EPS = 1e-5
RMS_EPS = 1e-5
NEG_INF = -1e30
IN0_WIDTH = 3 * A_WIDTH + B_WIDTH + B_CONV_DIM + B_HEADS
IN0_PAD = 6144

LANES = 128
SUBLANES = 8
VMEM_LIMIT = 56 * 1024 * 1024

LN_SUBROWS = 256
ATT_QBLOCK = 512
ATT_QSUB = 256
ATT_KSUB = ATT_QSUB + A_WINDOW
S5_JBLK = 16
S5_NJ = C_GROUPS // S5_JBLK
S5_JW = S5_JBLK * C_STATE


def _cparams(sem):
    return pltpu.CompilerParams(dimension_semantics=sem, vmem_limit_bytes=VMEM_LIMIT)


def _split_bf16(x):
    hi = x.astype(BF16)
    lo = (x - hi.astype(F32)).astype(BF16)
    return hi, lo


def _mxu(a, b, nt=False):
    if nt:
        return lax.dot_general(a, b, (((1,), (1,)), ((), ())), preferred_element_type=F32)
    return jnp.dot(a, b, preferred_element_type=F32)


def _dot1(a, b, nt=False):
    return _mxu(a.astype(BF16), b.astype(BF16), nt)


def _dot3(a, b, nt=False):
    ah, al = _split_bf16(a.astype(F32))
    bh, bl = _split_bf16(b.astype(F32))
    return _mxu(ah, bh, nt) + (_mxu(ah, bl, nt) + _mxu(al, bh, nt))


def _dot_sel(sel_bf16, x):
    x1 = x.astype(BF16)
    r1 = x - x1.astype(F32)
    x2 = r1.astype(BF16)
    x3 = (r1 - x2.astype(F32)).astype(BF16)
    return _mxu(sel_bf16, x1) + (_mxu(sel_bf16, x2) + _mxu(sel_bf16, x3))


def _dot_rsel(x, sel_bf16):
    x1 = x.astype(BF16)
    r1 = x - x1.astype(F32)
    x2 = r1.astype(BF16)
    x3 = (r1 - x2.astype(F32)).astype(BF16)
    return _mxu(x1, sel_bf16) + (_mxu(x2, sel_bf16) + _mxu(x3, sel_bf16))


def _silu(x):
    return x * jax.nn.sigmoid(x)


def _row_spec(arr, tm):
    d = arr.shape[-1]
    if arr.shape[0] == 1:
        return pl.BlockSpec((1, d), lambda i, *_: (0, 0))
    return pl.BlockSpec((tm, d), lambda i, *_: (i, 0))


def _adaln_kernel(c_ref, w_ref, b_ref, o_ref):
    c = c_ref[...]
    o_ref[0] = _dot3(_silu(c), w_ref[0]) + b_ref[0]


def adaln_all(c_rows, ada_w, ada_b):
    r, d = c_rows.shape
    n = ada_w.shape[-1]
    tn = 512
    w = ada_w.reshape(2 * DEPTH, d, n)
    b = ada_b.reshape(2 * DEPTH, 1, n)
    return pl.pallas_call(
        _adaln_kernel,
        grid=(2 * DEPTH, n // tn),
        in_specs=[pl.BlockSpec((r, d), lambda i, j: (0, 0)),
                  pl.BlockSpec((1, d, tn), lambda i, j: (i, 0, j)),
                  pl.BlockSpec((1, 1, tn), lambda i, j: (i, 0, j))],
        out_specs=pl.BlockSpec((1, r, tn), lambda i, j: (i, 0, j)),
        out_shape=jax.ShapeDtypeStruct((2 * DEPTH, r, n), F32),
        compiler_params=_cparams(("arbitrary", "arbitrary")),
        name="adaln",
    )(c_rows, w, b)


def _modulate_kernel(x_ref, sc_ref, sh_ref, o_ref):
    o_ref[...] = (x_ref[...] * (1.0 + sc_ref[...]) + sh_ref[...]).astype(o_ref.dtype)


def modulate(x, scale, shift, out_dtype):
    t, d = x.shape
    tm = min(t, 1024)
    return pl.pallas_call(
        _modulate_kernel,
        grid=(t // tm,),
        in_specs=[pl.BlockSpec((tm, d), lambda i: (i, 0)), _row_spec(scale, tm), _row_spec(shift, tm)],
        out_specs=pl.BlockSpec((tm, d), lambda i: (i, 0)),
        out_shape=jax.ShapeDtypeStruct((t, d), out_dtype),
        compiler_params=_cparams(("arbitrary",)),
        name="modulate",
    )(x, scale, shift)


def _mm_kernel(a_ref, w_ref, o_ref, *, dot, split_cols):
    res = dot(a_ref[...], w_ref[...]).astype(o_ref.dtype)
    if split_cols:
        for cb in range(o_ref.shape[0]):
            o_ref[cb] = res[:, cb * LANES:(cb + 1) * LANES]
    else:
        o_ref[...] = res


def matmul(a, w, dot, tn=1024, split_cols=False):
    t, k = a.shape
    n = w.shape[1]
    tm = min(t, 1024)
    if split_cols:
        out_spec = pl.BlockSpec((tn // LANES, tm, LANES), lambda i, j: (j, i, 0))
        out_shape = jax.ShapeDtypeStruct((n // LANES, t, LANES), F32)
    else:
        out_spec = pl.BlockSpec((tm, tn), lambda i, j: (i, j))
        out_shape = jax.ShapeDtypeStruct((t, n), F32)
    return pl.pallas_call(
        functools.partial(_mm_kernel, dot=dot, split_cols=split_cols),
        grid=(t // tm, n // tn),
        in_specs=[pl.BlockSpec((tm, k), lambda i, j: (i, 0)),
                  pl.BlockSpec((k, tn), lambda i, j: (0, j))],
        out_specs=out_spec,
        out_shape=out_shape,
        compiler_params=_cparams(("arbitrary", "arbitrary")),
        name="matmul",
    )(a, w)


def _ffn_up_kernel(*refs, dot, gated):
    if gated:
        a_ref, wg_ref, wu_ref, gates_ref, o_ref = refs
    else:
        a_ref, wg_ref, wu_ref, o_ref = refs
    a = a_ref[...]
    g = dot(a, wg_ref[0])
    u = dot(a, wu_ref[0])
    act = _silu(g) * u
    if gated:
        e = pl.program_id(1)
        gt = gates_ref[...]
        lane = lax.broadcasted_iota(jnp.int32, gt.shape, 1)
        act = act * jnp.sum(jnp.where(lane == e, gt, 0.0), axis=1, keepdims=True)
    o_ref[...] = act.astype(o_ref.dtype)


def ffn_up(a, w_up, dot, out_dtype, tm, tn, gates=None):
    t, d = a.shape
    e_n, _, f2 = w_up.shape
    f = f2 // 2
    nf = f // tn
    in_specs = [pl.BlockSpec((tm, d), lambda i, e, j: (i, 0)),
                pl.BlockSpec((1, d, tn), lambda i, e, j: (e, 0, j)),
                pl.BlockSpec((1, d, tn), lambda i, e, j: (e, 0, j + nf))]
    args = [a, w_up, w_up]
    if gates is not None:
        in_specs.append(pl.BlockSpec((tm, LANES), lambda i, e, j: (i, 0)))
        args.append(gates)
    return pl.pallas_call(
        functools.partial(_ffn_up_kernel, dot=dot, gated=gates is not None),
        grid=(t // tm, e_n, nf),
        in_specs=in_specs,
        out_specs=pl.BlockSpec((tm, tn), lambda i, e, j: (i, e * nf + j)),
        out_shape=jax.ShapeDtypeStruct((t, e_n * f), out_dtype),
        compiler_params=_cparams(("arbitrary", "arbitrary", "arbitrary")),
        name="ffn_up",
    )(*args)


def _mm_ln_kernel(*refs, n_a, nk, dot, glu, has_next):
    a_refs = refs[:n_a]
    w_refs = refs[n_a:2 * n_a]
    pos = 2 * n_a
    x_ref, gate_ref, g_ref, b_ref = refs[pos:pos + 4]
    pos += 4
    if has_next:
        sc_ref, sh_ref = refs[pos:pos + 2]
        pos += 2
    xo_ref = refs[pos]
    pos += 1
    if has_next:
        ho_ref = refs[pos]
        pos += 1
    acc_ref = refs[pos] if nk > 1 else None

    tm = x_ref.shape[0]
    sub = min(tm, max(LANES, min(LN_SUBROWS, tm // 2)))

    def load_a(a_ref, rows):
        if len(a_ref.shape) == 3:
            return jnp.concatenate([a_ref[cb, rows, :] for cb in range(a_ref.shape[0])], axis=1)
        return a_ref[rows, :]

    def product(rows):
        part = dot(load_a(a_refs[0], rows), w_refs[0][...])
        for a_ref, w_ref in zip(a_refs[1:], w_refs[1:]):
            part = part + dot(load_a(a_ref, rows), w_ref[...])
        return part

    def per_row(ref, rows):
        return ref[...] if ref.shape[0] == 1 else ref[rows, :]

    def epilogue(acc, rows):
        if glu:
            d = acc.shape[1] // 2
            out = acc[:, :d] * jax.nn.sigmoid(acc[:, d:])
        else:
            out = acc
        r = ALPHA * x_ref[rows, :] + (1.0 + per_row(gate_ref, rows)) * out
        mu = jnp.mean(r, axis=-1, keepdims=True)
        dev = r - mu
        var = jnp.mean(dev * dev, axis=-1, keepdims=True)
        xn = dev * lax.rsqrt(var + LN_EPS) * g_ref[...] + b_ref[...]
        xo_ref[rows, :] = xn
        if has_next:
            ho_ref[rows, :] = (xn * (1.0 + per_row(sc_ref, rows)) + per_row(sh_ref, rows)).astype(ho_ref.dtype)

    blocks = [slice(r0, r0 + sub) for r0 in range(0, tm, sub)]
    if nk == 1:
        for rows in blocks:
            epilogue(product(rows), rows)
    else:
        k = pl.program_id(1)
        part = product(slice(0, tm))

        @pl.when(k == 0)
        def _():
            acc_ref[...] = part

        @pl.when(k > 0)
        def _():
            acc_ref[...] += part

        @pl.when(k == nk - 1)
        def _():
            for rows in blocks:
                epilogue(acc_ref[rows, :], rows)


def matmul_ln(a_list, w, x, gate, ln_g, ln_b, dot, tm, tk=None, glu=False, nxt=None, next_dtype=None):
    t, d = x.shape
    n = w.shape[1]
    n_a = len(a_list)
    if n_a > 1:
        nk = 1
        in_specs = [pl.BlockSpec((tm, a.shape[1]), lambda i, k: (i, 0)) for a in a_list]
        off = 0
        for a in a_list:
            ka = a.shape[1]
            assert off % ka == 0
            in_specs.append(pl.BlockSpec((ka, n), functools.partial(lambda i, k, o: (o, 0), o=off // ka),
                                         pipeline_mode=pl.Buffered(1)))
            off += ka
        w_args = [w] * n_a
    else:
        a0 = a_list[0]
        split_cols = a0.ndim == 3
        ktot = a0.shape[0] * LANES if split_cols else a0.shape[1]
        tk = ktot if tk is None else tk
        nk = ktot // tk
        if split_cols:
            a_spec = pl.BlockSpec((tk // LANES, tm, LANES), lambda i, k: (k, i, 0))
        else:
            a_spec = pl.BlockSpec((tm, tk), lambda i, k: (i, k))
        w_mode = dict(pipeline_mode=pl.Buffered(1)) if nk == 1 else {}
        in_specs = [a_spec, pl.BlockSpec((tk, n), lambda i, k: (k, 0), **w_mode)]
        w_args = [w]
    in_specs += [pl.BlockSpec((tm, d), lambda i, k: (i, 0)), _row_spec(gate, tm),
                 pl.BlockSpec((1, d), lambda i, k: (0, 0)), pl.BlockSpec((1, d), lambda i, k: (0, 0))]
    args = list(a_list) + w_args + [x, gate, ln_g.reshape(1, d), ln_b.reshape(1, d)]
    out_specs = [pl.BlockSpec((tm, d), lambda i, k: (i, 0))]
    out_shape = [jax.ShapeDtypeStruct((t, d), F32)]
    if nxt is not None:
        in_specs += [_row_spec(nxt[0], tm), _row_spec(nxt[1], tm)]
        args += [nxt[0], nxt[1]]
        out_specs.append(pl.BlockSpec((tm, d), lambda i, k: (i, 0)))
        out_shape.append(jax.ShapeDtypeStruct((t, d), next_dtype))
    scratch = [pltpu.VMEM((tm, n), F32)] if nk > 1 else []
    res = pl.pallas_call(
        functools.partial(_mm_ln_kernel, n_a=n_a, nk=nk, dot=dot, glu=glu, has_next=nxt is not None),
        grid=(t // tm, nk),
        in_specs=in_specs,
        out_specs=out_specs,
        out_shape=out_shape,
        scratch_shapes=scratch,
        compiler_params=_cparams(("arbitrary", "arbitrary")),
        name="matmul_ln",
    )(*args)
    return res if nxt is not None else (res[0], None)


def _softmax_pv(scores, values, dot):
    m = scores[0].max(axis=-1, keepdims=True)
    for s in scores[1:]:
        m = jnp.maximum(m, s.max(axis=-1, keepdims=True))
    num = None
    den = None
    for s, v in zip(scores, values):
        p = jnp.exp(s - m)
        l = p.sum(axis=-1, keepdims=True)
        o = dot(p, v)
        num = o if num is None else num + o
        den = l if den is None else den + l
    return num / den


def _attn_prompt_kernel(q_ref, kp_ref, kc_ref, vp_ref, vc_ref, bias_ref, o_ref, *, dot):
    i = pl.program_id(0)
    lo = lax.broadcasted_iota(jnp.int32, (1, LANES), 1) < A_HEAD_DIM
    q = q_ref[...] * (A_HEAD_DIM ** -0.5)
    k = jnp.concatenate([kp_ref[...], kc_ref[...]], axis=0)
    v = jnp.concatenate([vp_ref[...], vc_ref[...]], axis=0)
    kidx = lax.broadcasted_iota(jnp.int32, (1, ATT_KSUB), 1)
    for sub in range(ATT_QBLOCK // ATT_QSUB):
        r0 = sub * ATT_QSUB
        qs = q[r0:r0 + ATT_QSUB]
        ks = k[r0:r0 + ATT_KSUB]
        vs = v[r0:r0 + ATT_KSUB]
        before = jnp.where(jnp.logical_and(i == 0, kidx + r0 < ATT_QBLOCK), NEG_INF, 0.0)
        outs = []
        for hh in range(2):
            qm = jnp.where(lo if hh == 0 else jnp.logical_not(lo), qs, 0.0)
            s = dot(qm, ks, nt=True) + bias_ref[0, hh] + before
            outs.append(_softmax_pv([s], [vs], dot))
        o_ref[r0:r0 + ATT_QSUB, :] = jnp.where(lo, outs[0], outs[1]).astype(o_ref.dtype)


def _band_table_kernel(base_ref, allow_ref, o_ref):
    lq, lk = o_ref.shape[1], o_ref.shape[2]
    wide = jnp.broadcast_to(base_ref[0], (lq, base_ref.shape[2]))
    toep = pltpu.roll(wide, 0, 1, stride=1, stride_axis=0)
    o_ref[0] = jnp.where(allow_ref[...] > 0.0, toep[:, :lk], NEG_INF)


def _band_tables(rel_bias, lq, lk, q_pos0, k_pos0):
    n_heads = rel_bias.shape[0]
    width = pl.next_power_of_2(lq + lk)
    c = np.arange(width)
    m = np.where(c < lk, c, c - width)
    idx = np.clip((q_pos0 - k_pos0) - m, -REL_CLIP, REL_CLIP) + REL_CLIP
    base = rel_bias[:, idx].astype(F32).reshape(n_heads, 1, width)
    q_pos = q_pos0 + np.arange(lq)
    k_pos = k_pos0 + np.arange(lk)
    qc = q_pos[:, None] // CHUNK
    kc = k_pos[None, :] // CHUNK
    allowed = ((kc <= qc) & (kc >= qc - A_PAST_CHUNKS) & (k_pos[None, :] >= 0)).astype(np.float32)
    return pl.pallas_call(
        _band_table_kernel,
        grid=(n_heads,),
        in_specs=[pl.BlockSpec((1, 1, width), lambda h: (h, 0, 0)),
                  pl.BlockSpec((lq, lk), lambda h: (0, 0))],
        out_specs=pl.BlockSpec((1, lq, lk), lambda h: (h, 0, 0)),
        out_shape=jax.ShapeDtypeStruct((n_heads, lq, lk), F32),
        compiler_params=_cparams(("arbitrary",)),
        name="band_table",
    )(base, jnp.asarray(allowed))


def attention_prompt(proj, rel_bias, dot, out_dtype):
    t = proj.shape[0]
    nb = t // ATT_QBLOCK
    npair = A_HEADS // 2
    tab = _band_tables(rel_bias, ATT_QSUB, ATT_KSUB, A_WINDOW, 0).reshape(npair, 2, ATT_QSUB, ATT_KSUB)
    blk = (ATT_QBLOCK, LANES)
    return pl.pallas_call(
        functools.partial(_attn_prompt_kernel, dot=dot),
        grid=(nb, npair),
        in_specs=[pl.BlockSpec(blk, lambda i, p: (i, p)),
                  pl.BlockSpec(blk, lambda i, p: (jnp.maximum(i - 1, 0), npair + p)),
                  pl.BlockSpec(blk, lambda i, p: (i, npair + p)),
                  pl.BlockSpec(blk, lambda i, p: (jnp.maximum(i - 1, 0), 2 * npair + p)),
                  pl.BlockSpec(blk, lambda i, p: (i, 2 * npair + p)),
                  pl.BlockSpec((1, 2, ATT_QSUB, ATT_KSUB), lambda i, p: (p, 0, 0, 0))],
        out_specs=pl.BlockSpec(blk, lambda i, p: (i, p)),
        out_shape=jax.ShapeDtypeStruct((t, A_WIDTH), out_dtype),
        compiler_params=_cparams(("arbitrary", "arbitrary")),
        name="attn_prompt",
    )(proj, proj, proj, proj, proj, tab)


def _attn_sample_kernel(q_ref, kn_ref, vn_ref, kc_ref, vc_ref, bias_ref, o_ref, *, dot, w):
    lo = lax.broadcasted_iota(jnp.int32, (1, LANES), 1) < A_HEAD_DIM
    lq = q_ref.shape[0]
    for p in range(A_HEADS // 2):
        cols = slice(p * LANES, (p + 1) * LANES)
        q = q_ref[:, cols] * (A_HEAD_DIM ** -0.5)
        kc = kc_ref[0, :, cols]
        vc = vc_ref[0, :, cols]
        kn = kn_ref[:, cols]
        vn = vn_ref[:, cols]
        outs = []
        for hh in range(2):
            qm = jnp.where(lo if hh == 0 else jnp.logical_not(lo), q, 0.0)
            bias = bias_ref[2 * p + hh]
            s_c = dot(qm, kc, nt=True) + bias[:, :w]
            s_n = dot(qm, kn, nt=True) + bias[:, w:w + lq]
            outs.append(_softmax_pv([s_c, s_n], [vc, vn], dot))
        o_ref[:, cols] = jnp.where(lo, outs[0], outs[1]).astype(o_ref.dtype)


def attention_sample(proj, k_cache, v_cache, rel_bias, bsz, lq, dot, out_dtype):
    w = k_cache.shape[1]
    lkp = ((w + lq + LANES - 1) // LANES) * LANES
    tab = _band_tables(rel_bias, lq, lkp, PAST_LEN, PAST_LEN - w)
    kc = k_cache.reshape(bsz, w, A_WIDTH)
    vc = v_cache.reshape(bsz, w, A_WIDTH)
    return pl.pallas_call(
        functools.partial(_attn_sample_kernel, dot=dot, w=w),
        grid=(bsz,),
        in_specs=[pl.BlockSpec((lq, A_WIDTH), lambda b: (b, 0)),
                  pl.BlockSpec((lq, A_WIDTH), lambda b: (b, 1)),
                  pl.BlockSpec((lq, A_WIDTH), lambda b: (b, 2)),
                  pl.BlockSpec((1, w, A_WIDTH), lambda b: (b, 0, 0)),
                  pl.BlockSpec((1, w, A_WIDTH), lambda b: (b, 0, 0)),
                  pl.BlockSpec((A_HEADS, lq, lkp), lambda b: (0, 0, 0))],
        out_specs=pl.BlockSpec((lq, A_WIDTH), lambda b: (b, 0)),
        out_shape=jax.ShapeDtypeStruct((bsz * lq, A_WIDTH), out_dtype),
        compiler_params=_cparams(("arbitrary",)),
        name="attn_sample",
    )(proj, proj, proj, kc, vc, tab)


def _ssd_kernel(z_ref, x0_ref, x1_ref, x2_ref, dt_ref, cbuf_ref, h0_ref,
                cw_ref, cb_ref, dtb_ref, alog_ref, dsk_ref, ng_ref, exp_ref,
                y_ref, cnew_ref, hout_ref, xp_ref, st_ref, *, dot, q):
    c = pl.program_id(1)
    half = B_WIDTH // B_GROUPS

    @pl.when(c == 0)
    def _():
        xp_ref[0:SUBLANES, :] = cbuf_ref[0]
        st_ref[...] = h0_ref[0]

    xp_ref[SUBLANES:SUBLANES + q, 0:512] = x0_ref[...]
    xp_ref[SUBLANES:SUBLANES + q, 512:1024] = x1_ref[...]
    xp_ref[SUBLANES:SUBLANES + q, 1024:1536] = x2_ref[...]
    base = SUBLANES - (B_CONV - 1)
    conv = cb_ref[...] + xp_ref[base:base + q, :] * cw_ref[0:1, :]
    for tap in range(1, B_CONV):
        conv = conv + xp_ref[base + tap:base + tap + q, :] * cw_ref[tap:tap + 1, :]
    tail = xp_ref[q:q + SUBLANES, :]
    xp_ref[0:SUBLANES, :] = tail
    cnew_ref[0] = tail

    xbc = _silu(conv)
    xs = xbc[:, :B_WIDTH]
    bm = xbc[:, B_WIDTH:B_WIDTH + B_GROUPS * B_STATE]
    cm = xbc[:, B_WIDTH + B_GROUPS * B_STATE:]

    lane = lax.broadcasted_iota(jnp.int32, (1, LANES), 1)
    head_ok = lane < B_HEADS
    dt = jnp.where(head_ok, jax.nn.softplus(dt_ref[...] + dtb_ref[...]), 0.0)
    a_neg = -jnp.exp(alog_ref[...])
    row = lax.broadcasted_iota(jnp.int32, (q, q), 0)
    col = lax.broadcasted_iota(jnp.int32, (q, q), 1)
    tril = row >= col
    acs = _dot_sel(tril.astype(BF16), dt * a_neg)
    acs_t = acs.T
    eacs = jnp.exp(acs)
    to_end = jnp.exp(acs[q - 1:q, :] - acs)
    wide = _dot_rsel(jnp.concatenate([dt, eacs, to_end], axis=0), exp_ref[...])
    dt_w = wide[0:q]
    eacs_w = wide[q:2 * q]
    toend_w = wide[2 * q:3 * q]
    xdt = xs * dt_w
    xend = xdt * toend_w
    lo = lane < B_HEAD_DIM

    y_parts = []
    for g in range(B_GROUPS):
        bg = bm[:, g * B_STATE:(g + 1) * B_STATE]
        cg = cm[:, g * B_STATE:(g + 1) * B_STATE]
        gmat = dot(cg, bg, nt=True)
        y_off = dot(cg, st_ref[g]) * eacs_w[:, g * half:(g + 1) * half]
        for pr in range(half // LANES):
            cols = slice(g * half + pr * LANES, g * half + (pr + 1) * LANES)
            x_pair = xdt[:, cols]
            outs = []
            for hh in range(2):
                h = (g * half + pr * LANES) // B_HEAD_DIM + hh
                diff = acs[:, h:h + 1] - acs_t[h:h + 1, :]
                decay = jnp.exp(jnp.where(tril, diff, -jnp.inf))
                outs.append(dot(gmat * decay, x_pair))
            y_parts.append(jnp.where(lo, outs[0], outs[1]) + y_off[:, pr * LANES:(pr + 1) * LANES])
        st_new = dot(bg.T, xend[:, g * half:(g + 1) * half])
        st_ref[g] = st_ref[g] * eacs_w[q - 1:q, g * half:(g + 1) * half] + st_new
    hout_ref[0] = st_ref[...]

    y = jnp.concatenate(y_parts, axis=1) + dsk_ref[...] * xs
    y = y * _silu(z_ref[...])
    y = y * lax.rsqrt(jnp.mean(y * y, axis=-1, keepdims=True) + RMS_EPS) * ng_ref[...]
    y_ref[...] = y.astype(y_ref.dtype)


def ssd_mixer(proj, conv_buf, h0, conv_w, conv_b, dt_bias, a_log, d_skip, norm_g, bsz, seq, q, dot, out_dtype):
    t = bsz * seq
    nc = seq // q
    half = B_WIDTH // B_GROUPS
    cbuf = jnp.pad(conv_buf, ((0, 0), (SUBLANES - (B_CONV - 1), 0), (0, 0)))
    h0_t = jnp.transpose(h0.reshape(bsz, B_GROUPS, half, B_STATE), (0, 1, 3, 2))
    pad = lambda v: jnp.pad(v.reshape(1, -1), ((0, 0), (0, LANES - v.shape[-1])))
    expand = jnp.asarray(np.repeat(np.eye(LANES, B_HEADS, dtype=np.float32).T, B_HEAD_DIM, axis=0).T, BF16)
    dsk_w = jnp.repeat(d_skip, B_HEAD_DIM).reshape(1, B_WIDTH)
    zb, xb, db = 3 * A_WIDTH // B_WIDTH, (3 * A_WIDTH + B_WIDTH) // 512, (3 * A_WIDTH + B_WIDTH + B_CONV_DIM) // LANES
    const = lambda shp: pl.BlockSpec(shp, lambda b, c: (0,) * len(shp))
    y, cnew, hout = pl.pallas_call(
        functools.partial(_ssd_kernel, dot=dot, q=q),
        grid=(bsz, nc),
        in_specs=[pl.BlockSpec((q, B_WIDTH), lambda b, c: (b * nc + c, zb)),
                  pl.BlockSpec((q, 512), lambda b, c: (b * nc + c, xb)),
                  pl.BlockSpec((q, 512), lambda b, c: (b * nc + c, xb + 1)),
                  pl.BlockSpec((q, 512), lambda b, c: (b * nc + c, xb + 2)),
                  pl.BlockSpec((q, LANES), lambda b, c: (b * nc + c, db)),
                  pl.BlockSpec((1, SUBLANES, B_CONV_DIM), lambda b, c: (b, 0, 0)),
                  pl.BlockSpec((1, B_GROUPS, B_STATE, half), lambda b, c: (b, 0, 0, 0)),
                  const((B_CONV, B_CONV_DIM)), const((1, B_CONV_DIM)), const((1, LANES)), const((1, LANES)),
                  const((1, B_WIDTH)), const((1, B_WIDTH)), const((LANES, B_WIDTH))],
        out_specs=[pl.BlockSpec((q, B_WIDTH), lambda b, c: (b * nc + c, 0)),
                   pl.BlockSpec((1, SUBLANES, B_CONV_DIM), lambda b, c: (b, 0, 0)),
                   pl.BlockSpec((1, B_GROUPS, B_STATE, half), lambda b, c: (b, 0, 0, 0))],
        out_shape=[jax.ShapeDtypeStruct((t, B_WIDTH), out_dtype),
                   jax.ShapeDtypeStruct((bsz, SUBLANES, B_CONV_DIM), F32),
                   jax.ShapeDtypeStruct((bsz, B_GROUPS, B_STATE, half), F32)],
        scratch_shapes=[pltpu.VMEM((q + SUBLANES, B_CONV_DIM), F32),
                        pltpu.VMEM((B_GROUPS, B_STATE, half), F32)],
        compiler_params=_cparams(("arbitrary", "arbitrary")),
        name="ssd",
    )(proj, proj, proj, proj, proj, cbuf, h0_t,
      conv_w, conv_b.reshape(1, -1), pad(dt_bias), pad(a_log), dsk_w, norm_g.reshape(1, -1), expand)
    conv_new = cnew[:, SUBLANES - (B_CONV - 1):, :]
    h_final = jnp.transpose(hout, (0, 1, 3, 2)).reshape(bsz, B_HEADS, B_HEAD_DIM, B_STATE)
    return y, conv_new, h_final


def _s5_prep_kernel(lre_ref, lim_ref, ls_ref, bre_ref, bim_ref, are_ref, aim_ref, bbre_ref, bbim_ref):
    lre = lre_ref[...]
    lim = lim_ref[...]
    step = jnp.exp(ls_ref[...])
    mag = jnp.exp(lre * step)
    ang = lim * step
    ab_re = mag * jnp.cos(ang)
    ab_im = mag * jnp.sin(ang)
    den = lre * lre + lim * lim
    f_re = ((ab_re - 1.0) * lre + ab_im * lim) / den
    f_im = (ab_im * lre - (ab_re - 1.0) * lim) / den
    br = bre_ref[...]
    bi = bim_ref[...]
    are_ref[...] = ab_re
    aim_ref[...] = ab_im
    bbre_ref[...] = f_re * br - f_im * bi
    bbim_ref[...] = f_re * bi + f_im * br


def _cmul(ar, ai, br, bi):
    return ar * br - ai * bi, ar * bi + ai * br


def _gelu_tanh(x):
    return 0.5 * x * (1.0 + jnp.tanh(math.sqrt(2.0 / math.pi) * (x + 0.044715 * (x * x * x))))


def _s5_kernel(u_ref, bb_ref, ccre_ref, ccim_ref, a_ref, d_ref, h0_ref, y_ref, hout_ref,
               s_ref, pw_ref, pwb_ref, carry_ref, *, dot, seg, chain):
    t = pl.program_id(2)
    w = S5_JW
    unroll = min(seg, 4)
    a_re = a_ref[0, 0:1, :]
    a_im = a_ref[0, 1:2, :]

    @pl.when(t == 0)
    def _():
        carry_ref[0:h0_ref.shape[2], :] = h0_ref[0, 0]
        pw_ref[0:1, 0:w] = a_re
        pw_ref[0:1, w:2 * w] = a_im

        def power(i, carry):
            pr, pi = _cmul(pw_ref[pl.ds(i - 1, 1), 0:w], pw_ref[pl.ds(i - 1, 1), w:2 * w], a_re, a_im)
            pw_ref[pl.ds(i, 1), 0:w] = pr
            pw_ref[pl.ds(i, 1), w:2 * w] = pi
            return carry

        lax.fori_loop(1, seg, power, 0)

        def spread(i, carry):
            rows = pl.ds(pl.multiple_of(i * SUBLANES, SUBLANES), SUBLANES)
            pwb_ref[rows, :] = jnp.broadcast_to(pw_ref[pl.ds(i, 1), :], (SUBLANES, 2 * w))
            return carry

        lax.fori_loop(0, seg, spread, 0)

    ncb = u_ref.shape[0]
    up = jnp.concatenate(
        [jnp.concatenate([u_ref[cb, pl.ds(i, SUBLANES, stride=seg), :] for i in range(seg)], axis=0)
         for cb in range(ncb)], axis=1)
    s_ref[...] = dot(up, bb_ref[0])

    are_b = jnp.broadcast_to(a_re, (SUBLANES, w))
    aim_b = jnp.broadcast_to(a_im, (SUBLANES, w))

    def scan(i, st):
        sre, sim = st
        rows = pl.ds(pl.multiple_of(i * SUBLANES, SUBLANES), SUBLANES)
        pr, pi = _cmul(are_b, aim_b, sre, sim)
        nre = pr + s_ref[rows, 0:w]
        nim = pi + s_ref[rows, w:2 * w]
        s_ref[rows, 0:w] = nre
        s_ref[rows, w:2 * w] = nim
        return nre, nim

    zero = jnp.zeros((SUBLANES, w), F32)
    end_re, end_im = lax.fori_loop(0, seg, scan, (zero, zero), unroll=unroll)

    sg_re = pw_ref[seg - 1:seg, 0:w]
    sg_im = pw_ref[seg - 1:seg, w:2 * w]
    if chain:
        c_re = carry_ref[0:1, 0:w]
        c_im = carry_ref[0:1, w:2 * w]
        rows_re, rows_im = [], []
        for s in range(SUBLANES):
            rows_re.append(c_re)
            rows_im.append(c_im)
            pr, pi = _cmul(sg_re, sg_im, c_re, c_im)
            c_re = pr + end_re[s:s + 1, :]
            c_im = pi + end_im[s:s + 1, :]
        carry_ref[0:1, 0:w] = c_re
        carry_ref[0:1, w:2 * w] = c_im
        hout_ref[0, 0] = carry_ref[0:1, :]
        in_re = jnp.concatenate(rows_re, axis=0)
        in_im = jnp.concatenate(rows_im, axis=0)
    else:
        in_re = carry_ref[:, 0:w]
        in_im = carry_ref[:, w:2 * w]
        pr, pi = _cmul(sg_re, sg_im, in_re, in_im)
        carry_ref[:, 0:w] = pr + end_re
        carry_ref[:, w:2 * w] = pi + end_im
        hout_ref[0, 0] = carry_ref[...]

    def fixup(i, carry):
        rows = pl.ds(pl.multiple_of(i * SUBLANES, SUBLANES), SUBLANES)
        pr, pi = _cmul(pwb_ref[rows, 0:w], pwb_ref[rows, w:2 * w], in_re, in_im)
        s_ref[rows, 0:w] = s_ref[rows, 0:w] + pr
        s_ref[rows, w:2 * w] = s_ref[rows, w:2 * w] + pi
        return carry

    lax.fori_loop(0, seg, fixup, 0, unroll=unroll)

    y = dot(s_ref[:, 0:w], ccre_ref[0]) - dot(s_ref[:, w:2 * w], ccim_ref[0]) + d_ref[...] * up
    y = _gelu_tanh(y)
    for cb in range(ncb):
        for i in range(seg):
            y_ref[cb, pl.ds(i, SUBLANES, stride=seg), :] = (
                y[i * SUBLANES:(i + 1) * SUBLANES, cb * LANES:(cb + 1) * LANES])


def _block_diag(m):
    nj, j, r, c = m.shape
    eye = jnp.eye(j, dtype=m.dtype)
    return (m[:, :, :, None, :] * eye[None, :, None, :, None]).reshape(nj, j * r, j * c)


def s5_mixer(u, h0_re, h0_im, lam_re, lam_im, log_step, b_re, b_im, c_re, c_im, d_skip,
             bsz, seq, tm, dot, w_dtype, chain):
    t = bsz * seq
    g, p, ch = C_GROUPS, C_STATE, C_GROUP_CH
    rep = lambda v: jnp.repeat(v, ch, axis=-1)
    ls = jnp.broadcast_to(log_step[:, None], (g, p))
    pc = pl.BlockSpec((g, p * ch), lambda: (0, 0))
    shp = jax.ShapeDtypeStruct((g, p * ch), F32)
    a_re_x, a_im_x, bb_re, bb_im = pl.pallas_call(
        _s5_prep_kernel, in_specs=[pc] * 5, out_specs=[pc] * 4, out_shape=[shp] * 4, name="s5_prep",
    )(rep(lam_re), rep(lam_im), rep(ls), b_re.reshape(g, p * ch), b_im.reshape(g, p * ch))
    a_re = a_re_x[:, ::ch]
    a_im = a_im_x[:, ::ch]
    to_blk = lambda m: jnp.transpose(m.reshape(S5_NJ, S5_JBLK, p, ch), (0, 1, 3, 2))
    bb = jnp.concatenate([_block_diag(to_blk(bb_re)), _block_diag(to_blk(bb_im))], axis=-1).astype(w_dtype)
    cblk = lambda m: jnp.transpose(m.reshape(S5_NJ, S5_JBLK, ch, p), (0, 1, 3, 2))
    cc_re = _block_diag(cblk(c_re)).astype(w_dtype)
    cc_im = _block_diag(cblk(c_im)).astype(w_dtype)
    a_rows = jnp.stack([a_re.reshape(S5_NJ, S5_JW), a_im.reshape(S5_NJ, S5_JW)], axis=1)
    h0 = jnp.concatenate([h0_re.reshape(bsz, S5_NJ, S5_JW), h0_im.reshape(bsz, S5_NJ, S5_JW)], axis=-1)
    if chain:
        nb, srows, rows_per_b = bsz, 1, seq
        h0 = h0[:, :, None, :]
    else:
        assert bsz == SUBLANES and tm == bsz * seq
        nb, srows, rows_per_b = 1, SUBLANES, bsz * seq
        h0 = jnp.transpose(h0, (1, 0, 2))[None]
    nt = rows_per_b // tm
    seg = tm // SUBLANES
    uw = S5_JBLK * ch
    ncb = uw // LANES
    state_spec = pl.BlockSpec((1, 1, srows, 2 * S5_JW), lambda b, j, i: (b, j, 0, 0))
    y, hout = pl.pallas_call(
        functools.partial(_s5_kernel, dot=dot, seg=seg, chain=chain),
        grid=(nb, S5_NJ, nt),
        in_specs=[pl.BlockSpec((ncb, tm, LANES), lambda b, j, i: (j, b * nt + i, 0)),
                  pl.BlockSpec((1, uw, 2 * S5_JW), lambda b, j, i: (j, 0, 0)),
                  pl.BlockSpec((1, S5_JW, uw), lambda b, j, i: (j, 0, 0)),
                  pl.BlockSpec((1, S5_JW, uw), lambda b, j, i: (j, 0, 0)),
                  pl.BlockSpec((1, 2, S5_JW), lambda b, j, i: (j, 0, 0)),
                  pl.BlockSpec((1, uw), lambda b, j, i: (0, j)),
                  state_spec],
        out_specs=[pl.BlockSpec((ncb, tm, LANES), lambda b, j, i: (j, b * nt + i, 0)), state_spec],
        out_shape=[jax.ShapeDtypeStruct((g * ch // LANES, t, LANES), F32),
                   jax.ShapeDtypeStruct((nb, S5_NJ, srows, 2 * S5_JW), F32)],
        scratch_shapes=[pltpu.VMEM((tm, 2 * S5_JW), F32),
                        pltpu.VMEM((seg, 2 * S5_JW), F32),
                        pltpu.VMEM((tm, 2 * S5_JW), F32),
                        pltpu.VMEM((SUBLANES, 2 * S5_JW), F32)],
        compiler_params=_cparams(("arbitrary", "arbitrary", "arbitrary")),
        name="s5_scan",
    )(u, bb, cc_re, cc_im, a_rows, d_skip.reshape(1, -1), h0)
    hout = hout[:, :, 0, :] if chain else jnp.transpose(hout[0], (1, 0, 2))
    s_re = hout[:, :, :S5_JW].reshape(bsz, g, p)
    s_im = hout[:, :, S5_JW:].reshape(bsz, g, p)
    return y, s_re, s_im


def _router_kernel(x_ref, sc_ref, sh_ref, w_ref, b_ref, o_ref, r_ref, cnt_ref, run_ref):
    h = x_ref[...] * (1.0 + sc_ref[...]) + sh_ref[...]
    logits = _dot3(h, w_ref[...]) + b_ref[...]
    lane = lax.broadcasted_iota(jnp.int32, logits.shape, 1).astype(F32)
    logits = jnp.where(lane < N_EXPERTS, logits, -jnp.inf)
    m1 = logits.max(axis=-1, keepdims=True)
    i1 = jnp.min(jnp.where(logits == m1, lane, float(LANES)), axis=-1, keepdims=True)
    rest = jnp.where(lane == i1, -jnp.inf, logits)
    m2 = rest.max(axis=-1, keepdims=True)
    i2 = jnp.min(jnp.where(rest == m2, lane, float(LANES)), axis=-1, keepdims=True)
    e2 = jnp.exp(m2 - m1)
    w1 = 1.0 / (1.0 + e2)
    w2 = e2 / (1.0 + e2)
    o_ref[...] = jnp.where(lane == i1, w1, 0.0) + jnp.where(lane == i2, w2, 0.0)

    step = pl.program_id(0)

    @pl.when(step == 0)
    def _():
        run_ref[...] = jnp.zeros_like(run_ref)

    tm = logits.shape[0]
    sel = jnp.where(jnp.logical_or(lane == i1, lane == i2), 1.0, 0.0)
    row = lax.broadcasted_iota(jnp.int32, (tm, tm), 0)
    col = lax.broadcasted_iota(jnp.int32, (tm, tm), 1)
    before = (row > col).astype(BF16)
    rank = _mxu(before, sel.astype(BF16)) + run_ref[0:1, :]
    pos1 = jnp.sum(jnp.where(lane == i1, rank, 0.0), axis=-1, keepdims=True)
    pos2 = jnp.sum(jnp.where(lane == i2, rank, 0.0), axis=-1, keepdims=True)
    run_ref[0:1, :] = run_ref[0:1, :] + jnp.sum(sel, axis=0, keepdims=True)
    cnt_ref[...] = run_ref[0:1, :]
    rec = jnp.zeros_like(logits)
    for k, val in enumerate((i1, i2, pos1, pos2, w1, w2)):
        rec = jnp.where(lane == float(k), val, rec)
    r_ref[...] = rec


ROUTE_E1, ROUTE_E2, ROUTE_POS1, ROUTE_POS2, ROUTE_W1, ROUTE_W2 = range(6)


def router(x, scale, shift, router_w, router_b):
    t, d = x.shape
    tm = min(t, 512)
    w = jnp.pad(router_w, ((0, 0), (0, LANES - N_EXPERTS)))
    b = jnp.pad(router_b.reshape(1, -1), ((0, 0), (0, LANES - N_EXPERTS)))
    return pl.pallas_call(
        _router_kernel,
        grid=(t // tm,),
        in_specs=[pl.BlockSpec((tm, d), lambda i: (i, 0)), _row_spec(scale, tm), _row_spec(shift, tm),
                  pl.BlockSpec((d, LANES), lambda i: (0, 0)), pl.BlockSpec((1, LANES), lambda i: (0, 0))],
        out_specs=[pl.BlockSpec((tm, LANES), lambda i: (i, 0)), pl.BlockSpec((tm, LANES), lambda i: (i, 0)),
                   pl.BlockSpec((1, LANES), lambda i: (0, 0))],
        out_shape=[jax.ShapeDtypeStruct((t, LANES), F32), jax.ShapeDtypeStruct((t, LANES), F32),
                   jax.ShapeDtypeStruct((1, LANES), F32)],
        scratch_shapes=[pltpu.VMEM((SUBLANES, LANES), F32)],
        compiler_params=_cparams(("arbitrary",)),
        name="router",
    )(x, scale, shift, w, b)


MOE_TILE = 256


def _dispatch_kernel(fill_ref, d1_ref, d2_ref, h_ref, xs_ref, zero_ref, sem, zsem):
    tm = h_ref.shape[0]

    @pl.when(pl.program_id(0) == 0)
    def _():
        zero_ref[...] = jnp.zeros_like(zero_ref)

        def fill_copy(k):
            row0 = pl.multiple_of(jnp.maximum(fill_ref[k], 0), MOE_TILE)
            return pltpu.make_async_copy(zero_ref, xs_ref.at[pl.ds(row0, MOE_TILE)], zsem)

        for k in range(fill_ref.shape[0]):
            @pl.when(fill_ref[k] >= 0)
            def _():
                fill_copy(k).start()

        for k in range(fill_ref.shape[0]):
            @pl.when(fill_ref[k] >= 0)
            def _():
                fill_copy(k).wait()

    def issue(t, carry):
        src = h_ref.at[pl.ds(t, 1)]
        pltpu.make_async_copy(src, xs_ref.at[pl.ds(d1_ref[0, 0, t], 1)], sem).start()
        pltpu.make_async_copy(src, xs_ref.at[pl.ds(d2_ref[0, 0, t], 1)], sem).start()
        return carry

    lax.fori_loop(0, tm, issue, 0)
    pltpu.make_async_copy(xs_ref.at[pl.ds(0, 2 * tm)], xs_ref.at[pl.ds(0, 2 * tm)], sem).wait()


def moe_dispatch(h, dest1, dest2, fill_rows, n_rows):
    t, d = h.shape
    tm = 512
    idx_spec = pl.BlockSpec((1, 1, tm), lambda i, fr: (i, 0, 0), memory_space=pltpu.SMEM)
    grid_spec = pltpu.PrefetchScalarGridSpec(
        num_scalar_prefetch=1,
        grid=(t // tm,),
        in_specs=[idx_spec, idx_spec, pl.BlockSpec((tm, d), lambda i, fr: (i, 0))],
        out_specs=pl.BlockSpec(memory_space=pl.ANY),
        scratch_shapes=[pltpu.VMEM((MOE_TILE, d), h.dtype), pltpu.SemaphoreType.DMA(()),
                        pltpu.SemaphoreType.DMA(())])
    return pl.pallas_call(
        _dispatch_kernel,
        grid_spec=grid_spec,
        out_shape=jax.ShapeDtypeStruct((n_rows, d), h.dtype),
        compiler_params=_cparams(("arbitrary",)),
        name="moe_dispatch",
    )(fill_rows, dest1.reshape(t // tm, 1, tm), dest2.reshape(t // tm, 1, tm), h)


def _expert_up_kernel(te_ref, nv_ref, x_ref, wg_ref, wu_ref, o_ref, *, dot):
    del te_ref

    @pl.when(pl.program_id(1) < nv_ref[0])
    def _():
        x = x_ref[...]
        g = dot(x, wg_ref[0])
        u = dot(x, wu_ref[0])
        o_ref[...] = (_silu(g) * u).astype(o_ref.dtype)

    @pl.when(pl.program_id(1) >= nv_ref[0])
    def _():
        o_ref[...] = jnp.zeros_like(o_ref)


def expert_up(xs, w_up, tile_expert, n_valid, dot, tn):
    r, d = xs.shape
    f = w_up.shape[2] // 2
    nf = f // tn
    nt = r // MOE_TILE
    grid_spec = pltpu.PrefetchScalarGridSpec(
        num_scalar_prefetch=2,
        grid=(nf, nt),
        in_specs=[pl.BlockSpec((MOE_TILE, d), lambda j, i, te, nv: (i, 0)),
                  pl.BlockSpec((1, d, tn), lambda j, i, te, nv: (te[i], 0, j)),
                  pl.BlockSpec((1, d, tn), lambda j, i, te, nv: (te[i], 0, j + nf))],
        out_specs=pl.BlockSpec((MOE_TILE, tn), lambda j, i, te, nv: (i, j)))
    return pl.pallas_call(
        functools.partial(_expert_up_kernel, dot=dot),
        grid_spec=grid_spec,
        out_shape=jax.ShapeDtypeStruct((r, f), BF16),
        compiler_params=_cparams(("arbitrary", "arbitrary")),
        name="expert_up",
    )(tile_expert, n_valid, xs, w_up, w_up)


def _expert_down_kernel(te_ref, nv_ref, a_ref, w_ref, o_ref, *, dot):
    del te_ref

    @pl.when(pl.program_id(0) < nv_ref[0])
    def _():
        o_ref[...] = dot(a_ref[...], w_ref[0])

    @pl.when(pl.program_id(0) >= nv_ref[0])
    def _():
        o_ref[...] = jnp.zeros_like(o_ref)


def expert_down(act, w_down, tile_expert, n_valid, dot):
    r, f = act.shape
    d = w_down.shape[2]
    grid_spec = pltpu.PrefetchScalarGridSpec(
        num_scalar_prefetch=2,
        grid=(r // MOE_TILE,),
        in_specs=[pl.BlockSpec((MOE_TILE, f), lambda i, te, nv: (i, 0)),
                  pl.BlockSpec((1, f, d), lambda i, te, nv: (te[i], 0, 0))],
        out_specs=pl.BlockSpec((MOE_TILE, d), lambda i, te, nv: (i, 0)))
    return pl.pallas_call(
        functools.partial(_expert_down_kernel, dot=dot),
        grid_spec=grid_spec,
        out_shape=jax.ShapeDtypeStruct((r, d), F32),
        compiler_params=_cparams(("arbitrary",)),
        name="expert_down",
    )(tile_expert, n_valid, act, w_down)


def _combine_ln_kernel(d1_ref, d2_ref, y_ref, rec_ref, x_ref, gate_ref, g_ref, b_ref, xo_ref, buf1, buf2, sem):
    tm = x_ref.shape[0]

    def issue(t, carry):
        pltpu.make_async_copy(y_ref.at[pl.ds(d1_ref[0, 0, t], 1)], buf1.at[pl.ds(t, 1)], sem).start()
        pltpu.make_async_copy(y_ref.at[pl.ds(d2_ref[0, 0, t], 1)], buf2.at[pl.ds(t, 1)], sem).start()
        return carry

    lax.fori_loop(0, tm, issue, 0)
    pltpu.make_async_copy(y_ref.at[pl.ds(0, tm)], buf1, sem).wait()
    pltpu.make_async_copy(y_ref.at[pl.ds(0, tm)], buf2, sem).wait()

    rec = rec_ref[...]
    lane = lax.broadcasted_iota(jnp.int32, rec.shape, 1)
    w1 = jnp.sum(jnp.where(lane == ROUTE_W1, rec, 0.0), axis=-1, keepdims=True)
    w2 = jnp.sum(jnp.where(lane == ROUTE_W2, rec, 0.0), axis=-1, keepdims=True)
    out = w1 * buf1[...] + w2 * buf2[...]
    r = ALPHA * x_ref[...] + (1.0 + gate_ref[...]) * out
    mu = jnp.mean(r, axis=-1, keepdims=True)
    dev = r - mu
    var = jnp.mean(dev * dev, axis=-1, keepdims=True)
    xo_ref[...] = dev * lax.rsqrt(var + LN_EPS) * g_ref[...] + b_ref[...]


def moe_combine_ln(y_sorted, dest1, dest2, rec, x, gate, ln_g, ln_b):
    t, d = x.shape
    tm = 256
    idx_spec = pl.BlockSpec((1, 1, tm), lambda i: (i, 0, 0), memory_space=pltpu.SMEM)
    return pl.pallas_call(
        _combine_ln_kernel,
        grid=(t // tm,),
        in_specs=[idx_spec, idx_spec, pl.BlockSpec(memory_space=pl.ANY),
                  pl.BlockSpec((tm, LANES), lambda i: (i, 0)), pl.BlockSpec((tm, d), lambda i: (i, 0)),
                  _row_spec(gate, tm), pl.BlockSpec((1, d), lambda i: (0, 0)), pl.BlockSpec((1, d), lambda i: (0, 0))],
        out_specs=pl.BlockSpec((tm, d), lambda i: (i, 0)),
        out_shape=jax.ShapeDtypeStruct((t, d), F32),
        scratch_shapes=[pltpu.VMEM((tm, d), F32), pltpu.VMEM((tm, d), F32), pltpu.SemaphoreType.DMA(())],
        compiler_params=_cparams(("arbitrary",)),
        name="moe_combine_ln",
    )(dest1.reshape(t // tm, 1, tm), dest2.reshape(t // tm, 1, tm), y_sorted, rec, x, gate,
      ln_g.reshape(1, d), ln_b.reshape(1, d))


def moe_top2(h, x, rec, counts, gate, ln_g, ln_b, w_up, w_down, dot):
    t, d = h.shape
    n_tiles = (TOP_K * t) // MOE_TILE + N_EXPERTS
    n_rows = n_tiles * MOE_TILE
    cnt = counts[0, :N_EXPERTS].astype(jnp.int32)
    padded = ((cnt + MOE_TILE - 1) // MOE_TILE) * MOE_TILE
    ends = jnp.cumsum(padded)
    starts = ends - padded
    col = lambda k: rec[:, k].astype(jnp.int32)
    experts = jnp.arange(N_EXPERTS, dtype=jnp.int32)[None, :]
    start_of = lambda e: jnp.sum(jnp.where(e[:, None] == experts, starts[None, :], 0), axis=1)
    dest1 = start_of(col(ROUTE_E1)) + col(ROUTE_POS1)
    dest2 = start_of(col(ROUTE_E2)) + col(ROUTE_POS2)
    tile_start = jnp.arange(n_tiles, dtype=jnp.int32) * MOE_TILE
    tile_expert = jnp.minimum(jnp.searchsorted(ends, tile_start, side="right"), N_EXPERTS - 1).astype(jnp.int32)
    n_valid = (ends[-1:] // MOE_TILE).astype(jnp.int32)
    last_tile = jnp.where(padded > 0, ends - MOE_TILE, -1)
    spare = ends[-1] + jnp.arange(N_EXPERTS, dtype=jnp.int32) * MOE_TILE
    fill_rows = jnp.concatenate([last_tile, jnp.where(spare < n_rows, spare, -1)]).astype(jnp.int32)
    xs = moe_dispatch(h, dest1, dest2, fill_rows, n_rows)
    act = expert_up(xs, w_up, tile_expert, n_valid, dot, tn=D_FF_EXPERT // 2)
    y_sorted = expert_down(act, w_down, tile_expert, n_valid, dot)
    return moe_combine_ln(y_sorted, dest1, dest2, rec, x, gate, ln_g, ln_b)


def _trunk(x, mods, caches, wts, precise):
    bsz, seq, d = x.shape
    t = bsz * seq
    prompt = caches is None
    dot = _dot3 if precise else _dot1
    act_dtype = F32 if precise else BF16
    tm = min(t, 512)
    x0 = x.reshape(t, d)

    sh, sc, gt = mods[0][0]
    h = modulate(x0, sc, sh, act_dtype)
    proj = matmul(h, wts["w_in0"], dot)
    k_all = proj[:, A_WIDTH:2 * A_WIDTH].reshape(bsz, seq, A_HEADS, A_HEAD_DIM)
    v_all = proj[:, 2 * A_WIDTH:3 * A_WIDTH].reshape(bsz, seq, A_HEADS, A_HEAD_DIM)
    if prompt:
        att = attention_prompt(proj, wts["rel_bias"], dot, act_dtype)
        keep = min(A_WINDOW, seq)
        k_new, v_new = k_all[:, seq - keep:], v_all[:, seq - keep:]
        conv_buf = jnp.zeros((bsz, B_CONV - 1, B_CONV_DIM), F32)
        ssm_h0 = jnp.zeros((bsz, B_HEADS, B_HEAD_DIM, B_STATE), F32)
        q_len = min(seq, 256)
    else:
        att = attention_sample(proj, caches["k"], caches["v"], wts["rel_bias"], bsz, seq, dot, act_dtype)
        k_new, v_new = k_all, v_all
        conv_buf, ssm_h0 = caches["conv"], caches["ssm"]
        q_len = seq
    y_ssd, conv_new, ssm_new = ssd_mixer(proj, conv_buf, ssm_h0, wts["conv_w"], wts["conv_b"], wts["dt_bias"],
                                         wts["a_log"], wts["ssd_d"], wts["ssd_norm_g"], bsz, seq, q_len,
                                         dot, act_dtype)
    x1, h1 = matmul_ln([att, y_ssd], wts["w_out0"], x0, gt, wts["ln_g"][0, 0], wts["ln_b"][0, 0], dot, tm,
                       nxt=(mods[0][1][1], mods[0][1][0]), next_dtype=act_dtype)
    act = ffn_up(h1, wts["ffn_w_up"], dot, act_dtype, tm=min(t, 1024), tn=512)
    x2, h2 = matmul_ln([act], wts["ffn_w_down"], x1, mods[0][1][2], wts["ln_g"][0, 1], wts["ln_b"][0, 1], dot,
                       min(t, 256), tk=None if prompt else 1408, nxt=(mods[1][0][1], mods[1][0][0]), next_dtype=act_dtype)

    u = matmul(h2, wts["w_in1"], dot, split_cols=True)
    if prompt:
        s5_re0 = jnp.zeros((bsz, C_GROUPS, C_STATE), F32)
        s5_im0 = jnp.zeros((bsz, C_GROUPS, C_STATE), F32)
    else:
        s5_re0, s5_im0 = caches["s5_re"], caches["s5_im"]
    y5, s5_re, s5_im = s5_mixer(u, s5_re0, s5_im0, wts["s5_lam_re"], wts["s5_lam_im"], wts["s5_log_step"],
                                wts["s5_b_re"], wts["s5_b_im"], wts["s5_c_re"], wts["s5_c_im"], wts["s5_d"],
                                bsz, seq, 512 if prompt else bsz * seq, dot, F32 if precise else BF16,
                                chain=prompt)
    x3, h3 = matmul_ln([y5], wts["glu_w"], x2, mods[1][0][2], wts["ln_g"][1, 0], wts["ln_b"][1, 0], dot,
                       min(t, 256), tk=None if prompt else 512, glu=True,
                       nxt=(mods[1][1][1], mods[1][1][0]), next_dtype=F32)
    gates, rec, counts = router(x3, mods[1][1][1], mods[1][1][0], wts["router_w"], wts["router_b"])
    if prompt:
        x4 = moe_top2(h3, x3, rec, counts, mods[1][1][2], wts["ln_g"][1, 1], wts["ln_b"][1, 1],
                      wts["moe_w_up"], wts["moe_w_down"], dot)
    else:
        act = ffn_up(h3, wts["moe_w_up_bf16"], _dot1, BF16, tm=min(t, 2048), tn=256, gates=gates)
        w_down = wts["moe_w_down_bf16"].reshape(N_EXPERTS * D_FF_EXPERT, d)
        x4, _ = matmul_ln([act], w_down, x3, mods[1][1][2], wts["ln_g"][1, 1], wts["ln_b"][1, 1], _dot1, tm,
                          tk=2816)
    return (x4.reshape(bsz, seq, d), k_new[None], v_new[None], conv_new[None], ssm_new[None],
            s5_re[None], s5_im[None])


def kernel(x_prompt, x_sample, cache_attn_k, cache_attn_v, state_ssd_conv, state_ssd, state_s5_re, state_s5_im, c_prompt, c_sample, ada_w, ada_b, ln_g, ln_b, w_in0, w_out0, rel_bias, conv_w, conv_b, dt_bias, a_log, ssd_d, ssd_norm_g, ffn_w_up, ffn_w_down, w_in1, s5_lam_re, s5_lam_im, s5_log_step, s5_b_re, s5_b_im, s5_c_re, s5_c_im, s5_d, glu_w, router_w, router_b, moe_w_up, moe_w_down):
    d = D_MODEL
    bp, lp, _ = x_prompt.shape
    bs, ls, _ = x_sample.shape

    n_c = bp + bs
    rows = ((n_c + SUBLANES - 1) // SUBLANES) * SUBLANES
    c_all = jnp.pad(jnp.concatenate([c_prompt, c_sample], axis=0), ((0, rows - n_c), (0, 0)))
    mod = adaln_all(c_all, ada_w, ada_b).reshape(DEPTH, 2, rows, 3, d)

    def mods_for(r0, nb, per_row):
        out = []
        for layer in range(DEPTH):
            out.append([])
            for j in range(2):
                trip = []
                for part in range(3):
                    m = mod[layer, j, r0:r0 + nb, part]
                    trip.append(jnp.repeat(m, per_row, axis=0) if nb > 1 else m)
                out[-1].append(tuple(trip))
        return out

    assert bp == 1
    mods_p = mods_for(0, bp, lp)
    mods_s = mods_for(bp, bs, ls)

    in0_pad = ((0, 0), (0, IN0_PAD - IN0_WIDTH))
    shared = dict(ln_g=ln_g, ln_b=ln_b, rel_bias=rel_bias[0], conv_w=conv_w[0], conv_b=conv_b[0],
                  dt_bias=dt_bias[0], a_log=a_log[0], ssd_d=ssd_d[0], ssd_norm_g=ssd_norm_g[0],
                  s5_lam_re=s5_lam_re[0], s5_lam_im=s5_lam_im[0], s5_log_step=s5_log_step[0],
                  s5_b_re=s5_b_re[0], s5_b_im=s5_b_im[0], s5_c_re=s5_c_re[0], s5_c_im=s5_c_im[0], s5_d=s5_d[0],
                  router_w=router_w[0], router_b=router_b[0])
    big = dict(w_in0=jnp.pad(w_in0[0], in0_pad), w_out0=w_out0[0], ffn_w_up=ffn_w_up, ffn_w_down=ffn_w_down[0],
               w_in1=w_in1[0], glu_w=glu_w[0], moe_w_up=moe_w_up[0],
               moe_w_down=moe_w_down[0])
    wts_p = dict(shared, **{k: v.astype(BF16) for k, v in big.items()})
    wts_s = dict(shared, moe_w_up_bf16=wts_p["moe_w_up"], moe_w_down_bf16=wts_p["moe_w_down"], **big)

    y_p, k_p, v_p, conv_p, ssd_p, re_p, im_p = _trunk(x_prompt, mods_p, None, wts_p, precise=False)
    caches = dict(k=cache_attn_k[0], v=cache_attn_v[0], conv=state_ssd_conv[0], ssm=state_ssd[0],
                  s5_re=state_s5_re[0], s5_im=state_s5_im[0])
    y_s, k_s, v_s, conv_s, ssd_s, re_s, im_s = _trunk(x_sample, mods_s, caches, wts_s, precise=True)
    return (y_p, y_s, k_p, v_p, conv_p, ssd_p, re_p, im_p, k_s, v_s, conv_s, ssd_s, re_s, im_s)
```

```python
import functools
import math

import numpy as np
import jax
import jax.numpy as jnp
from jax import lax
from jax.experimental import pallas as pl
from jax.experimental.pallas import tpu as pltpu

F32 = jnp.float32
BF16 = jnp.bfloat16

D_MODEL = 2048
DEPTH = 2
PAST_LEN = 2048
CHUNK = 64
A_HEADS = 16
A_HEAD_DIM = 64
A_WIDTH = A_HEADS * A_HEAD_DIM
A_PAST_CHUNKS = 8
A_WINDOW = A_PAST_CHUNKS * CHUNK
REL_CLIP = 128
B_HEADS = 16
B_HEAD_DIM = 64
B_WIDTH = B_HEADS * B_HEAD_DIM
B_GROUPS = 2
B_STATE = 128
B_CONV = 4
B_CONV_DIM = B_WIDTH + 2 * B_GROUPS * B_STATE
C_GROUP_CH = 16
C_GROUPS = D_MODEL // C_GROUP_CH
C_STATE = 64
D_FF = 5632
N_EXPERTS = 8
TOP_K = 2
D_FF_EXPERT = 2816
ALPHA = (2.0 * DEPTH) ** 0.25
LN_EPS = 1e-5
RMS_EPS = 1e-5
NEG_INF = -1e30
IN0_WIDTH = 3 * A_WIDTH + B_WIDTH + B_CONV_DIM + B_HEADS
IN0_PAD = 6144

LANES = 128
SUBLANES = 8
VMEM_LIMIT = 56 * 1024 * 1024

LN_SUBROWS = 256
ATT_QBLOCK = 512
ATT_QSUB = 256
ATT_KSUB = ATT_QSUB + A_WINDOW
S5_JBLK = 16
S5_NJ = C_GROUPS // S5_JBLK
S5_JW = S5_JBLK * C_STATE


def _cparams(sem):
    return pltpu.CompilerParams(dimension_semantics=sem, vmem_limit_bytes=VMEM_LIMIT)


def _split_bf16(x):
    hi = x.astype(BF16)
    lo = (x - hi.astype(F32)).astype(BF16)
    return hi, lo


def _mxu(a, b, nt=False):
    if nt:
        return lax.dot_general(a, b, (((1,), (1,)), ((), ())), preferred_element_type=F32)
    return jnp.dot(a, b, preferred_element_type=F32)


def _dot1(a, b, nt=False):
    return _mxu(a.astype(BF16), b.astype(BF16), nt)


def _dot3(a, b, nt=False):
    ah, al = _split_bf16(a.astype(F32))
    bh, bl = _split_bf16(b.astype(F32))
    return _mxu(ah, bh, nt) + (_mxu(ah, bl, nt) + _mxu(al, bh, nt))


def _dot_sel(sel_bf16, x):
    x1 = x.astype(BF16)
    r1 = x - x1.astype(F32)
    x2 = r1.astype(BF16)
    x3 = (r1 - x2.astype(F32)).astype(BF16)
    return _mxu(sel_bf16, x1) + (_mxu(sel_bf16, x2) + _mxu(sel_bf16, x3))


def _dot_rsel(x, sel_bf16):
    x1 = x.astype(BF16)
    r1 = x - x1.astype(F32)
    x2 = r1.astype(BF16)
    x3 = (r1 - x2.astype(F32)).astype(BF16)
    return _mxu(x1, sel_bf16) + (_mxu(x2, sel_bf16) + _mxu(x3, sel_bf16))


def _silu(x):
    return x * jax.nn.sigmoid(x)


def _row_spec(arr, tm):
    d = arr.shape[-1]
    if arr.shape[0] == 1:
        return pl.BlockSpec((1, d), lambda i, *_: (0, 0))
    return pl.BlockSpec((tm, d), lambda i, *_: (i, 0))


def _adaln_kernel(c_ref, w_ref, b_ref, o_ref):
    c = c_ref[...]
    o_ref[0] = _dot3(_silu(c), w_ref[0]) + b_ref[0]


def adaln_all(c_rows, ada_w, ada_b):
    r, d = c_rows.shape
    n = ada_w.shape[-1]
    tn = 512
    w = ada_w.reshape(2 * DEPTH, d, n)
    b = ada_b.reshape(2 * DEPTH, 1, n)
    return pl.pallas_call(
        _adaln_kernel,
        grid=(2 * DEPTH, n // tn),
        in_specs=[pl.BlockSpec((r, d), lambda i, j: (0, 0)),
                  pl.BlockSpec((1, d, tn), lambda i, j: (i, 0, j)),
                  pl.BlockSpec((1, 1, tn), lambda i, j: (i, 0, j))],
        out_specs=pl.BlockSpec((1, r, tn), lambda i, j: (i, 0, j)),
        out_shape=jax.ShapeDtypeStruct((2 * DEPTH, r, n), F32),
        compiler_params=_cparams(("arbitrary", "arbitrary")),
        name="adaln",
    )(c_rows, w, b)


def _mm_kernel(*refs, dot, split_cols, modulated):
    if modulated:
        x_ref, sc_ref, sh_ref, w_ref, o_ref, h_ref = refs

        @pl.when(pl.program_id(1) == 0)
        def _():
            h_ref[...] = (x_ref[...] * (1.0 + sc_ref[...]) + sh_ref[...]).astype(h_ref.dtype)

        a = h_ref[...]
    else:
        a_ref, w_ref, o_ref = refs
        a = a_ref[...]
    res = dot(a, w_ref[...]).astype(o_ref.dtype)
    if split_cols:
        for cb in range(o_ref.shape[0]):
            o_ref[cb] = res[:, cb * LANES:(cb + 1) * LANES]
    else:
        o_ref[...] = res


def matmul(a, w, dot, tn=1024, split_cols=False, mod=None, mod_dtype=None):
    t, k = a.shape
    n = w.shape[1]
    tm = min(t, 1024)
    if split_cols:
        out_spec = pl.BlockSpec((tn // LANES, tm, LANES), lambda i, j: (j, i, 0))
        out_shape = jax.ShapeDtypeStruct((n // LANES, t, LANES), F32)
    else:
        out_spec = pl.BlockSpec((tm, tn), lambda i, j: (i, j))
        out_shape = jax.ShapeDtypeStruct((t, n), F32)
    in_specs = [pl.BlockSpec((tm, k), lambda i, j: (i, 0))]
    args = [a]
    scratch = []
    if mod is not None:
        in_specs += [_row_spec(mod[0], tm), _row_spec(mod[1], tm)]
        args += [mod[0], mod[1]]
        scratch = [pltpu.VMEM((tm, k), mod_dtype)]
    in_specs.append(pl.BlockSpec((k, tn), lambda i, j: (0, j)))
    args.append(w)
    return pl.pallas_call(
        functools.partial(_mm_kernel, dot=dot, split_cols=split_cols, modulated=mod is not None),
        grid=(t // tm, n // tn),
        in_specs=in_specs,
        out_specs=out_spec,
        out_shape=out_shape,
        scratch_shapes=scratch,
        compiler_params=_cparams(("arbitrary", "arbitrary")),
        name="matmul",
    )(*args)


def _ffn_up_kernel(*refs, dot, gated):
    if gated:
        a_ref, wg_ref, wu_ref, gates_ref, o_ref = refs
    else:
        a_ref, wg_ref, wu_ref, o_ref = refs
    a = a_ref[...]
    g = dot(a, wg_ref[0])
    u = dot(a, wu_ref[0])
    act = _silu(g) * u
    if gated:
        e = pl.program_id(1)
        gt = gates_ref[...]
        lane = lax.broadcasted_iota(jnp.int32, gt.shape, 1)
        act = act * jnp.sum(jnp.where(lane == e, gt, 0.0), axis=1, keepdims=True)
    o_ref[...] = act.astype(o_ref.dtype)


def ffn_up(a, w_up, dot, out_dtype, tm, tn, gates=None):
    t, d = a.shape
    e_n, _, f2 = w_up.shape
    f = f2 // 2
    nf = f // tn
    in_specs = [pl.BlockSpec((tm, d), lambda i, e, j: (i, 0)),
                pl.BlockSpec((1, d, tn), lambda i, e, j: (e, 0, j)),
                pl.BlockSpec((1, d, tn), lambda i, e, j: (e, 0, j + nf))]
    args = [a, w_up, w_up]
    if gates is not None:
        in_specs.append(pl.BlockSpec((tm, LANES), lambda i, e, j: (i, 0)))
        args.append(gates)
    return pl.pallas_call(
        functools.partial(_ffn_up_kernel, dot=dot, gated=gates is not None),
        grid=(t // tm, e_n, nf),
        in_specs=in_specs,
        out_specs=pl.BlockSpec((tm, tn), lambda i, e, j: (i, e * nf + j)),
        out_shape=jax.ShapeDtypeStruct((t, e_n * f), out_dtype),
        compiler_params=_cparams(("arbitrary", "arbitrary", "arbitrary")),
        name="ffn_up",
    )(*args)


def _mm_ln_kernel(*refs, n_a, nk, dot, glu, has_next):
    a_refs = refs[:n_a]
    w_refs = refs[n_a:2 * n_a]
    pos = 2 * n_a
    x_ref, gate_ref, g_ref, b_ref = refs[pos:pos + 4]
    pos += 4
    if has_next:
        sc_ref, sh_ref = refs[pos:pos + 2]
        pos += 2
    xo_ref = refs[pos]
    pos += 1
    if has_next:
        ho_ref = refs[pos]
        pos += 1
    acc_ref = refs[pos] if nk > 1 else None

    tm = x_ref.shape[0]
    sub = min(tm, max(LANES, min(LN_SUBROWS, tm // 2)))

    def load_a(a_ref, rows):
        if len(a_ref.shape) == 3:
            return jnp.concatenate([a_ref[cb, rows, :] for cb in range(a_ref.shape[0])], axis=1)
        return a_ref[rows, :]

    def product(rows):
        part = dot(load_a(a_refs[0], rows), w_refs[0][...])
        for a_ref, w_ref in zip(a_refs[1:], w_refs[1:]):
            part = part + dot(load_a(a_ref, rows), w_ref[...])
        return part

    def per_row(ref, rows):
        return ref[...] if ref.shape[0] == 1 else ref[rows, :]

    def epilogue(acc, rows):
        if glu:
            d = acc.shape[1] // 2
            out = acc[:, :d] * jax.nn.sigmoid(acc[:, d:])
        else:
            out = acc
        r = ALPHA * x_ref[rows, :] + (1.0 + per_row(gate_ref, rows)) * out
        mu = jnp.mean(r, axis=-1, keepdims=True)
        dev = r - mu
        var = jnp.mean(dev * dev, axis=-1, keepdims=True)
        xn = dev * lax.rsqrt(var + LN_EPS) * g_ref[...] + b_ref[...]
        xo_ref[rows, :] = xn
        if has_next:
            ho_ref[rows, :] = (xn * (1.0 + per_row(sc_ref, rows)) + per_row(sh_ref, rows)).astype(ho_ref.dtype)

    blocks = [slice(r0, r0 + sub) for r0 in range(0, tm, sub)]
    if nk == 1:
        for rows in blocks:
            epilogue(product(rows), rows)
    else:
        k = pl.program_id(1)
        part = product(slice(0, tm))

        @pl.when(k == 0)
        def _():
            acc_ref[...] = part

        @pl.when(k > 0)
        def _():
            acc_ref[...] += part

        @pl.when(k == nk - 1)
        def _():
            for rows in blocks:
                epilogue(acc_ref[rows, :], rows)


def matmul_ln(a_list, w, x, gate, ln_g, ln_b, dot, tm, tk=None, glu=False, nxt=None, next_dtype=None):
    t, d = x.shape
    n = w.shape[1]
    n_a = len(a_list)
    if n_a > 1:
        nk = 1
        in_specs = [pl.BlockSpec((tm, a.shape[1]), lambda i, k: (i, 0)) for a in a_list]
        off = 0
        for a in a_list:
            ka = a.shape[1]
            assert off % ka == 0
            in_specs.append(pl.BlockSpec((ka, n), functools.partial(lambda i, k, o: (o, 0), o=off // ka),
                                         pipeline_mode=pl.Buffered(1)))
            off += ka
        w_args = [w] * n_a
    else:
        a0 = a_list[0]
        split_cols = a0.ndim == 3
        ktot = a0.shape[0] * LANES if split_cols else a0.shape[1]
        tk = ktot if tk is None else tk
        nk = ktot // tk
        if split_cols:
            a_spec = pl.BlockSpec((tk // LANES, tm, LANES), lambda i, k: (k, i, 0))
        else:
            a_spec = pl.BlockSpec((tm, tk), lambda i, k: (i, k))
        w_mode = dict(pipeline_mode=pl.Buffered(1)) if nk == 1 else {}
        in_specs = [a_spec, pl.BlockSpec((tk, n), lambda i, k: (k, 0), **w_mode)]
        w_args = [w]
    in_specs += [pl.BlockSpec((tm, d), lambda i, k: (i, 0)), _row_spec(gate, tm),
                 pl.BlockSpec((1, d), lambda i, k: (0, 0)), pl.BlockSpec((1, d), lambda i, k: (0, 0))]
    args = list(a_list) + w_args + [x, gate, ln_g.reshape(1, d), ln_b.reshape(1, d)]
    out_specs = [pl.BlockSpec((tm, d), lambda i, k: (i, 0))]
    out_shape = [jax.ShapeDtypeStruct((t, d), F32)]
    if nxt is not None:
        in_specs += [_row_spec(nxt[0], tm), _row_spec(nxt[1], tm)]
        args += [nxt[0], nxt[1]]
        out_specs.append(pl.BlockSpec((tm, d), lambda i, k: (i, 0)))
        out_shape.append(jax.ShapeDtypeStruct((t, d), next_dtype))
    scratch = [pltpu.VMEM((tm, n), F32)] if nk > 1 else []
    res = pl.pallas_call(
        functools.partial(_mm_ln_kernel, n_a=n_a, nk=nk, dot=dot, glu=glu, has_next=nxt is not None),
        grid=(t // tm, nk),
        in_specs=in_specs,
        out_specs=out_specs,
        out_shape=out_shape,
        scratch_shapes=scratch,
        compiler_params=_cparams(("arbitrary", "arbitrary")),
        name="matmul_ln",
    )(*args)
    return res if nxt is not None else (res[0], None)


def _softmax_pv(scores, values, dot):
    m = scores[0].max(axis=-1, keepdims=True)
    for s in scores[1:]:
        m = jnp.maximum(m, s.max(axis=-1, keepdims=True))
    num = None
    den = None
    for s, v in zip(scores, values):
        p = jnp.exp(s - m)
        l = p.sum(axis=-1, keepdims=True)
        o = dot(p, v)
        num = o if num is None else num + o
        den = l if den is None else den + l
    return num / den


def _attn_prompt_kernel(q_ref, kp_ref, kc_ref, vp_ref, vc_ref, bias_ref, o_ref, *, dot):
    i = pl.program_id(0)
    lo = lax.broadcasted_iota(jnp.int32, (1, LANES), 1) < A_HEAD_DIM
    q = q_ref[...] * (A_HEAD_DIM ** -0.5)
    k = jnp.concatenate([kp_ref[...], kc_ref[...]], axis=0)
    v = jnp.concatenate([vp_ref[...], vc_ref[...]], axis=0)
    kidx = lax.broadcasted_iota(jnp.int32, (1, ATT_KSUB), 1)
    for sub in range(ATT_QBLOCK // ATT_QSUB):
        r0 = sub * ATT_QSUB
        qs = q[r0:r0 + ATT_QSUB]
        ks = k[r0:r0 + ATT_KSUB]
        vs = v[r0:r0 + ATT_KSUB]
        before = jnp.where(jnp.logical_and(i == 0, kidx + r0 < ATT_QBLOCK), NEG_INF, 0.0)
        outs = []
        for hh in range(2):
            qm = jnp.where(lo if hh == 0 else jnp.logical_not(lo), qs, 0.0)
            s = dot(qm, ks, nt=True) + bias_ref[0, hh] + before
            outs.append(_softmax_pv([s], [vs], dot))
        o_ref[r0:r0 + ATT_QSUB, :] = jnp.where(lo, outs[0], outs[1]).astype(o_ref.dtype)


def _band_table_kernel(base_ref, allow_ref, o_ref):
    lq, lk = o_ref.shape[1], o_ref.shape[2]
    wide = jnp.broadcast_to(base_ref[0], (lq, base_ref.shape[2]))
    toep = pltpu.roll(wide, 0, 1, stride=1, stride_axis=0)
    o_ref[0] = jnp.where(allow_ref[...] > 0.0, toep[:, :lk], NEG_INF)


def _band_tables(rel_bias, lq, lk, q_pos0, k_pos0):
    n_heads = rel_bias.shape[0]
    width = pl.next_power_of_2(lq + lk)
    c = np.arange(width)
    m = np.where(c < lk, c, c - width)
    idx = np.clip((q_pos0 - k_pos0) - m, -REL_CLIP, REL_CLIP) + REL_CLIP
    base = rel_bias[:, idx].astype(F32).reshape(n_heads, 1, width)
    q_pos = q_pos0 + np.arange(lq)
    k_pos = k_pos0 + np.arange(lk)
    qc = q_pos[:, None] // CHUNK
    kc = k_pos[None, :] // CHUNK
    allowed = ((kc <= qc) & (kc >= qc - A_PAST_CHUNKS) & (k_pos[None, :] >= 0)).astype(np.float32)
    return pl.pallas_call(
        _band_table_kernel,
        grid=(n_heads,),
        in_specs=[pl.BlockSpec((1, 1, width), lambda h: (h, 0, 0)),
                  pl.BlockSpec((lq, lk), lambda h: (0, 0))],
        out_specs=pl.BlockSpec((1, lq, lk), lambda h: (h, 0, 0)),
        out_shape=jax.ShapeDtypeStruct((n_heads, lq, lk), F32),
        compiler_params=_cparams(("arbitrary",)),
        name="band_table",
    )(base, jnp.asarray(allowed))


def attention_prompt(proj, rel_bias, dot, out_dtype):
    t = proj.shape[0]
    nb = t // ATT_QBLOCK
    npair = A_HEADS // 2
    tab = _band_tables(rel_bias, ATT_QSUB, ATT_KSUB, A_WINDOW, 0).reshape(npair, 2, ATT_QSUB, ATT_KSUB)
    blk = (ATT_QBLOCK, LANES)
    return pl.pallas_call(
        functools.partial(_attn_prompt_kernel, dot=dot),
        grid=(nb, npair),
        in_specs=[pl.BlockSpec(blk, lambda i, p: (i, p)),
                  pl.BlockSpec(blk, lambda i, p: (jnp.maximum(i - 1, 0), npair + p)),
                  pl.BlockSpec(blk, lambda i, p: (i, npair + p)),
                  pl.BlockSpec(blk, lambda i, p: (jnp.maximum(i - 1, 0), 2 * npair + p)),
                  pl.BlockSpec(blk, lambda i, p: (i, 2 * npair + p)),
                  pl.BlockSpec((1, 2, ATT_QSUB, ATT_KSUB), lambda i, p: (p, 0, 0, 0))],
        out_specs=pl.BlockSpec(blk, lambda i, p: (i, p)),
        out_shape=jax.ShapeDtypeStruct((t, A_WIDTH), out_dtype),
        compiler_params=_cparams(("arbitrary", "arbitrary")),
        name="attn_prompt",
    )(proj, proj, proj, proj, proj, tab)


def _attn_sample_kernel(q_ref, kn_ref, vn_ref, kc_ref, vc_ref, bias_ref, o_ref, *, dot, w):
    lo = lax.broadcasted_iota(jnp.int32, (1, LANES), 1) < A_HEAD_DIM
    lq = q_ref.shape[0]
    for p in range(A_HEADS // 2):
        cols = slice(p * LANES, (p + 1) * LANES)
        q = q_ref[:, cols] * (A_HEAD_DIM ** -0.5)
        kc = kc_ref[0, :, cols]
        vc = vc_ref[0, :, cols]
        kn = kn_ref[:, cols]
        vn = vn_ref[:, cols]
        outs = []
        for hh in range(2):
            qm = jnp.where(lo if hh == 0 else jnp.logical_not(lo), q, 0.0)
            bias = bias_ref[2 * p + hh]
            s_c = dot(qm, kc, nt=True) + bias[:, :w]
            s_n = dot(qm, kn, nt=True) + bias[:, w:w + lq]
            outs.append(_softmax_pv([s_c, s_n], [vc, vn], dot))
        o_ref[:, cols] = jnp.where(lo, outs[0], outs[1]).astype(o_ref.dtype)


def attention_sample(proj, k_cache, v_cache, rel_bias, bsz, lq, dot, out_dtype):
    w = k_cache.shape[1]
    lkp = ((w + lq + LANES - 1) // LANES) * LANES
    tab = _band_tables(rel_bias, lq, lkp, PAST_LEN, PAST_LEN - w)
    kc = k_cache.reshape(bsz, w, A_WIDTH)
    vc = v_cache.reshape(bsz, w, A_WIDTH)
    return pl.pallas_call(
        functools.partial(_attn_sample_kernel, dot=dot, w=w),
        grid=(bsz,),
        in_specs=[pl.BlockSpec((lq, A_WIDTH), lambda b: (b, 0)),
                  pl.BlockSpec((lq, A_WIDTH), lambda b: (b, 1)),
                  pl.BlockSpec((lq, A_WIDTH), lambda b: (b, 2)),
                  pl.BlockSpec((1, w, A_WIDTH), lambda b: (b, 0, 0)),
                  pl.BlockSpec((1, w, A_WIDTH), lambda b: (b, 0, 0)),
                  pl.BlockSpec((A_HEADS, lq, lkp), lambda b: (0, 0, 0))],
        out_specs=pl.BlockSpec((lq, A_WIDTH), lambda b: (b, 0)),
        out_shape=jax.ShapeDtypeStruct((bsz * lq, A_WIDTH), out_dtype),
        compiler_params=_cparams(("arbitrary",)),
        name="attn_sample",
    )(proj, proj, proj, kc, vc, tab)


def _ssd_kernel(z_ref, x0_ref, x1_ref, x2_ref, dt_ref, cbuf_ref, h0_ref,
                cw_ref, cb_ref, dtb_ref, alog_ref, dsk_ref, ng_ref, exp_ref,
                y_ref, cnew_ref, hout_ref, xp_ref, st_ref, *, dot, q):
    c = pl.program_id(1)
    half = B_WIDTH // B_GROUPS

    @pl.when(c == 0)
    def _():
        xp_ref[0:SUBLANES, :] = cbuf_ref[0]
        st_ref[...] = h0_ref[0]

    xp_ref[SUBLANES:SUBLANES + q, 0:512] = x0_ref[...]
    xp_ref[SUBLANES:SUBLANES + q, 512:1024] = x1_ref[...]
    xp_ref[SUBLANES:SUBLANES + q, 1024:1536] = x2_ref[...]
    base = SUBLANES - (B_CONV - 1)
    conv = cb_ref[...] + xp_ref[base:base + q, :] * cw_ref[0:1, :]
    for tap in range(1, B_CONV):
        conv = conv + xp_ref[base + tap:base + tap + q, :] * cw_ref[tap:tap + 1, :]
    tail = xp_ref[q:q + SUBLANES, :]
    xp_ref[0:SUBLANES, :] = tail
    cnew_ref[0] = tail

    xbc = _silu(conv)
    xs = xbc[:, :B_WIDTH]
    bm = xbc[:, B_WIDTH:B_WIDTH + B_GROUPS * B_STATE]
    cm = xbc[:, B_WIDTH + B_GROUPS * B_STATE:]

    lane = lax.broadcasted_iota(jnp.int32, (1, LANES), 1)
    head_ok = lane < B_HEADS
    dt = jnp.where(head_ok, jax.nn.softplus(dt_ref[...] + dtb_ref[...]), 0.0)
    a_neg = -jnp.exp(alog_ref[...])
    row = lax.broadcasted_iota(jnp.int32, (q, q), 0)
    col = lax.broadcasted_iota(jnp.int32, (q, q), 1)
    tril = row >= col
    acs = _dot_sel(tril.astype(BF16), dt * a_neg)
    acs_t = acs.T
    eacs = jnp.exp(acs)
    to_end = jnp.exp(acs[q - 1:q, :] - acs)
    wide = _dot_rsel(jnp.concatenate([dt, eacs, to_end], axis=0), exp_ref[...])
    dt_w = wide[0:q]
    eacs_w = wide[q:2 * q]
    toend_w = wide[2 * q:3 * q]
    xdt = xs * dt_w
    xend = xdt * toend_w
    lo = lane < B_HEAD_DIM

    y_parts = []
    for g in range(B_GROUPS):
        bg = bm[:, g * B_STATE:(g + 1) * B_STATE]
        cg = cm[:, g * B_STATE:(g + 1) * B_STATE]
        gmat = dot(cg, bg, nt=True)
        y_off = dot(cg, st_ref[g]) * eacs_w[:, g * half:(g + 1) * half]
        for pr in range(half // LANES):
            cols = slice(g * half + pr * LANES, g * half + (pr + 1) * LANES)
            x_pair = xdt[:, cols]
            outs = []
            for hh in range(2):
                h = (g * half + pr * LANES) // B_HEAD_DIM + hh
                diff = acs[:, h:h + 1] - acs_t[h:h + 1, :]
                decay = jnp.exp(jnp.where(tril, diff, -jnp.inf))
                outs.append(dot(gmat * decay, x_pair))
            y_parts.append(jnp.where(lo, outs[0], outs[1]) + y_off[:, pr * LANES:(pr + 1) * LANES])
        st_new = dot(bg.T, xend[:, g * half:(g + 1) * half])
        st_ref[g] = st_ref[g] * eacs_w[q - 1:q, g * half:(g + 1) * half] + st_new
    hout_ref[0] = st_ref[...]

    y = jnp.concatenate(y_parts, axis=1) + dsk_ref[...] * xs
    y = y * _silu(z_ref[...])
    y = y * lax.rsqrt(jnp.mean(y * y, axis=-1, keepdims=True) + RMS_EPS) * ng_ref[...]
    y_ref[...] = y.astype(y_ref.dtype)


def ssd_mixer(proj, conv_buf, h0, conv_w, conv_b, dt_bias, a_log, d_skip, norm_g, bsz, seq, q, dot, out_dtype):
    t = bsz * seq
    nc = seq // q
    half = B_WIDTH // B_GROUPS
    cbuf = jnp.pad(conv_buf, ((0, 0), (SUBLANES - (B_CONV - 1), 0), (0, 0)))
    h0_t = jnp.transpose(h0.reshape(bsz, B_GROUPS, half, B_STATE), (0, 1, 3, 2))
    pad = lambda v: jnp.pad(v.reshape(1, -1), ((0, 0), (0, LANES - v.shape[-1])))
    expand = jnp.asarray(np.repeat(np.eye(LANES, B_HEADS, dtype=np.float32).T, B_HEAD_DIM, axis=0).T, BF16)
    dsk_w = jnp.repeat(d_skip, B_HEAD_DIM).reshape(1, B_WIDTH)
    zb, xb, db = 3 * A_WIDTH // B_WIDTH, (3 * A_WIDTH + B_WIDTH) // 512, (3 * A_WIDTH + B_WIDTH + B_CONV_DIM) // LANES
    const = lambda shp: pl.BlockSpec(shp, lambda b, c: (0,) * len(shp))
    y, cnew, hout = pl.pallas_call(
        functools.partial(_ssd_kernel, dot=dot, q=q),
        grid=(bsz, nc),
        in_specs=[pl.BlockSpec((q, B_WIDTH), lambda b, c: (b * nc + c, zb)),
                  pl.BlockSpec((q, 512), lambda b, c: (b * nc + c, xb)),
                  pl.BlockSpec((q, 512), lambda b, c: (b * nc + c, xb + 1)),
                  pl.BlockSpec((q, 512), lambda b, c: (b * nc + c, xb + 2)),
                  pl.BlockSpec((q, LANES), lambda b, c: (b * nc + c, db)),
                  pl.BlockSpec((1, SUBLANES, B_CONV_DIM), lambda b, c: (b, 0, 0)),
                  pl.BlockSpec((1, B_GROUPS, B_STATE, half), lambda b, c: (b, 0, 0, 0)),
                  const((B_CONV, B_CONV_DIM)), const((1, B_CONV_DIM)), const((1, LANES)), const((1, LANES)),
                  const((1, B_WIDTH)), const((1, B_WIDTH)), const((LANES, B_WIDTH))],
        out_specs=[pl.BlockSpec((q, B_WIDTH), lambda b, c: (b * nc + c, 0)),
                   pl.BlockSpec((1, SUBLANES, B_CONV_DIM), lambda b, c: (b, 0, 0)),
                   pl.BlockSpec((1, B_GROUPS, B_STATE, half), lambda b, c: (b, 0, 0, 0))],
        out_shape=[jax.ShapeDtypeStruct((t, B_WIDTH), out_dtype),
                   jax.ShapeDtypeStruct((bsz, SUBLANES, B_CONV_DIM), F32),
                   jax.ShapeDtypeStruct((bsz, B_GROUPS, B_STATE, half), F32)],
        scratch_shapes=[pltpu.VMEM((q + SUBLANES, B_CONV_DIM), F32),
                        pltpu.VMEM((B_GROUPS, B_STATE, half), F32)],
        compiler_params=_cparams(("arbitrary", "arbitrary")),
        name="ssd",
    )(proj, proj, proj, proj, proj, cbuf, h0_t,
      conv_w, conv_b.reshape(1, -1), pad(dt_bias), pad(a_log), dsk_w, norm_g.reshape(1, -1), expand)
    conv_new = cnew[:, SUBLANES - (B_CONV - 1):, :]
    h_final = jnp.transpose(hout, (0, 1, 3, 2)).reshape(bsz, B_HEADS, B_HEAD_DIM, B_STATE)
    return y, conv_new, h_final


def _s5_prep_kernel(lre_ref, lim_ref, ls_ref, bre_ref, bim_ref, are_ref, aim_ref, bbre_ref, bbim_ref):
    lre = lre_ref[...]
    lim = lim_ref[...]
    step = jnp.exp(ls_ref[...])
    mag = jnp.exp(lre * step)
    ang = lim * step
    ab_re = mag * jnp.cos(ang)
    ab_im = mag * jnp.sin(ang)
    den = lre * lre + lim * lim
    f_re = ((ab_re - 1.0) * lre + ab_im * lim) / den
    f_im = (ab_im * lre - (ab_re - 1.0) * lim) / den
    br = bre_ref[...]
    bi = bim_ref[...]
    are_ref[...] = ab_re
    aim_ref[...] = ab_im
    bbre_ref[...] = f_re * br - f_im * bi
    bbim_ref[...] = f_re * bi + f_im * br


def _cmul(ar, ai, br, bi):
    return ar * br - ai * bi, ar * bi + ai * br


def _gelu_tanh(x):
    return 0.5 * x * (1.0 + jnp.tanh(math.sqrt(2.0 / math.pi) * (x + 0.044715 * (x * x * x))))


def _s5_kernel(u_ref, bb_ref, ccre_ref, ccim_ref, a_ref, d_ref, h0_ref, y_ref, hout_ref,
               s_ref, sb_ref, pw_ref, pwb_ref, carry_ref, *, dot, seg, chain):
    t = pl.program_id(2)
    w = S5_JW
    sdt = sb_ref.dtype
    g = 2 if sdt == BF16 else 1
    grows = g * SUBLANES
    n_groups = seg // g
    unroll = min(n_groups, 4 // g)
    a_re = a_ref[0, 0:1, :]
    a_im = a_ref[0, 1:2, :]

    def group(k):
        return pl.ds(pl.multiple_of(k * grows, grows), grows)

    @pl.when(t == 0)
    def _():
        carry_ref[0:h0_ref.shape[2], :] = h0_ref[0, 0]
        pw_ref[0:1, 0:w] = a_re
        pw_ref[0:1, w:2 * w] = a_im

        def power(i, carry):
            pr, pi = _cmul(pw_ref[pl.ds(i - 1, 1), 0:w], pw_ref[pl.ds(i - 1, 1), w:2 * w], a_re, a_im)
            pw_ref[pl.ds(i, 1), 0:w] = pr
            pw_ref[pl.ds(i, 1), w:2 * w] = pi
            return carry

        lax.fori_loop(1, seg, power, 0)

        def spread(k, carry):
            rows = [jnp.broadcast_to(pw_ref[pl.ds(k * g + q, 1), :], (SUBLANES, 2 * w)) for q in range(g)]
            pwb_ref[group(k), :] = jnp.concatenate(rows, axis=0).astype(sdt)
            return carry

        lax.fori_loop(0, n_groups, spread, 0)

    ncb = u_ref.shape[0]
    up = jnp.concatenate(
        [jnp.concatenate([u_ref[cb, pl.ds(i, SUBLANES, stride=seg), :] for i in range(seg)], axis=0)
         for cb in range(ncb)], axis=1)
    s_ref[...] = dot(up, bb_ref[0])

    are_b = jnp.broadcast_to(a_re, (SUBLANES, w))
    aim_b = jnp.broadcast_to(a_im, (SUBLANES, w))

    def scan(k, st):
        sre, sim = st
        res, ims = [], []
        for q in range(g):
            rows = pl.ds(pl.multiple_of((k * g + q) * SUBLANES, SUBLANES), SUBLANES)
            pr, pi = _cmul(are_b, aim_b, sre, sim)
            sre = pr + s_ref[rows, 0:w]
            sim = pi + s_ref[rows, w:2 * w]
            res.append(sre)
            ims.append(sim)
        sb_ref[group(k), 0:w] = jnp.concatenate(res, axis=0).astype(sdt)
        sb_ref[group(k), w:2 * w] = jnp.concatenate(ims, axis=0).astype(sdt)
        return sre, sim

    zero = jnp.zeros((SUBLANES, w), F32)
    end_re, end_im = lax.fori_loop(0, n_groups, scan, (zero, zero), unroll=unroll)

    sg_re = pw_ref[seg - 1:seg, 0:w]
    sg_im = pw_ref[seg - 1:seg, w:2 * w]
    if chain:
        c_re = carry_ref[0:1, 0:w]
        c_im = carry_ref[0:1, w:2 * w]
        rows_re, rows_im = [], []
        for s in range(SUBLANES):
            rows_re.append(c_re)
            rows_im.append(c_im)
            pr, pi = _cmul(sg_re, sg_im, c_re, c_im)
            c_re = pr + end_re[s:s + 1, :]
            c_im = pi + end_im[s:s + 1, :]
        carry_ref[0:1, 0:w] = c_re
        carry_ref[0:1, w:2 * w] = c_im
        hout_ref[0, 0] = carry_ref[0:1, :]
        in_re = jnp.concatenate(rows_re, axis=0)
        in_im = jnp.concatenate(rows_im, axis=0)
    else:
        in_re = carry_ref[:, 0:w]
        in_im = carry_ref[:, w:2 * w]
        pr, pi = _cmul(sg_re, sg_im, in_re, in_im)
        carry_ref[:, 0:w] = pr + end_re
        carry_ref[:, w:2 * w] = pi + end_im
        hout_ref[0, 0] = carry_ref[...]

    in_re_g = jnp.concatenate([in_re] * g, axis=0).astype(sdt)
    in_im_g = jnp.concatenate([in_im] * g, axis=0).astype(sdt)

    def fixup(k, carry):
        rows = group(k)
        pr, pi = _cmul(pwb_ref[rows, 0:w], pwb_ref[rows, w:2 * w], in_re_g, in_im_g)
        sb_ref[rows, 0:w] = sb_ref[rows, 0:w] + pr
        sb_ref[rows, w:2 * w] = sb_ref[rows, w:2 * w] + pi
        return carry

    lax.fori_loop(0, n_groups, fixup, 0, unroll=unroll)

    y = dot(sb_ref[:, 0:w], ccre_ref[0]) - dot(sb_ref[:, w:2 * w], ccim_ref[0]) + d_ref[...] * up
    y = _gelu_tanh(y)
    for cb in range(ncb):
        for i in range(seg):
            y_ref[cb, pl.ds(i, SUBLANES, stride=seg), :] = (
                y[i * SUBLANES:(i + 1) * SUBLANES, cb * LANES:(cb + 1) * LANES])


def _block_diag(m):
    nj, j, r, c = m.shape
    eye = jnp.eye(j, dtype=m.dtype)
    return (m[:, :, :, None, :] * eye[None, :, None, :, None]).reshape(nj, j * r, j * c)


def s5_mixer(u, h0_re, h0_im, lam_re, lam_im, log_step, b_re, b_im, c_re, c_im, d_skip,
             bsz, seq, tm, dot, w_dtype, chain):
    t = bsz * seq
    g, p, ch = C_GROUPS, C_STATE, C_GROUP_CH
    rep = lambda v: jnp.repeat(v, ch, axis=-1)
    ls = jnp.broadcast_to(log_step[:, None], (g, p))
    pc = pl.BlockSpec((g, p * ch), lambda: (0, 0))
    shp = jax.ShapeDtypeStruct((g, p * ch), F32)
    a_re_x, a_im_x, bb_re, bb_im = pl.pallas_call(
        _s5_prep_kernel, in_specs=[pc] * 5, out_specs=[pc] * 4, out_shape=[shp] * 4, name="s5_prep",
    )(rep(lam_re), rep(lam_im), rep(ls), b_re.reshape(g, p * ch), b_im.reshape(g, p * ch))
    a_re = a_re_x[:, ::ch]
    a_im = a_im_x[:, ::ch]
    to_blk = lambda m: jnp.transpose(m.reshape(S5_NJ, S5_JBLK, p, ch), (0, 1, 3, 2))
    bb = jnp.concatenate([_block_diag(to_blk(bb_re)), _block_diag(to_blk(bb_im))], axis=-1).astype(w_dtype)
    cblk = lambda m: jnp.transpose(m.reshape(S5_NJ, S5_JBLK, ch, p), (0, 1, 3, 2))
    cc_re = _block_diag(cblk(c_re)).astype(w_dtype)
    cc_im = _block_diag(cblk(c_im)).astype(w_dtype)
    a_rows = jnp.stack([a_re.reshape(S5_NJ, S5_JW), a_im.reshape(S5_NJ, S5_JW)], axis=1)
    h0 = jnp.concatenate([h0_re.reshape(bsz, S5_NJ, S5_JW), h0_im.reshape(bsz, S5_NJ, S5_JW)], axis=-1)
    if chain:
        nb, srows, rows_per_b = bsz, 1, seq
        h0 = h0[:, :, None, :]
    else:
        assert bsz == SUBLANES and tm == bsz * seq
        nb, srows, rows_per_b = 1, SUBLANES, bsz * seq
        h0 = jnp.transpose(h0, (1, 0, 2))[None]
    nt = rows_per_b // tm
    seg = tm // SUBLANES
    uw = S5_JBLK * ch
    ncb = uw // LANES
    state_spec = pl.BlockSpec((1, 1, srows, 2 * S5_JW), lambda b, j, i: (b, j, 0, 0))
    y, hout = pl.pallas_call(
        functools.partial(_s5_kernel, dot=dot, seg=seg, chain=chain),
        grid=(nb, S5_NJ, nt),
        in_specs=[pl.BlockSpec((ncb, tm, LANES), lambda b, j, i: (j, b * nt + i, 0)),
                  pl.BlockSpec((1, uw, 2 * S5_JW), lambda b, j, i: (j, 0, 0)),
                  pl.BlockSpec((1, S5_JW, uw), lambda b, j, i: (j, 0, 0)),
                  pl.BlockSpec((1, S5_JW, uw), lambda b, j, i: (j, 0, 0)),
                  pl.BlockSpec((1, 2, S5_JW), lambda b, j, i: (j, 0, 0)),
                  pl.BlockSpec((1, uw), lambda b, j, i: (0, j)),
                  state_spec],
        out_specs=[pl.BlockSpec((ncb, tm, LANES), lambda b, j, i: (j, b * nt + i, 0)), state_spec],
        out_shape=[jax.ShapeDtypeStruct((g * ch // LANES, t, LANES), F32),
                   jax.ShapeDtypeStruct((nb, S5_NJ, srows, 2 * S5_JW), F32)],
        scratch_shapes=[pltpu.VMEM((tm, 2 * S5_JW), F32),
                        pltpu.VMEM((tm, 2 * S5_JW), w_dtype),
                        pltpu.VMEM((seg, 2 * S5_JW), F32),
                        pltpu.VMEM((tm, 2 * S5_JW), w_dtype),
                        pltpu.VMEM((SUBLANES, 2 * S5_JW), F32)],
        compiler_params=_cparams(("arbitrary", "arbitrary", "arbitrary")),
        name="s5_scan",
    )(u, bb, cc_re, cc_im, a_rows, d_skip.reshape(1, -1), h0)
    hout = hout[:, :, 0, :] if chain else jnp.transpose(hout[0], (1, 0, 2))
    s_re = hout[:, :, :S5_JW].reshape(bsz, g, p)
    s_im = hout[:, :, S5_JW:].reshape(bsz, g, p)
    return y, s_re, s_im


def _router_kernel(x_ref, sc_ref, sh_ref, w_ref, b_ref, o_ref, r_ref, cnt_ref, run_ref):
    h = x_ref[...] * (1.0 + sc_ref[...]) + sh_ref[...]
    logits = _dot3(h, w_ref[...]) + b_ref[...]
    lane = lax.broadcasted_iota(jnp.int32, logits.shape, 1).astype(F32)
    logits = jnp.where(lane < N_EXPERTS, logits, -jnp.inf)
    m1 = logits.max(axis=-1, keepdims=True)
    i1 = jnp.min(jnp.where(logits == m1, lane, float(LANES)), axis=-1, keepdims=True)
    rest = jnp.where(lane == i1, -jnp.inf, logits)
    m2 = rest.max(axis=-1, keepdims=True)
    i2 = jnp.min(jnp.where(rest == m2, lane, float(LANES)), axis=-1, keepdims=True)
    e2 = jnp.exp(m2 - m1)
    w1 = 1.0 / (1.0 + e2)
    w2 = e2 / (1.0 + e2)
    o_ref[...] = jnp.where(lane == i1, w1, 0.0) + jnp.where(lane == i2, w2, 0.0)

    step = pl.program_id(0)

    @pl.when(step == 0)
    def _():
        run_ref[...] = jnp.zeros_like(run_ref)

    tm = logits.shape[0]
    sel = jnp.where(jnp.logical_or(lane == i1, lane == i2), 1.0, 0.0)
    row = lax.broadcasted_iota(jnp.int32, (tm, tm), 0)
    col = lax.broadcasted_iota(jnp.int32, (tm, tm), 1)
    before = (row > col).astype(BF16)
    rank = _mxu(before, sel.astype(BF16)) + run_ref[0:1, :]
    pos1 = jnp.sum(jnp.where(lane == i1, rank, 0.0), axis=-1, keepdims=True)
    pos2 = jnp.sum(jnp.where(lane == i2, rank, 0.0), axis=-1, keepdims=True)
    run_ref[0:1, :] = run_ref[0:1, :] + jnp.sum(sel, axis=0, keepdims=True)
    cnt_ref[...] = run_ref[0:1, :]
    rec = jnp.zeros_like(logits)
    for k, val in enumerate((i1, i2, pos1, pos2, w1, w2)):
        rec = jnp.where(lane == float(k), val, rec)
    r_ref[...] = rec


ROUTE_E1, ROUTE_E2, ROUTE_POS1, ROUTE_POS2, ROUTE_W1, ROUTE_W2 = range(6)


def router(x, scale, shift, router_w, router_b):
    t, d = x.shape
    tm = min(t, 512)
    w = jnp.pad(router_w, ((0, 0), (0, LANES - N_EXPERTS)))
    b = jnp.pad(router_b.reshape(1, -1), ((0, 0), (0, LANES - N_EXPERTS)))
    return pl.pallas_call(
        _router_kernel,
        grid=(t // tm,),
        in_specs=[pl.BlockSpec((tm, d), lambda i: (i, 0)), _row_spec(scale, tm), _row_spec(shift, tm),
                  pl.BlockSpec((d, LANES), lambda i: (0, 0)), pl.BlockSpec((1, LANES), lambda i: (0, 0))],
        out_specs=[pl.BlockSpec((tm, LANES), lambda i: (i, 0)), pl.BlockSpec((tm, LANES), lambda i: (i, 0)),
                   pl.BlockSpec((1, LANES), lambda i: (0, 0))],
        out_shape=[jax.ShapeDtypeStruct((t, LANES), F32), jax.ShapeDtypeStruct((t, LANES), F32),
                   jax.ShapeDtypeStruct((1, LANES), F32)],
        scratch_shapes=[pltpu.VMEM((SUBLANES, LANES), F32)],
        compiler_params=_cparams(("arbitrary",)),
        name="router",
    )(x, scale, shift, w, b)


MOE_TILE = 256
DMA_ISSUE_UNROLL = 8


def _dispatch_kernel(fill_ref, d1_ref, d2_ref, h_ref, xs_ref, zero_ref, sem, zsem):
    tm = h_ref.shape[0]

    @pl.when(pl.program_id(0) == 0)
    def _():
        zero_ref[...] = jnp.zeros_like(zero_ref)

        def fill_copy(k):
            row0 = pl.multiple_of(jnp.maximum(fill_ref[k], 0), MOE_TILE)
            return pltpu.make_async_copy(zero_ref, xs_ref.at[pl.ds(row0, MOE_TILE)], zsem)

        for k in range(fill_ref.shape[0]):
            @pl.when(fill_ref[k] >= 0)
            def _():
                fill_copy(k).start()

        for k in range(fill_ref.shape[0]):
            @pl.when(fill_ref[k] >= 0)
            def _():
                fill_copy(k).wait()

    def issue(t, carry):
        src = h_ref.at[pl.ds(t, 1)]
        pltpu.make_async_copy(src, xs_ref.at[pl.ds(d1_ref[0, 0, t], 1)], sem).start()
        pltpu.make_async_copy(src, xs_ref.at[pl.ds(d2_ref[0, 0, t], 1)], sem).start()
        return carry

    lax.fori_loop(0, tm, issue, 0, unroll=DMA_ISSUE_UNROLL)
    pltpu.make_async_copy(xs_ref.at[pl.ds(0, 2 * tm)], xs_ref.at[pl.ds(0, 2 * tm)], sem).wait()


def moe_dispatch(h, dest1, dest2, fill_rows, n_rows):
    t, d = h.shape
    tm = 512
    idx_spec = pl.BlockSpec((1, 1, tm), lambda i, fr: (i, 0, 0), memory_space=pltpu.SMEM)
    grid_spec = pltpu.PrefetchScalarGridSpec(
        num_scalar_prefetch=1,
        grid=(t // tm,),
        in_specs=[idx_spec, idx_spec, pl.BlockSpec((tm, d), lambda i, fr: (i, 0))],
        out_specs=pl.BlockSpec(memory_space=pl.ANY),
        scratch_shapes=[pltpu.VMEM((MOE_TILE, d), h.dtype), pltpu.SemaphoreType.DMA(()),
                        pltpu.SemaphoreType.DMA(())])
    return pl.pallas_call(
        _dispatch_kernel,
        grid_spec=grid_spec,
        out_shape=jax.ShapeDtypeStruct((n_rows, d), h.dtype),
        compiler_params=_cparams(("arbitrary",)),
        name="moe_dispatch",
    )(fill_rows, dest1.reshape(t // tm, 1, tm), dest2.reshape(t // tm, 1, tm), h)


def _expert_up_kernel(te_ref, nv_ref, x_ref, wg_ref, wu_ref, o_ref, *, dot):
    del te_ref

    @pl.when(pl.program_id(1) < nv_ref[0])
    def _():
        x = x_ref[...]
        g = dot(x, wg_ref[0])
        u = dot(x, wu_ref[0])
        o_ref[...] = (_silu(g) * u).astype(o_ref.dtype)

    @pl.when(pl.program_id(1) >= nv_ref[0])
    def _():
        o_ref[...] = jnp.zeros_like(o_ref)


def expert_up(xs, w_up, tile_expert, n_valid, dot, tn):
    r, d = xs.shape
    f = w_up.shape[2] // 2
    nf = f // tn
    nt = r // MOE_TILE
    grid_spec = pltpu.PrefetchScalarGridSpec(
        num_scalar_prefetch=2,
        grid=(nf, nt),
        in_specs=[pl.BlockSpec((MOE_TILE, d), lambda j, i, te, nv: (i, 0)),
                  pl.BlockSpec((1, d, tn), lambda j, i, te, nv: (te[i], 0, j)),
                  pl.BlockSpec((1, d, tn), lambda j, i, te, nv: (te[i], 0, j + nf))],
        out_specs=pl.BlockSpec((MOE_TILE, tn), lambda j, i, te, nv: (i, j)))
    return pl.pallas_call(
        functools.partial(_expert_up_kernel, dot=dot),
        grid_spec=grid_spec,
        out_shape=jax.ShapeDtypeStruct((r, f), BF16),
        compiler_params=_cparams(("arbitrary", "arbitrary")),
        name="expert_up",
    )(tile_expert, n_valid, xs, w_up, w_up)


def _expert_down_kernel(te_ref, nv_ref, a_ref, w_ref, o_ref, *, dot):
    del te_ref

    @pl.when(pl.program_id(0) < nv_ref[0])
    def _():
        o_ref[...] = dot(a_ref[...], w_ref[0])

    @pl.when(pl.program_id(0) >= nv_ref[0])
    def _():
        o_ref[...] = jnp.zeros_like(o_ref)


def expert_down(act, w_down, tile_expert, n_valid, dot):
    r, f = act.shape
    d = w_down.shape[2]
    grid_spec = pltpu.PrefetchScalarGridSpec(
        num_scalar_prefetch=2,
        grid=(r // MOE_TILE,),
        in_specs=[pl.BlockSpec((MOE_TILE, f), lambda i, te, nv: (i, 0)),
                  pl.BlockSpec((1, f, d), lambda i, te, nv: (te[i], 0, 0))],
        out_specs=pl.BlockSpec((MOE_TILE, d), lambda i, te, nv: (i, 0)))
    return pl.pallas_call(
        functools.partial(_expert_down_kernel, dot=dot),
        grid_spec=grid_spec,
        out_shape=jax.ShapeDtypeStruct((r, d), F32),
        compiler_params=_cparams(("arbitrary",)),
        name="expert_down",
    )(tile_expert, n_valid, act, w_down)


def _combine_ln_kernel(d1_ref, d2_ref, d1n_ref, d2n_ref, y_ref, rec_ref, x_ref, gate_ref, g_ref, b_ref, xo_ref,
                       buf1, buf2, sem):
    tm = x_ref.shape[0]
    step = pl.program_id(0)
    slot = step % 2

    def gather(idx1, idx2, to):
        def issue(t, carry):
            pltpu.make_async_copy(y_ref.at[pl.ds(idx1[0, 0, t], 1)], buf1.at[to, pl.ds(t, 1)], sem.at[to]).start()
            pltpu.make_async_copy(y_ref.at[pl.ds(idx2[0, 0, t], 1)], buf2.at[to, pl.ds(t, 1)], sem.at[to]).start()
            return carry

        lax.fori_loop(0, tm, issue, 0, unroll=DMA_ISSUE_UNROLL)

    @pl.when(step == 0)
    def _():
        gather(d1_ref, d2_ref, 0)

    @pl.when(step + 1 < pl.num_programs(0))
    def _():
        gather(d1n_ref, d2n_ref, 1 - slot)

    pltpu.make_async_copy(y_ref.at[pl.ds(0, tm)], buf1.at[slot], sem.at[slot]).wait()
    pltpu.make_async_copy(y_ref.at[pl.ds(0, tm)], buf2.at[slot], sem.at[slot]).wait()

    rec = rec_ref[...]
    lane = lax.broadcasted_iota(jnp.int32, rec.shape, 1)
    w1 = jnp.sum(jnp.where(lane == ROUTE_W1, rec, 0.0), axis=-1, keepdims=True)
    w2 = jnp.sum(jnp.where(lane == ROUTE_W2, rec, 0.0), axis=-1, keepdims=True)
    out = w1 * buf1[slot] + w2 * buf2[slot]
    r = ALPHA * x_ref[...] + (1.0 + gate_ref[...]) * out
    mu = jnp.mean(r, axis=-1, keepdims=True)
    dev = r - mu
    var = jnp.mean(dev * dev, axis=-1, keepdims=True)
    xo_ref[...] = dev * lax.rsqrt(var + LN_EPS) * g_ref[...] + b_ref[...]


def moe_combine_ln(y_sorted, dest1, dest2, rec, x, gate, ln_g, ln_b):
    t, d = x.shape
    tm = 256
    nt = t // tm
    idx_spec = pl.BlockSpec((1, 1, tm), lambda i: (i, 0, 0), memory_space=pltpu.SMEM)
    nxt_spec = pl.BlockSpec((1, 1, tm), lambda i: (jnp.minimum(i + 1, nt - 1), 0, 0), memory_space=pltpu.SMEM)
    d1 = dest1.reshape(nt, 1, tm)
    d2 = dest2.reshape(nt, 1, tm)
    return pl.pallas_call(
        _combine_ln_kernel,
        grid=(nt,),
        in_specs=[idx_spec, idx_spec, nxt_spec, nxt_spec, pl.BlockSpec(memory_space=pl.ANY),
                  pl.BlockSpec((tm, LANES), lambda i: (i, 0)), pl.BlockSpec((tm, d), lambda i: (i, 0)),
                  _row_spec(gate, tm), pl.BlockSpec((1, d), lambda i: (0, 0)), pl.BlockSpec((1, d), lambda i: (0, 0))],
        out_specs=pl.BlockSpec((tm, d), lambda i: (i, 0)),
        out_shape=jax.ShapeDtypeStruct((t, d), F32),
        scratch_shapes=[pltpu.VMEM((2, tm, d), F32), pltpu.VMEM((2, tm, d), F32), pltpu.SemaphoreType.DMA((2,))],
        compiler_params=_cparams(("arbitrary",)),
        name="moe_combine_ln",
    )(d1, d2, d1, d2, y_sorted, rec, x, gate, ln_g.reshape(1, d), ln_b.reshape(1, d))


def moe_top2(h, x, rec, counts, gate, ln_g, ln_b, w_up, w_down, dot):
    t, d = h.shape
    n_tiles = (TOP_K * t) // MOE_TILE + N_EXPERTS
    n_rows = n_tiles * MOE_TILE
    cnt = counts[0, :N_EXPERTS].astype(jnp.int32)
    padded = ((cnt + MOE_TILE - 1) // MOE_TILE) * MOE_TILE
    ends = jnp.cumsum(padded)
    starts = ends - padded
    col = lambda k: rec[:, k].astype(jnp.int32)
    experts = jnp.arange(N_EXPERTS, dtype=jnp.int32)[None, :]
    start_of = lambda e: jnp.sum(jnp.where(e[:, None] == experts, starts[None, :], 0), axis=1)
    dest1 = start_of(col(ROUTE_E1)) + col(ROUTE_POS1)
    dest2 = start_of(col(ROUTE_E2)) + col(ROUTE_POS2)
    tile_start = jnp.arange(n_tiles, dtype=jnp.int32) * MOE_TILE
    tile_expert = jnp.minimum(jnp.searchsorted(ends, tile_start, side="right"), N_EXPERTS - 1).astype(jnp.int32)
    n_valid = (ends[-1:] // MOE_TILE).astype(jnp.int32)
    last_tile = jnp.where(padded > 0, ends - MOE_TILE, -1)
    spare = ends[-1] + jnp.arange(N_EXPERTS, dtype=jnp.int32) * MOE_TILE
    fill_rows = jnp.concatenate([last_tile, jnp.where(spare < n_rows, spare, -1)]).astype(jnp.int32)
    xs = moe_dispatch(h, dest1, dest2, fill_rows, n_rows)
    act = expert_up(xs, w_up, tile_expert, n_valid, dot, tn=D_FF_EXPERT // 2)
    y_sorted = expert_down(act, w_down, tile_expert, n_valid, dot)
    return moe_combine_ln(y_sorted, dest1, dest2, rec, x, gate, ln_g, ln_b)


def _trunk(x, mods, caches, wts, precise):
    bsz, seq, d = x.shape
    t = bsz * seq
    prompt = caches is None
    dot = _dot3 if precise else _dot1
    act_dtype = F32 if precise else BF16
    tm = min(t, 512)
    x0 = x.reshape(t, d)

    sh, sc, gt = mods[0][0]
    proj = matmul(x0, wts["w_in0"], dot, mod=(sc, sh), mod_dtype=act_dtype)
    k_all = proj[:, A_WIDTH:2 * A_WIDTH].reshape(bsz, seq, A_HEADS, A_HEAD_DIM)
    v_all = proj[:, 2 * A_WIDTH:3 * A_WIDTH].reshape(bsz, seq, A_HEADS, A_HEAD_DIM)
    if prompt:
        att = attention_prompt(proj, wts["rel_bias"], dot, act_dtype)
        keep = min(A_WINDOW, seq)
        k_new, v_new = k_all[:, seq - keep:], v_all[:, seq - keep:]
        conv_buf = jnp.zeros((bsz, B_CONV - 1, B_CONV_DIM), F32)
        ssm_h0 = jnp.zeros((bsz, B_HEADS, B_HEAD_DIM, B_STATE), F32)
        q_len = min(seq, 256)
    else:
        att = attention_sample(proj, caches["k"], caches["v"], wts["rel_bias"], bsz, seq, dot, act_dtype)
        k_new, v_new = k_all, v_all
        conv_buf, ssm_h0 = caches["conv"], caches["ssm"]
        q_len = seq
    y_ssd, conv_new, ssm_new = ssd_mixer(proj, conv_buf, ssm_h0, wts["conv_w"], wts["conv_b"], wts["dt_bias"],
                                         wts["a_log"], wts["ssd_d"], wts["ssd_norm_g"], bsz, seq, q_len,
                                         dot, act_dtype)
    x1, h1 = matmul_ln([att, y_ssd], wts["w_out0"], x0, gt, wts["ln_g"][0, 0], wts["ln_b"][0, 0], dot, tm,
                       nxt=(mods[0][1][1], mods[0][1][0]), next_dtype=act_dtype)
    act = ffn_up(h1, wts["ffn_w_up"], dot, act_dtype, tm=min(t, 1024), tn=512)
    x2, h2 = matmul_ln([act], wts["ffn_w_down"], x1, mods[0][1][2], wts["ln_g"][0, 1], wts["ln_b"][0, 1], dot,
                       min(t, 256), tk=None if prompt else 1408, nxt=(mods[1][0][1], mods[1][0][0]), next_dtype=act_dtype)

    u = matmul(h2, wts["w_in1"], dot, split_cols=True)
    if prompt:
        s5_re0 = jnp.zeros((bsz, C_GROUPS, C_STATE), F32)
        s5_im0 = jnp.zeros((bsz, C_GROUPS, C_STATE), F32)
    else:
        s5_re0, s5_im0 = caches["s5_re"], caches["s5_im"]
    y5, s5_re, s5_im = s5_mixer(u, s5_re0, s5_im0, wts["s5_lam_re"], wts["s5_lam_im"], wts["s5_log_step"],
                                wts["s5_b_re"], wts["s5_b_im"], wts["s5_c_re"], wts["s5_c_im"], wts["s5_d"],
                                bsz, seq, 512 if prompt else bsz * seq, dot, F32 if precise else BF16,
                                chain=prompt)
    x3, h3 = matmul_ln([y5], wts["glu_w"], x2, mods[1][0][2], wts["ln_g"][1, 0], wts["ln_b"][1, 0], dot,
                       min(t, 256), tk=None if prompt else 512, glu=True,
                       nxt=(mods[1][1][1], mods[1][1][0]), next_dtype=F32)
    gates, rec, counts = router(x3, mods[1][1][1], mods[1][1][0], wts["router_w"], wts["router_b"])
    if prompt:
        x4 = moe_top2(h3, x3, rec, counts, mods[1][1][2], wts["ln_g"][1, 1], wts["ln_b"][1, 1],
                      wts["moe_w_up"], wts["moe_w_down"], dot)
    else:
        act = ffn_up(h3, wts["moe_w_up_bf16"], _dot1, BF16, tm=min(t, 2048), tn=256, gates=gates)
        w_down = wts["moe_w_down_bf16"].reshape(N_EXPERTS * D_FF_EXPERT, d)
        x4, _ = matmul_ln([act], w_down, x3, mods[1][1][2], wts["ln_g"][1, 1], wts["ln_b"][1, 1], _dot1, tm,
                          tk=2816)
    return (x4.reshape(bsz, seq, d), k_new[None], v_new[None], conv_new[None], ssm_new[None],
            s5_re[None], s5_im[None])


def kernel(x_prompt, x_sample, cache_attn_k, cache_attn_v, state_ssd_conv, state_ssd, state_s5_re, state_s5_im, c_prompt, c_sample, ada_w, ada_b, ln_g, ln_b, w_in0, w_out0, rel_bias, conv_w, conv_b, dt_bias, a_log, ssd_d, ssd_norm_g, ffn_w_up, ffn_w_down, w_in1, s5_lam_re, s5_lam_im, s5_log_step, s5_b_re, s5_b_im, s5_c_re, s5_c_im, s5_d, glu_w, router_w, router_b, moe_w_up, moe_w_down):
    d = D_MODEL
    bp, lp, _ = x_prompt.shape
    bs, ls, _ = x_sample.shape

    n_c = bp + bs
    rows = ((n_c + SUBLANES - 1) // SUBLANES) * SUBLANES
    c_all = jnp.pad(jnp.concatenate([c_prompt, c_sample], axis=0), ((0, rows - n_c), (0, 0)))
    mod = adaln_all(c_all, ada_w, ada_b).reshape(DEPTH, 2, rows, 3, d)

    def mods_for(r0, nb, per_row):
        out = []
        for layer in range(DEPTH):
            out.append([])
            for j in range(2):
                trip = []
                for part in range(3):
                    m = mod[layer, j, r0:r0 + nb, part]
                    trip.append(jnp.repeat(m, per_row, axis=0) if nb > 1 else m)
                out[-1].append(tuple(trip))
        return out

    assert bp == 1
    mods_p = mods_for(0, bp, lp)
    mods_s = mods_for(bp, bs, ls)

    in0_pad = ((0, 0), (0, IN0_PAD - IN0_WIDTH))
    shared = dict(ln_g=ln_g, ln_b=ln_b, rel_bias=rel_bias[0], conv_w=conv_w[0], conv_b=conv_b[0],
                  dt_bias=dt_bias[0], a_log=a_log[0], ssd_d=ssd_d[0], ssd_norm_g=ssd_norm_g[0],
                  s5_lam_re=s5_lam_re[0], s5_lam_im=s5_lam_im[0], s5_log_step=s5_log_step[0],
                  s5_b_re=s5_b_re[0], s5_b_im=s5_b_im[0], s5_c_re=s5_c_re[0], s5_c_im=s5_c_im[0], s5_d=s5_d[0],
                  router_w=router_w[0], router_b=router_b[0])
    big = dict(w_in0=jnp.pad(w_in0[0], in0_pad), w_out0=w_out0[0], ffn_w_up=ffn_w_up, ffn_w_down=ffn_w_down[0],
               w_in1=w_in1[0], glu_w=glu_w[0], moe_w_up=moe_w_up[0],
               moe_w_down=moe_w_down[0])
    wts_p = dict(shared, **{k: v.astype(BF16) for k, v in big.items()})
    wts_s = dict(shared, moe_w_up_bf16=wts_p["moe_w_up"], moe_w_down_bf16=wts_p["moe_w_down"], **big)

    y_p, k_p, v_p, conv_p, ssd_p, re_p, im_p = _trunk(x_prompt, mods_p, None, wts_p, precise=False)
    caches = dict(k=cache_attn_k[0], v=cache_attn_v[0], conv=state_ssd_conv[0], ssm=state_ssd[0],
                  s5_re=state_s5_re[0], s5_im=state_s5_im[0])
    y_s, k_s, v_s, conv_s, ssd_s, re_s, im_s = _trunk(x_sample, mods_s, caches, wts_s, precise=True)
    return (y_p, y_s, k_p, v_p, conv_p, ssd_p, re_p, im_p, k_s, v_s, conv_s, ssd_s, re_s, im_s)
```

```python
import functools
import math

import numpy as np
import jax
import jax.numpy as jnp
from jax import lax
from jax.experimental import pallas as pl
from jax.experimental.pallas import tpu as pltpu

F32 = jnp.float32
BF16 = jnp.bfloat16

D_MODEL = 2048
DEPTH = 2
PAST_LEN = 2048
CHUNK = 64
A_HEADS = 16
A_HEAD_DIM = 64
A_WIDTH = A_HEADS * A_HEAD_DIM
A_PAST_CHUNKS = 8
A_WINDOW = A_PAST_CHUNKS * CHUNK
REL_CLIP = 128
B_HEADS = 16
B_HEAD_DIM = 64
B_WIDTH = B_HEADS * B_HEAD_DIM
B_GROUPS = 2
B_STATE = 128
B_CONV = 4
B_CONV_DIM = B_WIDTH + 2 * B_GROUPS * B_STATE
C_GROUP_CH = 16
C_GROUPS = D_MODEL // C_GROUP_CH
C_STATE = 64
D_FF = 5632
N_EXPERTS = 8
TOP_K = 2
D_FF_EXPERT = 2816
ALPHA = (2.0 * DEPTH) ** 0.25
LN_EPS = 1e-5
RMS_EPS = 1e-5
NEG_INF = -1e30
IN0_WIDTH = 3 * A_WIDTH + B_WIDTH + B_CONV_DIM + B_HEADS
IN0_PAD = 6144

LANES = 128
SUBLANES = 8
VMEM_LIMIT = 56 * 1024 * 1024

LN_SUBROWS = 256
ATT_QBLOCK = 512
ATT_QSUB = 256
ATT_KSUB = ATT_QSUB + A_WINDOW
S5_JBLK = 16
S5_NJ = C_GROUPS // S5_JBLK
S5_JW = S5_JBLK * C_STATE


def _cparams(sem):
    return pltpu.CompilerParams(dimension_semantics=sem, vmem_limit_bytes=VMEM_LIMIT)


def _split_bf16(x):
    hi = x.astype(BF16)
    lo = (x - hi.astype(F32)).astype(BF16)
    return hi, lo


def _mxu(a, b, nt=False):
    if nt:
        return lax.dot_general(a, b, (((1,), (1,)), ((), ())), preferred_element_type=F32)
    return jnp.dot(a, b, preferred_element_type=F32)


def _dot1(a, b, nt=False):
    return _mxu(a.astype(BF16), b.astype(BF16), nt)


def _dot3(a, b, nt=False):
    ah, al = _split_bf16(a.astype(F32))
    bh, bl = _split_bf16(b.astype(F32))
    return _mxu(ah, bh, nt) + (_mxu(ah, bl, nt) + _mxu(al, bh, nt))


def _dot_sel(sel_bf16, x):
    x1 = x.astype(BF16)
    r1 = x - x1.astype(F32)
    x2 = r1.astype(BF16)
    x3 = (r1 - x2.astype(F32)).astype(BF16)
    return _mxu(sel_bf16, x1) + (_mxu(sel_bf16, x2) + _mxu(sel_bf16, x3))


def _dot_rsel(x, sel_bf16):
    x1 = x.astype(BF16)
    r1 = x - x1.astype(F32)
    x2 = r1.astype(BF16)
    x3 = (r1 - x2.astype(F32)).astype(BF16)
    return _mxu(x1, sel_bf16) + (_mxu(x2, sel_bf16) + _mxu(x3, sel_bf16))


def _silu(x):
    return x * jax.nn.sigmoid(x)


def _row_spec(arr, tm):
    d = arr.shape[-1]
    if arr.shape[0] == 1:
        return pl.BlockSpec((1, d), lambda i, *_: (0, 0))
    return pl.BlockSpec((tm, d), lambda i, *_: (i, 0))


def _adaln_kernel(c_ref, w_ref, b_ref, o_ref):
    c = c_ref[...]
    o_ref[0] = _dot3(_silu(c), w_ref[0]) + b_ref[0]


def adaln_all(c_rows, ada_w, ada_b):
    r, d = c_rows.shape
    n = ada_w.shape[-1]
    tn = 512
    w = ada_w.reshape(2 * DEPTH, d, n)
    b = ada_b.reshape(2 * DEPTH, 1, n)
    return pl.pallas_call(
        _adaln_kernel,
        grid=(2 * DEPTH, n // tn),
        in_specs=[pl.BlockSpec((r, d), lambda i, j: (0, 0)),
                  pl.BlockSpec((1, d, tn), lambda i, j: (i, 0, j)),
                  pl.BlockSpec((1, 1, tn), lambda i, j: (i, 0, j))],
        out_specs=pl.BlockSpec((1, r, tn), lambda i, j: (i, 0, j)),
        out_shape=jax.ShapeDtypeStruct((2 * DEPTH, r, n), F32),
        compiler_params=_cparams(("arbitrary", "arbitrary")),
        name="adaln",
    )(c_rows, w, b)


def _mm_kernel(*refs, dot, split_cols, modulated):
    if modulated:
        x_ref, sc_ref, sh_ref, w_ref, o_ref, h_ref = refs

        @pl.when(pl.program_id(1) == 0)
        def _():
            h_ref[...] = (x_ref[...] * (1.0 + sc_ref[...]) + sh_ref[...]).astype(h_ref.dtype)

        a = h_ref[...]
    else:
        a_ref, w_ref, o_ref = refs
        a = a_ref[...]
    res = dot(a, w_ref[...]).astype(o_ref.dtype)
    if split_cols:
        for cb in range(o_ref.shape[0]):
            o_ref[cb] = res[:, cb * LANES:(cb + 1) * LANES]
    else:
        o_ref[...] = res


def matmul(a, w, dot, tn=1024, split_cols=False, mod=None, mod_dtype=None):
    t, k = a.shape
    n = w.shape[1]
    tm = min(t, 1024)
    if split_cols:
        out_spec = pl.BlockSpec((tn // LANES, tm, LANES), lambda i, j: (j, i, 0))
        out_shape = jax.ShapeDtypeStruct((n // LANES, t, LANES), F32)
    else:
        out_spec = pl.BlockSpec((tm, tn), lambda i, j: (i, j))
        out_shape = jax.ShapeDtypeStruct((t, n), F32)
    in_specs = [pl.BlockSpec((tm, k), lambda i, j: (i, 0))]
    args = [a]
    scratch = []
    if mod is not None:
        in_specs += [_row_spec(mod[0], tm), _row_spec(mod[1], tm)]
        args += [mod[0], mod[1]]
        scratch = [pltpu.VMEM((tm, k), mod_dtype)]
    in_specs.append(pl.BlockSpec((k, tn), lambda i, j: (0, j)))
    args.append(w)
    return pl.pallas_call(
        functools.partial(_mm_kernel, dot=dot, split_cols=split_cols, modulated=mod is not None),
        grid=(t // tm, n // tn),
        in_specs=in_specs,
        out_specs=out_spec,
        out_shape=out_shape,
        scratch_shapes=scratch,
        compiler_params=_cparams(("arbitrary", "arbitrary")),
        name="matmul",
    )(*args)


def _ffn_up_kernel(*refs, dot, gated):
    if gated:
        a_ref, wg_ref, wu_ref, gates_ref, o_ref = refs
    else:
        a_ref, wg_ref, wu_ref, o_ref = refs
    a = a_ref[...]
    g = dot(a, wg_ref[0])
    u = dot(a, wu_ref[0])
    act = _silu(g) * u
    if gated:
        e = pl.program_id(1)
        gt = gates_ref[...]
        lane = lax.broadcasted_iota(jnp.int32, gt.shape, 1)
        act = act * jnp.sum(jnp.where(lane == e, gt, 0.0), axis=1, keepdims=True)
    o_ref[...] = act.astype(o_ref.dtype)


def ffn_up(a, w_up, dot, out_dtype, tm, tn, gates=None, chunk=None):
    t, d = a.shape
    e_n, _, f2 = w_up.shape
    f = f2 // 2
    nf = f // tn
    per = (f if chunk is None else chunk) // tn
    in_specs = [pl.BlockSpec((tm, d), lambda i, e, j: (i, 0)),
                pl.BlockSpec((1, d, tn), lambda i, e, j: (e, 0, 2 * (j // per) * per + j % per)),
                pl.BlockSpec((1, d, tn), lambda i, e, j: (e, 0, (2 * (j // per) + 1) * per + j % per))]
    args = [a, w_up, w_up]
    if gates is not None:
        in_specs.append(pl.BlockSpec((tm, LANES), lambda i, e, j: (i, 0)))
        args.append(gates)
    return pl.pallas_call(
        functools.partial(_ffn_up_kernel, dot=dot, gated=gates is not None),
        grid=(t // tm, e_n, nf),
        in_specs=in_specs,
        out_specs=pl.BlockSpec((tm, tn), lambda i, e, j: (i, e * nf + j)),
        out_shape=jax.ShapeDtypeStruct((t, e_n * f), out_dtype),
        compiler_params=_cparams(("arbitrary", "arbitrary", "arbitrary")),
        name="ffn_up",
    )(*args)


def _mm_ln_kernel(*refs, n_a, nk, dot, glu, has_next):
    a_refs = refs[:n_a]
    w_refs = refs[n_a:2 * n_a]
    pos = 2 * n_a
    x_ref, gate_ref, g_ref, b_ref = refs[pos:pos + 4]
    pos += 4
    if has_next:
        sc_ref, sh_ref = refs[pos:pos + 2]
        pos += 2
    xo_ref = refs[pos]
    pos += 1
    if has_next:
        ho_ref = refs[pos]
        pos += 1
    acc_ref = refs[pos] if nk > 1 else None

    tm = x_ref.shape[0]
    sub = min(tm, max(LANES, min(LN_SUBROWS, tm // 2)))

    def load_a(a_ref, rows):
        if len(a_ref.shape) == 3:
            return jnp.concatenate([a_ref[cb, rows, :] for cb in range(a_ref.shape[0])], axis=1)
        return a_ref[rows, :]

    def product(rows):
        w0 = w_refs[0][0] if len(w_refs[0].shape) == 3 else w_refs[0][...]
        part = dot(load_a(a_refs[0], rows), w0)
        for a_ref, w_ref in zip(a_refs[1:], w_refs[1:]):
            part = part + dot(load_a(a_ref, rows), w_ref[...])
        return part

    def per_row(ref, rows):
        return ref[...] if ref.shape[0] == 1 else ref[rows, :]

    def epilogue(acc, rows):
        if glu:
            d = acc.shape[1] // 2
            out = acc[:, :d] * jax.nn.sigmoid(acc[:, d:])
        else:
            out = acc
        r = ALPHA * x_ref[rows, :] + (1.0 + per_row(gate_ref, rows)) * out
        mu = jnp.mean(r, axis=-1, keepdims=True)
        dev = r - mu
        var = jnp.mean(dev * dev, axis=-1, keepdims=True)
        xn = dev * lax.rsqrt(var + LN_EPS) * g_ref[...] + b_ref[...]
        xo_ref[rows, :] = xn
        if has_next:
            ho_ref[rows, :] = (xn * (1.0 + per_row(sc_ref, rows)) + per_row(sh_ref, rows)).astype(ho_ref.dtype)

    blocks = [slice(r0, r0 + sub) for r0 in range(0, tm, sub)]
    if nk == 1:
        for rows in blocks:
            epilogue(product(rows), rows)
    else:
        k = pl.program_id(1)
        part = product(slice(0, tm))

        @pl.when(k == 0)
        def _():
            acc_ref[...] = part

        @pl.when(k > 0)
        def _():
            acc_ref[...] += part

        @pl.when(k == nk - 1)
        def _():
            for rows in blocks:
                epilogue(acc_ref[rows, :], rows)


def matmul_ln(a_list, w, x, gate, ln_g, ln_b, dot, tm, tk=None, glu=False, nxt=None, next_dtype=None):
    t, d = x.shape
    n = w.shape[-1]
    n_a = len(a_list)
    if n_a > 1:
        nk = 1
        in_specs = [pl.BlockSpec((tm, a.shape[1]), lambda i, k: (i, 0)) for a in a_list]
        off = 0
        for a in a_list:
            ka = a.shape[1]
            assert off % ka == 0
            in_specs.append(pl.BlockSpec((ka, n), functools.partial(lambda i, k, o: (o, 0), o=off // ka),
                                         pipeline_mode=pl.Buffered(1)))
            off += ka
        w_args = [w] * n_a
    else:
        a0 = a_list[0]
        split_cols = a0.ndim == 3
        ktot = a0.shape[0] * LANES if split_cols else a0.shape[1]
        tk = ktot if tk is None else tk
        nk = ktot // tk
        if split_cols:
            a_spec = pl.BlockSpec((tk // LANES, tm, LANES), lambda i, k: (k, i, 0))
        else:
            a_spec = pl.BlockSpec((tm, tk), lambda i, k: (i, k))
        w_mode = dict(pipeline_mode=pl.Buffered(1)) if nk == 1 else {}
        if w.ndim == 3:
            assert w.shape[1] == tk
            w_spec = pl.BlockSpec((1, tk, n), lambda i, k: (k, 0, 0))
        else:
            w_spec = pl.BlockSpec((tk, n), lambda i, k: (k, 0), **w_mode)
        in_specs = [a_spec, w_spec]
        w_args = [w]
    in_specs += [pl.BlockSpec((tm, d), lambda i, k: (i, 0)), _row_spec(gate, tm),
                 pl.BlockSpec((1, d), lambda i, k: (0, 0)), pl.BlockSpec((1, d), lambda i, k: (0, 0))]
    args = list(a_list) + w_args + [x, gate, ln_g.reshape(1, d), ln_b.reshape(1, d)]
    out_specs = [pl.BlockSpec((tm, d), lambda i, k: (i, 0))]
    out_shape = [jax.ShapeDtypeStruct((t, d), F32)]
    if nxt is not None:
        in_specs += [_row_spec(nxt[0], tm), _row_spec(nxt[1], tm)]
        args += [nxt[0], nxt[1]]
        out_specs.append(pl.BlockSpec((tm, d), lambda i, k: (i, 0)))
        out_shape.append(jax.ShapeDtypeStruct((t, d), next_dtype))
    scratch = [pltpu.VMEM((tm, n), F32)] if nk > 1 else []
    res = pl.pallas_call(
        functools.partial(_mm_ln_kernel, n_a=n_a, nk=nk, dot=dot, glu=glu, has_next=nxt is not None),
        grid=(t // tm, nk),
        in_specs=in_specs,
        out_specs=out_specs,
        out_shape=out_shape,
        scratch_shapes=scratch,
        compiler_params=_cparams(("arbitrary", "arbitrary")),
        name="matmul_ln",
    )(*args)
    return res if nxt is not None else (res[0], None)


def _softmax_pv(scores, values, dot):
    m = scores[0].max(axis=-1, keepdims=True)
    for s in scores[1:]:
        m = jnp.maximum(m, s.max(axis=-1, keepdims=True))
    num = None
    den = None
    for s, v in zip(scores, values):
        p = jnp.exp(s - m)
        l = p.sum(axis=-1, keepdims=True)
        o = dot(p, v)
        num = o if num is None else num + o
        den = l if den is None else den + l
    return num / den


def _attn_prompt_kernel(q_ref, kp_ref, kc_ref, vp_ref, vc_ref, bias_ref, o_ref, *, dot):
    i = pl.program_id(0)
    lo = lax.broadcasted_iota(jnp.int32, (1, LANES), 1) < A_HEAD_DIM
    q = q_ref[...] * (A_HEAD_DIM ** -0.5)
    k = jnp.concatenate([kp_ref[...], kc_ref[...]], axis=0)
    v = jnp.concatenate([vp_ref[...], vc_ref[...]], axis=0)
    kidx = lax.broadcasted_iota(jnp.int32, (1, ATT_KSUB), 1)
    for sub in range(ATT_QBLOCK // ATT_QSUB):
        r0 = sub * ATT_QSUB
        qs = q[r0:r0 + ATT_QSUB]
        ks = k[r0:r0 + ATT_KSUB]
        vs = v[r0:r0 + ATT_KSUB]
        before = jnp.where(jnp.logical_and(i == 0, kidx + r0 < ATT_QBLOCK), NEG_INF, 0.0)
        outs = []
        for hh in range(2):
            qm = jnp.where(lo if hh == 0 else jnp.logical_not(lo), qs, 0.0)
            s = dot(qm, ks, nt=True) + bias_ref[0, hh] + before
            outs.append(_softmax_pv([s], [vs], dot))
        o_ref[r0:r0 + ATT_QSUB, :] = jnp.where(lo, outs[0], outs[1]).astype(o_ref.dtype)


def _band_table_kernel(base_ref, allow_ref, o_ref):
    lq, lk = o_ref.shape[1], o_ref.shape[2]
    wide = jnp.broadcast_to(base_ref[0], (lq, base_ref.shape[2]))
    toep = pltpu.roll(wide, 0, 1, stride=1, stride_axis=0)
    o_ref[0] = jnp.where(allow_ref[...] > 0.0, toep[:, :lk], NEG_INF)


def _band_tables(rel_bias, lq, lk, q_pos0, k_pos0):
    n_heads = rel_bias.shape[0]
    width = pl.next_power_of_2(lq + lk)
    c = np.arange(width)
    m = np.where(c < lk, c, c - width)
    idx = np.clip((q_pos0 - k_pos0) - m, -REL_CLIP, REL_CLIP) + REL_CLIP
    base = rel_bias[:, idx].astype(F32).reshape(n_heads, 1, width)
    q_pos = q_pos0 + np.arange(lq)
    k_pos = k_pos0 + np.arange(lk)
    qc = q_pos[:, None] // CHUNK
    kc = k_pos[None, :] // CHUNK
    allowed = ((kc <= qc) & (kc >= qc - A_PAST_CHUNKS) & (k_pos[None, :] >= 0)).astype(np.float32)
    return pl.pallas_call(
        _band_table_kernel,
        grid=(n_heads,),
        in_specs=[pl.BlockSpec((1, 1, width), lambda h: (h, 0, 0)),
                  pl.BlockSpec((lq, lk), lambda h: (0, 0))],
        out_specs=pl.BlockSpec((1, lq, lk), lambda h: (h, 0, 0)),
        out_shape=jax.ShapeDtypeStruct((n_heads, lq, lk), F32),
        compiler_params=_cparams(("arbitrary",)),
        name="band_table",
    )(base, jnp.asarray(allowed))


def attention_prompt(proj, rel_bias, dot, out_dtype):
    t = proj.shape[0]
    nb = t // ATT_QBLOCK
    npair = A_HEADS // 2
    tab = _band_tables(rel_bias, ATT_QSUB, ATT_KSUB, A_WINDOW, 0).reshape(npair, 2, ATT_QSUB, ATT_KSUB)
    blk = (ATT_QBLOCK, LANES)
    return pl.pallas_call(
        functools.partial(_attn_prompt_kernel, dot=dot),
        grid=(nb, npair),
        in_specs=[pl.BlockSpec(blk, lambda i, p: (i, p)),
                  pl.BlockSpec(blk, lambda i, p: (jnp.maximum(i - 1, 0), npair + p)),
                  pl.BlockSpec(blk, lambda i, p: (i, npair + p)),
                  pl.BlockSpec(blk, lambda i, p: (jnp.maximum(i - 1, 0), 2 * npair + p)),
                  pl.BlockSpec(blk, lambda i, p: (i, 2 * npair + p)),
                  pl.BlockSpec((1, 2, ATT_QSUB, ATT_KSUB), lambda i, p: (p, 0, 0, 0))],
        out_specs=pl.BlockSpec(blk, lambda i, p: (i, p)),
        out_shape=jax.ShapeDtypeStruct((t, A_WIDTH), out_dtype),
        compiler_params=_cparams(("arbitrary", "arbitrary")),
        name="attn_prompt",
    )(proj, proj, proj, proj, proj, tab)


def _attn_sample_kernel(q_ref, kn_ref, vn_ref, kc_ref, vc_ref, bias_ref, o_ref, *, dot, w):
    lo = lax.broadcasted_iota(jnp.int32, (1, LANES), 1) < A_HEAD_DIM
    lq = q_ref.shape[0]
    for p in range(A_HEADS // 2):
        cols = slice(p * LANES, (p + 1) * LANES)
        q = q_ref[:, cols] * (A_HEAD_DIM ** -0.5)
        kc = kc_ref[0, :, cols]
        vc = vc_ref[0, :, cols]
        kn = kn_ref[:, cols]
        vn = vn_ref[:, cols]
        outs = []
        for hh in range(2):
            qm = jnp.where(lo if hh == 0 else jnp.logical_not(lo), q, 0.0)
            bias = bias_ref[2 * p + hh]
            s_c = dot(qm, kc, nt=True) + bias[:, :w]
            s_n = dot(qm, kn, nt=True) + bias[:, w:w + lq]
            outs.append(_softmax_pv([s_c, s_n], [vc, vn], dot))
        o_ref[:, cols] = jnp.where(lo, outs[0], outs[1]).astype(o_ref.dtype)


def attention_sample(proj, k_cache, v_cache, rel_bias, bsz, lq, dot, out_dtype):
    w = k_cache.shape[1]
    lkp = ((w + lq + LANES - 1) // LANES) * LANES
    tab = _band_tables(rel_bias, lq, lkp, PAST_LEN, PAST_LEN - w)
    kc = k_cache.reshape(bsz, w, A_WIDTH)
    vc = v_cache.reshape(bsz, w, A_WIDTH)
    return pl.pallas_call(
        functools.partial(_attn_sample_kernel, dot=dot, w=w),
        grid=(bsz,),
        in_specs=[pl.BlockSpec((lq, A_WIDTH), lambda b: (b, 0)),
                  pl.BlockSpec((lq, A_WIDTH), lambda b: (b, 1)),
                  pl.BlockSpec((lq, A_WIDTH), lambda b: (b, 2)),
                  pl.BlockSpec((1, w, A_WIDTH), lambda b: (b, 0, 0)),
                  pl.BlockSpec((1, w, A_WIDTH), lambda b: (b, 0, 0)),
                  pl.BlockSpec((A_HEADS, lq, lkp), lambda b: (0, 0, 0))],
        out_specs=pl.BlockSpec((lq, A_WIDTH), lambda b: (b, 0)),
        out_shape=jax.ShapeDtypeStruct((bsz * lq, A_WIDTH), out_dtype),
        compiler_params=_cparams(("arbitrary",)),
        name="attn_sample",
    )(proj, proj, proj, kc, vc, tab)


def _ssd_kernel(z_ref, x0_ref, x1_ref, x2_ref, dt_ref, cbuf_ref, h0_ref,
                cw_ref, cb_ref, dtb_ref, alog_ref, dsk_ref, ng_ref, exp_ref,
                y_ref, cnew_ref, hout_ref, xp_ref, st_ref, *, dot, q):
    c = pl.program_id(1)
    half = B_WIDTH // B_GROUPS

    @pl.when(c == 0)
    def _():
        xp_ref[0:SUBLANES, :] = cbuf_ref[0]
        st_ref[...] = h0_ref[0]

    xp_ref[SUBLANES:SUBLANES + q, 0:512] = x0_ref[...]
    xp_ref[SUBLANES:SUBLANES + q, 512:1024] = x1_ref[...]
    xp_ref[SUBLANES:SUBLANES + q, 1024:1536] = x2_ref[...]
    base = SUBLANES - (B_CONV - 1)
    conv = cb_ref[...] + xp_ref[base:base + q, :] * cw_ref[0:1, :]
    for tap in range(1, B_CONV):
        conv = conv + xp_ref[base + tap:base + tap + q, :] * cw_ref[tap:tap + 1, :]
    tail = xp_ref[q:q + SUBLANES, :]
    xp_ref[0:SUBLANES, :] = tail
    cnew_ref[0] = tail

    xbc = _silu(conv)
    xs = xbc[:, :B_WIDTH]
    bm = xbc[:, B_WIDTH:B_WIDTH + B_GROUPS * B_STATE]
    cm = xbc[:, B_WIDTH + B_GROUPS * B_STATE:]

    lane = lax.broadcasted_iota(jnp.int32, (1, LANES), 1)
    head_ok = lane < B_HEADS
    dt = jnp.where(head_ok, jax.nn.softplus(dt_ref[...] + dtb_ref[...]), 0.0)
    a_neg = -jnp.exp(alog_ref[...])
    row = lax.broadcasted_iota(jnp.int32, (q, q), 0)
    col = lax.broadcasted_iota(jnp.int32, (q, q), 1)
    tril = row >= col
    acs = _dot_sel(tril.astype(BF16), dt * a_neg)
    acs_t = acs.T
    eacs = jnp.exp(acs)
    to_end = jnp.exp(acs[q - 1:q, :] - acs)
    wide = _dot_rsel(jnp.concatenate([dt, eacs, to_end], axis=0), exp_ref[...])
    dt_w = wide[0:q]
    eacs_w = wide[q:2 * q]
    toend_w = wide[2 * q:3 * q]
    xdt = xs * dt_w
    xend = xdt * toend_w
    lo = lane < B_HEAD_DIM

    y_parts = []
    for g in range(B_GROUPS):
        bg = bm[:, g * B_STATE:(g + 1) * B_STATE]
        cg = cm[:, g * B_STATE:(g + 1) * B_STATE]
        gmat = dot(cg, bg, nt=True)
        y_off = dot(cg, st_ref[g]) * eacs_w[:, g * half:(g + 1) * half]
        for pr in range(half // LANES):
            cols = slice(g * half + pr * LANES, g * half + (pr + 1) * LANES)
            x_pair = xdt[:, cols]
            outs = []
            for hh in range(2):
                h = (g * half + pr * LANES) // B_HEAD_DIM + hh
                diff = acs[:, h:h + 1] - acs_t[h:h + 1, :]
                decay = jnp.exp(jnp.where(tril, diff, -jnp.inf))
                outs.append(dot(gmat * decay, x_pair))
            y_parts.append(jnp.where(lo, outs[0], outs[1]) + y_off[:, pr * LANES:(pr + 1) * LANES])
        st_new = dot(bg.T, xend[:, g * half:(g + 1) * half])
        st_ref[g] = st_ref[g] * eacs_w[q - 1:q, g * half:(g + 1) * half] + st_new
    hout_ref[0] = st_ref[...]

    y = jnp.concatenate(y_parts, axis=1) + dsk_ref[...] * xs
    y = y * _silu(z_ref[...])
    y = y * lax.rsqrt(jnp.mean(y * y, axis=-1, keepdims=True) + RMS_EPS) * ng_ref[...]
    y_ref[...] = y.astype(y_ref.dtype)


def ssd_mixer(proj, conv_buf, h0, conv_w, conv_b, dt_bias, a_log, d_skip, norm_g, bsz, seq, q, dot, out_dtype):
    t = bsz * seq
    nc = seq // q
    half = B_WIDTH // B_GROUPS
    cbuf = jnp.pad(conv_buf, ((0, 0), (SUBLANES - (B_CONV - 1), 0), (0, 0)))
    h0_t = jnp.transpose(h0.reshape(bsz, B_GROUPS, half, B_STATE), (0, 1, 3, 2))
    pad = lambda v: jnp.pad(v.reshape(1, -1), ((0, 0), (0, LANES - v.shape[-1])))
    expand = jnp.asarray(np.repeat(np.eye(LANES, B_HEADS, dtype=np.float32).T, B_HEAD_DIM, axis=0).T, BF16)
    dsk_w = jnp.repeat(d_skip, B_HEAD_DIM).reshape(1, B_WIDTH)
    zb, xb, db = 3 * A_WIDTH // B_WIDTH, (3 * A_WIDTH + B_WIDTH) // 512, (3 * A_WIDTH + B_WIDTH + B_CONV_DIM) // LANES
    const = lambda shp: pl.BlockSpec(shp, lambda b, c: (0,) * len(shp))
    y, cnew, hout = pl.pallas_call(
        functools.partial(_ssd_kernel, dot=dot, q=q),
        grid=(bsz, nc),
        in_specs=[pl.BlockSpec((q, B_WIDTH), lambda b, c: (b * nc + c, zb)),
                  pl.BlockSpec((q, 512), lambda b, c: (b * nc + c, xb)),
                  pl.BlockSpec((q, 512), lambda b, c: (b * nc + c, xb + 1)),
                  pl.BlockSpec((q, 512), lambda b, c: (b * nc + c, xb + 2)),
                  pl.BlockSpec((q, LANES), lambda b, c: (b * nc + c, db)),
                  pl.BlockSpec((1, SUBLANES, B_CONV_DIM), lambda b, c: (b, 0, 0)),
                  pl.BlockSpec((1, B_GROUPS, B_STATE, half), lambda b, c: (b, 0, 0, 0)),
                  const((B_CONV, B_CONV_DIM)), const((1, B_CONV_DIM)), const((1, LANES)), const((1, LANES)),
                  const((1, B_WIDTH)), const((1, B_WIDTH)), const((LANES, B_WIDTH))],
        out_specs=[pl.BlockSpec((q, B_WIDTH), lambda b, c: (b * nc + c, 0)),
                   pl.BlockSpec((1, SUBLANES, B_CONV_DIM), lambda b, c: (b, 0, 0)),
                   pl.BlockSpec((1, B_GROUPS, B_STATE, half), lambda b, c: (b, 0, 0, 0))],
        out_shape=[jax.ShapeDtypeStruct((t, B_WIDTH), out_dtype),
                   jax.ShapeDtypeStruct((bsz, SUBLANES, B_CONV_DIM), F32),
                   jax.ShapeDtypeStruct((bsz, B_GROUPS, B_STATE, half), F32)],
        scratch_shapes=[pltpu.VMEM((q + SUBLANES, B_CONV_DIM), F32),
                        pltpu.VMEM((B_GROUPS, B_STATE, half), F32)],
        compiler_params=_cparams(("arbitrary", "arbitrary")),
        name="ssd",
    )(proj, proj, proj, proj, proj, cbuf, h0_t,
      conv_w, conv_b.reshape(1, -1), pad(dt_bias), pad(a_log), dsk_w, norm_g.reshape(1, -1), expand)
    conv_new = cnew[:, SUBLANES - (B_CONV - 1):, :]
    h_final = jnp.transpose(hout, (0, 1, 3, 2)).reshape(bsz, B_HEADS, B_HEAD_DIM, B_STATE)
    return y, conv_new, h_final


def _s5_prep_kernel(lre_ref, lim_ref, ls_ref, bre_ref, bim_ref, are_ref, aim_ref, bbre_ref, bbim_ref):
    lre = lre_ref[...]
    lim = lim_ref[...]
    step = jnp.exp(ls_ref[...])
    mag = jnp.exp(lre * step)
    ang = lim * step
    ab_re = mag * jnp.cos(ang)
    ab_im = mag * jnp.sin(ang)
    den = lre * lre + lim * lim
    f_re = ((ab_re - 1.0) * lre + ab_im * lim) / den
    f_im = (ab_im * lre - (ab_re - 1.0) * lim) / den
    br = bre_ref[...]
    bi = bim_ref[...]
    are_ref[...] = ab_re
    aim_ref[...] = ab_im
    bbre_ref[...] = f_re * br - f_im * bi
    bbim_ref[...] = f_re * bi + f_im * br


def _cmul(ar, ai, br, bi):
    return ar * br - ai * bi, ar * bi + ai * br


def _gelu_tanh(x):
    return 0.5 * x * (1.0 + jnp.tanh(math.sqrt(2.0 / math.pi) * (x + 0.044715 * (x * x * x))))


def _s5_kernel(u_ref, bb_ref, ccre_ref, ccim_ref, a_ref, d_ref, h0_ref, y_ref, hout_ref,
               s_ref, sb_ref, pw_ref, pwb_ref, carry_ref, *, dot, seg, chain):
    t = pl.program_id(2)
    w = S5_JW
    sdt = sb_ref.dtype
    g = 2 if sdt == BF16 else 1
    grows = g * SUBLANES
    n_groups = seg // g
    unroll = min(n_groups, 4 // g)
    a_re = a_ref[0, 0:1, :]
    a_im = a_ref[0, 1:2, :]

    def group(k):
        return pl.ds(pl.multiple_of(k * grows, grows), grows)

    @pl.when(t == 0)
    def _():
        carry_ref[0:h0_ref.shape[2], :] = h0_ref[0, 0]
        pw_ref[0:1, 0:w] = a_re
        pw_ref[0:1, w:2 * w] = a_im

        def power(i, carry):
            pr, pi = _cmul(pw_ref[pl.ds(i - 1, 1), 0:w], pw_ref[pl.ds(i - 1, 1), w:2 * w], a_re, a_im)
            pw_ref[pl.ds(i, 1), 0:w] = pr
            pw_ref[pl.ds(i, 1), w:2 * w] = pi
            return carry

        lax.fori_loop(1, seg, power, 0)

        def spread(k, carry):
            rows = [jnp.broadcast_to(pw_ref[pl.ds(k * g + q, 1), :], (SUBLANES, 2 * w)) for q in range(g)]
            pwb_ref[group(k), :] = jnp.concatenate(rows, axis=0).astype(sdt)
            return carry

        lax.fori_loop(0, n_groups, spread, 0)

    ncb = u_ref.shape[0]
    up = jnp.concatenate(
        [jnp.concatenate([u_ref[cb, pl.ds(i, SUBLANES, stride=seg), :] for i in range(seg)], axis=0)
         for cb in range(ncb)], axis=1)
    s_ref[...] = dot(up, bb_ref[0])

    are_b = jnp.broadcast_to(a_re, (SUBLANES, w))
    aim_b = jnp.broadcast_to(a_im, (SUBLANES, w))

    def scan(k, st):
        sre, sim = st
        res, ims = [], []
        for q in range(g):
            rows = pl.ds(pl.multiple_of((k * g + q) * SUBLANES, SUBLANES), SUBLANES)
            pr, pi = _cmul(are_b, aim_b, sre, sim)
            sre = pr + s_ref[rows, 0:w]
            sim = pi + s_ref[rows, w:2 * w]
            res.append(sre)
            ims.append(sim)
        sb_ref[group(k), 0:w] = jnp.concatenate(res, axis=0).astype(sdt)
        sb_ref[group(k), w:2 * w] = jnp.concatenate(ims, axis=0).astype(sdt)
        return sre, sim

    zero = jnp.zeros((SUBLANES, w), F32)
    end_re, end_im = lax.fori_loop(0, n_groups, scan, (zero, zero), unroll=unroll)

    sg_re = pw_ref[seg - 1:seg, 0:w]
    sg_im = pw_ref[seg - 1:seg, w:2 * w]
    if chain:
        c_re = carry_ref[0:1, 0:w]
        c_im = carry_ref[0:1, w:2 * w]
        rows_re, rows_im = [], []
        for s in range(SUBLANES):
            rows_re.append(c_re)
            rows_im.append(c_im)
            pr, pi = _cmul(sg_re, sg_im, c_re, c_im)
            c_re = pr + end_re[s:s + 1, :]
            c_im = pi + end_im[s:s + 1, :]
        carry_ref[0:1, 0:w] = c_re
        carry_ref[0:1, w:2 * w] = c_im
        hout_ref[0, 0] = carry_ref[0:1, :]
        in_re = jnp.concatenate(rows_re, axis=0)
        in_im = jnp.concatenate(rows_im, axis=0)
    else:
        in_re = carry_ref[:, 0:w]
        in_im = carry_ref[:, w:2 * w]
        pr, pi = _cmul(sg_re, sg_im, in_re, in_im)
        carry_ref[:, 0:w] = pr + end_re
        carry_ref[:, w:2 * w] = pi + end_im
        hout_ref[0, 0] = carry_ref[...]

    in_re_g = jnp.concatenate([in_re] * g, axis=0).astype(sdt)
    in_im_g = jnp.concatenate([in_im] * g, axis=0).astype(sdt)

    def fixup(k, carry):
        rows = group(k)
        pr, pi = _cmul(pwb_ref[rows, 0:w], pwb_ref[rows, w:2 * w], in_re_g, in_im_g)
        sb_ref[rows, 0:w] = sb_ref[rows, 0:w] + pr
        sb_ref[rows, w:2 * w] = sb_ref[rows, w:2 * w] + pi
        return carry

    lax.fori_loop(0, n_groups, fixup, 0, unroll=unroll)

    y = dot(sb_ref[:, 0:w], ccre_ref[0]) - dot(sb_ref[:, w:2 * w], ccim_ref[0]) + d_ref[...] * up
    y = _gelu_tanh(y)
    for cb in range(ncb):
        for i in range(seg):
            y_ref[cb, pl.ds(i, SUBLANES, stride=seg), :] = (
                y[i * SUBLANES:(i + 1) * SUBLANES, cb * LANES:(cb + 1) * LANES])


def _block_diag(m):
    nj, j, r, c = m.shape
    eye = jnp.eye(j, dtype=m.dtype)
    return (m[:, :, :, None, :] * eye[None, :, None, :, None]).reshape(nj, j * r, j * c)


def s5_mixer(u, h0_re, h0_im, lam_re, lam_im, log_step, b_re, b_im, c_re, c_im, d_skip,
             bsz, seq, tm, dot, w_dtype, chain):
    t = bsz * seq
    g, p, ch = C_GROUPS, C_STATE, C_GROUP_CH
    rep = lambda v: jnp.repeat(v, ch, axis=-1)
    ls = jnp.broadcast_to(log_step[:, None], (g, p))
    pc = pl.BlockSpec((g, p * ch), lambda: (0, 0))
    shp = jax.ShapeDtypeStruct((g, p * ch), F32)
    a_re_x, a_im_x, bb_re, bb_im = pl.pallas_call(
        _s5_prep_kernel, in_specs=[pc] * 5, out_specs=[pc] * 4, out_shape=[shp] * 4, name="s5_prep",
    )(rep(lam_re), rep(lam_im), rep(ls), b_re.reshape(g, p * ch), b_im.reshape(g, p * ch))
    a_re = a_re_x[:, ::ch]
    a_im = a_im_x[:, ::ch]
    to_blk = lambda m: jnp.transpose(m.reshape(S5_NJ, S5_JBLK, p, ch), (0, 1, 3, 2))
    bb = jnp.concatenate([_block_diag(to_blk(bb_re)), _block_diag(to_blk(bb_im))], axis=-1).astype(w_dtype)
    cblk = lambda m: jnp.transpose(m.reshape(S5_NJ, S5_JBLK, ch, p), (0, 1, 3, 2))
    cc_re = _block_diag(cblk(c_re)).astype(w_dtype)
    cc_im = _block_diag(cblk(c_im)).astype(w_dtype)
    a_rows = jnp.stack([a_re.reshape(S5_NJ, S5_JW), a_im.reshape(S5_NJ, S5_JW)], axis=1)
    h0 = jnp.concatenate([h0_re.reshape(bsz, S5_NJ, S5_JW), h0_im.reshape(bsz, S5_NJ, S5_JW)], axis=-1)
    if chain:
        nb, srows, rows_per_b = bsz, 1, seq
        h0 = h0[:, :, None, :]
    else:
        assert bsz == SUBLANES and tm == bsz * seq
        nb, srows, rows_per_b = 1, SUBLANES, bsz * seq
        h0 = jnp.transpose(h0, (1, 0, 2))[None]
    nt = rows_per_b // tm
    seg = tm // SUBLANES
    uw = S5_JBLK * ch
    ncb = uw // LANES
    state_spec = pl.BlockSpec((1, 1, srows, 2 * S5_JW), lambda b, j, i: (b, j, 0, 0))
    y, hout = pl.pallas_call(
        functools.partial(_s5_kernel, dot=dot, seg=seg, chain=chain),
        grid=(nb, S5_NJ, nt),
        in_specs=[pl.BlockSpec((ncb, tm, LANES), lambda b, j, i: (j, b * nt + i, 0)),
                  pl.BlockSpec((1, uw, 2 * S5_JW), lambda b, j, i: (j, 0, 0)),
                  pl.BlockSpec((1, S5_JW, uw), lambda b, j, i: (j, 0, 0)),
                  pl.BlockSpec((1, S5_JW, uw), lambda b, j, i: (j, 0, 0)),
                  pl.BlockSpec((1, 2, S5_JW), lambda b, j, i: (j, 0, 0)),
                  pl.BlockSpec((1, uw), lambda b, j, i: (0, j)),
                  state_spec],
        out_specs=[pl.BlockSpec((ncb, tm, LANES), lambda b, j, i: (j, b * nt + i, 0)), state_spec],
        out_shape=[jax.ShapeDtypeStruct((g * ch // LANES, t, LANES), F32),
                   jax.ShapeDtypeStruct((nb, S5_NJ, srows, 2 * S5_JW), F32)],
        scratch_shapes=[pltpu.VMEM((tm, 2 * S5_JW), F32),
                        pltpu.VMEM((tm, 2 * S5_JW), w_dtype),
                        pltpu.VMEM((seg, 2 * S5_JW), F32),
                        pltpu.VMEM((tm, 2 * S5_JW), w_dtype),
                        pltpu.VMEM((SUBLANES, 2 * S5_JW), F32)],
        compiler_params=_cparams(("arbitrary", "arbitrary", "arbitrary")),
        name="s5_scan",
    )(u, bb, cc_re, cc_im, a_rows, d_skip.reshape(1, -1), h0)
    hout = hout[:, :, 0, :] if chain else jnp.transpose(hout[0], (1, 0, 2))
    s_re = hout[:, :, :S5_JW].reshape(bsz, g, p)
    s_im = hout[:, :, S5_JW:].reshape(bsz, g, p)
    return y, s_re, s_im


def _router_kernel(x_ref, sc_ref, sh_ref, w_ref, b_ref, o_ref, r_ref, cnt_ref, run_ref):
    h = x_ref[...] * (1.0 + sc_ref[...]) + sh_ref[...]
    logits = _dot3(h, w_ref[...]) + b_ref[...]
    lane = lax.broadcasted_iota(jnp.int32, logits.shape, 1).astype(F32)
    logits = jnp.where(lane < N_EXPERTS, logits, -jnp.inf)
    m1 = logits.max(axis=-1, keepdims=True)
    i1 = jnp.min(jnp.where(logits == m1, lane, float(LANES)), axis=-1, keepdims=True)
    rest = jnp.where(lane == i1, -jnp.inf, logits)
    m2 = rest.max(axis=-1, keepdims=True)
    i2 = jnp.min(jnp.where(rest == m2, lane, float(LANES)), axis=-1, keepdims=True)
    e2 = jnp.exp(m2 - m1)
    w1 = 1.0 / (1.0 + e2)
    w2 = e2 / (1.0 + e2)
    o_ref[...] = jnp.where(lane == i1, w1, 0.0) + jnp.where(lane == i2, w2, 0.0)

    step = pl.program_id(0)

    @pl.when(step == 0)
    def _():
        run_ref[...] = jnp.zeros_like(run_ref)

    tm = logits.shape[0]
    sel = jnp.where(jnp.logical_or(lane == i1, lane == i2), 1.0, 0.0)
    row = lax.broadcasted_iota(jnp.int32, (tm, tm), 0)
    col = lax.broadcasted_iota(jnp.int32, (tm, tm), 1)
    before = (row > col).astype(BF16)
    rank = _mxu(before, sel.astype(BF16)) + run_ref[0:1, :]
    pos1 = jnp.sum(jnp.where(lane == i1, rank, 0.0), axis=-1, keepdims=True)
    pos2 = jnp.sum(jnp.where(lane == i2, rank, 0.0), axis=-1, keepdims=True)
    run_ref[0:1, :] = run_ref[0:1, :] + jnp.sum(sel, axis=0, keepdims=True)
    cnt_ref[...] = run_ref[0:1, :]
    rec = jnp.zeros_like(logits)
    for k, val in enumerate((i1, i2, pos1, pos2, w1, w2)):
        rec = jnp.where(lane == float(k), val, rec)
    r_ref[...] = rec


ROUTE_E1, ROUTE_E2, ROUTE_POS1, ROUTE_POS2, ROUTE_W1, ROUTE_W2 = range(6)


def router(x, scale, shift, router_w, router_b):
    t, d = x.shape
    tm = min(t, 512)
    w = jnp.pad(router_w, ((0, 0), (0, LANES - N_EXPERTS)))
    b = jnp.pad(router_b.reshape(1, -1), ((0, 0), (0, LANES - N_EXPERTS)))
    return pl.pallas_call(
        _router_kernel,
        grid=(t // tm,),
        in_specs=[pl.BlockSpec((tm, d), lambda i: (i, 0)), _row_spec(scale, tm), _row_spec(shift, tm),
                  pl.BlockSpec((d, LANES), lambda i: (0, 0)), pl.BlockSpec((1, LANES), lambda i: (0, 0))],
        out_specs=[pl.BlockSpec((tm, LANES), lambda i: (i, 0)), pl.BlockSpec((tm, LANES), lambda i: (i, 0)),
                   pl.BlockSpec((1, LANES), lambda i: (0, 0))],
        out_shape=[jax.ShapeDtypeStruct((t, LANES), F32), jax.ShapeDtypeStruct((t, LANES), F32),
                   jax.ShapeDtypeStruct((1, LANES), F32)],
        scratch_shapes=[pltpu.VMEM((SUBLANES, LANES), F32)],
        compiler_params=_cparams(("arbitrary",)),
        name="router",
    )(x, scale, shift, w, b)


MOE_TILE = 256
MOE_UP_TN = D_FF_EXPERT // 2
DMA_ISSUE_UNROLL = 8


def _dispatch_kernel(fill_ref, d1_ref, d2_ref, h_ref, xs_ref, zero_ref, sem, zsem):
    tm = h_ref.shape[0]

    @pl.when(pl.program_id(0) == 0)
    def _():
        zero_ref[...] = jnp.zeros_like(zero_ref)

        def fill_copy(k):
            row0 = pl.multiple_of(jnp.maximum(fill_ref[k], 0), MOE_TILE)
            return pltpu.make_async_copy(zero_ref, xs_ref.at[pl.ds(row0, MOE_TILE)], zsem)

        for k in range(fill_ref.shape[0]):
            @pl.when(fill_ref[k] >= 0)
            def _():
                fill_copy(k).start()

        for k in range(fill_ref.shape[0]):
            @pl.when(fill_ref[k] >= 0)
            def _():
                fill_copy(k).wait()

    def issue(t, carry):
        src = h_ref.at[pl.ds(t, 1)]
        pltpu.make_async_copy(src, xs_ref.at[pl.ds(d1_ref[0, 0, t], 1)], sem).start()
        pltpu.make_async_copy(src, xs_ref.at[pl.ds(d2_ref[0, 0, t], 1)], sem).start()
        return carry

    lax.fori_loop(0, tm, issue, 0, unroll=DMA_ISSUE_UNROLL)
    pltpu.make_async_copy(xs_ref.at[pl.ds(0, 2 * tm)], xs_ref.at[pl.ds(0, 2 * tm)], sem).wait()


def moe_dispatch(h, dest1, dest2, fill_rows, n_rows):
    t, d = h.shape
    tm = 512
    idx_spec = pl.BlockSpec((1, 1, tm), lambda i, fr: (i, 0, 0), memory_space=pltpu.SMEM)
    grid_spec = pltpu.PrefetchScalarGridSpec(
        num_scalar_prefetch=1,
        grid=(t // tm,),
        in_specs=[idx_spec, idx_spec, pl.BlockSpec((tm, d), lambda i, fr: (i, 0))],
        out_specs=pl.BlockSpec(memory_space=pl.ANY),
        scratch_shapes=[pltpu.VMEM((MOE_TILE, d), h.dtype), pltpu.SemaphoreType.DMA(()),
                        pltpu.SemaphoreType.DMA(())])
    return pl.pallas_call(
        _dispatch_kernel,
        grid_spec=grid_spec,
        out_shape=jax.ShapeDtypeStruct((n_rows, d), h.dtype),
        compiler_params=_cparams(("arbitrary",)),
        name="moe_dispatch",
    )(fill_rows, dest1.reshape(t // tm, 1, tm), dest2.reshape(t // tm, 1, tm), h)


def _expert_up_kernel(te_ref, nv_ref, x_ref, w_ref, o_ref, *, dot):
    del te_ref

    @pl.when(pl.program_id(1) < nv_ref[0])
    def _():
        gu = dot(x_ref[...], w_ref[0])
        tn = o_ref.shape[1]
        o_ref[...] = (_silu(gu[:, :tn]) * gu[:, tn:]).astype(o_ref.dtype)

    @pl.when(pl.program_id(1) >= nv_ref[0])
    def _():
        o_ref[...] = jnp.zeros_like(o_ref)


def interleave_gate_up(w_up, tn):
    e_n, d, f2 = w_up.shape
    nf = f2 // 2 // tn
    return jnp.transpose(w_up.reshape(e_n, d, 2, nf, tn), (0, 1, 3, 2, 4)).reshape(e_n, d, f2)


def expert_up(xs, w_up_il, tile_expert, n_valid, dot, tn):
    r, d = xs.shape
    f = w_up_il.shape[2] // 2
    nf = f // tn
    nt = r // MOE_TILE
    grid_spec = pltpu.PrefetchScalarGridSpec(
        num_scalar_prefetch=2,
        grid=(nf, nt),
        in_specs=[pl.BlockSpec((MOE_TILE, d), lambda j, i, te, nv: (i, 0)),
                  pl.BlockSpec((1, d, 2 * tn), lambda j, i, te, nv: (te[i], 0, j))],
        out_specs=pl.BlockSpec((MOE_TILE, tn), lambda j, i, te, nv: (i, j)))
    return pl.pallas_call(
        functools.partial(_expert_up_kernel, dot=dot),
        grid_spec=grid_spec,
        out_shape=jax.ShapeDtypeStruct((r, f), BF16),
        compiler_params=_cparams(("arbitrary", "arbitrary")),
        name="expert_up",
    )(tile_expert, n_valid, xs, w_up_il)


def _expert_down_kernel(te_ref, nv_ref, a_ref, w_ref, o_ref, *, dot):
    del te_ref

    @pl.when(pl.program_id(0) < nv_ref[0])
    def _():
        o_ref[...] = dot(a_ref[...], w_ref[0])

    @pl.when(pl.program_id(0) >= nv_ref[0])
    def _():
        o_ref[...] = jnp.zeros_like(o_ref)


def expert_down(act, w_down, tile_expert, n_valid, dot):
    r, f = act.shape
    d = w_down.shape[2]
    grid_spec = pltpu.PrefetchScalarGridSpec(
        num_scalar_prefetch=2,
        grid=(r // MOE_TILE,),
        in_specs=[pl.BlockSpec((MOE_TILE, f), lambda i, te, nv: (i, 0)),
                  pl.BlockSpec((1, f, d), lambda i, te, nv: (te[i], 0, 0))],
        out_specs=pl.BlockSpec((MOE_TILE, d), lambda i, te, nv: (i, 0)))
    return pl.pallas_call(
        functools.partial(_expert_down_kernel, dot=dot),
        grid_spec=grid_spec,
        out_shape=jax.ShapeDtypeStruct((r, d), F32),
        compiler_params=_cparams(("arbitrary",)),
        name="expert_down",
    )(tile_expert, n_valid, act, w_down)


def _combine_ln_kernel(d1_ref, d2_ref, d1n_ref, d2n_ref, y_ref, rec_ref, x_ref, gate_ref, g_ref, b_ref, xo_ref,
                       buf1, buf2, sem):
    tm = x_ref.shape[0]
    step = pl.program_id(0)
    slot = step % 2

    def gather(idx1, idx2, to):
        def issue(t, carry):
            pltpu.make_async_copy(y_ref.at[pl.ds(idx1[0, 0, t], 1)], buf1.at[to, pl.ds(t, 1)], sem.at[to]).start()
            pltpu.make_async_copy(y_ref.at[pl.ds(idx2[0, 0, t], 1)], buf2.at[to, pl.ds(t, 1)], sem.at[to]).start()
            return carry

        lax.fori_loop(0, tm, issue, 0, unroll=DMA_ISSUE_UNROLL)

    @pl.when(step == 0)
    def _():
        gather(d1_ref, d2_ref, 0)

    @pl.when(step + 1 < pl.num_programs(0))
    def _():
        gather(d1n_ref, d2n_ref, 1 - slot)

    pltpu.make_async_copy(y_ref.at[pl.ds(0, tm)], buf1.at[slot], sem.at[slot]).wait()
    pltpu.make_async_copy(y_ref.at[pl.ds(0, tm)], buf2.at[slot], sem.at[slot]).wait()

    rec = rec_ref[...]
    lane = lax.broadcasted_iota(jnp.int32, rec.shape, 1)
    w1 = jnp.sum(jnp.where(lane == ROUTE_W1, rec, 0.0), axis=-1, keepdims=True)
    w2 = jnp.sum(jnp.where(lane == ROUTE_W2, rec, 0.0), axis=-1, keepdims=True)
    out = w1 * buf1[slot] + w2 * buf2[slot]
    r = ALPHA * x_ref[...] + (1.0 + gate_ref[...]) * out
    mu = jnp.mean(r, axis=-1, keepdims=True)
    dev = r - mu
    var = jnp.mean(dev * dev, axis=-1, keepdims=True)
    xo_ref[...] = dev * lax.rsqrt(var + LN_EPS) * g_ref[...] + b_ref[...]


def moe_combine_ln(y_sorted, dest1, dest2, rec, x, gate, ln_g, ln_b):
    t, d = x.shape
    tm = 256
    nt = t // tm
    idx_spec = pl.BlockSpec((1, 1, tm), lambda i: (i, 0, 0), memory_space=pltpu.SMEM)
    nxt_spec = pl.BlockSpec((1, 1, tm), lambda i: (jnp.minimum(i + 1, nt - 1), 0, 0), memory_space=pltpu.SMEM)
    d1 = dest1.reshape(nt, 1, tm)
    d2 = dest2.reshape(nt, 1, tm)
    return pl.pallas_call(
        _combine_ln_kernel,
        grid=(nt,),
        in_specs=[idx_spec, idx_spec, nxt_spec, nxt_spec, pl.BlockSpec(memory_space=pl.ANY),
                  pl.BlockSpec((tm, LANES), lambda i: (i, 0)), pl.BlockSpec((tm, d), lambda i: (i, 0)),
                  _row_spec(gate, tm), pl.BlockSpec((1, d), lambda i: (0, 0)), pl.BlockSpec((1, d), lambda i: (0, 0))],
        out_specs=pl.BlockSpec((tm, d), lambda i: (i, 0)),
        out_shape=jax.ShapeDtypeStruct((t, d), F32),
        scratch_shapes=[pltpu.VMEM((2, tm, d), F32), pltpu.VMEM((2, tm, d), F32), pltpu.SemaphoreType.DMA((2,))],
        compiler_params=_cparams(("arbitrary",)),
        name="moe_combine_ln",
    )(d1, d2, d1, d2, y_sorted, rec, x, gate, ln_g.reshape(1, d), ln_b.reshape(1, d))


def moe_top2(h, x, rec, counts, gate, ln_g, ln_b, w_up, w_down, dot):
    t, d = h.shape
    n_tiles = (TOP_K * t) // MOE_TILE + N_EXPERTS
    n_rows = n_tiles * MOE_TILE
    cnt = counts[0, :N_EXPERTS].astype(jnp.int32)
    padded = ((cnt + MOE_TILE - 1) // MOE_TILE) * MOE_TILE
    ends = jnp.cumsum(padded)
    starts = ends - padded
    col = lambda k: rec[:, k].astype(jnp.int32)
    experts = jnp.arange(N_EXPERTS, dtype=jnp.int32)[None, :]
    start_of = lambda e: jnp.sum(jnp.where(e[:, None] == experts, starts[None, :], 0), axis=1)
    dest1 = start_of(col(ROUTE_E1)) + col(ROUTE_POS1)
    dest2 = start_of(col(ROUTE_E2)) + col(ROUTE_POS2)
    tile_start = jnp.arange(n_tiles, dtype=jnp.int32) * MOE_TILE
    tile_expert = jnp.minimum(jnp.sum(tile_start[:, None] >= ends[None, :], axis=1), N_EXPERTS - 1).astype(jnp.int32)
    n_valid = (ends[-1:] // MOE_TILE).astype(jnp.int32)
    last_tile = jnp.where(padded > 0, ends - MOE_TILE, -1)
    spare = ends[-1] + jnp.arange(N_EXPERTS, dtype=jnp.int32) * MOE_TILE
    fill_rows = jnp.concatenate([last_tile, jnp.where(spare < n_rows, spare, -1)]).astype(jnp.int32)
    xs = moe_dispatch(h, dest1, dest2, fill_rows, n_rows)
    act = expert_up(xs, w_up, tile_expert, n_valid, dot, tn=MOE_UP_TN)
    y_sorted = expert_down(act, w_down, tile_expert, n_valid, dot)
    return moe_combine_ln(y_sorted, dest1, dest2, rec, x, gate, ln_g, ln_b)


def _trunk(x, mods, caches, wts, precise):
    bsz, seq, d = x.shape
    t = bsz * seq
    prompt = caches is None
    dot = _dot3 if precise else _dot1
    act_dtype = F32 if precise else BF16
    tm = min(t, 512)
    x0 = x.reshape(t, d)

    sh, sc, gt = mods[0][0]
    proj = matmul(x0, wts["w_in0"], dot, mod=(sc, sh), mod_dtype=act_dtype)
    keep = min(A_WINDOW, seq) if prompt else seq
    kv_rows = proj.reshape(bsz, seq, IN0_PAD)[:, seq - keep:, A_WIDTH:3 * A_WIDTH]
    k_new = kv_rows[..., :A_WIDTH].reshape(bsz, keep, A_HEADS, A_HEAD_DIM)
    v_new = kv_rows[..., A_WIDTH:].reshape(bsz, keep, A_HEADS, A_HEAD_DIM)
    if prompt:
        att = attention_prompt(proj, wts["rel_bias"], dot, act_dtype)
        conv_buf = jnp.zeros((bsz, B_CONV - 1, B_CONV_DIM), F32)
        ssm_h0 = jnp.zeros((bsz, B_HEADS, B_HEAD_DIM, B_STATE), F32)
        q_len = min(seq, 256)
    else:
        att = attention_sample(proj, caches["k"], caches["v"], wts["rel_bias"], bsz, seq, dot, act_dtype)
        conv_buf, ssm_h0 = caches["conv"], caches["ssm"]
        q_len = seq
    y_ssd, conv_new, ssm_new = ssd_mixer(proj, conv_buf, ssm_h0, wts["conv_w"], wts["conv_b"], wts["dt_bias"],
                                         wts["a_log"], wts["ssd_d"], wts["ssd_norm_g"], bsz, seq, q_len,
                                         dot, act_dtype)
    x1, h1 = matmul_ln([att, y_ssd], wts["w_out0"], x0, gt, wts["ln_g"][0, 0], wts["ln_b"][0, 0], dot, tm,
                       nxt=(mods[0][1][1], mods[0][1][0]), next_dtype=act_dtype)
    act = ffn_up(h1, wts["ffn_w_up"], dot, act_dtype, tm=min(t, 1024), tn=512)
    x2, h2 = matmul_ln([act], wts["ffn_w_down"], x1, mods[0][1][2], wts["ln_g"][0, 1], wts["ln_b"][0, 1], dot,
                       min(t, 256), tk=None if prompt else 1408, nxt=(mods[1][0][1], mods[1][0][0]), next_dtype=act_dtype)

    u = matmul(h2, wts["w_in1"], dot, split_cols=True)
    if prompt:
        s5_re0 = jnp.zeros((bsz, C_GROUPS, C_STATE), F32)
        s5_im0 = jnp.zeros((bsz, C_GROUPS, C_STATE), F32)
    else:
        s5_re0, s5_im0 = caches["s5_re"], caches["s5_im"]
    y5, s5_re, s5_im = s5_mixer(u, s5_re0, s5_im0, wts["s5_lam_re"], wts["s5_lam_im"], wts["s5_log_step"],
                                wts["s5_b_re"], wts["s5_b_im"], wts["s5_c_re"], wts["s5_c_im"], wts["s5_d"],
                                bsz, seq, 512 if prompt else bsz * seq, dot, F32 if precise else BF16,
                                chain=prompt)
    x3, h3 = matmul_ln([y5], wts["glu_w"], x2, mods[1][0][2], wts["ln_g"][1, 0], wts["ln_b"][1, 0], dot,
                       min(t, 256), tk=None if prompt else 512, glu=True,
                       nxt=(mods[1][1][1], mods[1][1][0]), next_dtype=F32)
    gates, rec, counts = router(x3, mods[1][1][1], mods[1][1][0], wts["router_w"], wts["router_b"])
    if prompt:
        x4 = moe_top2(h3, x3, rec, counts, mods[1][1][2], wts["ln_g"][1, 1], wts["ln_b"][1, 1],
                      wts["moe_w_up_il"], wts["moe_w_down"], dot)
    else:
        act = ffn_up(h3, wts["moe_w_up_il"], _dot1, BF16, tm=min(t, 2048), tn=LANES, gates=gates, chunk=MOE_UP_TN)
        x4, _ = matmul_ln([act], wts["moe_w_down"], x3, mods[1][1][2], wts["ln_g"][1, 1], wts["ln_b"][1, 1],
                          _dot1, tm, tk=D_FF_EXPERT)
    return (x4.reshape(bsz, seq, d), k_new[None], v_new[None], conv_new[None], ssm_new[None],
            s5_re[None], s5_im[None])


def kernel(x_prompt, x_sample, cache_attn_k, cache_attn_v, state_ssd_conv, state_ssd, state_s5_re, state_s5_im, c_prompt, c_sample, ada_w, ada_b, ln_g, ln_b, w_in0, w_out0, rel_bias, conv_w, conv_b, dt_bias, a_log, ssd_d, ssd_norm_g, ffn_w_up, ffn_w_down, w_in1, s5_lam_re, s5_lam_im, s5_log_step, s5_b_re, s5_b_im, s5_c_re, s5_c_im, s5_d, glu_w, router_w, router_b, moe_w_up, moe_w_down):
    d = D_MODEL
    bp, lp, _ = x_prompt.shape
    bs, ls, _ = x_sample.shape

    n_c = bp + bs
    rows = ((n_c + SUBLANES - 1) // SUBLANES) * SUBLANES
    c_all = jnp.pad(jnp.concatenate([c_prompt, c_sample], axis=0), ((0, rows - n_c), (0, 0)))
    mod = adaln_all(c_all, ada_w, ada_b).reshape(DEPTH, 2, rows, 3, d)

    def mods_for(r0, nb, per_row):
        out = []
        for layer in range(DEPTH):
            out.append([])
            for j in range(2):
                trip = []
                for part in range(3):
                    m = mod[layer, j, r0:r0 + nb, part]
                    trip.append(jnp.repeat(m, per_row, axis=0) if nb > 1 else m)
                out[-1].append(tuple(trip))
        return out

    assert bp == 1
    mods_p = mods_for(0, bp, lp)
    mods_s = mods_for(bp, bs, ls)

    in0_pad = ((0, 0), (0, IN0_PAD - IN0_WIDTH))
    shared = dict(ln_g=ln_g, ln_b=ln_b, rel_bias=rel_bias[0], conv_w=conv_w[0], conv_b=conv_b[0],
                  dt_bias=dt_bias[0], a_log=a_log[0], ssd_d=ssd_d[0], ssd_norm_g=ssd_norm_g[0],
                  s5_lam_re=s5_lam_re[0], s5_lam_im=s5_lam_im[0], s5_log_step=s5_log_step[0],
                  s5_b_re=s5_b_re[0], s5_b_im=s5_b_im[0], s5_c_re=s5_c_re[0], s5_c_im=s5_c_im[0], s5_d=s5_d[0],
                  router_w=router_w[0], router_b=router_b[0])
    big = dict(w_in0=jnp.pad(w_in0[0], in0_pad), w_out0=w_out0[0], ffn_w_up=ffn_w_up, ffn_w_down=ffn_w_down[0],
               w_in1=w_in1[0], glu_w=glu_w[0])
    moe = dict(moe_w_up_il=interleave_gate_up(moe_w_up[0], MOE_UP_TN).astype(BF16), moe_w_down=moe_w_down[0].astype(BF16))
    wts_p = dict(shared, **moe, **{k: v.astype(BF16) for k, v in big.items()})
    wts_s = dict(shared, **moe, **big)

    y_p, k_p, v_p, conv_p, ssd_p, re_p, im_p = _trunk(x_prompt, mods_p, None, wts_p, precise=False)
    caches = dict(k=cache_attn_k[0], v=cache_attn_v[0], conv=state_ssd_conv[0], ssm=state_ssd[0],
                  s5_re=state_s5_re[0], s5_im=state_s5_im[0])
    y_s, k_s, v_s, conv_s, ssd_s, re_s, im_s = _trunk(x_sample, mods_s, caches, wts_s, precise=True)
    return (y_p, y_s, k_p, v_p, conv_p, ssd_p, re_p, im_p, k_s, v_s, conv_s, ssd_s, re_s, im_s)
```

```python
import functools
import math

import numpy as np
import jax
import jax.numpy as jnp
from jax import lax
from jax.experimental import pallas as pl
from jax.experimental.pallas import tpu as pltpu

F32 = jnp.float32
BF16 = jnp.bfloat16

D_MODEL = 2048
DEPTH = 2
PAST_LEN = 2048
CHUNK = 64
A_HEADS = 16
A_HEAD_DIM = 64
A_WIDTH = A_HEADS * A_HEAD_DIM
A_PAST_CHUNKS = 8
A_WINDOW = A_PAST_CHUNKS * CHUNK
REL_CLIP = 128
B_HEADS = 16
B_HEAD_DIM = 64
B_WIDTH = B_HEADS * B_HEAD_DIM
B_GROUPS = 2
B_STATE = 128
B_CONV = 4
B_CONV_DIM = B_WIDTH + 2 * B_GROUPS * B_STATE
C_GROUP_CH = 16
C_GROUPS = D_MODEL // C_GROUP_CH
C_STATE = 64
D_FF = 5632
N_EXPERTS = 8
TOP_K = 2
D_FF_EXPERT = 2816
ALPHA = (2.0 * DEPTH) ** 0.25
LN_EPS = 1e-5
RMS_EPS = 1e-5
NEG_INF = -1e30
IN0_WIDTH = 3 * A_WIDTH + B_WIDTH + B_CONV_DIM + B_HEADS
IN0_PAD = 6144

LANES = 128
SUBLANES = 8
VMEM_LIMIT = 56 * 1024 * 1024

LN_SUBROWS = 256
ATT_QBLOCK = 512
ATT_QSUB = 256
ATT_KSUB = ATT_QSUB + A_WINDOW
S5_JBLK = 16
S5_NJ = C_GROUPS // S5_JBLK
S5_JW = S5_JBLK * C_STATE


def _cparams(sem):
    return pltpu.CompilerParams(dimension_semantics=sem, vmem_limit_bytes=VMEM_LIMIT)


def _split_bf16(x):
    hi = x.astype(BF16)
    lo = (x - hi.astype(F32)).astype(BF16)
    return hi, lo


def _mxu(a, b, nt=False):
    if nt:
        return lax.dot_general(a, b, (((1,), (1,)), ((), ())), preferred_element_type=F32)
    return jnp.dot(a, b, preferred_element_type=F32)


def _dot1(a, b, nt=False):
    return _mxu(a.astype(BF16), b.astype(BF16), nt)


def _dot3(a, b, nt=False):
    ah, al = _split_bf16(a.astype(F32))
    bh, bl = _split_bf16(b.astype(F32))
    return _mxu(ah, bh, nt) + (_mxu(ah, bl, nt) + _mxu(al, bh, nt))


def _dot_sel(sel_bf16, x):
    x1 = x.astype(BF16)
    r1 = x - x1.astype(F32)
    x2 = r1.astype(BF16)
    x3 = (r1 - x2.astype(F32)).astype(BF16)
    return _mxu(sel_bf16, x1) + (_mxu(sel_bf16, x2) + _mxu(sel_bf16, x3))


def _dot_rsel(x, sel_bf16):
    x1 = x.astype(BF16)
    r1 = x - x1.astype(F32)
    x2 = r1.astype(BF16)
    x3 = (r1 - x2.astype(F32)).astype(BF16)
    return _mxu(x1, sel_bf16) + (_mxu(x2, sel_bf16) + _mxu(x3, sel_bf16))


def _silu(x):
    return x * jax.nn.sigmoid(x)


def _row_spec(arr, tm):
    d = arr.shape[-1]
    if arr.shape[0] == 1:
        return pl.BlockSpec((1, d), lambda i, *_: (0, 0))
    return pl.BlockSpec((tm, d), lambda i, *_: (i, 0))


def _adaln_kernel(c_ref, w_ref, b_ref, o_ref):
    c = c_ref[...]
    o_ref[0] = _dot3(_silu(c), w_ref[0]) + b_ref[0]


def adaln_all(c_rows, ada_w, ada_b):
    r, d = c_rows.shape
    n = ada_w.shape[-1]
    tn = 512
    w = ada_w.reshape(2 * DEPTH, d, n)
    b = ada_b.reshape(2 * DEPTH, 1, n)
    return pl.pallas_call(
        _adaln_kernel,
        grid=(2 * DEPTH, n // tn),
        in_specs=[pl.BlockSpec((r, d), lambda i, j: (0, 0)),
                  pl.BlockSpec((1, d, tn), lambda i, j: (i, 0, j)),
                  pl.BlockSpec((1, 1, tn), lambda i, j: (i, 0, j))],
        out_specs=pl.BlockSpec((1, r, tn), lambda i, j: (i, 0, j)),
        out_shape=jax.ShapeDtypeStruct((2 * DEPTH, r, n), F32),
        compiler_params=_cparams(("arbitrary", "arbitrary")),
        name="adaln",
    )(c_rows, w, b)


def _mm_kernel(*refs, dot, split_cols, modulated):
    if modulated:
        x_ref, sc_ref, sh_ref, w_ref, o_ref, h_ref = refs

        @pl.when(pl.program_id(1) == 0)
        def _():
            h_ref[...] = (x_ref[...] * (1.0 + sc_ref[...]) + sh_ref[...]).astype(h_ref.dtype)

        a = h_ref[...]
    else:
        a_ref, w_ref, o_ref = refs
        a = a_ref[...]
    res = dot(a, w_ref[...]).astype(o_ref.dtype)
    if split_cols:
        for cb in range(o_ref.shape[0]):
            o_ref[cb] = res[:, cb * LANES:(cb + 1) * LANES]
    else:
        o_ref[...] = res


def matmul(a, w, dot, tn=1024, split_cols=False, mod=None, mod_dtype=None):
    t, k = a.shape
    n = w.shape[1]
    tm = min(t, 1024)
    if split_cols:
        out_spec = pl.BlockSpec((tn // LANES, tm, LANES), lambda i, j: (j, i, 0))
        out_shape = jax.ShapeDtypeStruct((n // LANES, t, LANES), F32)
    else:
        out_spec = pl.BlockSpec((tm, tn), lambda i, j: (i, j))
        out_shape = jax.ShapeDtypeStruct((t, n), F32)
    in_specs = [pl.BlockSpec((tm, k), lambda i, j: (i, 0))]
    args = [a]
    scratch = []
    if mod is not None:
        in_specs += [_row_spec(mod[0], tm), _row_spec(mod[1], tm)]
        args += [mod[0], mod[1]]
        scratch = [pltpu.VMEM((tm, k), mod_dtype)]
    in_specs.append(pl.BlockSpec((k, tn), lambda i, j: (0, j)))
    args.append(w)
    return pl.pallas_call(
        functools.partial(_mm_kernel, dot=dot, split_cols=split_cols, modulated=mod is not None),
        grid=(t // tm, n // tn),
        in_specs=in_specs,
        out_specs=out_spec,
        out_shape=out_shape,
        scratch_shapes=scratch,
        compiler_params=_cparams(("arbitrary", "arbitrary")),
        name="matmul",
    )(*args)


def _ffn_up_kernel(*refs, dot, gated):
    if gated:
        a_ref, wg_ref, wu_ref, gates_ref, o_ref = refs
    else:
        a_ref, wg_ref, wu_ref, o_ref = refs
    a = a_ref[...]
    g = dot(a, wg_ref[0])
    u = dot(a, wu_ref[0])
    act = _silu(g) * u
    if gated:
        e = pl.program_id(1)
        gt = gates_ref[...]
        lane = lax.broadcasted_iota(jnp.int32, gt.shape, 1)
        act = act * jnp.sum(jnp.where(lane == e, gt, 0.0), axis=1, keepdims=True)
    o_ref[...] = act.astype(o_ref.dtype)


def ffn_up(a, w_up, dot, out_dtype, tm, tn, gates=None):
    t, d = a.shape
    e_n, _, f2 = w_up.shape
    f = f2 // 2
    nf = f // tn
    in_specs = [pl.BlockSpec((tm, d), lambda i, e, j: (i, 0)),
                pl.BlockSpec((1, d, tn), lambda i, e, j: (e, 0, j)),
                pl.BlockSpec((1, d, tn), lambda i, e, j: (e, 0, j + nf))]
    args = [a, w_up, w_up]
    if gates is not None:
        in_specs.append(pl.BlockSpec((tm, LANES), lambda i, e, j: (i, 0)))
        args.append(gates)
    return pl.pallas_call(
        functools.partial(_ffn_up_kernel, dot=dot, gated=gates is not None),
        grid=(t // tm, e_n, nf),
        in_specs=in_specs,
        out_specs=pl.BlockSpec((tm, tn), lambda i, e, j: (i, e * nf + j)),
        out_shape=jax.ShapeDtypeStruct((t, e_n * f), out_dtype),
        compiler_params=_cparams(("arbitrary", "arbitrary", "arbitrary")),
        name="ffn_up",
    )(*args)


def _mm_ln_kernel(*refs, n_a, nk, dot, glu, has_next):
    a_refs = refs[:n_a]
    w_refs = refs[n_a:2 * n_a]
    pos = 2 * n_a
    x_ref, gate_ref, g_ref, b_ref = refs[pos:pos + 4]
    pos += 4
    if has_next:
        sc_ref, sh_ref = refs[pos:pos + 2]
        pos += 2
    xo_ref = refs[pos]
    pos += 1
    if has_next:
        ho_ref = refs[pos]
        pos += 1
    acc_ref = refs[pos] if nk > 1 else None

    tm = x_ref.shape[0]
    sub = min(tm, max(LANES, min(LN_SUBROWS, tm // 2)))

    def load_a(a_ref, rows):
        if len(a_ref.shape) == 3:
            return jnp.concatenate([a_ref[cb, rows, :] for cb in range(a_ref.shape[0])], axis=1)
        return a_ref[rows, :]

    def product(rows):
        w0 = w_refs[0][0] if len(w_refs[0].shape) == 3 else w_refs[0][...]
        part = dot(load_a(a_refs[0], rows), w0)
        for a_ref, w_ref in zip(a_refs[1:], w_refs[1:]):
            part = part + dot(load_a(a_ref, rows), w_ref[...])
        return part

    def per_row(ref, rows):
        return ref[...] if ref.shape[0] == 1 else ref[rows, :]

    def epilogue(acc, rows):
        if glu:
            d = acc.shape[1] // 2
            out = acc[:, :d] * jax.nn.sigmoid(acc[:, d:])
        else:
            out = acc
        r = ALPHA * x_ref[rows, :] + (1.0 + per_row(gate_ref, rows)) * out
        mu = jnp.mean(r, axis=-1, keepdims=True)
        dev = r - mu
        var = jnp.mean(dev * dev, axis=-1, keepdims=True)
        xn = dev * lax.rsqrt(var + LN_EPS) * g_ref[...] + b_ref[...]
        xo_ref[rows, :] = xn
        if has_next:
            ho_ref[rows, :] = (xn * (1.0 + per_row(sc_ref, rows)) + per_row(sh_ref, rows)).astype(ho_ref.dtype)

    blocks = [slice(r0, r0 + sub) for r0 in range(0, tm, sub)]
    if nk == 1:
        for rows in blocks:
            epilogue(product(rows), rows)
    else:
        k = pl.program_id(1)
        part = product(slice(0, tm))

        @pl.when(k == 0)
        def _():
            acc_ref[...] = part

        @pl.when(k > 0)
        def _():
            acc_ref[...] += part

        @pl.when(k == nk - 1)
        def _():
            for rows in blocks:
                epilogue(acc_ref[rows, :], rows)


def matmul_ln(a_list, w, x, gate, ln_g, ln_b, dot, tm, tk=None, glu=False, nxt=None, next_dtype=None):
    t, d = x.shape
    n = w.shape[-1]
    n_a = len(a_list)
    if n_a > 1:
        nk = 1
        in_specs = [pl.BlockSpec((tm, a.shape[1]), lambda i, k: (i, 0)) for a in a_list]
        off = 0
        for a in a_list:
            ka = a.shape[1]
            assert off % ka == 0
            in_specs.append(pl.BlockSpec((ka, n), functools.partial(lambda i, k, o: (o, 0), o=off // ka),
                                         pipeline_mode=pl.Buffered(1)))
            off += ka
        w_args = [w] * n_a
    else:
        a0 = a_list[0]
        split_cols = a0.ndim == 3
        ktot = a0.shape[0] * LANES if split_cols else a0.shape[1]
        tk = ktot if tk is None else tk
        nk = ktot // tk
        if split_cols:
            a_spec = pl.BlockSpec((tk // LANES, tm, LANES), lambda i, k: (k, i, 0))
        else:
            a_spec = pl.BlockSpec((tm, tk), lambda i, k: (i, k))
        w_mode = dict(pipeline_mode=pl.Buffered(1)) if nk == 1 else {}
        if w.ndim == 3:
            assert w.shape[1] == tk
            w_spec = pl.BlockSpec((1, tk, n), lambda i, k: (k, 0, 0))
        else:
            w_spec = pl.BlockSpec((tk, n), lambda i, k: (k, 0), **w_mode)
        in_specs = [a_spec, w_spec]
        w_args = [w]
    in_specs += [pl.BlockSpec((tm, d), lambda i, k: (i, 0)), _row_spec(gate, tm),
                 pl.BlockSpec((1, d), lambda i, k: (0, 0)), pl.BlockSpec((1, d), lambda i, k: (0, 0))]
    args = list(a_list) + w_args + [x, gate, ln_g.reshape(1, d), ln_b.reshape(1, d)]
    out_specs = [pl.BlockSpec((tm, d), lambda i, k: (i, 0))]
    out_shape = [jax.ShapeDtypeStruct((t, d), F32)]
    if nxt is not None:
        in_specs += [_row_spec(nxt[0], tm), _row_spec(nxt[1], tm)]
        args += [nxt[0], nxt[1]]
        out_specs.append(pl.BlockSpec((tm, d), lambda i, k: (i, 0)))
        out_shape.append(jax.ShapeDtypeStruct((t, d), next_dtype))
    scratch = [pltpu.VMEM((tm, n), F32)] if nk > 1 else []
    res = pl.pallas_call(
        functools.partial(_mm_ln_kernel, n_a=n_a, nk=nk, dot=dot, glu=glu, has_next=nxt is not None),
        grid=(t // tm, nk),
        in_specs=in_specs,
        out_specs=out_specs,
        out_shape=out_shape,
        scratch_shapes=scratch,
        compiler_params=_cparams(("arbitrary", "arbitrary")),
        name="matmul_ln",
    )(*args)
    return res if nxt is not None else (res[0], None)


def _softmax_pv(scores, values, dot):
    m = scores[0].max(axis=-1, keepdims=True)
    for s in scores[1:]:
        m = jnp.maximum(m, s.max(axis=-1, keepdims=True))
    num = None
    den = None
    for s, v in zip(scores, values):
        p = jnp.exp(s - m)
        l = p.sum(axis=-1, keepdims=True)
        o = dot(p, v)
        num = o if num is None else num + o
        den = l if den is None else den + l
    return num / den


def _attn_prompt_kernel(q_ref, kp_ref, kc_ref, vp_ref, vc_ref, bias_ref, o_ref, *, dot):
    i = pl.program_id(0)
    lo = lax.broadcasted_iota(jnp.int32, (1, LANES), 1) < A_HEAD_DIM
    q = q_ref[...] * (A_HEAD_DIM ** -0.5)
    k = jnp.concatenate([kp_ref[...], kc_ref[...]], axis=0)
    v = jnp.concatenate([vp_ref[...], vc_ref[...]], axis=0)
    kidx = lax.broadcasted_iota(jnp.int32, (1, ATT_KSUB), 1)
    for sub in range(ATT_QBLOCK // ATT_QSUB):
        r0 = sub * ATT_QSUB
        qs = q[r0:r0 + ATT_QSUB]
        ks = k[r0:r0 + ATT_KSUB]
        vs = v[r0:r0 + ATT_KSUB]
        before = jnp.where(jnp.logical_and(i == 0, kidx + r0 < ATT_QBLOCK), NEG_INF, 0.0)
        outs = []
        for hh in range(2):
            qm = jnp.where(lo if hh == 0 else jnp.logical_not(lo), qs, 0.0)
            s = dot(qm, ks, nt=True) + bias_ref[0, hh] + before
            outs.append(_softmax_pv([s], [vs], dot))
        o_ref[r0:r0 + ATT_QSUB, :] = jnp.where(lo, outs[0], outs[1]).astype(o_ref.dtype)


def _band_table_kernel(base_ref, allow_ref, o_ref):
    lq, lk = o_ref.shape[1], o_ref.shape[2]
    wide = jnp.broadcast_to(base_ref[0], (lq, base_ref.shape[2]))
    toep = pltpu.roll(wide, 0, 1, stride=1, stride_axis=0)
    o_ref[0] = jnp.where(allow_ref[...] > 0.0, toep[:, :lk], NEG_INF)


def _band_tables(rel_bias, lq, lk, q_pos0, k_pos0):
    n_heads = rel_bias.shape[0]
    width = pl.next_power_of_2(lq + lk)
    c = np.arange(width)
    m = np.where(c < lk, c, c - width)
    idx = np.clip((q_pos0 - k_pos0) - m, -REL_CLIP, REL_CLIP) + REL_CLIP
    base = rel_bias[:, idx].astype(F32).reshape(n_heads, 1, width)
    q_pos = q_pos0 + np.arange(lq)
    k_pos = k_pos0 + np.arange(lk)
    qc = q_pos[:, None] // CHUNK
    kc = k_pos[None, :] // CHUNK
    allowed = ((kc <= qc) & (kc >= qc - A_PAST_CHUNKS) & (k_pos[None, :] >= 0)).astype(np.float32)
    return pl.pallas_call(
        _band_table_kernel,
        grid=(n_heads,),
        in_specs=[pl.BlockSpec((1, 1, width), lambda h: (h, 0, 0)),
                  pl.BlockSpec((lq, lk), lambda h: (0, 0))],
        out_specs=pl.BlockSpec((1, lq, lk), lambda h: (h, 0, 0)),
        out_shape=jax.ShapeDtypeStruct((n_heads, lq, lk), F32),
        compiler_params=_cparams(("arbitrary",)),
        name="band_table",
    )(base, jnp.asarray(allowed))


def attention_prompt(proj, rel_bias, dot, out_dtype):
    t = proj.shape[0]
    nb = t // ATT_QBLOCK
    npair = A_HEADS // 2
    tab = _band_tables(rel_bias, ATT_QSUB, ATT_KSUB, A_WINDOW, 0).reshape(npair, 2, ATT_QSUB, ATT_KSUB)
    blk = (ATT_QBLOCK, LANES)
    return pl.pallas_call(
        functools.partial(_attn_prompt_kernel, dot=dot),
        grid=(nb, npair),
        in_specs=[pl.BlockSpec(blk, lambda i, p: (i, p)),
                  pl.BlockSpec(blk, lambda i, p: (jnp.maximum(i - 1, 0), npair + p)),
                  pl.BlockSpec(blk, lambda i, p: (i, npair + p)),
                  pl.BlockSpec(blk, lambda i, p: (jnp.maximum(i - 1, 0), 2 * npair + p)),
                  pl.BlockSpec(blk, lambda i, p: (i, 2 * npair + p)),
                  pl.BlockSpec((1, 2, ATT_QSUB, ATT_KSUB), lambda i, p: (p, 0, 0, 0))],
        out_specs=pl.BlockSpec(blk, lambda i, p: (i, p)),
        out_shape=jax.ShapeDtypeStruct((t, A_WIDTH), out_dtype),
        compiler_params=_cparams(("arbitrary", "arbitrary")),
        name="attn_prompt",
    )(proj, proj, proj, proj, proj, tab)


def _attn_sample_kernel(q_ref, kn_ref, vn_ref, kc_ref, vc_ref, bias_ref, o_ref, *, dot, w):
    lo = lax.broadcasted_iota(jnp.int32, (1, LANES), 1) < A_HEAD_DIM
    lq = q_ref.shape[0]
    for p in range(A_HEADS // 2):
        cols = slice(p * LANES, (p + 1) * LANES)
        q = q_ref[:, cols] * (A_HEAD_DIM ** -0.5)
        kc = kc_ref[0, :, cols]
        vc = vc_ref[0, :, cols]
        kn = kn_ref[:, cols]
        vn = vn_ref[:, cols]
        outs = []
        for hh in range(2):
            qm = jnp.where(lo if hh == 0 else jnp.logical_not(lo), q, 0.0)
            bias = bias_ref[2 * p + hh]
            s_c = dot(qm, kc, nt=True) + bias[:, :w]
            s_n = dot(qm, kn, nt=True) + bias[:, w:w + lq]
            outs.append(_softmax_pv([s_c, s_n], [vc, vn], dot))
        o_ref[:, cols] = jnp.where(lo, outs[0], outs[1]).astype(o_ref.dtype)


def attention_sample(proj, k_cache, v_cache, rel_bias, bsz, lq, dot, out_dtype):
    w = k_cache.shape[1]
    lkp = ((w + lq + LANES - 1) // LANES) * LANES
    tab = _band_tables(rel_bias, lq, lkp, PAST_LEN, PAST_LEN - w)
    kc = k_cache.reshape(bsz, w, A_WIDTH)
    vc = v_cache.reshape(bsz, w, A_WIDTH)
    return pl.pallas_call(
        functools.partial(_attn_sample_kernel, dot=dot, w=w),
        grid=(bsz,),
        in_specs=[pl.BlockSpec((lq, A_WIDTH), lambda b: (b, 0)),
                  pl.BlockSpec((lq, A_WIDTH), lambda b: (b, 1)),
                  pl.BlockSpec((lq, A_WIDTH), lambda b: (b, 2)),
                  pl.BlockSpec((1, w, A_WIDTH), lambda b: (b, 0, 0)),
                  pl.BlockSpec((1, w, A_WIDTH), lambda b: (b, 0, 0)),
                  pl.BlockSpec((A_HEADS, lq, lkp), lambda b: (0, 0, 0))],
        out_specs=pl.BlockSpec((lq, A_WIDTH), lambda b: (b, 0)),
        out_shape=jax.ShapeDtypeStruct((bsz * lq, A_WIDTH), out_dtype),
        compiler_params=_cparams(("arbitrary",)),
        name="attn_sample",
    )(proj, proj, proj, kc, vc, tab)


def _ssd_kernel(z_ref, x0_ref, x1_ref, x2_ref, dt_ref, cbuf_ref, h0_ref,
                cw_ref, cb_ref, dtb_ref, alog_ref, dsk_ref, ng_ref, exp_ref,
                y_ref, cnew_ref, hout_ref, xp_ref, st_ref, *, dot, q):
    c = pl.program_id(1)
    half = B_WIDTH // B_GROUPS

    @pl.when(c == 0)
    def _():
        xp_ref[0:SUBLANES, :] = cbuf_ref[0]
        st_ref[...] = h0_ref[0]

    xp_ref[SUBLANES:SUBLANES + q, 0:512] = x0_ref[...]
    xp_ref[SUBLANES:SUBLANES + q, 512:1024] = x1_ref[...]
    xp_ref[SUBLANES:SUBLANES + q, 1024:1536] = x2_ref[...]
    base = SUBLANES - (B_CONV - 1)
    conv = cb_ref[...] + xp_ref[base:base + q, :] * cw_ref[0:1, :]
    for tap in range(1, B_CONV):
        conv = conv + xp_ref[base + tap:base + tap + q, :] * cw_ref[tap:tap + 1, :]
    tail = xp_ref[q:q + SUBLANES, :]
    xp_ref[0:SUBLANES, :] = tail
    cnew_ref[0] = tail

    xbc = _silu(conv)
    xs = xbc[:, :B_WIDTH]
    bm = xbc[:, B_WIDTH:B_WIDTH + B_GROUPS * B_STATE]
    cm = xbc[:, B_WIDTH + B_GROUPS * B_STATE:]

    lane = lax.broadcasted_iota(jnp.int32, (1, LANES), 1)
    head_ok = lane < B_HEADS
    dt = jnp.where(head_ok, jax.nn.softplus(dt_ref[...] + dtb_ref[...]), 0.0)
    a_neg = -jnp.exp(alog_ref[...])
    row = lax.broadcasted_iota(jnp.int32, (q, q), 0)
    col = lax.broadcasted_iota(jnp.int32, (q, q), 1)
    tril = row >= col
    acs = _dot_sel(tril.astype(BF16), dt * a_neg)
    acs_t = acs.T
    eacs = jnp.exp(acs)
    to_end = jnp.exp(acs[q - 1:q, :] - acs)
    wide = _dot_rsel(jnp.concatenate([dt, eacs, to_end], axis=0), exp_ref[...])
    dt_w = wide[0:q]
    eacs_w = wide[q:2 * q]
    toend_w = wide[2 * q:3 * q]
    xdt = xs * dt_w
    xend = xdt * toend_w
    lo = lane < B_HEAD_DIM

    y_parts = []
    for g in range(B_GROUPS):
        bg = bm[:, g * B_STATE:(g + 1) * B_STATE]
        cg = cm[:, g * B_STATE:(g + 1) * B_STATE]
        gmat = dot(cg, bg, nt=True)
        y_off = dot(cg, st_ref[g]) * eacs_w[:, g * half:(g + 1) * half]
        for pr in range(half // LANES):
            cols = slice(g * half + pr * LANES, g * half + (pr + 1) * LANES)
            x_pair = xdt[:, cols]
            outs = []
            for hh in range(2):
                h = (g * half + pr * LANES) // B_HEAD_DIM + hh
                diff = acs[:, h:h + 1] - acs_t[h:h + 1, :]
                decay = jnp.exp(jnp.where(tril, diff, -jnp.inf))
                outs.append(dot(gmat * decay, x_pair))
            y_parts.append(jnp.where(lo, outs[0], outs[1]) + y_off[:, pr * LANES:(pr + 1) * LANES])
        st_new = dot(bg.T, xend[:, g * half:(g + 1) * half])
        st_ref[g] = st_ref[g] * eacs_w[q - 1:q, g * half:(g + 1) * half] + st_new
    hout_ref[0] = st_ref[...]

    y = jnp.concatenate(y_parts, axis=1) + dsk_ref[...] * xs
    y = y * _silu(z_ref[...])
    y = y * lax.rsqrt(jnp.mean(y * y, axis=-1, keepdims=True) + RMS_EPS) * ng_ref[...]
    y_ref[...] = y.astype(y_ref.dtype)


def ssd_mixer(proj, conv_buf, h0, conv_w, conv_b, dt_bias, a_log, d_skip, norm_g, bsz, seq, q, dot, out_dtype):
    t = bsz * seq
    nc = seq // q
    half = B_WIDTH // B_GROUPS
    cbuf = jnp.pad(conv_buf, ((0, 0), (SUBLANES - (B_CONV - 1), 0), (0, 0)))
    h0_t = jnp.transpose(h0.reshape(bsz, B_GROUPS, half, B_STATE), (0, 1, 3, 2))
    pad = lambda v: jnp.pad(v.reshape(1, -1), ((0, 0), (0, LANES - v.shape[-1])))
    expand = jnp.asarray(np.repeat(np.eye(LANES, B_HEADS, dtype=np.float32).T, B_HEAD_DIM, axis=0).T, BF16)
    dsk_w = jnp.repeat(d_skip, B_HEAD_DIM).reshape(1, B_WIDTH)
    zb, xb, db = 3 * A_WIDTH // B_WIDTH, (3 * A_WIDTH + B_WIDTH) // 512, (3 * A_WIDTH + B_WIDTH + B_CONV_DIM) // LANES
    const = lambda shp: pl.BlockSpec(shp, lambda b, c: (0,) * len(shp))
    y, cnew, hout = pl.pallas_call(
        functools.partial(_ssd_kernel, dot=dot, q=q),
        grid=(bsz, nc),
        in_specs=[pl.BlockSpec((q, B_WIDTH), lambda b, c: (b * nc + c, zb)),
                  pl.BlockSpec((q, 512), lambda b, c: (b * nc + c, xb)),
                  pl.BlockSpec((q, 512), lambda b, c: (b * nc + c, xb + 1)),
                  pl.BlockSpec((q, 512), lambda b, c: (b * nc + c, xb + 2)),
                  pl.BlockSpec((q, LANES), lambda b, c: (b * nc + c, db)),
                  pl.BlockSpec((1, SUBLANES, B_CONV_DIM), lambda b, c: (b, 0, 0)),
                  pl.BlockSpec((1, B_GROUPS, B_STATE, half), lambda b, c: (b, 0, 0, 0)),
                  const((B_CONV, B_CONV_DIM)), const((1, B_CONV_DIM)), const((1, LANES)), const((1, LANES)),
                  const((1, B_WIDTH)), const((1, B_WIDTH)), const((LANES, B_WIDTH))],
        out_specs=[pl.BlockSpec((q, B_WIDTH), lambda b, c: (b * nc + c, 0)),
                   pl.BlockSpec((1, SUBLANES, B_CONV_DIM), lambda b, c: (b, 0, 0)),
                   pl.BlockSpec((1, B_GROUPS, B_STATE, half), lambda b, c: (b, 0, 0, 0))],
        out_shape=[jax.ShapeDtypeStruct((t, B_WIDTH), out_dtype),
                   jax.ShapeDtypeStruct((bsz, SUBLANES, B_CONV_DIM), F32),
                   jax.ShapeDtypeStruct((bsz, B_GROUPS, B_STATE, half), F32)],
        scratch_shapes=[pltpu.VMEM((q + SUBLANES, B_CONV_DIM), F32),
                        pltpu.VMEM((B_GROUPS, B_STATE, half), F32)],
        compiler_params=_cparams(("arbitrary", "arbitrary")),
        name="ssd",
    )(proj, proj, proj, proj, proj, cbuf, h0_t,
      conv_w, conv_b.reshape(1, -1), pad(dt_bias), pad(a_log), dsk_w, norm_g.reshape(1, -1), expand)
    conv_new = cnew[:, SUBLANES - (B_CONV - 1):, :]
    h_final = jnp.transpose(hout, (0, 1, 3, 2)).reshape(bsz, B_HEADS, B_HEAD_DIM, B_STATE)
    return y, conv_new, h_final


def _s5_prep_kernel(lre_ref, lim_ref, ls_ref, bre_ref, bim_ref, are_ref, aim_ref, bbre_ref, bbim_ref):
    lre = lre_ref[...]
    lim = lim_ref[...]
    step = jnp.exp(ls_ref[...])
    mag = jnp.exp(lre * step)
    ang = lim * step
    ab_re = mag * jnp.cos(ang)
    ab_im = mag * jnp.sin(ang)
    den = lre * lre + lim * lim
    f_re = ((ab_re - 1.0) * lre + ab_im * lim) / den
    f_im = (ab_im * lre - (ab_re - 1.0) * lim) / den
    br = bre_ref[...]
    bi = bim_ref[...]
    are_ref[...] = ab_re
    aim_ref[...] = ab_im
    bbre_ref[...] = f_re * br - f_im * bi
    bbim_ref[...] = f_re * bi + f_im * br


def _cmul(ar, ai, br, bi):
    return ar * br - ai * bi, ar * bi + ai * br


def _gelu_tanh(x):
    return 0.5 * x * (1.0 + jnp.tanh(math.sqrt(2.0 / math.pi) * (x + 0.044715 * (x * x * x))))


def _s5_kernel(u_ref, bb_ref, ccre_ref, ccim_ref, a_ref, d_ref, h0_ref, y_ref, hout_ref,
               s_ref, sb_ref, pw_ref, pwb_ref, carry_ref, *, dot, seg, chain):
    t = pl.program_id(2)
    w = S5_JW
    sdt = sb_ref.dtype
    g = 2 if sdt == BF16 else 1
    grows = g * SUBLANES
    n_groups = seg // g
    unroll = min(n_groups, 4 // g)
    a_re = a_ref[0, 0:1, :]
    a_im = a_ref[0, 1:2, :]

    def group(k):
        return pl.ds(pl.multiple_of(k * grows, grows), grows)

    @pl.when(t == 0)
    def _():
        carry_ref[0:h0_ref.shape[2], :] = h0_ref[0, 0]
        pw_ref[0:1, 0:w] = a_re
        pw_ref[0:1, w:2 * w] = a_im

        def power(i, carry):
            pr, pi = _cmul(pw_ref[pl.ds(i - 1, 1), 0:w], pw_ref[pl.ds(i - 1, 1), w:2 * w], a_re, a_im)
            pw_ref[pl.ds(i, 1), 0:w] = pr
            pw_ref[pl.ds(i, 1), w:2 * w] = pi
            return carry

        lax.fori_loop(1, seg, power, 0)

        def spread(k, carry):
            rows = [jnp.broadcast_to(pw_ref[pl.ds(k * g + q, 1), :], (SUBLANES, 2 * w)) for q in range(g)]
            pwb_ref[group(k), :] = jnp.concatenate(rows, axis=0).astype(sdt)
            return carry

        lax.fori_loop(0, n_groups, spread, 0)

    ncb = u_ref.shape[0]
    up = jnp.concatenate(
        [jnp.concatenate([u_ref[cb, pl.ds(i, SUBLANES, stride=seg), :] for i in range(seg)], axis=0)
         for cb in range(ncb)], axis=1)
    s_ref[...] = dot(up, bb_ref[0])

    are_b = jnp.broadcast_to(a_re, (SUBLANES, w))
    aim_b = jnp.broadcast_to(a_im, (SUBLANES, w))

    def scan(k, st):
        sre, sim = st
        res, ims = [], []
        for q in range(g):
            rows = pl.ds(pl.multiple_of((k * g + q) * SUBLANES, SUBLANES), SUBLANES)
            pr, pi = _cmul(are_b, aim_b, sre, sim)
            sre = pr + s_ref[rows, 0:w]
            sim = pi + s_ref[rows, w:2 * w]
            res.append(sre)
            ims.append(sim)
        sb_ref[group(k), 0:w] = jnp.concatenate(res, axis=0).astype(sdt)
        sb_ref[group(k), w:2 * w] = jnp.concatenate(ims, axis=0).astype(sdt)
        return sre, sim

    zero = jnp.zeros((SUBLANES, w), F32)
    end_re, end_im = lax.fori_loop(0, n_groups, scan, (zero, zero), unroll=unroll)

    sg_re = pw_ref[seg - 1:seg, 0:w]
    sg_im = pw_ref[seg - 1:seg, w:2 * w]
    if chain:
        c_re = carry_ref[0:1, 0:w]
        c_im = carry_ref[0:1, w:2 * w]
        rows_re, rows_im = [], []
        for s in range(SUBLANES):
            rows_re.append(c_re)
            rows_im.append(c_im)
            pr, pi = _cmul(sg_re, sg_im, c_re, c_im)
            c_re = pr + end_re[s:s + 1, :]
            c_im = pi + end_im[s:s + 1, :]
        carry_ref[0:1, 0:w] = c_re
        carry_ref[0:1, w:2 * w] = c_im
        hout_ref[0, 0] = carry_ref[0:1, :]
        in_re = jnp.concatenate(rows_re, axis=0)
        in_im = jnp.concatenate(rows_im, axis=0)
    else:
        in_re = carry_ref[:, 0:w]
        in_im = carry_ref[:, w:2 * w]
        pr, pi = _cmul(sg_re, sg_im, in_re, in_im)
        carry_ref[:, 0:w] = pr + end_re
        carry_ref[:, w:2 * w] = pi + end_im
        hout_ref[0, 0] = carry_ref[...]

    in_re_g = jnp.concatenate([in_re] * g, axis=0).astype(sdt)
    in_im_g = jnp.concatenate([in_im] * g, axis=0).astype(sdt)

    def fixup(k, carry):
        rows = group(k)
        pr, pi = _cmul(pwb_ref[rows, 0:w], pwb_ref[rows, w:2 * w], in_re_g, in_im_g)
        sb_ref[rows, 0:w] = sb_ref[rows, 0:w] + pr
        sb_ref[rows, w:2 * w] = sb_ref[rows, w:2 * w] + pi
        return carry

    lax.fori_loop(0, n_groups, fixup, 0, unroll=unroll)

    y = dot(sb_ref[:, 0:w], ccre_ref[0]) - dot(sb_ref[:, w:2 * w], ccim_ref[0]) + d_ref[...] * up
    y = _gelu_tanh(y)
    for cb in range(ncb):
        for i in range(seg):
            y_ref[cb, pl.ds(i, SUBLANES, stride=seg), :] = (
                y[i * SUBLANES:(i + 1) * SUBLANES, cb * LANES:(cb + 1) * LANES])


def _block_diag(m):
    nj, j, r, c = m.shape
    eye = jnp.eye(j, dtype=m.dtype)
    return (m[:, :, :, None, :] * eye[None, :, None, :, None]).reshape(nj, j * r, j * c)


def s5_mixer(u, h0_re, h0_im, lam_re, lam_im, log_step, b_re, b_im, c_re, c_im, d_skip,
             bsz, seq, tm, dot, w_dtype, chain):
    t = bsz * seq
    g, p, ch = C_GROUPS, C_STATE, C_GROUP_CH
    rep = lambda v: jnp.repeat(v, ch, axis=-1)
    ls = jnp.broadcast_to(log_step[:, None], (g, p))
    pc = pl.BlockSpec((g, p * ch), lambda: (0, 0))
    shp = jax.ShapeDtypeStruct((g, p * ch), F32)
    a_re_x, a_im_x, bb_re, bb_im = pl.pallas_call(
        _s5_prep_kernel, in_specs=[pc] * 5, out_specs=[pc] * 4, out_shape=[shp] * 4, name="s5_prep",
    )(rep(lam_re), rep(lam_im), rep(ls), b_re.reshape(g, p * ch), b_im.reshape(g, p * ch))
    a_re = a_re_x[:, ::ch]
    a_im = a_im_x[:, ::ch]
    to_blk = lambda m: jnp.transpose(m.reshape(S5_NJ, S5_JBLK, p, ch), (0, 1, 3, 2))
    bb = jnp.concatenate([_block_diag(to_blk(bb_re)), _block_diag(to_blk(bb_im))], axis=-1).astype(w_dtype)
    cblk = lambda m: jnp.transpose(m.reshape(S5_NJ, S5_JBLK, ch, p), (0, 1, 3, 2))
    cc_re = _block_diag(cblk(c_re)).astype(w_dtype)
    cc_im = _block_diag(cblk(c_im)).astype(w_dtype)
    a_rows = jnp.stack([a_re.reshape(S5_NJ, S5_JW), a_im.reshape(S5_NJ, S5_JW)], axis=1)
    h0 = jnp.concatenate([h0_re.reshape(bsz, S5_NJ, S5_JW), h0_im.reshape(bsz, S5_NJ, S5_JW)], axis=-1)
    if chain:
        nb, srows, rows_per_b = bsz, 1, seq
        h0 = h0[:, :, None, :]
    else:
        assert bsz == SUBLANES and tm == bsz * seq
        nb, srows, rows_per_b = 1, SUBLANES, bsz * seq
        h0 = jnp.transpose(h0, (1, 0, 2))[None]
    nt = rows_per_b // tm
    seg = tm // SUBLANES
    uw = S5_JBLK * ch
    ncb = uw // LANES
    state_spec = pl.BlockSpec((1, 1, srows, 2 * S5_JW), lambda b, j, i: (b, j, 0, 0))
    y, hout = pl.pallas_call(
        functools.partial(_s5_kernel, dot=dot, seg=seg, chain=chain),
        grid=(nb, S5_NJ, nt),
        in_specs=[pl.BlockSpec((ncb, tm, LANES), lambda b, j, i: (j, b * nt + i, 0)),
                  pl.BlockSpec((1, uw, 2 * S5_JW), lambda b, j, i: (j, 0, 0)),
                  pl.BlockSpec((1, S5_JW, uw), lambda b, j, i: (j, 0, 0)),
                  pl.BlockSpec((1, S5_JW, uw), lambda b, j, i: (j, 0, 0)),
                  pl.BlockSpec((1, 2, S5_JW), lambda b, j, i: (j, 0, 0)),
                  pl.BlockSpec((1, uw), lambda b, j, i: (0, j)),
                  state_spec],
        out_specs=[pl.BlockSpec((ncb, tm, LANES), lambda b, j, i: (j, b * nt + i, 0)), state_spec],
        out_shape=[jax.ShapeDtypeStruct((g * ch // LANES, t, LANES), F32),
                   jax.ShapeDtypeStruct((nb, S5_NJ, srows, 2 * S5_JW), F32)],
        scratch_shapes=[pltpu.VMEM((tm, 2 * S5_JW), F32),
                        pltpu.VMEM((tm, 2 * S5_JW), w_dtype),
                        pltpu.VMEM((seg, 2 * S5_JW), F32),
                        pltpu.VMEM((tm, 2 * S5_JW), w_dtype),
                        pltpu.VMEM((SUBLANES, 2 * S5_JW), F32)],
        compiler_params=_cparams(("arbitrary", "arbitrary", "arbitrary")),
        name="s5_scan",
    )(u, bb, cc_re, cc_im, a_rows, d_skip.reshape(1, -1), h0)
    hout = hout[:, :, 0, :] if chain else jnp.transpose(hout[0], (1, 0, 2))
    s_re = hout[:, :, :S5_JW].reshape(bsz, g, p)
    s_im = hout[:, :, S5_JW:].reshape(bsz, g, p)
    return y, s_re, s_im


def _router_kernel(x_ref, sc_ref, sh_ref, w_ref, b_ref, o_ref, r_ref, cnt_ref, run_ref):
    h = x_ref[...] * (1.0 + sc_ref[...]) + sh_ref[...]
    logits = _dot3(h, w_ref[...]) + b_ref[...]
    lane = lax.broadcasted_iota(jnp.int32, logits.shape, 1).astype(F32)
    logits = jnp.where(lane < N_EXPERTS, logits, -jnp.inf)
    m1 = logits.max(axis=-1, keepdims=True)
    i1 = jnp.min(jnp.where(logits == m1, lane, float(LANES)), axis=-1, keepdims=True)
    rest = jnp.where(lane == i1, -jnp.inf, logits)
    m2 = rest.max(axis=-1, keepdims=True)
    i2 = jnp.min(jnp.where(rest == m2, lane, float(LANES)), axis=-1, keepdims=True)
    e2 = jnp.exp(m2 - m1)
    w1 = 1.0 / (1.0 + e2)
    w2 = e2 / (1.0 + e2)
    o_ref[...] = jnp.where(lane == i1, w1, 0.0) + jnp.where(lane == i2, w2, 0.0)

    step = pl.program_id(0)

    @pl.when(step == 0)
    def _():
        run_ref[...] = jnp.zeros_like(run_ref)

    tm = logits.shape[0]
    sel = jnp.where(jnp.logical_or(lane == i1, lane == i2), 1.0, 0.0)
    row = lax.broadcasted_iota(jnp.int32, (tm, tm), 0)
    col = lax.broadcasted_iota(jnp.int32, (tm, tm), 1)
    before = (row > col).astype(BF16)
    rank = _mxu(before, sel.astype(BF16)) + run_ref[0:1, :]
    pos1 = jnp.sum(jnp.where(lane == i1, rank, 0.0), axis=-1, keepdims=True)
    pos2 = jnp.sum(jnp.where(lane == i2, rank, 0.0), axis=-1, keepdims=True)
    run_ref[0:1, :] = run_ref[0:1, :] + jnp.sum(sel, axis=0, keepdims=True)
    cnt_ref[...] = run_ref[0:1, :]
    rec = jnp.zeros_like(logits)
    for k, val in enumerate((i1, i2, pos1, pos2, w1, w2)):
        rec = jnp.where(lane == float(k), val, rec)
    r_ref[...] = rec


ROUTE_E1, ROUTE_E2, ROUTE_POS1, ROUTE_POS2, ROUTE_W1, ROUTE_W2 = range(6)


def router(x, scale, shift, router_w, router_b):
    t, d = x.shape
    tm = min(t, 512)
    w = jnp.pad(router_w, ((0, 0), (0, LANES - N_EXPERTS)))
    b = jnp.pad(router_b.reshape(1, -1), ((0, 0), (0, LANES - N_EXPERTS)))
    return pl.pallas_call(
        _router_kernel,
        grid=(t // tm,),
        in_specs=[pl.BlockSpec((tm, d), lambda i: (i, 0)), _row_spec(scale, tm), _row_spec(shift, tm),
                  pl.BlockSpec((d, LANES), lambda i: (0, 0)), pl.BlockSpec((1, LANES), lambda i: (0, 0))],
        out_specs=[pl.BlockSpec((tm, LANES), lambda i: (i, 0)), pl.BlockSpec((tm, LANES), lambda i: (i, 0)),
                   pl.BlockSpec((1, LANES), lambda i: (0, 0))],
        out_shape=[jax.ShapeDtypeStruct((t, LANES), F32), jax.ShapeDtypeStruct((t, LANES), F32),
                   jax.ShapeDtypeStruct((1, LANES), F32)],
        scratch_shapes=[pltpu.VMEM((SUBLANES, LANES), F32)],
        compiler_params=_cparams(("arbitrary",)),
        name="router",
    )(x, scale, shift, w, b)


MOE_TILE = 256
MOE_UP_TN = D_FF_EXPERT // 2
DMA_ISSUE_UNROLL = 8


def _dispatch_kernel(fill_ref, d1_ref, d2_ref, h_ref, xs_ref, zero_ref, sem, zsem):
    tm = h_ref.shape[0]

    @pl.when(pl.program_id(0) == 0)
    def _():
        zero_ref[...] = jnp.zeros_like(zero_ref)

        def fill_copy(k):
            row0 = pl.multiple_of(jnp.maximum(fill_ref[k], 0), MOE_TILE)
            return pltpu.make_async_copy(zero_ref, xs_ref.at[pl.ds(row0, MOE_TILE)], zsem)

        for k in range(fill_ref.shape[0]):
            @pl.when(fill_ref[k] >= 0)
            def _():
                fill_copy(k).start()

        for k in range(fill_ref.shape[0]):
            @pl.when(fill_ref[k] >= 0)
            def _():
                fill_copy(k).wait()

    def issue(t, carry):
        src = h_ref.at[pl.ds(t, 1)]
        pltpu.make_async_copy(src, xs_ref.at[pl.ds(d1_ref[0, 0, t], 1)], sem).start()
        pltpu.make_async_copy(src, xs_ref.at[pl.ds(d2_ref[0, 0, t], 1)], sem).start()
        return carry

    lax.fori_loop(0, tm, issue, 0, unroll=DMA_ISSUE_UNROLL)
    pltpu.make_async_copy(xs_ref.at[pl.ds(0, 2 * tm)], xs_ref.at[pl.ds(0, 2 * tm)], sem).wait()


def moe_dispatch(h, dest1, dest2, fill_rows, n_rows):
    t, d = h.shape
    tm = 512
    idx_spec = pl.BlockSpec((1, 1, tm), lambda i, fr: (i, 0, 0), memory_space=pltpu.SMEM)
    grid_spec = pltpu.PrefetchScalarGridSpec(
        num_scalar_prefetch=1,
        grid=(t // tm,),
        in_specs=[idx_spec, idx_spec, pl.BlockSpec((tm, d), lambda i, fr: (i, 0))],
        out_specs=pl.BlockSpec(memory_space=pl.ANY),
        scratch_shapes=[pltpu.VMEM((MOE_TILE, d), h.dtype), pltpu.SemaphoreType.DMA(()),
                        pltpu.SemaphoreType.DMA(())])
    return pl.pallas_call(
        _dispatch_kernel,
        grid_spec=grid_spec,
        out_shape=jax.ShapeDtypeStruct((n_rows, d), h.dtype),
        compiler_params=_cparams(("arbitrary",)),
        name="moe_dispatch",
    )(fill_rows, dest1.reshape(t // tm, 1, tm), dest2.reshape(t // tm, 1, tm), h)


def _expert_up_kernel(te_ref, nv_ref, x_ref, wg_ref, wu_ref, o_ref, *, dot):
    del te_ref

    @pl.when(pl.program_id(1) < nv_ref[0])
    def _():
        gu = dot(x_ref[...], jnp.concatenate([wg_ref[0], wu_ref[0]], axis=1))
        tn = o_ref.shape[1]
        o_ref[...] = (_silu(gu[:, :tn]) * gu[:, tn:]).astype(o_ref.dtype)

    @pl.when(pl.program_id(1) >= nv_ref[0])
    def _():
        o_ref[...] = jnp.zeros_like(o_ref)


def expert_up(xs, w_up, tile_expert, n_valid, dot, tn):
    r, d = xs.shape
    f = w_up.shape[2] // 2
    nf = f // tn
    nt = r // MOE_TILE
    grid_spec = pltpu.PrefetchScalarGridSpec(
        num_scalar_prefetch=2,
        grid=(nf, nt),
        in_specs=[pl.BlockSpec((MOE_TILE, d), lambda j, i, te, nv: (i, 0)),
                  pl.BlockSpec((1, d, tn), lambda j, i, te, nv: (te[i], 0, j)),
                  pl.BlockSpec((1, d, tn), lambda j, i, te, nv: (te[i], 0, j + nf))],
        out_specs=pl.BlockSpec((MOE_TILE, tn), lambda j, i, te, nv: (i, j)))
    return pl.pallas_call(
        functools.partial(_expert_up_kernel, dot=dot),
        grid_spec=grid_spec,
        out_shape=jax.ShapeDtypeStruct((r, f), BF16),
        compiler_params=_cparams(("arbitrary", "arbitrary")),
        name="expert_up",
    )(tile_expert, n_valid, xs, w_up, w_up)


def _expert_down_kernel(te_ref, nv_ref, a_ref, w_ref, o_ref, *, dot):
    del te_ref

    @pl.when(pl.program_id(0) < nv_ref[0])
    def _():
        o_ref[...] = dot(a_ref[...], w_ref[0])

    @pl.when(pl.program_id(0) >= nv_ref[0])
    def _():
        o_ref[...] = jnp.zeros_like(o_ref)


def expert_down(act, w_down, tile_expert, n_valid, dot):
    r, f = act.shape
    d = w_down.shape[2]
    grid_spec = pltpu.PrefetchScalarGridSpec(
        num_scalar_prefetch=2,
        grid=(r // MOE_TILE,),
        in_specs=[pl.BlockSpec((MOE_TILE, f), lambda i, te, nv: (i, 0)),
                  pl.BlockSpec((1, f, d), lambda i, te, nv: (te[i], 0, 0))],
        out_specs=pl.BlockSpec((MOE_TILE, d), lambda i, te, nv: (i, 0)))
    return pl.pallas_call(
        functools.partial(_expert_down_kernel, dot=dot),
        grid_spec=grid_spec,
        out_shape=jax.ShapeDtypeStruct((r, d), F32),
        compiler_params=_cparams(("arbitrary",)),
        name="expert_down",
    )(tile_expert, n_valid, act, w_down)


def _combine_ln_kernel(d1_ref, d2_ref, d1n_ref, d2n_ref, y_ref, rec_ref, x_ref, gate_ref, g_ref, b_ref, xo_ref,
                       buf1, buf2, sem):
    tm = x_ref.shape[0]
    step = pl.program_id(0)
    slot = step % 2

    def gather(idx1, idx2, to):
        def issue(t, carry):
            pltpu.make_async_copy(y_ref.at[pl.ds(idx1[0, 0, t], 1)], buf1.at[to, pl.ds(t, 1)], sem.at[to]).start()
            pltpu.make_async_copy(y_ref.at[pl.ds(idx2[0, 0, t], 1)], buf2.at[to, pl.ds(t, 1)], sem.at[to]).start()
            return carry

        lax.fori_loop(0, tm, issue, 0, unroll=DMA_ISSUE_UNROLL)

    @pl.when(step == 0)
    def _():
        gather(d1_ref, d2_ref, 0)

    @pl.when(step + 1 < pl.num_programs(0))
    def _():
        gather(d1n_ref, d2n_ref, 1 - slot)

    pltpu.make_async_copy(y_ref.at[pl.ds(0, tm)], buf1.at[slot], sem.at[slot]).wait()
    pltpu.make_async_copy(y_ref.at[pl.ds(0, tm)], buf2.at[slot], sem.at[slot]).wait()

    rec = rec_ref[...]
    lane = lax.broadcasted_iota(jnp.int32, rec.shape, 1)
    w1 = jnp.sum(jnp.where(lane == ROUTE_W1, rec, 0.0), axis=-1, keepdims=True)
    w2 = jnp.sum(jnp.where(lane == ROUTE_W2, rec, 0.0), axis=-1, keepdims=True)
    out = w1 * buf1[slot] + w2 * buf2[slot]
    r = ALPHA * x_ref[...] + (1.0 + gate_ref[...]) * out
    mu = jnp.mean(r, axis=-1, keepdims=True)
    dev = r - mu
    var = jnp.mean(dev * dev, axis=-1, keepdims=True)
    xo_ref[...] = dev * lax.rsqrt(var + LN_EPS) * g_ref[...] + b_ref[...]


def moe_combine_ln(y_sorted, dest1, dest2, rec, x, gate, ln_g, ln_b):
    t, d = x.shape
    tm = 256
    nt = t // tm
    idx_spec = pl.BlockSpec((1, 1, tm), lambda i: (i, 0, 0), memory_space=pltpu.SMEM)
    nxt_spec = pl.BlockSpec((1, 1, tm), lambda i: (jnp.minimum(i + 1, nt - 1), 0, 0), memory_space=pltpu.SMEM)
    d1 = dest1.reshape(nt, 1, tm)
    d2 = dest2.reshape(nt, 1, tm)
    return pl.pallas_call(
        _combine_ln_kernel,
        grid=(nt,),
        in_specs=[idx_spec, idx_spec, nxt_spec, nxt_spec, pl.BlockSpec(memory_space=pl.ANY),
                  pl.BlockSpec((tm, LANES), lambda i: (i, 0)), pl.BlockSpec((tm, d), lambda i: (i, 0)),
                  _row_spec(gate, tm), pl.BlockSpec((1, d), lambda i: (0, 0)), pl.BlockSpec((1, d), lambda i: (0, 0))],
        out_specs=pl.BlockSpec((tm, d), lambda i: (i, 0)),
        out_shape=jax.ShapeDtypeStruct((t, d), F32),
        scratch_shapes=[pltpu.VMEM((2, tm, d), F32), pltpu.VMEM((2, tm, d), F32), pltpu.SemaphoreType.DMA((2,))],
        compiler_params=_cparams(("arbitrary",)),
        name="moe_combine_ln",
    )(d1, d2, d1, d2, y_sorted, rec, x, gate, ln_g.reshape(1, d), ln_b.reshape(1, d))


def moe_top2(h, x, rec, counts, gate, ln_g, ln_b, w_up, w_down, dot):
    t, d = h.shape
    n_tiles = (TOP_K * t) // MOE_TILE + N_EXPERTS
    n_rows = n_tiles * MOE_TILE
    cnt = counts[0, :N_EXPERTS].astype(jnp.int32)
    padded = ((cnt + MOE_TILE - 1) // MOE_TILE) * MOE_TILE
    ends = jnp.cumsum(padded)
    starts = ends - padded
    col = lambda k: rec[:, k].astype(jnp.int32)
    experts = jnp.arange(N_EXPERTS, dtype=jnp.int32)[None, :]
    start_of = lambda e: jnp.sum(jnp.where(e[:, None] == experts, starts[None, :], 0), axis=1)
    dest1 = start_of(col(ROUTE_E1)) + col(ROUTE_POS1)
    dest2 = start_of(col(ROUTE_E2)) + col(ROUTE_POS2)
    tile_start = jnp.arange(n_tiles, dtype=jnp.int32) * MOE_TILE
    tile_expert = jnp.minimum(jnp.sum(tile_start[:, None] >= ends[None, :], axis=1), N_EXPERTS - 1).astype(jnp.int32)
    n_valid = (ends[-1:] // MOE_TILE).astype(jnp.int32)
    last_tile = jnp.where(padded > 0, ends - MOE_TILE, -1)
    spare = ends[-1] + jnp.arange(N_EXPERTS, dtype=jnp.int32) * MOE_TILE
    fill_rows = jnp.concatenate([last_tile, jnp.where(spare < n_rows, spare, -1)]).astype(jnp.int32)
    xs = moe_dispatch(h, dest1, dest2, fill_rows, n_rows)
    act = expert_up(xs, w_up, tile_expert, n_valid, dot, tn=MOE_UP_TN)
    y_sorted = expert_down(act, w_down, tile_expert, n_valid, dot)
    return moe_combine_ln(y_sorted, dest1, dest2, rec, x, gate, ln_g, ln_b)


def _trunk(x, mods, caches, wts, precise):
    bsz, seq, d = x.shape
    t = bsz * seq
    prompt = caches is None
    dot = _dot3 if precise else _dot1
    act_dtype = F32 if precise else BF16
    tm = min(t, 512)
    x0 = x.reshape(t, d)

    sh, sc, gt = mods[0][0]
    proj = matmul(x0, wts["w_in0"], dot, mod=(sc, sh), mod_dtype=act_dtype)
    keep = min(A_WINDOW, seq) if prompt else seq
    assert keep == seq or bsz == 1
    kv_rows = proj[t - bsz * keep:, A_WIDTH:3 * A_WIDTH]
    k_new = kv_rows[:, :A_WIDTH].reshape(bsz, keep, A_HEADS, A_HEAD_DIM)
    v_new = kv_rows[:, A_WIDTH:].reshape(bsz, keep, A_HEADS, A_HEAD_DIM)
    if prompt:
        att = attention_prompt(proj, wts["rel_bias"], dot, act_dtype)
        conv_buf = jnp.zeros((bsz, B_CONV - 1, B_CONV_DIM), F32)
        ssm_h0 = jnp.zeros((bsz, B_HEADS, B_HEAD_DIM, B_STATE), F32)
        q_len = min(seq, 256)
    else:
        att = attention_sample(proj, caches["k"], caches["v"], wts["rel_bias"], bsz, seq, dot, act_dtype)
        conv_buf, ssm_h0 = caches["conv"], caches["ssm"]
        q_len = seq
    y_ssd, conv_new, ssm_new = ssd_mixer(proj, conv_buf, ssm_h0, wts["conv_w"], wts["conv_b"], wts["dt_bias"],
                                         wts["a_log"], wts["ssd_d"], wts["ssd_norm_g"], bsz, seq, q_len,
                                         dot, act_dtype)
    x1, h1 = matmul_ln([att, y_ssd], wts["w_out0"], x0, gt, wts["ln_g"][0, 0], wts["ln_b"][0, 0], dot, tm,
                       nxt=(mods[0][1][1], mods[0][1][0]), next_dtype=act_dtype)
    act = ffn_up(h1, wts["ffn_w_up"], dot, act_dtype, tm=min(t, 1024), tn=512)
    x2, h2 = matmul_ln([act], wts["ffn_w_down"], x1, mods[0][1][2], wts["ln_g"][0, 1], wts["ln_b"][0, 1], dot,
                       min(t, 256), tk=None if prompt else 1408, nxt=(mods[1][0][1], mods[1][0][0]), next_dtype=act_dtype)

    u = matmul(h2, wts["w_in1"], dot, split_cols=True)
    if prompt:
        s5_re0 = jnp.zeros((bsz, C_GROUPS, C_STATE), F32)
        s5_im0 = jnp.zeros((bsz, C_GROUPS, C_STATE), F32)
    else:
        s5_re0, s5_im0 = caches["s5_re"], caches["s5_im"]
    y5, s5_re, s5_im = s5_mixer(u, s5_re0, s5_im0, wts["s5_lam_re"], wts["s5_lam_im"], wts["s5_log_step"],
                                wts["s5_b_re"], wts["s5_b_im"], wts["s5_c_re"], wts["s5_c_im"], wts["s5_d"],
                                bsz, seq, 512 if prompt else bsz * seq, dot, F32 if precise else BF16,
                                chain=prompt)
    x3, h3 = matmul_ln([y5], wts["glu_w"], x2, mods[1][0][2], wts["ln_g"][1, 0], wts["ln_b"][1, 0], dot,
                       min(t, 256), tk=None if prompt else 512, glu=True,
                       nxt=(mods[1][1][1], mods[1][1][0]), next_dtype=F32)
    gates, rec, counts = router(x3, mods[1][1][1], mods[1][1][0], wts["router_w"], wts["router_b"])
    if prompt:
        x4 = moe_top2(h3, x3, rec, counts, mods[1][1][2], wts["ln_g"][1, 1], wts["ln_b"][1, 1],
                      wts["moe_w_up"], wts["moe_w_down"], dot)
    else:
        act = ffn_up(h3, wts["moe_w_up"], _dot1, BF16, tm=min(t, 2048), tn=256, gates=gates)
        x4, _ = matmul_ln([act], wts["moe_w_down"], x3, mods[1][1][2], wts["ln_g"][1, 1], wts["ln_b"][1, 1],
                          _dot1, tm, tk=D_FF_EXPERT)
    return (x4.reshape(bsz, seq, d), k_new[None], v_new[None], conv_new[None], ssm_new[None],
            s5_re[None], s5_im[None])


def kernel(x_prompt, x_sample, cache_attn_k, cache_attn_v, state_ssd_conv, state_ssd, state_s5_re, state_s5_im, c_prompt, c_sample, ada_w, ada_b, ln_g, ln_b, w_in0, w_out0, rel_bias, conv_w, conv_b, dt_bias, a_log, ssd_d, ssd_norm_g, ffn_w_up, ffn_w_down, w_in1, s5_lam_re, s5_lam_im, s5_log_step, s5_b_re, s5_b_im, s5_c_re, s5_c_im, s5_d, glu_w, router_w, router_b, moe_w_up, moe_w_down):
    d = D_MODEL
    bp, lp, _ = x_prompt.shape
    bs, ls, _ = x_sample.shape

    n_c = bp + bs
    rows = ((n_c + SUBLANES - 1) // SUBLANES) * SUBLANES
    c_all = jnp.pad(jnp.concatenate([c_prompt, c_sample], axis=0), ((0, rows - n_c), (0, 0)))
    mod = adaln_all(c_all, ada_w, ada_b).reshape(DEPTH, 2, rows, 3, d)

    def mods_for(r0, nb, per_row):
        out = []
        for layer in range(DEPTH):
            out.append([])
            for j in range(2):
                trip = []
                for part in range(3):
                    m = mod[layer, j, r0:r0 + nb, part]
                    trip.append(jnp.repeat(m, per_row, axis=0) if nb > 1 else m)
                out[-1].append(tuple(trip))
        return out

    assert bp == 1
    mods_p = mods_for(0, bp, lp)
    mods_s = mods_for(bp, bs, ls)

    in0_pad = ((0, 0), (0, IN0_PAD - IN0_WIDTH))
    shared = dict(ln_g=ln_g, ln_b=ln_b, rel_bias=rel_bias[0], conv_w=conv_w[0], conv_b=conv_b[0],
                  dt_bias=dt_bias[0], a_log=a_log[0], ssd_d=ssd_d[0], ssd_norm_g=ssd_norm_g[0],
                  s5_lam_re=s5_lam_re[0], s5_lam_im=s5_lam_im[0], s5_log_step=s5_log_step[0],
                  s5_b_re=s5_b_re[0], s5_b_im=s5_b_im[0], s5_c_re=s5_c_re[0], s5_c_im=s5_c_im[0], s5_d=s5_d[0],
                  router_w=router_w[0], router_b=router_b[0])
    big = dict(w_in0=jnp.pad(w_in0[0], in0_pad), w_out0=w_out0[0], ffn_w_up=ffn_w_up, ffn_w_down=ffn_w_down[0],
               w_in1=w_in1[0], glu_w=glu_w[0])
    moe = dict(moe_w_up=moe_w_up[0].astype(BF16), moe_w_down=moe_w_down[0].astype(BF16))
    wts_p = dict(shared, **moe, **{k: v.astype(BF16) for k, v in big.items()})
    wts_s = dict(shared, **moe, **big)

    y_p, k_p, v_p, conv_p, ssd_p, re_p, im_p = _trunk(x_prompt, mods_p, None, wts_p, precise=False)
    caches = dict(k=cache_attn_k[0], v=cache_attn_v[0], conv=state_ssd_conv[0], ssm=state_ssd[0],
                  s5_re=state_s5_re[0], s5_im=state_s5_im[0])
    y_s, k_s, v_s, conv_s, ssd_s, re_s, im_s = _trunk(x_sample, mods_s, caches, wts_s, precise=True)
    return (y_p, y_s, k_p, v_p, conv_p, ssd_p, re_p, im_p, k_s, v_s, conv_s, ssd_s, re_s, im_s)
```

```python
import functools
import math

import numpy as np
import jax
import jax.numpy as jnp
from jax import lax
from jax.experimental import pallas as pl
from jax.experimental.pallas import tpu as pltpu

F32 = jnp.float32
BF16 = jnp.bfloat16

D_MODEL = 2048
DEPTH = 2
PAST_LEN = 2048
CHUNK = 64
A_HEADS = 16
A_HEAD_DIM = 64
A_WIDTH = A_HEADS * A_HEAD_DIM
A_PAST_CHUNKS = 8
A_WINDOW = A_PAST_CHUNKS * CHUNK
REL_CLIP = 128
B_HEADS = 16
B_HEAD_DIM = 64
B_WIDTH = B_HEADS * B_HEAD_DIM
B_GROUPS = 2
B_STATE = 128
B_CONV = 4
B_CONV_DIM = B_WIDTH + 2 * B_GROUPS * B_STATE
C_GROUP_CH = 16
C_GROUPS = D_MODEL // C_GROUP_CH
C_STATE = 64
D_FF = 5632
N_EXPERTS = 8
TOP_K = 2
D_FF_EXPERT = 2816
ALPHA = (2.0 * DEPTH) ** 0.25
LN_EPS = 1e-5
RMS_EPS = 1e-5
NEG_INF = -1e30
IN0_WIDTH = 3 * A_WIDTH + B_WIDTH + B_CONV_DIM + B_HEADS
IN0_MAIN = IN0_WIDTH - B_HEADS

LANES = 128
SUBLANES = 8
VMEM_LIMIT = 56 * 1024 * 1024

LN_SUBROWS = 256
ATT_QBLOCK = 512
ATT_QSUB = 256
ATT_KSUB = ATT_QSUB + A_WINDOW
S5_JBLK = 16
S5_NJ = C_GROUPS // S5_JBLK
S5_JW = S5_JBLK * C_STATE


def _cparams(sem):
    return pltpu.CompilerParams(dimension_semantics=sem, vmem_limit_bytes=VMEM_LIMIT)


def _split_bf16(x):
    hi = x.astype(BF16)
    lo = (x - hi.astype(F32)).astype(BF16)
    return hi, lo


def _mxu(a, b, nt=False):
    if nt:
        return lax.dot_general(a, b, (((1,), (1,)), ((), ())), preferred_element_type=F32)
    return jnp.dot(a, b, preferred_element_type=F32)


def _dot1(a, b, nt=False):
    return _mxu(a.astype(BF16), b.astype(BF16), nt)


def _dot3(a, b, nt=False):
    ah, al = _split_bf16(a.astype(F32))
    bh, bl = _split_bf16(b.astype(F32))
    return _mxu(ah, bh, nt) + (_mxu(ah, bl, nt) + _mxu(al, bh, nt))


def _dot_sel(sel_bf16, x):
    x1 = x.astype(BF16)
    r1 = x - x1.astype(F32)
    x2 = r1.astype(BF16)
    x3 = (r1 - x2.astype(F32)).astype(BF16)
    return _mxu(sel_bf16, x1) + (_mxu(sel_bf16, x2) + _mxu(sel_bf16, x3))


def _dot_rsel(x, sel_bf16):
    x1 = x.astype(BF16)
    r1 = x - x1.astype(F32)
    x2 = r1.astype(BF16)
    x3 = (r1 - x2.astype(F32)).astype(BF16)
    return _mxu(x1, sel_bf16) + (_mxu(x2, sel_bf16) + _mxu(x3, sel_bf16))


def _silu(x):
    return x * jax.nn.sigmoid(x)


def _row_spec(arr, tm):
    d = arr.shape[-1]
    if arr.shape[0] == 1:
        return pl.BlockSpec((1, d), lambda i, *_: (0, 0))
    return pl.BlockSpec((tm, d), lambda i, *_: (i, 0))


def _adaln_kernel(c_ref, w_ref, b_ref, o_ref):
    c = c_ref[...]
    o_ref[0] = _dot3(_silu(c), w_ref[0]) + b_ref[0]


def adaln_all(c_rows, ada_w, ada_b):
    r, d = c_rows.shape
    n = ada_w.shape[-1]
    tn = 1024
    w = ada_w.reshape(2 * DEPTH, d, n)
    b = ada_b.reshape(2 * DEPTH, 1, n)
    return pl.pallas_call(
        _adaln_kernel,
        grid=(2 * DEPTH, n // tn),
        in_specs=[pl.BlockSpec((r, d), lambda i, j: (0, 0)),
                  pl.BlockSpec((1, d, tn), lambda i, j: (i, 0, j)),
                  pl.BlockSpec((1, 1, tn), lambda i, j: (i, 0, j))],
        out_specs=pl.BlockSpec((1, r, tn), lambda i, j: (i, 0, j)),
        out_shape=jax.ShapeDtypeStruct((2 * DEPTH, r, n), F32),
        compiler_params=_cparams(("arbitrary", "arbitrary")),
        name="adaln",
    )(c_rows, w, b)


def _mm_kernel(a_ref, w_ref, o_ref, *, dot):
    res = dot(a_ref[...], w_ref[...])
    for cb in range(o_ref.shape[0]):
        o_ref[cb] = res[:, cb * LANES:(cb + 1) * LANES]


def matmul_slabs(a, w, dot, tn=1024):
    t, k = a.shape
    n = w.shape[1]
    tm = min(t, 1024)
    return pl.pallas_call(
        functools.partial(_mm_kernel, dot=dot),
        grid=(t // tm, n // tn),
        in_specs=[pl.BlockSpec((tm, k), lambda i, j: (i, 0)), pl.BlockSpec((k, tn), lambda i, j: (0, j))],
        out_specs=pl.BlockSpec((tn // LANES, tm, LANES), lambda i, j: (j, i, 0)),
        out_shape=jax.ShapeDtypeStruct((n // LANES, t, LANES), F32),
        compiler_params=_cparams(("arbitrary", "arbitrary")),
        name="matmul",
    )(a, w)


def _in_proj_kernel(x_ref, sc_ref, sh_ref, w_ref, wdt_ref, o_ref, dt_ref, h_ref, *, dot):
    j = pl.program_id(1)

    @pl.when(j == 0)
    def _():
        h_ref[...] = (x_ref[...] * (1.0 + sc_ref[...]) + sh_ref[...]).astype(h_ref.dtype)

    h = h_ref[...]
    o_ref[...] = dot(h, w_ref[...])

    @pl.when(j == pl.num_programs(1) - 1)
    def _():
        dt_ref[...] = dot(h, wdt_ref[...])


def in_projection(x, scale, shift, w_in, dot, h_dtype):
    t, k = x.shape
    tm = min(t, 1024)
    tn = 512
    n_tail = w_in.shape[1] - IN0_MAIN
    w_dt = jnp.pad(w_in[:, IN0_MAIN:], ((0, 0), (0, LANES - n_tail)))
    return pl.pallas_call(
        functools.partial(_in_proj_kernel, dot=dot),
        grid=(t // tm, IN0_MAIN // tn),
        in_specs=[pl.BlockSpec((tm, k), lambda i, j: (i, 0)), _row_spec(scale, tm), _row_spec(shift, tm),
                  pl.BlockSpec((k, tn), lambda i, j: (0, j)),
                  pl.BlockSpec((k, LANES), lambda i, j: (0, 0))],
        out_specs=[pl.BlockSpec((tm, tn), lambda i, j: (i, j)), pl.BlockSpec((tm, LANES), lambda i, j: (i, 0))],
        out_shape=[jax.ShapeDtypeStruct((t, IN0_MAIN), F32), jax.ShapeDtypeStruct((t, LANES), F32)],
        scratch_shapes=[pltpu.VMEM((tm, k), h_dtype)],
        compiler_params=_cparams(("arbitrary", "arbitrary")),
        name="in_proj",
    )(x, scale, shift, w_in, w_dt)


def _ffn_up_kernel(*refs, dot, gated):
    if gated:
        a_ref, wg_ref, wu_ref, gates_ref, o_ref = refs
    else:
        a_ref, wg_ref, wu_ref, o_ref = refs
    a = a_ref[...]
    g = dot(a, wg_ref[0])
    u = dot(a, wu_ref[0])
    act = _silu(g) * u
    if gated:
        e = pl.program_id(1)
        gt = gates_ref[...]
        lane = lax.broadcasted_iota(jnp.int32, gt.shape, 1)
        act = act * jnp.sum(jnp.where(lane == e, gt, 0.0), axis=1, keepdims=True)
    o_ref[...] = act.astype(o_ref.dtype)


def ffn_up(a, w_up, dot, out_dtype, tm, tn, gates=None):
    t, d = a.shape
    e_n, _, f2 = w_up.shape
    f = f2 // 2
    nf = f // tn
    in_specs = [pl.BlockSpec((tm, d), lambda i, e, j: (i, 0)),
                pl.BlockSpec((1, d, tn), lambda i, e, j: (e, 0, j)),
                pl.BlockSpec((1, d, tn), lambda i, e, j: (e, 0, j + nf))]
    args = [a, w_up, w_up]
    if gates is not None:
        in_specs.append(pl.BlockSpec((tm, LANES), lambda i, e, j: (i, 0)))
        args.append(gates)
    return pl.pallas_call(
        functools.partial(_ffn_up_kernel, dot=dot, gated=gates is not None),
        grid=(t // tm, e_n, nf),
        in_specs=in_specs,
        out_specs=pl.BlockSpec((tm, tn), lambda i, e, j: (i, e * nf + j)),
        out_shape=jax.ShapeDtypeStruct((t, e_n * f), out_dtype),
        compiler_params=_cparams(("arbitrary", "arbitrary", "arbitrary")),
        name="ffn_up",
    )(*args)


def _mm_ln_kernel(*refs, n_a, nk, dot, glu, has_next):
    a_refs = refs[:n_a]
    w_refs = refs[n_a:2 * n_a]
    pos = 2 * n_a
    x_ref, gate_ref, g_ref, b_ref = refs[pos:pos + 4]
    pos += 4
    if has_next:
        sc_ref, sh_ref = refs[pos:pos + 2]
        pos += 2
    xo_ref = refs[pos]
    pos += 1
    if has_next:
        ho_ref = refs[pos]
        pos += 1
    acc_ref = refs[pos] if nk > 1 else None

    tm = x_ref.shape[0]
    sub = min(tm, max(LANES, min(LN_SUBROWS, tm // 2)))

    def load_a(a_ref, rows):
        if len(a_ref.shape) == 3:
            return jnp.concatenate([a_ref[cb, rows, :] for cb in range(a_ref.shape[0])], axis=1)
        return a_ref[rows, :]

    def product(rows):
        w0 = w_refs[0][0] if len(w_refs[0].shape) == 3 else w_refs[0][...]
        part = dot(load_a(a_refs[0], rows), w0)
        for a_ref, w_ref in zip(a_refs[1:], w_refs[1:]):
            part = part + dot(load_a(a_ref, rows), w_ref[...])
        return part

    def per_row(ref, rows):
        return ref[...] if ref.shape[0] == 1 else ref[rows, :]

    def epilogue(acc, rows):
        if glu:
            d = acc.shape[1] // 2
            out = acc[:, :d] * jax.nn.sigmoid(acc[:, d:])
        else:
            out = acc
        r = ALPHA * x_ref[rows, :] + (1.0 + per_row(gate_ref, rows)) * out
        mu = jnp.mean(r, axis=-1, keepdims=True)
        dev = r - mu
        var = jnp.mean(dev * dev, axis=-1, keepdims=True)
        xn = dev * lax.rsqrt(var + LN_EPS) * g_ref[...] + b_ref[...]
        xo_ref[rows, :] = xn
        if has_next:
            ho_ref[rows, :] = (xn * (1.0 + per_row(sc_ref, rows)) + per_row(sh_ref, rows)).astype(ho_ref.dtype)

    blocks = [slice(r0, r0 + sub) for r0 in range(0, tm, sub)]
    if nk == 1:
        for rows in blocks:
            epilogue(product(rows), rows)
    else:
        k = pl.program_id(1)
        part = product(slice(0, tm))

        @pl.when(k == 0)
        def _():
            acc_ref[...] = part

        @pl.when(k > 0)
        def _():
            acc_ref[...] += part

        @pl.when(k == nk - 1)
        def _():
            for rows in blocks:
                epilogue(acc_ref[rows, :], rows)


def matmul_ln(a_list, w, x, gate, ln_g, ln_b, dot, tm, tk=None, glu=False, nxt=None, next_dtype=None):
    t, d = x.shape
    n = w.shape[-1]
    n_a = len(a_list)
    if n_a > 1:
        nk = 1
        in_specs = [pl.BlockSpec((tm, a.shape[1]), lambda i, k: (i, 0)) for a in a_list]
        off = 0
        for a in a_list:
            ka = a.shape[1]
            assert off % ka == 0
            in_specs.append(pl.BlockSpec((ka, n), functools.partial(lambda i, k, o: (o, 0), o=off // ka),
                                         pipeline_mode=pl.Buffered(1)))
            off += ka
        w_args = [w] * n_a
    else:
        a0 = a_list[0]
        split_cols = a0.ndim == 3
        ktot = a0.shape[0] * LANES if split_cols else a0.shape[1]
        tk = ktot if tk is None else tk
        nk = ktot // tk
        if split_cols:
            a_spec = pl.BlockSpec((tk // LANES, tm, LANES), lambda i, k: (k, i, 0))
        else:
            a_spec = pl.BlockSpec((tm, tk), lambda i, k: (i, k))
        w_mode = dict(pipeline_mode=pl.Buffered(1)) if nk == 1 else {}
        if w.ndim == 3:
            assert w.shape[1] == tk
            w_spec = pl.BlockSpec((1, tk, n), lambda i, k: (k, 0, 0))
        else:
            w_spec = pl.BlockSpec((tk, n), lambda i, k: (k, 0), **w_mode)
        in_specs = [a_spec, w_spec]
        w_args = [w]
    in_specs += [pl.BlockSpec((tm, d), lambda i, k: (i, 0)), _row_spec(gate, tm),
                 pl.BlockSpec((1, d), lambda i, k: (0, 0)), pl.BlockSpec((1, d), lambda i, k: (0, 0))]
    args = list(a_list) + w_args + [x, gate, ln_g.reshape(1, d), ln_b.reshape(1, d)]
    out_specs = [pl.BlockSpec((tm, d), lambda i, k: (i, 0))]
    out_shape = [jax.ShapeDtypeStruct((t, d), F32)]
    if nxt is not None:
        in_specs += [_row_spec(nxt[0], tm), _row_spec(nxt[1], tm)]
        args += [nxt[0], nxt[1]]
        out_specs.append(pl.BlockSpec((tm, d), lambda i, k: (i, 0)))
        out_shape.append(jax.ShapeDtypeStruct((t, d), next_dtype))
    scratch = [pltpu.VMEM((tm, n), F32)] if nk > 1 else []
    res = pl.pallas_call(
        functools.partial(_mm_ln_kernel, n_a=n_a, nk=nk, dot=dot, glu=glu, has_next=nxt is not None),
        grid=(t // tm, nk),
        in_specs=in_specs,
        out_specs=out_specs,
        out_shape=out_shape,
        scratch_shapes=scratch,
        compiler_params=_cparams(("arbitrary", "arbitrary")),
        name="matmul_ln",
    )(*args)
    return res if nxt is not None else (res[0], None)


def _softmax_pv(scores, values, dot):
    m = scores[0].max(axis=-1, keepdims=True)
    for s in scores[1:]:
        m = jnp.maximum(m, s.max(axis=-1, keepdims=True))
    num = None
    den = None
    for s, v in zip(scores, values):
        p = jnp.exp(s - m)
        l = p.sum(axis=-1, keepdims=True)
        o = dot(p, v)
        num = o if num is None else num + o
        den = l if den is None else den + l
    return num / den


def _attn_prompt_kernel(q_ref, kp_ref, kc_ref, vp_ref, vc_ref, bias_ref, o_ref, *, dot):
    i = pl.program_id(0)
    lo = lax.broadcasted_iota(jnp.int32, (1, LANES), 1) < A_HEAD_DIM
    q = q_ref[...] * (A_HEAD_DIM ** -0.5)
    k = jnp.concatenate([kp_ref[...], kc_ref[...]], axis=0)
    v = jnp.concatenate([vp_ref[...], vc_ref[...]], axis=0)
    kidx = lax.broadcasted_iota(jnp.int32, (1, ATT_KSUB), 1)
    for sub in range(ATT_QBLOCK // ATT_QSUB):
        r0 = sub * ATT_QSUB
        qs = q[r0:r0 + ATT_QSUB]
        ks = k[r0:r0 + ATT_KSUB]
        vs = v[r0:r0 + ATT_KSUB]
        before = jnp.where(jnp.logical_and(i == 0, kidx + r0 < ATT_QBLOCK), NEG_INF, 0.0)
        outs = []
        for hh in range(2):
            qm = jnp.where(lo if hh == 0 else jnp.logical_not(lo), qs, 0.0)
            s = dot(qm, ks, nt=True) + bias_ref[0, hh] + before
            outs.append(_softmax_pv([s], [vs], dot))
        o_ref[r0:r0 + ATT_QSUB, :] = jnp.where(lo, outs[0], outs[1]).astype(o_ref.dtype)


def _band_table_kernel(base_ref, allow_ref, o_ref):
    lq, lk = o_ref.shape[1], o_ref.shape[2]
    wide = jnp.broadcast_to(base_ref[0], (lq, base_ref.shape[2]))
    toep = pltpu.roll(wide, 0, 1, stride=1, stride_axis=0)
    o_ref[0] = jnp.where(allow_ref[...] > 0.0, toep[:, :lk], NEG_INF)


def _band_tables(rel_bias, lq, lk, q_pos0, k_pos0):
    n_heads = rel_bias.shape[0]
    width = pl.next_power_of_2(lq + lk)
    c = np.arange(width)
    m = np.where(c < lk, c, c - width)
    idx = np.clip((q_pos0 - k_pos0) - m, -REL_CLIP, REL_CLIP) + REL_CLIP
    base = rel_bias[:, idx].astype(F32).reshape(n_heads, 1, width)
    q_pos = q_pos0 + np.arange(lq)
    k_pos = k_pos0 + np.arange(lk)
    qc = q_pos[:, None] // CHUNK
    kc = k_pos[None, :] // CHUNK
    allowed = ((kc <= qc) & (kc >= qc - A_PAST_CHUNKS) & (k_pos[None, :] >= 0)).astype(np.float32)
    return pl.pallas_call(
        _band_table_kernel,
        grid=(n_heads,),
        in_specs=[pl.BlockSpec((1, 1, width), lambda h: (h, 0, 0)),
                  pl.BlockSpec((lq, lk), lambda h: (0, 0))],
        out_specs=pl.BlockSpec((1, lq, lk), lambda h: (h, 0, 0)),
        out_shape=jax.ShapeDtypeStruct((n_heads, lq, lk), F32),
        compiler_params=_cparams(("arbitrary",)),
        name="band_table",
    )(base, jnp.asarray(allowed))


def attention_prompt(proj, rel_bias, dot, out_dtype):
    t = proj.shape[0]
    nb = t // ATT_QBLOCK
    npair = A_HEADS // 2
    tab = _band_tables(rel_bias, ATT_QSUB, ATT_KSUB, A_WINDOW, 0).reshape(npair, 2, ATT_QSUB, ATT_KSUB)
    blk = (ATT_QBLOCK, LANES)
    return pl.pallas_call(
        functools.partial(_attn_prompt_kernel, dot=dot),
        grid=(nb, npair),
        in_specs=[pl.BlockSpec(blk, lambda i, p: (i, p)),
                  pl.BlockSpec(blk, lambda i, p: (jnp.maximum(i - 1, 0), npair + p)),
                  pl.BlockSpec(blk, lambda i, p: (i, npair + p)),
                  pl.BlockSpec(blk, lambda i, p: (jnp.maximum(i - 1, 0), 2 * npair + p)),
                  pl.BlockSpec(blk, lambda i, p: (i, 2 * npair + p)),
                  pl.BlockSpec((1, 2, ATT_QSUB, ATT_KSUB), lambda i, p: (p, 0, 0, 0))],
        out_specs=pl.BlockSpec(blk, lambda i, p: (i, p)),
        out_shape=jax.ShapeDtypeStruct((t, A_WIDTH), out_dtype),
        compiler_params=_cparams(("arbitrary", "arbitrary")),
        name="attn_prompt",
    )(proj, proj, proj, proj, proj, tab)


def _attn_sample_kernel(q_ref, kn_ref, vn_ref, kc_ref, vc_ref, bias_ref, o_ref, *, dot, w):
    lo = lax.broadcasted_iota(jnp.int32, (1, LANES), 1) < A_HEAD_DIM
    lq = q_ref.shape[0]
    for p in range(A_HEADS // 2):
        cols = slice(p * LANES, (p + 1) * LANES)
        q = q_ref[:, cols] * (A_HEAD_DIM ** -0.5)
        kc = kc_ref[0, :, cols]
        vc = vc_ref[0, :, cols]
        kn = kn_ref[:, cols]
        vn = vn_ref[:, cols]
        outs = []
        for hh in range(2):
            qm = jnp.where(lo if hh == 0 else jnp.logical_not(lo), q, 0.0)
            bias = bias_ref[2 * p + hh]
            s_c = dot(qm, kc, nt=True) + bias[:, :w]
            s_n = dot(qm, kn, nt=True) + bias[:, w:w + lq]
            outs.append(_softmax_pv([s_c, s_n], [vc, vn], dot))
        o_ref[:, cols] = jnp.where(lo, outs[0], outs[1]).astype(o_ref.dtype)


def attention_sample(proj, k_cache, v_cache, rel_bias, bsz, lq, dot, out_dtype):
    w = k_cache.shape[1]
    lkp = ((w + lq + LANES - 1) // LANES) * LANES
    tab = _band_tables(rel_bias, lq, lkp, PAST_LEN, PAST_LEN - w)
    kc = k_cache.reshape(bsz, w, A_WIDTH)
    vc = v_cache.reshape(bsz, w, A_WIDTH)
    return pl.pallas_call(
        functools.partial(_attn_sample_kernel, dot=dot, w=w),
        grid=(bsz,),
        in_specs=[pl.BlockSpec((lq, A_WIDTH), lambda b: (b, 0)),
                  pl.BlockSpec((lq, A_WIDTH), lambda b: (b, 1)),
                  pl.BlockSpec((lq, A_WIDTH), lambda b: (b, 2)),
                  pl.BlockSpec((1, w, A_WIDTH), lambda b: (b, 0, 0)),
                  pl.BlockSpec((1, w, A_WIDTH), lambda b: (b, 0, 0)),
                  pl.BlockSpec((A_HEADS, lq, lkp), lambda b: (0, 0, 0))],
        out_specs=pl.BlockSpec((lq, A_WIDTH), lambda b: (b, 0)),
        out_shape=jax.ShapeDtypeStruct((bsz * lq, A_WIDTH), out_dtype),
        compiler_params=_cparams(("arbitrary",)),
        name="attn_sample",
    )(proj, proj, proj, kc, vc, tab)


def _ssd_kernel(z_ref, x0_ref, x1_ref, x2_ref, dt_ref, cbuf_ref, h0_ref,
                cw_ref, cb_ref, dtb_ref, alog_ref, dsk_ref, ng_ref, exp_ref,
                y_ref, cnew_ref, hout_ref, xp_ref, st_ref, *, dot, q):
    c = pl.program_id(1)
    half = B_WIDTH // B_GROUPS

    @pl.when(c == 0)
    def _():
        xp_ref[0:SUBLANES, :] = cbuf_ref[0]
        st_ref[...] = h0_ref[0]

    xp_ref[SUBLANES:SUBLANES + q, 0:512] = x0_ref[...]
    xp_ref[SUBLANES:SUBLANES + q, 512:1024] = x1_ref[...]
    xp_ref[SUBLANES:SUBLANES + q, 1024:1536] = x2_ref[...]
    base = SUBLANES - (B_CONV - 1)
    conv = cb_ref[...] + xp_ref[base:base + q, :] * cw_ref[0:1, :]
    for tap in range(1, B_CONV):
        conv = conv + xp_ref[base + tap:base + tap + q, :] * cw_ref[tap:tap + 1, :]
    tail = xp_ref[q:q + SUBLANES, :]
    xp_ref[0:SUBLANES, :] = tail
    cnew_ref[0] = tail

    xbc = _silu(conv)
    xs = xbc[:, :B_WIDTH]
    bm = xbc[:, B_WIDTH:B_WIDTH + B_GROUPS * B_STATE]
    cm = xbc[:, B_WIDTH + B_GROUPS * B_STATE:]

    lane = lax.broadcasted_iota(jnp.int32, (1, LANES), 1)
    head_ok = lane < B_HEADS
    dt = jnp.where(head_ok, jax.nn.softplus(dt_ref[...] + dtb_ref[...]), 0.0)
    a_neg = -jnp.exp(alog_ref[...])
    row = lax.broadcasted_iota(jnp.int32, (q, q), 0)
    col = lax.broadcasted_iota(jnp.int32, (q, q), 1)
    tril = row >= col
    acs = _dot_sel(tril.astype(BF16), dt * a_neg)
    acs_t = acs.T
    eacs = jnp.exp(acs)
    to_end = jnp.exp(acs[q - 1:q, :] - acs)
    wide = _dot_rsel(jnp.concatenate([dt, eacs, to_end], axis=0), exp_ref[...])
    dt_w = wide[0:q]
    eacs_w = wide[q:2 * q]
    toend_w = wide[2 * q:3 * q]
    xdt = xs * dt_w
    xend = xdt * toend_w
    lo = lane < B_HEAD_DIM

    y_parts = []
    for g in range(B_GROUPS):
        bg = bm[:, g * B_STATE:(g + 1) * B_STATE]
        cg = cm[:, g * B_STATE:(g + 1) * B_STATE]
        gmat = dot(cg, bg, nt=True)
        y_off = dot(cg, st_ref[g]) * eacs_w[:, g * half:(g + 1) * half]
        for pr in range(half // LANES):
            cols = slice(g * half + pr * LANES, g * half + (pr + 1) * LANES)
            x_pair = xdt[:, cols]
            outs = []
            for hh in range(2):
                h = (g * half + pr * LANES) // B_HEAD_DIM + hh
                diff = acs[:, h:h + 1] - acs_t[h:h + 1, :]
                decay = jnp.exp(jnp.where(tril, diff, -jnp.inf))
                outs.append(dot(gmat * decay, x_pair))
            y_parts.append(jnp.where(lo, outs[0], outs[1]) + y_off[:, pr * LANES:(pr + 1) * LANES])
        st_new = dot(bg.T, xend[:, g * half:(g + 1) * half])
        st_ref[g] = st_ref[g] * eacs_w[q - 1:q, g * half:(g + 1) * half] + st_new
    hout_ref[0] = st_ref[...]

    y = jnp.concatenate(y_parts, axis=1) + dsk_ref[...] * xs
    y = y * _silu(z_ref[...])
    y = y * lax.rsqrt(jnp.mean(y * y, axis=-1, keepdims=True) + RMS_EPS) * ng_ref[...]
    y_ref[...] = y.astype(y_ref.dtype)


def ssd_mixer(proj, dt_raw, conv_buf, h0, conv_w, conv_b, dt_bias, a_log, d_skip, norm_g, bsz, seq, q, dot,
              out_dtype):
    t = bsz * seq
    nc = seq // q
    half = B_WIDTH // B_GROUPS
    cbuf = jnp.pad(conv_buf, ((0, 0), (SUBLANES - (B_CONV - 1), 0), (0, 0)))
    h0_t = jnp.transpose(h0.reshape(bsz, B_GROUPS, half, B_STATE), (0, 1, 3, 2))
    pad = lambda v: jnp.pad(v.reshape(1, -1), ((0, 0), (0, LANES - v.shape[-1])))
    expand = jnp.asarray(np.repeat(np.eye(LANES, B_HEADS, dtype=np.float32).T, B_HEAD_DIM, axis=0).T, BF16)
    dsk_w = jnp.repeat(d_skip, B_HEAD_DIM).reshape(1, B_WIDTH)
    zb, xb = 3 * A_WIDTH // B_WIDTH, (3 * A_WIDTH + B_WIDTH) // 512
    const = lambda shp: pl.BlockSpec(shp, lambda b, c: (0,) * len(shp))
    y, cnew, hout = pl.pallas_call(
        functools.partial(_ssd_kernel, dot=dot, q=q),
        grid=(bsz, nc),
        in_specs=[pl.BlockSpec((q, B_WIDTH), lambda b, c: (b * nc + c, zb)),
                  pl.BlockSpec((q, 512), lambda b, c: (b * nc + c, xb)),
                  pl.BlockSpec((q, 512), lambda b, c: (b * nc + c, xb + 1)),
                  pl.BlockSpec((q, 512), lambda b, c: (b * nc + c, xb + 2)),
                  pl.BlockSpec((q, LANES), lambda b, c: (b * nc + c, 0)),
                  pl.BlockSpec((1, SUBLANES, B_CONV_DIM), lambda b, c: (b, 0, 0)),
                  pl.BlockSpec((1, B_GROUPS, B_STATE, half), lambda b, c: (b, 0, 0, 0)),
                  const((B_CONV, B_CONV_DIM)), const((1, B_CONV_DIM)), const((1, LANES)), const((1, LANES)),
                  const((1, B_WIDTH)), const((1, B_WIDTH)), const((LANES, B_WIDTH))],
        out_specs=[pl.BlockSpec((q, B_WIDTH), lambda b, c: (b * nc + c, 0)),
                   pl.BlockSpec((1, SUBLANES, B_CONV_DIM), lambda b, c: (b, 0, 0)),
                   pl.BlockSpec((1, B_GROUPS, B_STATE, half), lambda b, c: (b, 0, 0, 0))],
        out_shape=[jax.ShapeDtypeStruct((t, B_WIDTH), out_dtype),
                   jax.ShapeDtypeStruct((bsz, SUBLANES, B_CONV_DIM), F32),
                   jax.ShapeDtypeStruct((bsz, B_GROUPS, B_STATE, half), F32)],
        scratch_shapes=[pltpu.VMEM((q + SUBLANES, B_CONV_DIM), F32),
                        pltpu.VMEM((B_GROUPS, B_STATE, half), F32)],
        compiler_params=_cparams(("arbitrary", "arbitrary")),
        name="ssd",
    )(proj, proj, proj, proj, dt_raw, cbuf, h0_t,
      conv_w, conv_b.reshape(1, -1), pad(dt_bias), pad(a_log), dsk_w, norm_g.reshape(1, -1), expand)
    conv_new = cnew[:, SUBLANES - (B_CONV - 1):, :]
    h_final = jnp.transpose(hout, (0, 1, 3, 2)).reshape(bsz, B_HEADS, B_HEAD_DIM, B_STATE)
    return y, conv_new, h_final


def _s5_prep_kernel(lre_ref, lim_ref, ls_ref, bre_ref, bim_ref, are_ref, aim_ref, bbre_ref, bbim_ref):
    lre = lre_ref[...]
    lim = lim_ref[...]
    step = jnp.exp(ls_ref[...])
    mag = jnp.exp(lre * step)
    ang = lim * step
    ab_re = mag * jnp.cos(ang)
    ab_im = mag * jnp.sin(ang)
    den = lre * lre + lim * lim
    f_re = ((ab_re - 1.0) * lre + ab_im * lim) / den
    f_im = (ab_im * lre - (ab_re - 1.0) * lim) / den
    br = bre_ref[...]
    bi = bim_ref[...]
    are_ref[...] = ab_re
    aim_ref[...] = ab_im
    bbre_ref[...] = f_re * br - f_im * bi
    bbim_ref[...] = f_re * bi + f_im * br


def _cmul(ar, ai, br, bi):
    return ar * br - ai * bi, ar * bi + ai * br


def _gelu_tanh(x):
    return 0.5 * x * (1.0 + jnp.tanh(math.sqrt(2.0 / math.pi) * (x + 0.044715 * (x * x * x))))


def _s5_kernel(u_ref, bb_ref, ccre_ref, ccim_ref, a_ref, d_ref, h0_ref, y_ref, hout_ref,
               s_ref, sb_ref, pw_ref, pwb_ref, carry_ref, *, dot, seg, chain):
    t = pl.program_id(2)
    w = S5_JW
    sdt = sb_ref.dtype
    g = 2 if sdt == BF16 else 1
    grows = g * SUBLANES
    n_groups = seg // g
    unroll = min(n_groups, 4 // g)
    a_re = a_ref[0, 0:1, :]
    a_im = a_ref[0, 1:2, :]

    def group(k):
        return pl.ds(pl.multiple_of(k * grows, grows), grows)

    @pl.when(t == 0)
    def _():
        carry_ref[0:h0_ref.shape[2], :] = h0_ref[0, 0]
        pw_ref[0:1, 0:w] = a_re
        pw_ref[0:1, w:2 * w] = a_im

        def power(i, carry):
            pr, pi = _cmul(pw_ref[pl.ds(i - 1, 1), 0:w], pw_ref[pl.ds(i - 1, 1), w:2 * w], a_re, a_im)
            pw_ref[pl.ds(i, 1), 0:w] = pr
            pw_ref[pl.ds(i, 1), w:2 * w] = pi
            return carry

        lax.fori_loop(1, seg, power, 0)

        def spread(k, carry):
            rows = [jnp.broadcast_to(pw_ref[pl.ds(k * g + q, 1), :], (SUBLANES, 2 * w)) for q in range(g)]
            pwb_ref[group(k), :] = jnp.concatenate(rows, axis=0).astype(sdt)
            return carry

        lax.fori_loop(0, n_groups, spread, 0)

    ncb = u_ref.shape[0]
    up = jnp.concatenate(
        [jnp.concatenate([u_ref[cb, pl.ds(i, SUBLANES, stride=seg), :] for i in range(seg)], axis=0)
         for cb in range(ncb)], axis=1)
    s_ref[...] = dot(up, bb_ref[0])

    are_b = jnp.broadcast_to(a_re, (SUBLANES, w))
    aim_b = jnp.broadcast_to(a_im, (SUBLANES, w))

    def scan(k, st):
        sre, sim = st
        res, ims = [], []
        for q in range(g):
            rows = pl.ds(pl.multiple_of((k * g + q) * SUBLANES, SUBLANES), SUBLANES)
            pr, pi = _cmul(are_b, aim_b, sre, sim)
            sre = pr + s_ref[rows, 0:w]
            sim = pi + s_ref[rows, w:2 * w]
            res.append(sre)
            ims.append(sim)
        sb_ref[group(k), 0:w] = jnp.concatenate(res, axis=0).astype(sdt)
        sb_ref[group(k), w:2 * w] = jnp.concatenate(ims, axis=0).astype(sdt)
        return sre, sim

    zero = jnp.zeros((SUBLANES, w), F32)
    end_re, end_im = lax.fori_loop(0, n_groups, scan, (zero, zero), unroll=unroll)

    sg_re = pw_ref[seg - 1:seg, 0:w]
    sg_im = pw_ref[seg - 1:seg, w:2 * w]
    if chain:
        c_re = carry_ref[0:1, 0:w]
        c_im = carry_ref[0:1, w:2 * w]
        rows_re, rows_im = [], []
        for s in range(SUBLANES):
            rows_re.append(c_re)
            rows_im.append(c_im)
            pr, pi = _cmul(sg_re, sg_im, c_re, c_im)
            c_re = pr + end_re[s:s + 1, :]
            c_im = pi + end_im[s:s + 1, :]
        carry_ref[0:1, 0:w] = c_re
        carry_ref[0:1, w:2 * w] = c_im
        hout_ref[0, 0] = carry_ref[0:1, :]
        in_re = jnp.concatenate(rows_re, axis=0)
        in_im = jnp.concatenate(rows_im, axis=0)
    else:
        in_re = carry_ref[:, 0:w]
        in_im = carry_ref[:, w:2 * w]
        pr, pi = _cmul(sg_re, sg_im, in_re, in_im)
        carry_ref[:, 0:w] = pr + end_re
        carry_ref[:, w:2 * w] = pi + end_im
        hout_ref[0, 0] = carry_ref[...]

    in_re_g = jnp.concatenate([in_re] * g, axis=0).astype(sdt)
    in_im_g = jnp.concatenate([in_im] * g, axis=0).astype(sdt)

    def fixup(k, carry):
        rows = group(k)
        pr, pi = _cmul(pwb_ref[rows, 0:w], pwb_ref[rows, w:2 * w], in_re_g, in_im_g)
        sb_ref[rows, 0:w] = sb_ref[rows, 0:w] + pr
        sb_ref[rows, w:2 * w] = sb_ref[rows, w:2 * w] + pi
        return carry

    lax.fori_loop(0, n_groups, fixup, 0, unroll=unroll)

    y = dot(sb_ref[:, 0:w], ccre_ref[0]) - dot(sb_ref[:, w:2 * w], ccim_ref[0]) + d_ref[...] * up
    y = _gelu_tanh(y)
    for cb in range(ncb):
        for i in range(seg):
            y_ref[cb, pl.ds(i, SUBLANES, stride=seg), :] = (
                y[i * SUBLANES:(i + 1) * SUBLANES, cb * LANES:(cb + 1) * LANES])


def _block_diag(m):
    nj, j, r, c = m.shape
    eye = jnp.eye(j, dtype=m.dtype)
    return (m[:, :, :, None, :] * eye[None, :, None, :, None]).reshape(nj, j * r, j * c)


def s5_mixer(u, h0_re, h0_im, lam_re, lam_im, log_step, b_re, b_im, c_re, c_im, d_skip,
             bsz, seq, tm, dot, w_dtype, chain):
    t = bsz * seq
    g, p, ch = C_GROUPS, C_STATE, C_GROUP_CH
    rep = lambda v: jnp.repeat(v, ch, axis=-1)
    ls = jnp.broadcast_to(log_step[:, None], (g, p))
    pc = pl.BlockSpec((g, p * ch), lambda: (0, 0))
    shp = jax.ShapeDtypeStruct((g, p * ch), F32)
    a_re_x, a_im_x, bb_re, bb_im = pl.pallas_call(
        _s5_prep_kernel, in_specs=[pc] * 5, out_specs=[pc] * 4, out_shape=[shp] * 4, name="s5_prep",
    )(rep(lam_re), rep(lam_im), rep(ls), b_re.reshape(g, p * ch), b_im.reshape(g, p * ch))
    a_re = a_re_x[:, ::ch]
    a_im = a_im_x[:, ::ch]
    to_blk = lambda m: jnp.transpose(m.reshape(S5_NJ, S5_JBLK, p, ch), (0, 1, 3, 2))
    bb = jnp.concatenate([_block_diag(to_blk(bb_re)), _block_diag(to_blk(bb_im))], axis=-1).astype(w_dtype)
    cblk = lambda m: jnp.transpose(m.reshape(S5_NJ, S5_JBLK, ch, p), (0, 1, 3, 2))
    cc_re = _block_diag(cblk(c_re)).astype(w_dtype)
    cc_im = _block_diag(cblk(c_im)).astype(w_dtype)
    a_rows = jnp.stack([a_re.reshape(S5_NJ, S5_JW), a_im.reshape(S5_NJ, S5_JW)], axis=1)
    h0 = jnp.concatenate([h0_re.reshape(bsz, S5_NJ, S5_JW), h0_im.reshape(bsz, S5_NJ, S5_JW)], axis=-1)
    if chain:
        nb, srows, rows_per_b = bsz, 1, seq
        h0 = h0[:, :, None, :]
    else:
        assert bsz == SUBLANES and tm == bsz * seq
        nb, srows, rows_per_b = 1, SUBLANES, bsz * seq
        h0 = jnp.transpose(h0, (1, 0, 2))[None]
    nt = rows_per_b // tm
    seg = tm // SUBLANES
    uw = S5_JBLK * ch
    ncb = uw // LANES
    state_spec = pl.BlockSpec((1, 1, srows, 2 * S5_JW), lambda b, j, i: (b, j, 0, 0))
    y, hout = pl.pallas_call(
        functools.partial(_s5_kernel, dot=dot, seg=seg, chain=chain),
        grid=(nb, S5_NJ, nt),
        in_specs=[pl.BlockSpec((ncb, tm, LANES), lambda b, j, i: (j, b * nt + i, 0)),
                  pl.BlockSpec((1, uw, 2 * S5_JW), lambda b, j, i: (j, 0, 0)),
                  pl.BlockSpec((1, S5_JW, uw), lambda b, j, i: (j, 0, 0)),
                  pl.BlockSpec((1, S5_JW, uw), lambda b, j, i: (j, 0, 0)),
                  pl.BlockSpec((1, 2, S5_JW), lambda b, j, i: (j, 0, 0)),
                  pl.BlockSpec((1, uw), lambda b, j, i: (0, j)),
                  state_spec],
        out_specs=[pl.BlockSpec((ncb, tm, LANES), lambda b, j, i: (j, b * nt + i, 0)), state_spec],
        out_shape=[jax.ShapeDtypeStruct((g * ch // LANES, t, LANES), F32),
                   jax.ShapeDtypeStruct((nb, S5_NJ, srows, 2 * S5_JW), F32)],
        scratch_shapes=[pltpu.VMEM((tm, 2 * S5_JW), F32),
                        pltpu.VMEM((tm, 2 * S5_JW), w_dtype),
                        pltpu.VMEM((seg, 2 * S5_JW), F32),
                        pltpu.VMEM((tm, 2 * S5_JW), w_dtype),
                        pltpu.VMEM((SUBLANES, 2 * S5_JW), F32)],
        compiler_params=_cparams(("arbitrary", "arbitrary", "arbitrary")),
        name="s5_scan",
    )(u, bb, cc_re, cc_im, a_rows, d_skip.reshape(1, -1), h0)
    hout = hout[:, :, 0, :] if chain else jnp.transpose(hout[0], (1, 0, 2))
    s_re = hout[:, :, :S5_JW].reshape(bsz, g, p)
    s_im = hout[:, :, S5_JW:].reshape(bsz, g, p)
    return y, s_re, s_im


def _router_kernel(x_ref, sc_ref, sh_ref, w_ref, b_ref, o_ref, r_ref, cnt_ref, run_ref):
    h = x_ref[...] * (1.0 + sc_ref[...]) + sh_ref[...]
    logits = _dot3(h, w_ref[...]) + b_ref[...]
    lane = lax.broadcasted_iota(jnp.int32, logits.shape, 1).astype(F32)
    logits = jnp.where(lane < N_EXPERTS, logits, -jnp.inf)
    m1 = logits.max(axis=-1, keepdims=True)
    i1 = jnp.min(jnp.where(logits == m1, lane, float(LANES)), axis=-1, keepdims=True)
    rest = jnp.where(lane == i1, -jnp.inf, logits)
    m2 = rest.max(axis=-1, keepdims=True)
    i2 = jnp.min(jnp.where(rest == m2, lane, float(LANES)), axis=-1, keepdims=True)
    e2 = jnp.exp(m2 - m1)
    w1 = 1.0 / (1.0 + e2)
    w2 = e2 / (1.0 + e2)
    o_ref[...] = jnp.where(lane == i1, w1, 0.0) + jnp.where(lane == i2, w2, 0.0)

    step = pl.program_id(0)

    @pl.when(step == 0)
    def _():
        run_ref[...] = jnp.zeros_like(run_ref)

    tm = logits.shape[0]
    sel = jnp.where(jnp.logical_or(lane == i1, lane == i2), 1.0, 0.0)
    row = lax.broadcasted_iota(jnp.int32, (tm, tm), 0)
    col = lax.broadcasted_iota(jnp.int32, (tm, tm), 1)
    before = (row > col).astype(BF16)
    rank = _mxu(before, sel.astype(BF16)) + run_ref[0:1, :]
    pos1 = jnp.sum(jnp.where(lane == i1, rank, 0.0), axis=-1, keepdims=True)
    pos2 = jnp.sum(jnp.where(lane == i2, rank, 0.0), axis=-1, keepdims=True)
    run_ref[0:1, :] = run_ref[0:1, :] + jnp.sum(sel, axis=0, keepdims=True)
    cnt_ref[...] = run_ref[0:1, :]
    rec = jnp.zeros_like(logits)
    for k, val in enumerate((i1, i2, pos1, pos2, w1, w2)):
        rec = jnp.where(lane == float(k), val, rec)
    r_ref[...] = rec


ROUTE_E1, ROUTE_E2, ROUTE_POS1, ROUTE_POS2, ROUTE_W1, ROUTE_W2 = range(6)


def router(x, scale, shift, router_w, router_b):
    t, d = x.shape
    tm = min(t, 512)
    w = jnp.pad(router_w, ((0, 0), (0, LANES - N_EXPERTS)))
    b = jnp.pad(router_b.reshape(1, -1), ((0, 0), (0, LANES - N_EXPERTS)))
    return pl.pallas_call(
        _router_kernel,
        grid=(t // tm,),
        in_specs=[pl.BlockSpec((tm, d), lambda i: (i, 0)), _row_spec(scale, tm), _row_spec(shift, tm),
                  pl.BlockSpec((d, LANES), lambda i: (0, 0)), pl.BlockSpec((1, LANES), lambda i: (0, 0))],
        out_specs=[pl.BlockSpec((tm, LANES), lambda i: (i, 0)), pl.BlockSpec((tm, LANES), lambda i: (i, 0)),
                   pl.BlockSpec((1, LANES), lambda i: (0, 0))],
        out_shape=[jax.ShapeDtypeStruct((t, LANES), F32), jax.ShapeDtypeStruct((t, LANES), F32),
                   jax.ShapeDtypeStruct((1, LANES), F32)],
        scratch_shapes=[pltpu.VMEM((SUBLANES, LANES), F32)],
        compiler_params=_cparams(("arbitrary",)),
        name="router",
    )(x, scale, shift, w, b)


MOE_TILE = 256
MOE_UP_TN = D_FF_EXPERT // 2
DMA_ISSUE_UNROLL = 8


def _dispatch_kernel(fill_ref, d1_ref, d2_ref, h_ref, xs_ref, zero_ref, sem, zsem):
    tm = h_ref.shape[0]

    @pl.when(pl.program_id(0) == 0)
    def _():
        zero_ref[...] = jnp.zeros_like(zero_ref)

        def fill_copy(k):
            row0 = pl.multiple_of(jnp.maximum(fill_ref[k], 0), MOE_TILE)
            return pltpu.make_async_copy(zero_ref, xs_ref.at[pl.ds(row0, MOE_TILE)], zsem)

        for k in range(fill_ref.shape[0]):
            @pl.when(fill_ref[k] >= 0)
            def _():
                fill_copy(k).start()

        for k in range(fill_ref.shape[0]):
            @pl.when(fill_ref[k] >= 0)
            def _():
                fill_copy(k).wait()

    def issue(t, carry):
        src = h_ref.at[pl.ds(t, 1)]
        pltpu.make_async_copy(src, xs_ref.at[pl.ds(d1_ref[0, 0, t], 1)], sem).start()
        pltpu.make_async_copy(src, xs_ref.at[pl.ds(d2_ref[0, 0, t], 1)], sem).start()
        return carry

    lax.fori_loop(0, tm, issue, 0, unroll=DMA_ISSUE_UNROLL)
    pltpu.make_async_copy(xs_ref.at[pl.ds(0, 2 * tm)], xs_ref.at[pl.ds(0, 2 * tm)], sem).wait()


def moe_dispatch(h, dest1, dest2, fill_rows, n_rows):
    t, d = h.shape
    tm = 512
    idx_spec = pl.BlockSpec((1, 1, tm), lambda i, fr: (i, 0, 0), memory_space=pltpu.SMEM)
    grid_spec = pltpu.PrefetchScalarGridSpec(
        num_scalar_prefetch=1,
        grid=(t // tm,),
        in_specs=[idx_spec, idx_spec, pl.BlockSpec((tm, d), lambda i, fr: (i, 0))],
        out_specs=pl.BlockSpec(memory_space=pl.ANY),
        scratch_shapes=[pltpu.VMEM((MOE_TILE, d), h.dtype), pltpu.SemaphoreType.DMA(()),
                        pltpu.SemaphoreType.DMA(())])
    return pl.pallas_call(
        _dispatch_kernel,
        grid_spec=grid_spec,
        out_shape=jax.ShapeDtypeStruct((n_rows, d), h.dtype),
        compiler_params=_cparams(("arbitrary",)),
        name="moe_dispatch",
    )(fill_rows, dest1.reshape(t // tm, 1, tm), dest2.reshape(t // tm, 1, tm), h)


def _expert_up_kernel(te_ref, nv_ref, x_ref, wg_ref, wu_ref, o_ref, *, dot):
    del te_ref

    @pl.when(pl.program_id(1) < nv_ref[0])
    def _():
        gu = dot(x_ref[...], jnp.concatenate([wg_ref[0], wu_ref[0]], axis=1))
        tn = o_ref.shape[1]
        o_ref[...] = (_silu(gu[:, :tn]) * gu[:, tn:]).astype(o_ref.dtype)

    @pl.when(pl.program_id(1) >= nv_ref[0])
    def _():
        o_ref[...] = jnp.zeros_like(o_ref)


def expert_up(xs, w_up, tile_expert, n_valid, dot, tn):
    r, d = xs.shape
    f = w_up.shape[2] // 2
    nf = f // tn
    nt = r // MOE_TILE
    grid_spec = pltpu.PrefetchScalarGridSpec(
        num_scalar_prefetch=2,
        grid=(nf, nt),
        in_specs=[pl.BlockSpec((MOE_TILE, d), lambda j, i, te, nv: (i, 0)),
                  pl.BlockSpec((1, d, tn), lambda j, i, te, nv: (te[i], 0, j)),
                  pl.BlockSpec((1, d, tn), lambda j, i, te, nv: (te[i], 0, j + nf))],
        out_specs=pl.BlockSpec((MOE_TILE, tn), lambda j, i, te, nv: (i, j)))
    return pl.pallas_call(
        functools.partial(_expert_up_kernel, dot=dot),
        grid_spec=grid_spec,
        out_shape=jax.ShapeDtypeStruct((r, f), BF16),
        compiler_params=_cparams(("arbitrary", "arbitrary")),
        name="expert_up",
    )(tile_expert, n_valid, xs, w_up, w_up)


def _expert_down_kernel(te_ref, nv_ref, a_ref, w_ref, o_ref, *, dot):
    del te_ref

    @pl.when(pl.program_id(0) < nv_ref[0])
    def _():
        o_ref[...] = dot(a_ref[...], w_ref[0])

    @pl.when(pl.program_id(0) >= nv_ref[0])
    def _():
        o_ref[...] = jnp.zeros_like(o_ref)


def expert_down(act, w_down, tile_expert, n_valid, dot):
    r, f = act.shape
    d = w_down.shape[2]
    grid_spec = pltpu.PrefetchScalarGridSpec(
        num_scalar_prefetch=2,
        grid=(r // MOE_TILE,),
        in_specs=[pl.BlockSpec((MOE_TILE, f), lambda i, te, nv: (i, 0)),
                  pl.BlockSpec((1, f, d), lambda i, te, nv: (te[i], 0, 0))],
        out_specs=pl.BlockSpec((MOE_TILE, d), lambda i, te, nv: (i, 0)))
    return pl.pallas_call(
        functools.partial(_expert_down_kernel, dot=dot),
        grid_spec=grid_spec,
        out_shape=jax.ShapeDtypeStruct((r, d), F32),
        compiler_params=_cparams(("arbitrary",)),
        name="expert_down",
    )(tile_expert, n_valid, act, w_down)


def _combine_ln_kernel(d1_ref, d2_ref, d1n_ref, d2n_ref, y_ref, rec_ref, x_ref, gate_ref, g_ref, b_ref, xo_ref,
                       buf1, buf2, sem):
    tm = x_ref.shape[0]
    step = pl.program_id(0)
    slot = step % 2

    def gather(idx1, idx2, to):
        def issue(t, carry):
            pltpu.make_async_copy(y_ref.at[pl.ds(idx1[0, 0, t], 1)], buf1.at[to, pl.ds(t, 1)], sem.at[to]).start()
            pltpu.make_async_copy(y_ref.at[pl.ds(idx2[0, 0, t], 1)], buf2.at[to, pl.ds(t, 1)], sem.at[to]).start()
            return carry

        lax.fori_loop(0, tm, issue, 0, unroll=DMA_ISSUE_UNROLL)

    @pl.when(step == 0)
    def _():
        gather(d1_ref, d2_ref, 0)

    @pl.when(step + 1 < pl.num_programs(0))
    def _():
        gather(d1n_ref, d2n_ref, 1 - slot)

    pltpu.make_async_copy(y_ref.at[pl.ds(0, tm)], buf1.at[slot], sem.at[slot]).wait()
    pltpu.make_async_copy(y_ref.at[pl.ds(0, tm)], buf2.at[slot], sem.at[slot]).wait()

    rec = rec_ref[...]
    lane = lax.broadcasted_iota(jnp.int32, rec.shape, 1)
    w1 = jnp.sum(jnp.where(lane == ROUTE_W1, rec, 0.0), axis=-1, keepdims=True)
    w2 = jnp.sum(jnp.where(lane == ROUTE_W2, rec, 0.0), axis=-1, keepdims=True)
    out = w1 * buf1[slot] + w2 * buf2[slot]
    r = ALPHA * x_ref[...] + (1.0 + gate_ref[...]) * out
    mu = jnp.mean(r, axis=-1, keepdims=True)
    dev = r - mu
    var = jnp.mean(dev * dev, axis=-1, keepdims=True)
    xo_ref[...] = dev * lax.rsqrt(var + LN_EPS) * g_ref[...] + b_ref[...]


def moe_combine_ln(y_sorted, dest1, dest2, rec, x, gate, ln_g, ln_b):
    t, d = x.shape
    tm = 256
    nt = t // tm
    idx_spec = pl.BlockSpec((1, 1, tm), lambda i: (i, 0, 0), memory_space=pltpu.SMEM)
    nxt_spec = pl.BlockSpec((1, 1, tm), lambda i: (jnp.minimum(i + 1, nt - 1), 0, 0), memory_space=pltpu.SMEM)
    d1 = dest1.reshape(nt, 1, tm)
    d2 = dest2.reshape(nt, 1, tm)
    return pl.pallas_call(
        _combine_ln_kernel,
        grid=(nt,),
        in_specs=[idx_spec, idx_spec, nxt_spec, nxt_spec, pl.BlockSpec(memory_space=pl.ANY),
                  pl.BlockSpec((tm, LANES), lambda i: (i, 0)), pl.BlockSpec((tm, d), lambda i: (i, 0)),
                  _row_spec(gate, tm), pl.BlockSpec((1, d), lambda i: (0, 0)), pl.BlockSpec((1, d), lambda i: (0, 0))],
        out_specs=pl.BlockSpec((tm, d), lambda i: (i, 0)),
        out_shape=jax.ShapeDtypeStruct((t, d), F32),
        scratch_shapes=[pltpu.VMEM((2, tm, d), F32), pltpu.VMEM((2, tm, d), F32), pltpu.SemaphoreType.DMA((2,))],
        compiler_params=_cparams(("arbitrary",)),
        name="moe_combine_ln",
    )(d1, d2, d1, d2, y_sorted, rec, x, gate, ln_g.reshape(1, d), ln_b.reshape(1, d))


def moe_top2(h, x, rec, counts, gate, ln_g, ln_b, w_up, w_down, dot):
    t, d = h.shape
    n_tiles = (TOP_K * t) // MOE_TILE + N_EXPERTS
    n_rows = n_tiles * MOE_TILE
    cnt = counts[0, :N_EXPERTS].astype(jnp.int32)
    padded = ((cnt + MOE_TILE - 1) // MOE_TILE) * MOE_TILE
    ends = jnp.cumsum(padded)
    starts = ends - padded
    col = lambda k: rec[:, k].astype(jnp.int32)
    experts = jnp.arange(N_EXPERTS, dtype=jnp.int32)[None, :]
    start_of = lambda e: jnp.sum(jnp.where(e[:, None] == experts, starts[None, :], 0), axis=1)
    dest1 = start_of(col(ROUTE_E1)) + col(ROUTE_POS1)
    dest2 = start_of(col(ROUTE_E2)) + col(ROUTE_POS2)
    tile_start = jnp.arange(n_tiles, dtype=jnp.int32) * MOE_TILE
    tile_expert = jnp.minimum(jnp.sum(tile_start[:, None] >= ends[None, :], axis=1), N_EXPERTS - 1).astype(jnp.int32)
    n_valid = (ends[-1:] // MOE_TILE).astype(jnp.int32)
    last_tile = jnp.where(padded > 0, ends - MOE_TILE, -1)
    spare = ends[-1] + jnp.arange(N_EXPERTS, dtype=jnp.int32) * MOE_TILE
    fill_rows = jnp.concatenate([last_tile, jnp.where(spare < n_rows, spare, -1)]).astype(jnp.int32)
    xs = moe_dispatch(h, dest1, dest2, fill_rows, n_rows)
    act = expert_up(xs, w_up, tile_expert, n_valid, dot, tn=MOE_UP_TN)
    y_sorted = expert_down(act, w_down, tile_expert, n_valid, dot)
    return moe_combine_ln(y_sorted, dest1, dest2, rec, x, gate, ln_g, ln_b)


def _trunk(x, mods, caches, wts, precise):
    bsz, seq, d = x.shape
    t = bsz * seq
    prompt = caches is None
    dot = _dot3 if precise else _dot1
    act_dtype = F32 if precise else BF16
    tm = min(t, 512)
    x0 = x.reshape(t, d)

    sh, sc, gt = mods[0][0]
    proj, dt_raw = in_projection(x0, sc, sh, wts["w_in0"], dot, act_dtype)
    keep = min(A_WINDOW, seq) if prompt else seq
    assert keep == seq or bsz == 1
    kv_rows = proj[t - bsz * keep:, A_WIDTH:3 * A_WIDTH]
    k_new = kv_rows[:, :A_WIDTH].reshape(bsz, keep, A_HEADS, A_HEAD_DIM)
    v_new = kv_rows[:, A_WIDTH:].reshape(bsz, keep, A_HEADS, A_HEAD_DIM)
    if prompt:
        att = attention_prompt(proj, wts["rel_bias"], dot, act_dtype)
        conv_buf = jnp.zeros((bsz, B_CONV - 1, B_CONV_DIM), F32)
        ssm_h0 = jnp.zeros((bsz, B_HEADS, B_HEAD_DIM, B_STATE), F32)
        q_len = min(seq, 256)
    else:
        att = attention_sample(proj, caches["k"], caches["v"], wts["rel_bias"], bsz, seq, dot, act_dtype)
        conv_buf, ssm_h0 = caches["conv"], caches["ssm"]
        q_len = seq
    y_ssd, conv_new, ssm_new = ssd_mixer(proj, dt_raw, conv_buf, ssm_h0, wts["conv_w"], wts["conv_b"], wts["dt_bias"],
                                         wts["a_log"], wts["ssd_d"], wts["ssd_norm_g"], bsz, seq, q_len,
                                         dot, act_dtype)
    x1, h1 = matmul_ln([att, y_ssd], wts["w_out0"], x0, gt, wts["ln_g"][0, 0], wts["ln_b"][0, 0], dot, tm,
                       nxt=(mods[0][1][1], mods[0][1][0]), next_dtype=act_dtype)
    act = ffn_up(h1, wts["ffn_w_up"], dot, act_dtype, tm=min(t, 1024), tn=512)
    x2, h2 = matmul_ln([act], wts["ffn_w_down"], x1, mods[0][1][2], wts["ln_g"][0, 1], wts["ln_b"][0, 1], dot,
                       min(t, 256), tk=None if prompt else 1408, nxt=(mods[1][0][1], mods[1][0][0]), next_dtype=act_dtype)

    u = matmul_slabs(h2, wts["w_in1"], dot)
    if prompt:
        s5_re0 = jnp.zeros((bsz, C_GROUPS, C_STATE), F32)
        s5_im0 = jnp.zeros((bsz, C_GROUPS, C_STATE), F32)
    else:
        s5_re0, s5_im0 = caches["s5_re"], caches["s5_im"]
    y5, s5_re, s5_im = s5_mixer(u, s5_re0, s5_im0, wts["s5_lam_re"], wts["s5_lam_im"], wts["s5_log_step"],
                                wts["s5_b_re"], wts["s5_b_im"], wts["s5_c_re"], wts["s5_c_im"], wts["s5_d"],
                                bsz, seq, 512 if prompt else bsz * seq, dot, F32 if precise else BF16,
                                chain=prompt)
    x3, h3 = matmul_ln([y5], wts["glu_w"], x2, mods[1][0][2], wts["ln_g"][1, 0], wts["ln_b"][1, 0], dot,
                       min(t, 256), tk=None if prompt else 512, glu=True,
                       nxt=(mods[1][1][1], mods[1][1][0]), next_dtype=F32)
    gates, rec, counts = router(x3, mods[1][1][1], mods[1][1][0], wts["router_w"], wts["router_b"])
    if prompt:
        x4 = moe_top2(h3, x3, rec, counts, mods[1][1][2], wts["ln_g"][1, 1], wts["ln_b"][1, 1],
                      wts["moe_w_up"], wts["moe_w_down"], dot)
    else:
        act = ffn_up(h3, wts["moe_w_up"], _dot1, BF16, tm=min(t, 2048), tn=MOE_UP_TN, gates=gates)
        x4, _ = matmul_ln([act], wts["moe_w_down"], x3, mods[1][1][2], wts["ln_g"][1, 1], wts["ln_b"][1, 1],
                          _dot1, tm, tk=D_FF_EXPERT)
    return (x4.reshape(bsz, seq, d), k_new[None], v_new[None], conv_new[None], ssm_new[None],
            s5_re[None], s5_im[None])


def kernel(x_prompt, x_sample, cache_attn_k, cache_attn_v, state_ssd_conv, state_ssd, state_s5_re, state_s5_im, c_prompt, c_sample, ada_w, ada_b, ln_g, ln_b, w_in0, w_out0, rel_bias, conv_w, conv_b, dt_bias, a_log, ssd_d, ssd_norm_g, ffn_w_up, ffn_w_down, w_in1, s5_lam_re, s5_lam_im, s5_log_step, s5_b_re, s5_b_im, s5_c_re, s5_c_im, s5_d, glu_w, router_w, router_b, moe_w_up, moe_w_down):
    d = D_MODEL
    bp, lp, _ = x_prompt.shape
    bs, ls, _ = x_sample.shape

    n_c = bp + bs
    rows = ((n_c + SUBLANES - 1) // SUBLANES) * SUBLANES
    c_all = jnp.pad(jnp.concatenate([c_prompt, c_sample], axis=0), ((0, rows - n_c), (0, 0)))
    mod = adaln_all(c_all, ada_w, ada_b).reshape(DEPTH, 2, rows, 3, d)

    def mods_for(r0, nb, per_row):
        out = []
        for layer in range(DEPTH):
            out.append([])
            for j in range(2):
                trip = []
                for part in range(3):
                    m = mod[layer, j, r0:r0 + nb, part]
                    trip.append(jnp.repeat(m, per_row, axis=0) if nb > 1 else m)
                out[-1].append(tuple(trip))
        return out

    assert bp == 1
    mods_p = mods_for(0, bp, lp)
    mods_s = mods_for(bp, bs, ls)

    shared = dict(ln_g=ln_g, ln_b=ln_b, rel_bias=rel_bias[0], conv_w=conv_w[0], conv_b=conv_b[0],
                  dt_bias=dt_bias[0], a_log=a_log[0], ssd_d=ssd_d[0], ssd_norm_g=ssd_norm_g[0],
                  s5_lam_re=s5_lam_re[0], s5_lam_im=s5_lam_im[0], s5_log_step=s5_log_step[0],
                  s5_b_re=s5_b_re[0], s5_b_im=s5_b_im[0], s5_c_re=s5_c_re[0], s5_c_im=s5_c_im[0], s5_d=s5_d[0],
                  router_w=router_w[0], router_b=router_b[0])
    big = dict(w_in0=w_in0[0], w_out0=w_out0[0], ffn_w_up=ffn_w_up, ffn_w_down=ffn_w_down[0],
               w_in1=w_in1[0], glu_w=glu_w[0])
    moe = dict(moe_w_up=moe_w_up[0].astype(BF16), moe_w_down=moe_w_down[0].astype(BF16))
    wts_p = dict(shared, **moe, **{k: v.astype(BF16) for k, v in big.items()})
    wts_s = dict(shared, **moe, **big)

    y_p, k_p, v_p, conv_p, ssd_p, re_p, im_p = _trunk(x_prompt, mods_p, None, wts_p, precise=False)
    caches = dict(k=cache_attn_k[0], v=cache_attn_v[0], conv=state_ssd_conv[0], ssm=state_ssd[0],
                  s5_re=state_s5_re[0], s5_im=state_s5_im[0])
    y_s, k_s, v_s, conv_s, ssd_s, re_s, im_s = _trunk(x_sample, mods_s, caches, wts_s, precise=True)
    return (y_p, y_s, k_p, v_p, conv_p, ssd_p, re_p, im_p, k_s, v_s, conv_s, ssd_s, re_s, im_s)
```

```python
import functools
import math

import numpy as np
import jax
import jax.numpy as jnp
from jax import lax
from jax.experimental import pallas as pl
from jax.experimental.pallas import tpu as pltpu

F32 = jnp.float32
BF16 = jnp.bfloat16

D_MODEL = 2048
DEPTH = 2
PAST_LEN = 2048
CHUNK = 64
A_HEADS = 16
A_HEAD_DIM = 64
A_WIDTH = A_HEADS * A_HEAD_DIM
A_PAST_CHUNKS = 8
A_WINDOW = A_PAST_CHUNKS * CHUNK
REL_CLIP = 128
B_HEADS = 16
B_HEAD_DIM = 64
B_WIDTH = B_HEADS * B_HEAD_DIM
B_GROUPS = 2
B_STATE = 128
B_CONV = 4
B_CONV_DIM = B_WIDTH + 2 * B_GROUPS * B_STATE
C_GROUP_CH = 16
C_GROUPS = D_MODEL // C_GROUP_CH
C_STATE = 64
D_FF = 5632
N_EXPERTS = 8
TOP_K = 2
D_FF_EXPERT = 2816
ALPHA = (2.0 * DEPTH) ** 0.25
LN_EPS = 1e-5
RMS_EPS = 1e-5
NEG_INF = -1e30
IN0_WIDTH = 3 * A_WIDTH + B_WIDTH + B_CONV_DIM + B_HEADS
IN0_MAIN = IN0_WIDTH - B_HEADS

LANES = 128
SUBLANES = 8
VMEM_LIMIT = 56 * 1024 * 1024

LN_SUBROWS = 256
ATT_QBLOCK = 512
ATT_QSUB = 256
ATT_KSUB = ATT_QSUB + A_WINDOW
S5_JBLK = 16
S5_NJ = C_GROUPS // S5_JBLK
S5_JW = S5_JBLK * C_STATE


def _cparams(sem):
    return pltpu.CompilerParams(dimension_semantics=sem, vmem_limit_bytes=VMEM_LIMIT)


def _split_bf16(x):
    hi = x.astype(BF16)
    lo = (x - hi.astype(F32)).astype(BF16)
    return hi, lo


def _mxu(a, b, nt=False):
    if nt:
        return lax.dot_general(a, b, (((1,), (1,)), ((), ())), preferred_element_type=F32)
    return jnp.dot(a, b, preferred_element_type=F32)


def _dot1(a, b, nt=False):
    return _mxu(a.astype(BF16), b.astype(BF16), nt)


def _dot3(a, b, nt=False):
    ah, al = _split_bf16(a.astype(F32))
    bh, bl = _split_bf16(b.astype(F32))
    return _mxu(ah, bh, nt) + (_mxu(ah, bl, nt) + _mxu(al, bh, nt))


def _dot_sel(sel_bf16, x):
    x1 = x.astype(BF16)
    r1 = x - x1.astype(F32)
    x2 = r1.astype(BF16)
    x3 = (r1 - x2.astype(F32)).astype(BF16)
    return _mxu(sel_bf16, x1) + (_mxu(sel_bf16, x2) + _mxu(sel_bf16, x3))


def _dot_rsel(x, sel_bf16):
    x1 = x.astype(BF16)
    r1 = x - x1.astype(F32)
    x2 = r1.astype(BF16)
    x3 = (r1 - x2.astype(F32)).astype(BF16)
    return _mxu(x1, sel_bf16) + (_mxu(x2, sel_bf16) + _mxu(x3, sel_bf16))


def _silu(x):
    return x * jax.nn.sigmoid(x)


def _row_spec(arr, tm):
    d = arr.shape[-1]
    if arr.shape[0] == 1:
        return pl.BlockSpec((1, d), lambda i, *_: (0, 0))
    return pl.BlockSpec((tm, d), lambda i, *_: (i, 0))


def _adaln_kernel(c_ref, w_ref, b_ref, o_ref):
    c = c_ref[...]
    o_ref[0] = _dot3(_silu(c), w_ref[0]) + b_ref[0]


def adaln_all(c_rows, ada_w, ada_b):
    r, d = c_rows.shape
    n = ada_w.shape[-1]
    tn = 1024
    w = ada_w.reshape(2 * DEPTH, d, n)
    b = ada_b.reshape(2 * DEPTH, 1, n)
    return pl.pallas_call(
        _adaln_kernel,
        grid=(2 * DEPTH, n // tn),
        in_specs=[pl.BlockSpec((r, d), lambda i, j: (0, 0)),
                  pl.BlockSpec((1, d, tn), lambda i, j: (i, 0, j)),
                  pl.BlockSpec((1, 1, tn), lambda i, j: (i, 0, j))],
        out_specs=pl.BlockSpec((1, r, tn), lambda i, j: (i, 0, j)),
        out_shape=jax.ShapeDtypeStruct((2 * DEPTH, r, n), F32),
        compiler_params=_cparams(("arbitrary", "arbitrary")),
        name="adaln",
    )(c_rows, w, b)


def _mm_kernel(a_ref, w_ref, o_ref, *, dot):
    res = dot(a_ref[...], w_ref[...])
    for cb in range(o_ref.shape[0]):
        o_ref[cb] = res[:, cb * LANES:(cb + 1) * LANES]


def matmul_slabs(a, w, dot, tn=1024):
    t, k = a.shape
    n = w.shape[1]
    tm = min(t, 1024)
    return pl.pallas_call(
        functools.partial(_mm_kernel, dot=dot),
        grid=(t // tm, n // tn),
        in_specs=[pl.BlockSpec((tm, k), lambda i, j: (i, 0)), pl.BlockSpec((k, tn), lambda i, j: (0, j))],
        out_specs=pl.BlockSpec((tn // LANES, tm, LANES), lambda i, j: (j, i, 0)),
        out_shape=jax.ShapeDtypeStruct((n // LANES, t, LANES), F32),
        compiler_params=_cparams(("arbitrary", "arbitrary")),
        name="matmul",
    )(a, w)


def _in_proj_kernel(x_ref, sc_ref, sh_ref, w_ref, wdt_ref, o_ref, dt_ref, h_ref, *, dot):
    j = pl.program_id(1)

    @pl.when(j == 0)
    def _():
        h_ref[...] = (x_ref[...] * (1.0 + sc_ref[...]) + sh_ref[...]).astype(h_ref.dtype)

    h = h_ref[...]
    o_ref[...] = dot(h, w_ref[...])

    @pl.when(j == pl.num_programs(1) - 1)
    def _():
        dt_ref[...] = dot(h, wdt_ref[...])


def in_projection(x, scale, shift, w_in, dot, h_dtype):
    t, k = x.shape
    tm = min(t, 1024)
    tn = 512
    n_tail = w_in.shape[1] - IN0_MAIN
    w_dt = jnp.pad(w_in[:, IN0_MAIN:], ((0, 0), (0, LANES - n_tail)))
    return pl.pallas_call(
        functools.partial(_in_proj_kernel, dot=dot),
        grid=(t // tm, IN0_MAIN // tn),
        in_specs=[pl.BlockSpec((tm, k), lambda i, j: (i, 0)), _row_spec(scale, tm), _row_spec(shift, tm),
                  pl.BlockSpec((k, tn), lambda i, j: (0, j)),
                  pl.BlockSpec((k, LANES), lambda i, j: (0, 0))],
        out_specs=[pl.BlockSpec((tm, tn), lambda i, j: (i, j)), pl.BlockSpec((tm, LANES), lambda i, j: (i, 0))],
        out_shape=[jax.ShapeDtypeStruct((t, IN0_MAIN), F32), jax.ShapeDtypeStruct((t, LANES), F32)],
        scratch_shapes=[pltpu.VMEM((tm, k), h_dtype)],
        compiler_params=_cparams(("arbitrary", "arbitrary")),
        name="in_proj",
    )(x, scale, shift, w_in, w_dt)


def _ffn_up_kernel(*refs, dot, gated):
    if gated:
        a_ref, wg_ref, wu_ref, gates_ref, o_ref = refs
    else:
        a_ref, wg_ref, wu_ref, o_ref = refs
    a = a_ref[...]
    g = dot(a, wg_ref[0])
    u = dot(a, wu_ref[0])
    act = _silu(g) * u
    if gated:
        e = pl.program_id(1)
        gt = gates_ref[...]
        lane = lax.broadcasted_iota(jnp.int32, gt.shape, 1)
        act = act * jnp.sum(jnp.where(lane == e, gt, 0.0), axis=1, keepdims=True)
    o_ref[...] = act.astype(o_ref.dtype)


def ffn_up(a, w_up, dot, out_dtype, tm, tn, gates=None):
    t, d = a.shape
    e_n, _, f2 = w_up.shape
    f = f2 // 2
    nf = f // tn
    in_specs = [pl.BlockSpec((tm, d), lambda i, e, j: (i, 0)),
                pl.BlockSpec((1, d, tn), lambda i, e, j: (e, 0, j)),
                pl.BlockSpec((1, d, tn), lambda i, e, j: (e, 0, j + nf))]
    args = [a, w_up, w_up]
    if gates is not None:
        in_specs.append(pl.BlockSpec((tm, LANES), lambda i, e, j: (i, 0)))
        args.append(gates)
    return pl.pallas_call(
        functools.partial(_ffn_up_kernel, dot=dot, gated=gates is not None),
        grid=(t // tm, e_n, nf),
        in_specs=in_specs,
        out_specs=pl.BlockSpec((tm, tn), lambda i, e, j: (i, e * nf + j)),
        out_shape=jax.ShapeDtypeStruct((t, e_n * f), out_dtype),
        compiler_params=_cparams(("arbitrary", "arbitrary", "arbitrary")),
        name="ffn_up",
    )(*args)


def _mm_ln_kernel(*refs, n_a, nk, dot, glu, has_next):
    a_refs = refs[:n_a]
    w_refs = refs[n_a:2 * n_a]
    pos = 2 * n_a
    x_ref, gate_ref, g_ref, b_ref = refs[pos:pos + 4]
    pos += 4
    if has_next:
        sc_ref, sh_ref = refs[pos:pos + 2]
        pos += 2
    xo_ref = refs[pos]
    pos += 1
    if has_next:
        ho_ref = refs[pos]
        pos += 1
    acc_ref = refs[pos] if nk > 1 else None

    tm = x_ref.shape[0]
    sub = min(tm, max(LANES, min(LN_SUBROWS, tm // 2)))

    def load_a(a_ref, rows):
        if len(a_ref.shape) == 3:
            return jnp.concatenate([a_ref[cb, rows, :] for cb in range(a_ref.shape[0])], axis=1)
        return a_ref[rows, :]

    def product(rows):
        w0 = w_refs[0][0] if len(w_refs[0].shape) == 3 else w_refs[0][...]
        part = dot(load_a(a_refs[0], rows), w0)
        for a_ref, w_ref in zip(a_refs[1:], w_refs[1:]):
            part = part + dot(load_a(a_ref, rows), w_ref[...])
        return part

    def per_row(ref, rows):
        return ref[...] if ref.shape[0] == 1 else ref[rows, :]

    def epilogue(acc, rows):
        if glu:
            d = acc.shape[1] // 2
            out = acc[:, :d] * jax.nn.sigmoid(acc[:, d:])
        else:
            out = acc
        r = ALPHA * x_ref[rows, :] + (1.0 + per_row(gate_ref, rows)) * out
        mu = jnp.mean(r, axis=-1, keepdims=True)
        dev = r - mu
        var = jnp.mean(dev * dev, axis=-1, keepdims=True)
        xn = dev * lax.rsqrt(var + LN_EPS) * g_ref[...] + b_ref[...]
        xo_ref[rows, :] = xn
        if has_next:
            ho_ref[rows, :] = (xn * (1.0 + per_row(sc_ref, rows)) + per_row(sh_ref, rows)).astype(ho_ref.dtype)

    blocks = [slice(r0, r0 + sub) for r0 in range(0, tm, sub)]
    if nk == 1:
        for rows in blocks:
            epilogue(product(rows), rows)
    else:
        k = pl.program_id(1)
        part = product(slice(0, tm))

        @pl.when(k == 0)
        def _():
            acc_ref[...] = part

        @pl.when(k > 0)
        def _():
            acc_ref[...] += part

        @pl.when(k == nk - 1)
        def _():
            for rows in blocks:
                epilogue(acc_ref[rows, :], rows)


def matmul_ln(a_list, w, x, gate, ln_g, ln_b, dot, tm, tk=None, glu=False, nxt=None, next_dtype=None):
    t, d = x.shape
    n = w.shape[-1]
    n_a = len(a_list)
    if n_a > 1:
        nk = 1
        in_specs = [pl.BlockSpec((tm, a.shape[1]), lambda i, k: (i, 0)) for a in a_list]
        off = 0
        for a in a_list:
            ka = a.shape[1]
            assert off % ka == 0
            in_specs.append(pl.BlockSpec((ka, n), functools.partial(lambda i, k, o: (o, 0), o=off // ka),
                                         pipeline_mode=pl.Buffered(1)))
            off += ka
        w_args = [w] * n_a
    else:
        a0 = a_list[0]
        split_cols = a0.ndim == 3
        ktot = a0.shape[0] * LANES if split_cols else a0.shape[1]
        tk = ktot if tk is None else tk
        nk = ktot // tk
        if split_cols:
            a_spec = pl.BlockSpec((tk // LANES, tm, LANES), lambda i, k: (k, i, 0))
        else:
            a_spec = pl.BlockSpec((tm, tk), lambda i, k: (i, k))
        w_mode = dict(pipeline_mode=pl.Buffered(1)) if nk == 1 else {}
        if w.ndim == 3:
            assert w.shape[1] == tk
            w_spec = pl.BlockSpec((1, tk, n), lambda i, k: (k, 0, 0))
        else:
            w_spec = pl.BlockSpec((tk, n), lambda i, k: (k, 0), **w_mode)
        in_specs = [a_spec, w_spec]
        w_args = [w]
    in_specs += [pl.BlockSpec((tm, d), lambda i, k: (i, 0)), _row_spec(gate, tm),
                 pl.BlockSpec((1, d), lambda i, k: (0, 0)), pl.BlockSpec((1, d), lambda i, k: (0, 0))]
    args = list(a_list) + w_args + [x, gate, ln_g.reshape(1, d), ln_b.reshape(1, d)]
    out_specs = [pl.BlockSpec((tm, d), lambda i, k: (i, 0))]
    out_shape = [jax.ShapeDtypeStruct((t, d), F32)]
    if nxt is not None:
        in_specs += [_row_spec(nxt[0], tm), _row_spec(nxt[1], tm)]
        args += [nxt[0], nxt[1]]
        out_specs.append(pl.BlockSpec((tm, d), lambda i, k: (i, 0)))
        out_shape.append(jax.ShapeDtypeStruct((t, d), next_dtype))
    scratch = [pltpu.VMEM((tm, n), F32)] if nk > 1 else []
    res = pl.pallas_call(
        functools.partial(_mm_ln_kernel, n_a=n_a, nk=nk, dot=dot, glu=glu, has_next=nxt is not None),
        grid=(t // tm, nk),
        in_specs=in_specs,
        out_specs=out_specs,
        out_shape=out_shape,
        scratch_shapes=scratch,
        compiler_params=_cparams(("arbitrary", "arbitrary")),
        name="matmul_ln",
    )(*args)
    return res if nxt is not None else (res[0], None)


def _softmax_pv(scores, values, dot):
    m = scores[0].max(axis=-1, keepdims=True)
    for s in scores[1:]:
        m = jnp.maximum(m, s.max(axis=-1, keepdims=True))
    num = None
    den = None
    for s, v in zip(scores, values):
        p = jnp.exp(s - m)
        l = p.sum(axis=-1, keepdims=True)
        o = dot(p, v)
        num = o if num is None else num + o
        den = l if den is None else den + l
    return num / den


def _attn_prompt_kernel(q_ref, kp_ref, kc_ref, vp_ref, vc_ref, bias_ref, o_ref, *, dot):
    i = pl.program_id(0)
    lo = lax.broadcasted_iota(jnp.int32, (1, LANES), 1) < A_HEAD_DIM
    q = q_ref[...] * (A_HEAD_DIM ** -0.5)
    k = jnp.concatenate([kp_ref[...], kc_ref[...]], axis=0)
    v = jnp.concatenate([vp_ref[...], vc_ref[...]], axis=0)
    kidx = lax.broadcasted_iota(jnp.int32, (1, ATT_KSUB), 1)
    for sub in range(ATT_QBLOCK // ATT_QSUB):
        r0 = sub * ATT_QSUB
        qs = q[r0:r0 + ATT_QSUB]
        ks = k[r0:r0 + ATT_KSUB]
        vs = v[r0:r0 + ATT_KSUB]
        before = jnp.where(jnp.logical_and(i == 0, kidx + r0 < ATT_QBLOCK), NEG_INF, 0.0)
        outs = []
        for hh in range(2):
            qm = jnp.where(lo if hh == 0 else jnp.logical_not(lo), qs, 0.0)
            s = dot(qm, ks, nt=True) + bias_ref[0, hh] + before
            outs.append(_softmax_pv([s], [vs], dot))
        o_ref[r0:r0 + ATT_QSUB, :] = jnp.where(lo, outs[0], outs[1]).astype(o_ref.dtype)


def _band_table_kernel(base_ref, allow_ref, o_ref):
    lq, lk = o_ref.shape[1], o_ref.shape[2]
    wide = jnp.broadcast_to(base_ref[0], (lq, base_ref.shape[2]))
    toep = pltpu.roll(wide, 0, 1, stride=1, stride_axis=0)
    o_ref[0] = jnp.where(allow_ref[...] > 0.0, toep[:, :lk], NEG_INF)


def _band_tables(rel_bias, lq, lk, q_pos0, k_pos0):
    n_heads = rel_bias.shape[0]
    width = pl.next_power_of_2(lq + lk)
    c = np.arange(width)
    m = np.where(c < lk, c, c - width)
    idx = np.clip((q_pos0 - k_pos0) - m, -REL_CLIP, REL_CLIP) + REL_CLIP
    base = rel_bias[:, idx].astype(F32).reshape(n_heads, 1, width)
    q_pos = q_pos0 + np.arange(lq)
    k_pos = k_pos0 + np.arange(lk)
    qc = q_pos[:, None] // CHUNK
    kc = k_pos[None, :] // CHUNK
    allowed = ((kc <= qc) & (kc >= qc - A_PAST_CHUNKS) & (k_pos[None, :] >= 0)).astype(np.float32)
    return pl.pallas_call(
        _band_table_kernel,
        grid=(n_heads,),
        in_specs=[pl.BlockSpec((1, 1, width), lambda h: (h, 0, 0)),
                  pl.BlockSpec((lq, lk), lambda h: (0, 0))],
        out_specs=pl.BlockSpec((1, lq, lk), lambda h: (h, 0, 0)),
        out_shape=jax.ShapeDtypeStruct((n_heads, lq, lk), F32),
        compiler_params=_cparams(("arbitrary",)),
        name="band_table",
    )(base, jnp.asarray(allowed))


def attention_prompt(proj, rel_bias, dot, out_dtype):
    t = proj.shape[0]
    nb = t // ATT_QBLOCK
    npair = A_HEADS // 2
    tab = _band_tables(rel_bias, ATT_QSUB, ATT_KSUB, A_WINDOW, 0).reshape(npair, 2, ATT_QSUB, ATT_KSUB)
    blk = (ATT_QBLOCK, LANES)
    return pl.pallas_call(
        functools.partial(_attn_prompt_kernel, dot=dot),
        grid=(nb, npair),
        in_specs=[pl.BlockSpec(blk, lambda i, p: (i, p)),
                  pl.BlockSpec(blk, lambda i, p: (jnp.maximum(i - 1, 0), npair + p)),
                  pl.BlockSpec(blk, lambda i, p: (i, npair + p)),
                  pl.BlockSpec(blk, lambda i, p: (jnp.maximum(i - 1, 0), 2 * npair + p)),
                  pl.BlockSpec(blk, lambda i, p: (i, 2 * npair + p)),
                  pl.BlockSpec((1, 2, ATT_QSUB, ATT_KSUB), lambda i, p: (p, 0, 0, 0))],
        out_specs=pl.BlockSpec(blk, lambda i, p: (i, p)),
        out_shape=jax.ShapeDtypeStruct((t, A_WIDTH), out_dtype),
        compiler_params=_cparams(("arbitrary", "arbitrary")),
        name="attn_prompt",
    )(proj, proj, proj, proj, proj, tab)


def _attn_sample_kernel(q_ref, kn_ref, vn_ref, kc_ref, vc_ref, bias_ref, o_ref, *, dot, w):
    lo = lax.broadcasted_iota(jnp.int32, (1, LANES), 1) < A_HEAD_DIM
    lq = q_ref.shape[0]
    for p in range(A_HEADS // 2):
        cols = slice(p * LANES, (p + 1) * LANES)
        q = q_ref[:, cols] * (A_HEAD_DIM ** -0.5)
        kc = kc_ref[0, :, cols]
        vc = vc_ref[0, :, cols]
        kn = kn_ref[:, cols]
        vn = vn_ref[:, cols]
        outs = []
        for hh in range(2):
            qm = jnp.where(lo if hh == 0 else jnp.logical_not(lo), q, 0.0)
            bias = bias_ref[2 * p + hh]
            s_c = dot(qm, kc, nt=True) + bias[:, :w]
            s_n = dot(qm, kn, nt=True) + bias[:, w:w + lq]
            outs.append(_softmax_pv([s_c, s_n], [vc, vn], dot))
        o_ref[:, cols] = jnp.where(lo, outs[0], outs[1]).astype(o_ref.dtype)


def attention_sample(proj, k_cache, v_cache, rel_bias, bsz, lq, dot, out_dtype):
    w = k_cache.shape[1]
    lkp = ((w + lq + LANES - 1) // LANES) * LANES
    tab = _band_tables(rel_bias, lq, lkp, PAST_LEN, PAST_LEN - w)
    kc = k_cache.reshape(bsz, w, A_WIDTH)
    vc = v_cache.reshape(bsz, w, A_WIDTH)
    return pl.pallas_call(
        functools.partial(_attn_sample_kernel, dot=dot, w=w),
        grid=(bsz,),
        in_specs=[pl.BlockSpec((lq, A_WIDTH), lambda b: (b, 0)),
                  pl.BlockSpec((lq, A_WIDTH), lambda b: (b, 1)),
                  pl.BlockSpec((lq, A_WIDTH), lambda b: (b, 2)),
                  pl.BlockSpec((1, w, A_WIDTH), lambda b: (b, 0, 0)),
                  pl.BlockSpec((1, w, A_WIDTH), lambda b: (b, 0, 0)),
                  pl.BlockSpec((A_HEADS, lq, lkp), lambda b: (0, 0, 0))],
        out_specs=pl.BlockSpec((lq, A_WIDTH), lambda b: (b, 0)),
        out_shape=jax.ShapeDtypeStruct((bsz * lq, A_WIDTH), out_dtype),
        compiler_params=_cparams(("arbitrary",)),
        name="attn_sample",
    )(proj, proj, proj, kc, vc, tab)


def _ssd_kernel(z_ref, x0_ref, x1_ref, x2_ref, dt_ref, cbuf_ref, h0_ref,
                cw_ref, cb_ref, dtb_ref, alog_ref, dsk_ref, ng_ref, exp_ref,
                y_ref, cnew_ref, hout_ref, xp_ref, st_ref, *, dot, q):
    c = pl.program_id(1)
    half = B_WIDTH // B_GROUPS

    @pl.when(c == 0)
    def _():
        xp_ref[0:SUBLANES, :] = cbuf_ref[0]
        st_ref[...] = h0_ref[0]

    xp_ref[SUBLANES:SUBLANES + q, 0:512] = x0_ref[...]
    xp_ref[SUBLANES:SUBLANES + q, 512:1024] = x1_ref[...]
    xp_ref[SUBLANES:SUBLANES + q, 1024:1536] = x2_ref[...]
    base = SUBLANES - (B_CONV - 1)
    conv = cb_ref[...] + xp_ref[base:base + q, :] * cw_ref[0:1, :]
    for tap in range(1, B_CONV):
        conv = conv + xp_ref[base + tap:base + tap + q, :] * cw_ref[tap:tap + 1, :]
    tail = xp_ref[q:q + SUBLANES, :]
    xp_ref[0:SUBLANES, :] = tail
    cnew_ref[0] = tail

    xbc = _silu(conv)
    xs = xbc[:, :B_WIDTH]
    bm = xbc[:, B_WIDTH:B_WIDTH + B_GROUPS * B_STATE]
    cm = xbc[:, B_WIDTH + B_GROUPS * B_STATE:]

    lane = lax.broadcasted_iota(jnp.int32, (1, LANES), 1)
    head_ok = lane < B_HEADS
    dt = jnp.where(head_ok, jax.nn.softplus(dt_ref[...] + dtb_ref[...]), 0.0)
    a_neg = -jnp.exp(alog_ref[...])
    row = lax.broadcasted_iota(jnp.int32, (q, q), 0)
    col = lax.broadcasted_iota(jnp.int32, (q, q), 1)
    tril = row >= col
    acs = _dot_sel(tril.astype(BF16), dt * a_neg)
    acs_t = acs.T
    eacs = jnp.exp(acs)
    to_end = jnp.exp(acs[q - 1:q, :] - acs)
    wide = _dot_rsel(jnp.concatenate([dt, eacs, to_end], axis=0), exp_ref[...])
    dt_w = wide[0:q]
    eacs_w = wide[q:2 * q]
    toend_w = wide[2 * q:3 * q]
    xdt = xs * dt_w
    xend = xdt * toend_w
    lo = lane < B_HEAD_DIM

    y_parts = []
    for g in range(B_GROUPS):
        bg = bm[:, g * B_STATE:(g + 1) * B_STATE]
        cg = cm[:, g * B_STATE:(g + 1) * B_STATE]
        gmat = dot(cg, bg, nt=True)
        y_off = dot(cg, st_ref[g]) * eacs_w[:, g * half:(g + 1) * half]
        for pr in range(half // LANES):
            cols = slice(g * half + pr * LANES, g * half + (pr + 1) * LANES)
            x_pair = xdt[:, cols]
            outs = []
            for hh in range(2):
                h = (g * half + pr * LANES) // B_HEAD_DIM + hh
                diff = acs[:, h:h + 1] - acs_t[h:h + 1, :]
                decay = jnp.exp(jnp.where(tril, diff, -jnp.inf))
                outs.append(dot(gmat * decay, x_pair))
            y_parts.append(jnp.where(lo, outs[0], outs[1]) + y_off[:, pr * LANES:(pr + 1) * LANES])
        st_new = dot(bg.T, xend[:, g * half:(g + 1) * half])
        st_ref[g] = st_ref[g] * eacs_w[q - 1:q, g * half:(g + 1) * half] + st_new
    hout_ref[0] = st_ref[...]

    y = jnp.concatenate(y_parts, axis=1) + dsk_ref[...] * xs
    y = y * _silu(z_ref[...])
    y = y * lax.rsqrt(jnp.mean(y * y, axis=-1, keepdims=True) + RMS_EPS) * ng_ref[...]
    y_ref[...] = y.astype(y_ref.dtype)


def ssd_mixer(proj, dt_raw, conv_buf, h0, conv_w, conv_b, dt_bias, a_log, d_skip, norm_g, bsz, seq, q, dot,
              out_dtype):
    t = bsz * seq
    nc = seq // q
    half = B_WIDTH // B_GROUPS
    cbuf = jnp.pad(conv_buf, ((0, 0), (SUBLANES - (B_CONV - 1), 0), (0, 0)))
    h0_t = jnp.transpose(h0.reshape(bsz, B_GROUPS, half, B_STATE), (0, 1, 3, 2))
    pad = lambda v: jnp.pad(v.reshape(1, -1), ((0, 0), (0, LANES - v.shape[-1])))
    expand = jnp.asarray(np.repeat(np.eye(LANES, B_HEADS, dtype=np.float32).T, B_HEAD_DIM, axis=0).T, BF16)
    dsk_w = jnp.repeat(d_skip, B_HEAD_DIM).reshape(1, B_WIDTH)
    zb, xb = 3 * A_WIDTH // B_WIDTH, (3 * A_WIDTH + B_WIDTH) // 512
    const = lambda shp: pl.BlockSpec(shp, lambda b, c: (0,) * len(shp))
    y, cnew, hout = pl.pallas_call(
        functools.partial(_ssd_kernel, dot=dot, q=q),
        grid=(bsz, nc),
        in_specs=[pl.BlockSpec((q, B_WIDTH), lambda b, c: (b * nc + c, zb)),
                  pl.BlockSpec((q, 512), lambda b, c: (b * nc + c, xb)),
                  pl.BlockSpec((q, 512), lambda b, c: (b * nc + c, xb + 1)),
                  pl.BlockSpec((q, 512), lambda b, c: (b * nc + c, xb + 2)),
                  pl.BlockSpec((q, LANES), lambda b, c: (b * nc + c, 0)),
                  pl.BlockSpec((1, SUBLANES, B_CONV_DIM), lambda b, c: (b, 0, 0)),
                  pl.BlockSpec((1, B_GROUPS, B_STATE, half), lambda b, c: (b, 0, 0, 0)),
                  const((B_CONV, B_CONV_DIM)), const((1, B_CONV_DIM)), const((1, LANES)), const((1, LANES)),
                  const((1, B_WIDTH)), const((1, B_WIDTH)), const((LANES, B_WIDTH))],
        out_specs=[pl.BlockSpec((q, B_WIDTH), lambda b, c: (b * nc + c, 0)),
                   pl.BlockSpec((1, SUBLANES, B_CONV_DIM), lambda b, c: (b, 0, 0)),
                   pl.BlockSpec((1, B_GROUPS, B_STATE, half), lambda b, c: (b, 0, 0, 0))],
        out_shape=[jax.ShapeDtypeStruct((t, B_WIDTH), out_dtype),
                   jax.ShapeDtypeStruct((bsz, SUBLANES, B_CONV_DIM), F32),
                   jax.ShapeDtypeStruct((bsz, B_GROUPS, B_STATE, half), F32)],
        scratch_shapes=[pltpu.VMEM((q + SUBLANES, B_CONV_DIM), F32),
                        pltpu.VMEM((B_GROUPS, B_STATE, half), F32)],
        compiler_params=_cparams(("arbitrary", "arbitrary")),
        name="ssd",
    )(proj, proj, proj, proj, dt_raw, cbuf, h0_t,
      conv_w, conv_b.reshape(1, -1), pad(dt_bias), pad(a_log), dsk_w, norm_g.reshape(1, -1), expand)
    conv_new = cnew[:, SUBLANES - (B_CONV - 1):, :]
    h_final = jnp.transpose(hout, (0, 1, 3, 2)).reshape(bsz, B_HEADS, B_HEAD_DIM, B_STATE)
    return y, conv_new, h_final


def _s5_prep_kernel(lre_ref, lim_ref, ls_ref, bre_ref, bim_ref, are_ref, aim_ref, bbre_ref, bbim_ref):
    lre = lre_ref[...]
    lim = lim_ref[...]
    step = jnp.exp(ls_ref[...])
    mag = jnp.exp(lre * step)
    ang = lim * step
    ab_re = mag * jnp.cos(ang)
    ab_im = mag * jnp.sin(ang)
    den = lre * lre + lim * lim
    f_re = ((ab_re - 1.0) * lre + ab_im * lim) / den
    f_im = (ab_im * lre - (ab_re - 1.0) * lim) / den
    br = bre_ref[...]
    bi = bim_ref[...]
    are_ref[...] = ab_re
    aim_ref[...] = ab_im
    bbre_ref[...] = f_re * br - f_im * bi
    bbim_ref[...] = f_re * bi + f_im * br


def _cmul(ar, ai, br, bi):
    return ar * br - ai * bi, ar * bi + ai * br


def _gelu_tanh(x):
    return 0.5 * x * (1.0 + jnp.tanh(math.sqrt(2.0 / math.pi) * (x + 0.044715 * (x * x * x))))


def _s5_kernel(u_ref, bb_ref, ccre_ref, ccim_ref, a_ref, d_ref, h0_ref, y_ref, hout_ref,
               s_ref, sb_ref, pw_ref, pwb_ref, carry_ref, *, dot, seg, chain):
    t = pl.program_id(2)
    w = S5_JW
    sdt = sb_ref.dtype
    g = 2 if sdt == BF16 else 1
    grows = g * SUBLANES
    n_groups = seg // g
    unroll = min(n_groups, 4 // g)
    a_re = a_ref[0, 0:1, :]
    a_im = a_ref[0, 1:2, :]

    def group(k):
        return pl.ds(pl.multiple_of(k * grows, grows), grows)

    @pl.when(t == 0)
    def _():
        carry_ref[0:h0_ref.shape[2], :] = h0_ref[0, 0]
        pw_ref[0:1, 0:w] = a_re
        pw_ref[0:1, w:2 * w] = a_im

        def power(i, carry):
            pr, pi = _cmul(pw_ref[pl.ds(i - 1, 1), 0:w], pw_ref[pl.ds(i - 1, 1), w:2 * w], a_re, a_im)
            pw_ref[pl.ds(i, 1), 0:w] = pr
            pw_ref[pl.ds(i, 1), w:2 * w] = pi
            return carry

        lax.fori_loop(1, seg, power, 0)

        def spread(k, carry):
            rows = [jnp.broadcast_to(pw_ref[pl.ds(k * g + q, 1), :], (SUBLANES, 2 * w)) for q in range(g)]
            pwb_ref[group(k), :] = jnp.concatenate(rows, axis=0).astype(sdt)
            return carry

        lax.fori_loop(0, n_groups, spread, 0)

    ncb = u_ref.shape[0]
    up = jnp.concatenate(
        [jnp.concatenate([u_ref[cb, pl.ds(i, SUBLANES, stride=seg), :] for i in range(seg)], axis=0)
         for cb in range(ncb)], axis=1)
    s_ref[...] = dot(up, bb_ref[0])

    are_b = jnp.broadcast_to(a_re, (SUBLANES, w))
    aim_b = jnp.broadcast_to(a_im, (SUBLANES, w))

    def scan(k, st):
        sre, sim = st
        res, ims = [], []
        for q in range(g):
            rows = pl.ds(pl.multiple_of((k * g + q) * SUBLANES, SUBLANES), SUBLANES)
            pr, pi = _cmul(are_b, aim_b, sre, sim)
            sre = pr + s_ref[rows, 0:w]
            sim = pi + s_ref[rows, w:2 * w]
            res.append(sre)
            ims.append(sim)
        sb_ref[group(k), 0:w] = jnp.concatenate(res, axis=0).astype(sdt)
        sb_ref[group(k), w:2 * w] = jnp.concatenate(ims, axis=0).astype(sdt)
        return sre, sim

    zero = jnp.zeros((SUBLANES, w), F32)
    end_re, end_im = lax.fori_loop(0, n_groups, scan, (zero, zero), unroll=unroll)

    sg_re = pw_ref[seg - 1:seg, 0:w]
    sg_im = pw_ref[seg - 1:seg, w:2 * w]
    if chain:
        c_re = carry_ref[0:1, 0:w]
        c_im = carry_ref[0:1, w:2 * w]
        rows_re, rows_im = [], []
        for s in range(SUBLANES):
            rows_re.append(c_re)
            rows_im.append(c_im)
            pr, pi = _cmul(sg_re, sg_im, c_re, c_im)
            c_re = pr + end_re[s:s + 1, :]
            c_im = pi + end_im[s:s + 1, :]
        carry_ref[0:1, 0:w] = c_re
        carry_ref[0:1, w:2 * w] = c_im
        hout_ref[0, 0] = carry_ref[0:1, :]
        in_re = jnp.concatenate(rows_re, axis=0)
        in_im = jnp.concatenate(rows_im, axis=0)
    else:
        in_re = carry_ref[:, 0:w]
        in_im = carry_ref[:, w:2 * w]
        pr, pi = _cmul(sg_re, sg_im, in_re, in_im)
        carry_ref[:, 0:w] = pr + end_re
        carry_ref[:, w:2 * w] = pi + end_im
        hout_ref[0, 0] = carry_ref[...]

    in_re_g = jnp.concatenate([in_re] * g, axis=0).astype(sdt)
    in_im_g = jnp.concatenate([in_im] * g, axis=0).astype(sdt)

    def fixup(k, carry):
        rows = group(k)
        pr, pi = _cmul(pwb_ref[rows, 0:w], pwb_ref[rows, w:2 * w], in_re_g, in_im_g)
        sb_ref[rows, 0:w] = sb_ref[rows, 0:w] + pr
        sb_ref[rows, w:2 * w] = sb_ref[rows, w:2 * w] + pi
        return carry

    lax.fori_loop(0, n_groups, fixup, 0, unroll=unroll)

    y = dot(sb_ref[:, 0:w], ccre_ref[0]) - dot(sb_ref[:, w:2 * w], ccim_ref[0]) + d_ref[...] * up
    y = _gelu_tanh(y)
    for cb in range(ncb):
        for i in range(seg):
            y_ref[cb, pl.ds(i, SUBLANES, stride=seg), :] = (
                y[i * SUBLANES:(i + 1) * SUBLANES, cb * LANES:(cb + 1) * LANES])


def _block_diag(m):
    nj, j, r, c = m.shape
    eye = jnp.eye(j, dtype=m.dtype)
    return (m[:, :, :, None, :] * eye[None, :, None, :, None]).reshape(nj, j * r, j * c)


def s5_mixer(u, h0_re, h0_im, lam_re, lam_im, log_step, b_re, b_im, c_re, c_im, d_skip,
             bsz, seq, tm, dot, w_dtype, chain):
    t = bsz * seq
    g, p, ch = C_GROUPS, C_STATE, C_GROUP_CH
    rep = lambda v: jnp.repeat(v, ch, axis=-1)
    ls = jnp.broadcast_to(log_step[:, None], (g, p))
    pc = pl.BlockSpec((g, p * ch), lambda: (0, 0))
    shp = jax.ShapeDtypeStruct((g, p * ch), F32)
    a_re_x, a_im_x, bb_re, bb_im = pl.pallas_call(
        _s5_prep_kernel, in_specs=[pc] * 5, out_specs=[pc] * 4, out_shape=[shp] * 4, name="s5_prep",
    )(rep(lam_re), rep(lam_im), rep(ls), b_re.reshape(g, p * ch), b_im.reshape(g, p * ch))
    a_re = a_re_x[:, ::ch]
    a_im = a_im_x[:, ::ch]
    to_blk = lambda m: jnp.transpose(m.reshape(S5_NJ, S5_JBLK, p, ch), (0, 1, 3, 2))
    bb = jnp.concatenate([_block_diag(to_blk(bb_re)), _block_diag(to_blk(bb_im))], axis=-1).astype(w_dtype)
    cblk = lambda m: jnp.transpose(m.reshape(S5_NJ, S5_JBLK, ch, p), (0, 1, 3, 2))
    cc_re = _block_diag(cblk(c_re)).astype(w_dtype)
    cc_im = _block_diag(cblk(c_im)).astype(w_dtype)
    a_rows = jnp.stack([a_re.reshape(S5_NJ, S5_JW), a_im.reshape(S5_NJ, S5_JW)], axis=1)
    h0 = jnp.concatenate([h0_re.reshape(bsz, S5_NJ, S5_JW), h0_im.reshape(bsz, S5_NJ, S5_JW)], axis=-1)
    if chain:
        nb, srows, rows_per_b = bsz, 1, seq
        h0 = h0[:, :, None, :]
    else:
        assert bsz == SUBLANES and tm == bsz * seq
        nb, srows, rows_per_b = 1, SUBLANES, bsz * seq
        h0 = jnp.transpose(h0, (1, 0, 2))[None]
    nt = rows_per_b // tm
    seg = tm // SUBLANES
    uw = S5_JBLK * ch
    ncb = uw // LANES
    state_spec = pl.BlockSpec((1, 1, srows, 2 * S5_JW), lambda b, j, i: (b, j, 0, 0))
    y, hout = pl.pallas_call(
        functools.partial(_s5_kernel, dot=dot, seg=seg, chain=chain),
        grid=(nb, S5_NJ, nt),
        in_specs=[pl.BlockSpec((ncb, tm, LANES), lambda b, j, i: (j, b * nt + i, 0)),
                  pl.BlockSpec((1, uw, 2 * S5_JW), lambda b, j, i: (j, 0, 0)),
                  pl.BlockSpec((1, S5_JW, uw), lambda b, j, i: (j, 0, 0)),
                  pl.BlockSpec((1, S5_JW, uw), lambda b, j, i: (j, 0, 0)),
                  pl.BlockSpec((1, 2, S5_JW), lambda b, j, i: (j, 0, 0)),
                  pl.BlockSpec((1, uw), lambda b, j, i: (0, j)),
                  state_spec],
        out_specs=[pl.BlockSpec((ncb, tm, LANES), lambda b, j, i: (j, b * nt + i, 0)), state_spec],
        out_shape=[jax.ShapeDtypeStruct((g * ch // LANES, t, LANES), F32),
                   jax.ShapeDtypeStruct((nb, S5_NJ, srows, 2 * S5_JW), F32)],
        scratch_shapes=[pltpu.VMEM((tm, 2 * S5_JW), F32),
                        pltpu.VMEM((tm, 2 * S5_JW), w_dtype),
                        pltpu.VMEM((seg, 2 * S5_JW), F32),
                        pltpu.VMEM((tm, 2 * S5_JW), w_dtype),
                        pltpu.VMEM((SUBLANES, 2 * S5_JW), F32)],
        compiler_params=_cparams(("arbitrary", "arbitrary", "arbitrary")),
        name="s5_scan",
    )(u, bb, cc_re, cc_im, a_rows, d_skip.reshape(1, -1), h0)
    hout = hout[:, :, 0, :] if chain else jnp.transpose(hout[0], (1, 0, 2))
    s_re = hout[:, :, :S5_JW].reshape(bsz, g, p)
    s_im = hout[:, :, S5_JW:].reshape(bsz, g, p)
    return y, s_re, s_im


def _router_kernel(x_ref, sc_ref, sh_ref, w_ref, b_ref, o_ref, r_ref, cnt_ref, run_ref):
    h = x_ref[...] * (1.0 + sc_ref[...]) + sh_ref[...]
    logits = _dot3(h, w_ref[...]) + b_ref[...]
    lane = lax.broadcasted_iota(jnp.int32, logits.shape, 1).astype(F32)
    logits = jnp.where(lane < N_EXPERTS, logits, -jnp.inf)
    m1 = logits.max(axis=-1, keepdims=True)
    i1 = jnp.min(jnp.where(logits == m1, lane, float(LANES)), axis=-1, keepdims=True)
    rest = jnp.where(lane == i1, -jnp.inf, logits)
    m2 = rest.max(axis=-1, keepdims=True)
    i2 = jnp.min(jnp.where(rest == m2, lane, float(LANES)), axis=-1, keepdims=True)
    e2 = jnp.exp(m2 - m1)
    w1 = 1.0 / (1.0 + e2)
    w2 = e2 / (1.0 + e2)
    o_ref[...] = jnp.where(lane == i1, w1, 0.0) + jnp.where(lane == i2, w2, 0.0)

    step = pl.program_id(0)

    @pl.when(step == 0)
    def _():
        run_ref[...] = jnp.zeros_like(run_ref)

    tm = logits.shape[0]
    sel = jnp.where(jnp.logical_or(lane == i1, lane == i2), 1.0, 0.0)
    row = lax.broadcasted_iota(jnp.int32, (tm, tm), 0)
    col = lax.broadcasted_iota(jnp.int32, (tm, tm), 1)
    before = (row > col).astype(BF16)
    rank = _mxu(before, sel.astype(BF16)) + run_ref[0:1, :]
    pos1 = jnp.sum(jnp.where(lane == i1, rank, 0.0), axis=-1, keepdims=True)
    pos2 = jnp.sum(jnp.where(lane == i2, rank, 0.0), axis=-1, keepdims=True)
    run_ref[0:1, :] = run_ref[0:1, :] + jnp.sum(sel, axis=0, keepdims=True)
    cnt_ref[...] = run_ref[0:1, :]
    rec = jnp.zeros_like(logits)
    for k, val in enumerate((i1, i2, pos1, pos2, w1, w2)):
        rec = jnp.where(lane == float(k), val, rec)
    r_ref[...] = rec


ROUTE_E1, ROUTE_E2, ROUTE_POS1, ROUTE_POS2, ROUTE_W1, ROUTE_W2 = range(6)


def router(x, scale, shift, router_w, router_b):
    t, d = x.shape
    tm = min(t, 512)
    w = jnp.pad(router_w, ((0, 0), (0, LANES - N_EXPERTS)))
    b = jnp.pad(router_b.reshape(1, -1), ((0, 0), (0, LANES - N_EXPERTS)))
    return pl.pallas_call(
        _router_kernel,
        grid=(t // tm,),
        in_specs=[pl.BlockSpec((tm, d), lambda i: (i, 0)), _row_spec(scale, tm), _row_spec(shift, tm),
                  pl.BlockSpec((d, LANES), lambda i: (0, 0)), pl.BlockSpec((1, LANES), lambda i: (0, 0))],
        out_specs=[pl.BlockSpec((tm, LANES), lambda i: (i, 0)), pl.BlockSpec((tm, LANES), lambda i: (i, 0)),
                   pl.BlockSpec((1, LANES), lambda i: (0, 0))],
        out_shape=[jax.ShapeDtypeStruct((t, LANES), F32), jax.ShapeDtypeStruct((t, LANES), F32),
                   jax.ShapeDtypeStruct((1, LANES), F32)],
        scratch_shapes=[pltpu.VMEM((SUBLANES, LANES), F32)],
        compiler_params=_cparams(("arbitrary",)),
        name="router",
    )(x, scale, shift, w, b)


MOE_TILE = 256
MOE_UP_TN = D_FF_EXPERT // 2
DMA_ISSUE_UNROLL = 8


def _dispatch_kernel(fill_ref, d1_ref, d2_ref, h_ref, xs_ref, zero_ref, sem, zsem):
    tm = h_ref.shape[0]

    @pl.when(pl.program_id(0) == 0)
    def _():
        zero_ref[...] = jnp.zeros_like(zero_ref)

        def fill_copy(k):
            row0 = pl.multiple_of(jnp.maximum(fill_ref[k], 0), MOE_TILE)
            return pltpu.make_async_copy(zero_ref, xs_ref.at[pl.ds(row0, MOE_TILE)], zsem)

        for k in range(fill_ref.shape[0]):
            @pl.when(fill_ref[k] >= 0)
            def _():
                fill_copy(k).start()

        for k in range(fill_ref.shape[0]):
            @pl.when(fill_ref[k] >= 0)
            def _():
                fill_copy(k).wait()

    def issue(t, carry):
        src = h_ref.at[pl.ds(t, 1)]
        pltpu.make_async_copy(src, xs_ref.at[pl.ds(d1_ref[0, 0, t], 1)], sem).start()
        pltpu.make_async_copy(src, xs_ref.at[pl.ds(d2_ref[0, 0, t], 1)], sem).start()
        return carry

    lax.fori_loop(0, tm, issue, 0, unroll=DMA_ISSUE_UNROLL)
    pltpu.make_async_copy(xs_ref.at[pl.ds(0, 2 * tm)], xs_ref.at[pl.ds(0, 2 * tm)], sem).wait()


def moe_dispatch(h, dest1, dest2, fill_rows, n_rows):
    t, d = h.shape
    tm = 512
    idx_spec = pl.BlockSpec((1, 1, tm), lambda i, fr: (i, 0, 0), memory_space=pltpu.SMEM)
    grid_spec = pltpu.PrefetchScalarGridSpec(
        num_scalar_prefetch=1,
        grid=(t // tm,),
        in_specs=[idx_spec, idx_spec, pl.BlockSpec((tm, d), lambda i, fr: (i, 0))],
        out_specs=pl.BlockSpec(memory_space=pl.ANY),
        scratch_shapes=[pltpu.VMEM((MOE_TILE, d), h.dtype), pltpu.SemaphoreType.DMA(()),
                        pltpu.SemaphoreType.DMA(())])
    return pl.pallas_call(
        _dispatch_kernel,
        grid_spec=grid_spec,
        out_shape=jax.ShapeDtypeStruct((n_rows, d), h.dtype),
        compiler_params=_cparams(("arbitrary",)),
        name="moe_dispatch",
    )(fill_rows, dest1.reshape(t // tm, 1, tm), dest2.reshape(t // tm, 1, tm), h)


def _expert_up_kernel(te_ref, nv_ref, x_ref, wg_ref, wu_ref, o_ref, *, dot):
    del te_ref

    @pl.when(pl.program_id(1) < nv_ref[0])
    def _():
        gu = dot(x_ref[...], jnp.concatenate([wg_ref[0], wu_ref[0]], axis=1))
        tn = o_ref.shape[1]
        o_ref[...] = (_silu(gu[:, :tn]) * gu[:, tn:]).astype(o_ref.dtype)

    @pl.when(pl.program_id(1) >= nv_ref[0])
    def _():
        o_ref[...] = jnp.zeros_like(o_ref)


def expert_up(xs, w_up, tile_expert, n_valid, dot, tn):
    r, d = xs.shape
    f = w_up.shape[2] // 2
    nf = f // tn
    nt = r // MOE_TILE
    grid_spec = pltpu.PrefetchScalarGridSpec(
        num_scalar_prefetch=2,
        grid=(nf, nt),
        in_specs=[pl.BlockSpec((MOE_TILE, d), lambda j, i, te, nv: (i, 0)),
                  pl.BlockSpec((1, d, tn), lambda j, i, te, nv: (te[i], 0, j)),
                  pl.BlockSpec((1, d, tn), lambda j, i, te, nv: (te[i], 0, j + nf))],
        out_specs=pl.BlockSpec((MOE_TILE, tn), lambda j, i, te, nv: (i, j)))
    return pl.pallas_call(
        functools.partial(_expert_up_kernel, dot=dot),
        grid_spec=grid_spec,
        out_shape=jax.ShapeDtypeStruct((r, f), BF16),
        compiler_params=_cparams(("arbitrary", "arbitrary")),
        name="expert_up",
    )(tile_expert, n_valid, xs, w_up, w_up)


def _expert_down_kernel(te_ref, nv_ref, a_ref, w_ref, o_ref, *, dot):
    del te_ref

    @pl.when(pl.program_id(0) < nv_ref[0])
    def _():
        o_ref[...] = dot(a_ref[...], w_ref[0])

    @pl.when(pl.program_id(0) >= nv_ref[0])
    def _():
        o_ref[...] = jnp.zeros_like(o_ref)


def expert_down(act, w_down, tile_expert, n_valid, dot):
    r, f = act.shape
    d = w_down.shape[2]
    grid_spec = pltpu.PrefetchScalarGridSpec(
        num_scalar_prefetch=2,
        grid=(r // MOE_TILE,),
        in_specs=[pl.BlockSpec((MOE_TILE, f), lambda i, te, nv: (i, 0)),
                  pl.BlockSpec((1, f, d), lambda i, te, nv: (te[i], 0, 0))],
        out_specs=pl.BlockSpec((MOE_TILE, d), lambda i, te, nv: (i, 0)))
    return pl.pallas_call(
        functools.partial(_expert_down_kernel, dot=dot),
        grid_spec=grid_spec,
        out_shape=jax.ShapeDtypeStruct((r, d), F32),
        compiler_params=_cparams(("arbitrary",)),
        name="expert_down",
    )(tile_expert, n_valid, act, w_down)


def _combine_ln_kernel(d1_ref, d2_ref, d1n_ref, d2n_ref, y_ref, rec_ref, x_ref, gate_ref, g_ref, b_ref, xo_ref,
                       buf1, buf2, sem):
    tm = x_ref.shape[0]
    step = pl.program_id(0)
    slot = step % 2

    def gather(idx1, idx2, to):
        def issue(t, carry):
            pltpu.make_async_copy(y_ref.at[pl.ds(idx1[0, 0, t], 1)], buf1.at[to, pl.ds(t, 1)], sem.at[to]).start()
            pltpu.make_async_copy(y_ref.at[pl.ds(idx2[0, 0, t], 1)], buf2.at[to, pl.ds(t, 1)], sem.at[to]).start()
            return carry

        lax.fori_loop(0, tm, issue, 0, unroll=DMA_ISSUE_UNROLL)

    @pl.when(step == 0)
    def _():
        gather(d1_ref, d2_ref, 0)

    @pl.when(step + 1 < pl.num_programs(0))
    def _():
        gather(d1n_ref, d2n_ref, 1 - slot)

    pltpu.make_async_copy(y_ref.at[pl.ds(0, tm)], buf1.at[slot], sem.at[slot]).wait()
    pltpu.make_async_copy(y_ref.at[pl.ds(0, tm)], buf2.at[slot], sem.at[slot]).wait()

    rec = rec_ref[...]
    lane = lax.broadcasted_iota(jnp.int32, rec.shape, 1)
    w1 = jnp.sum(jnp.where(lane == ROUTE_W1, rec, 0.0), axis=-1, keepdims=True)
    w2 = jnp.sum(jnp.where(lane == ROUTE_W2, rec, 0.0), axis=-1, keepdims=True)
    out = w1 * buf1[slot] + w2 * buf2[slot]
    r = ALPHA * x_ref[...] + (1.0 + gate_ref[...]) * out
    mu = jnp.mean(r, axis=-1, keepdims=True)
    dev = r - mu
    var = jnp.mean(dev * dev, axis=-1, keepdims=True)
    xo_ref[...] = dev * lax.rsqrt(var + LN_EPS) * g_ref[...] + b_ref[...]


def moe_combine_ln(y_sorted, dest1, dest2, rec, x, gate, ln_g, ln_b):
    t, d = x.shape
    tm = 256
    nt = t // tm
    idx_spec = pl.BlockSpec((1, 1, tm), lambda i: (i, 0, 0), memory_space=pltpu.SMEM)
    nxt_spec = pl.BlockSpec((1, 1, tm), lambda i: (jnp.minimum(i + 1, nt - 1), 0, 0), memory_space=pltpu.SMEM)
    d1 = dest1.reshape(nt, 1, tm)
    d2 = dest2.reshape(nt, 1, tm)
    return pl.pallas_call(
        _combine_ln_kernel,
        grid=(nt,),
        in_specs=[idx_spec, idx_spec, nxt_spec, nxt_spec, pl.BlockSpec(memory_space=pl.ANY),
                  pl.BlockSpec((tm, LANES), lambda i: (i, 0)), pl.BlockSpec((tm, d), lambda i: (i, 0)),
                  _row_spec(gate, tm), pl.BlockSpec((1, d), lambda i: (0, 0)), pl.BlockSpec((1, d), lambda i: (0, 0))],
        out_specs=pl.BlockSpec((tm, d), lambda i: (i, 0)),
        out_shape=jax.ShapeDtypeStruct((t, d), F32),
        scratch_shapes=[pltpu.VMEM((2, tm, d), F32), pltpu.VMEM((2, tm, d), F32), pltpu.SemaphoreType.DMA((2,))],
        compiler_params=_cparams(("arbitrary",)),
        name="moe_combine_ln",
    )(d1, d2, d1, d2, y_sorted, rec, x, gate, ln_g.reshape(1, d), ln_b.reshape(1, d))


def moe_top2(h, x, rec, counts, gate, ln_g, ln_b, w_up, w_down, dot):
    t, d = h.shape
    n_tiles = (TOP_K * t) // MOE_TILE + N_EXPERTS
    n_rows = n_tiles * MOE_TILE
    cnt = counts[0, :N_EXPERTS].astype(jnp.int32)
    padded = ((cnt + MOE_TILE - 1) // MOE_TILE) * MOE_TILE
    ends = jnp.cumsum(padded)
    starts = ends - padded
    col = lambda k: rec[:, k].astype(jnp.int32)
    experts = jnp.arange(N_EXPERTS, dtype=jnp.int32)[None, :]
    start_of = lambda e: jnp.sum(jnp.where(e[:, None] == experts, starts[None, :], 0), axis=1)
    dest1 = start_of(col(ROUTE_E1)) + col(ROUTE_POS1)
    dest2 = start_of(col(ROUTE_E2)) + col(ROUTE_POS2)
    tile_start = jnp.arange(n_tiles, dtype=jnp.int32) * MOE_TILE
    tile_expert = jnp.minimum(jnp.sum(tile_start[:, None] >= ends[None, :], axis=1), N_EXPERTS - 1).astype(jnp.int32)
    n_valid = (ends[-1:] // MOE_TILE).astype(jnp.int32)
    last_tile = jnp.where(padded > 0, ends - MOE_TILE, -1)
    spare = ends[-1] + jnp.arange(N_EXPERTS, dtype=jnp.int32) * MOE_TILE
    fill_rows = jnp.concatenate([last_tile, jnp.where(spare < n_rows, spare, -1)]).astype(jnp.int32)
    xs = moe_dispatch(h, dest1, dest2, fill_rows, n_rows)
    act = expert_up(xs, w_up, tile_expert, n_valid, dot, tn=MOE_UP_TN)
    y_sorted = expert_down(act, w_down, tile_expert, n_valid, dot)
    return moe_combine_ln(y_sorted, dest1, dest2, rec, x, gate, ln_g, ln_b)


def _trunk(x, mods, caches, wts, precise):
    bsz, seq, d = x.shape
    t = bsz * seq
    prompt = caches is None
    dot = _dot3 if precise else _dot1
    act_dtype = F32 if precise else BF16
    tm = min(t, 512)
    x0 = x.reshape(t, d)

    sh, sc, gt = mods[0][0]
    proj, dt_raw = in_projection(x0, sc, sh, wts["w_in0"], dot, act_dtype)
    keep = min(A_WINDOW, seq) if prompt else seq
    assert keep == seq or bsz == 1
    kv_rows = proj[t - bsz * keep:, A_WIDTH:3 * A_WIDTH]
    k_new = kv_rows[:, :A_WIDTH].reshape(bsz, keep, A_HEADS, A_HEAD_DIM)
    v_new = kv_rows[:, A_WIDTH:].reshape(bsz, keep, A_HEADS, A_HEAD_DIM)
    if prompt:
        att = attention_prompt(proj, wts["rel_bias"], dot, act_dtype)
        conv_buf = jnp.zeros((bsz, B_CONV - 1, B_CONV_DIM), F32)
        ssm_h0 = jnp.zeros((bsz, B_HEADS, B_HEAD_DIM, B_STATE), F32)
        q_len = min(seq, 256)
    else:
        att = attention_sample(proj, caches["k"], caches["v"], wts["rel_bias"], bsz, seq, dot, act_dtype)
        conv_buf, ssm_h0 = caches["conv"], caches["ssm"]
        q_len = seq
    y_ssd, conv_new, ssm_new = ssd_mixer(proj, dt_raw, conv_buf, ssm_h0, wts["conv_w"], wts["conv_b"], wts["dt_bias"],
                                         wts["a_log"], wts["ssd_d"], wts["ssd_norm_g"], bsz, seq, q_len,
                                         dot, act_dtype)
    x1, h1 = matmul_ln([att, y_ssd], wts["w_out0"], x0, gt, wts["ln_g"][0, 0], wts["ln_b"][0, 0], dot, tm,
                       nxt=(mods[0][1][1], mods[0][1][0]), next_dtype=act_dtype)
    act = ffn_up(h1, wts["ffn_w_up"], dot, act_dtype, tm=min(t, 1024), tn=512)
    x2, h2 = matmul_ln([act], wts["ffn_w_down"], x1, mods[0][1][2], wts["ln_g"][0, 1], wts["ln_b"][0, 1], dot,
                       min(t, 256), tk=None if prompt else 1408, nxt=(mods[1][0][1], mods[1][0][0]), next_dtype=act_dtype)

    u = matmul_slabs(h2, wts["w_in1"], dot)
    if prompt:
        s5_re0 = jnp.zeros((bsz, C_GROUPS, C_STATE), F32)
        s5_im0 = jnp.zeros((bsz, C_GROUPS, C_STATE), F32)
    else:
        s5_re0, s5_im0 = caches["s5_re"], caches["s5_im"]
    y5, s5_re, s5_im = s5_mixer(u, s5_re0, s5_im0, wts["s5_lam_re"], wts["s5_lam_im"], wts["s5_log_step"],
                                wts["s5_b_re"], wts["s5_b_im"], wts["s5_c_re"], wts["s5_c_im"], wts["s5_d"],
                                bsz, seq, 512 if prompt else bsz * seq, dot, F32 if precise else BF16,
                                chain=prompt)
    x3, h3 = matmul_ln([y5], wts["glu_w"], x2, mods[1][0][2], wts["ln_g"][1, 0], wts["ln_b"][1, 0], dot,
                       min(t, 256), tk=None if prompt else 512, glu=True,
                       nxt=(mods[1][1][1], mods[1][1][0]), next_dtype=F32)
    gates, rec, counts = router(x3, mods[1][1][1], mods[1][1][0], wts["router_w"], wts["router_b"])
    if prompt:
        x4 = moe_top2(h3, x3, rec, counts, mods[1][1][2], wts["ln_g"][1, 1], wts["ln_b"][1, 1],
                      wts["moe_w_up"], wts["moe_w_down"], dot)
    else:
        act = ffn_up(h3, wts["moe_w_up"], _dot1, BF16, tm=min(t, 2048), tn=MOE_UP_TN, gates=gates)
        x4, _ = matmul_ln([act], wts["moe_w_down"], x3, mods[1][1][2], wts["ln_g"][1, 1], wts["ln_b"][1, 1],
                          _dot1, tm, tk=D_FF_EXPERT)
    return (x4.reshape(bsz, seq, d), k_new[None], v_new[None], conv_new[None], ssm_new[None],
            s5_re[None], s5_im[None])


def kernel(x_prompt, x_sample, cache_attn_k, cache_attn_v, state_ssd_conv, state_ssd, state_s5_re, state_s5_im, c_prompt, c_sample, ada_w, ada_b, ln_g, ln_b, w_in0, w_out0, rel_bias, conv_w, conv_b, dt_bias, a_log, ssd_d, ssd_norm_g, ffn_w_up, ffn_w_down, w_in1, s5_lam_re, s5_lam_im, s5_log_step, s5_b_re, s5_b_im, s5_c_re, s5_c_im, s5_d, glu_w, router_w, router_b, moe_w_up, moe_w_down):
    d = D_MODEL
    bp, lp, _ = x_prompt.shape
    bs, ls, _ = x_sample.shape

    n_c = bp + bs
    rows = ((n_c + SUBLANES - 1) // SUBLANES) * SUBLANES
    c_all = jnp.pad(jnp.concatenate([c_prompt, c_sample], axis=0), ((0, rows - n_c), (0, 0)))
    mod = adaln_all(c_all, ada_w, ada_b).reshape(DEPTH, 2, rows, 3, d)

    def mods_for(r0, nb, per_row):
        out = []
        for layer in range(DEPTH):
            out.append([])
            for j in range(2):
                trip = []
                for part in range(3):
                    m = mod[layer, j, r0:r0 + nb, part]
                    trip.append(jnp.repeat(m, per_row, axis=0) if nb > 1 else m)
                out[-1].append(tuple(trip))
        return out

    assert bp == 1
    mods_p = mods_for(0, bp, lp)
    mods_s = mods_for(bp, bs, ls)

    shared = dict(ln_g=ln_g, ln_b=ln_b, rel_bias=rel_bias[0], conv_w=conv_w[0], conv_b=conv_b[0],
                  dt_bias=dt_bias[0], a_log=a_log[0], ssd_d=ssd_d[0], ssd_norm_g=ssd_norm_g[0],
                  s5_lam_re=s5_lam_re[0], s5_lam_im=s5_lam_im[0], s5_log_step=s5_log_step[0],
                  s5_b_re=s5_b_re[0], s5_b_im=s5_b_im[0], s5_c_re=s5_c_re[0], s5_c_im=s5_c_im[0], s5_d=s5_d[0],
                  router_w=router_w[0], router_b=router_b[0])
    streamed = dict(w_in0=w_in0[0], ffn_w_up=ffn_w_up, w_in1=w_in1[0])
    resident = dict(w_out0=w_out0[0], ffn_w_down=ffn_w_down[0], glu_w=glu_w[0])
    moe = dict(moe_w_up=moe_w_up[0].astype(BF16), moe_w_down=moe_w_down[0].astype(BF16))
    wts_p = dict(shared, **moe, **streamed, **{k: v.astype(BF16) for k, v in resident.items()})
    wts_s = dict(shared, **moe, **streamed, **resident)

    y_p, k_p, v_p, conv_p, ssd_p, re_p, im_p = _trunk(x_prompt, mods_p, None, wts_p, precise=False)
    caches = dict(k=cache_attn_k[0], v=cache_attn_v[0], conv=state_ssd_conv[0], ssm=state_ssd[0],
                  s5_re=state_s5_re[0], s5_im=state_s5_im[0])
    y_s, k_s, v_s, conv_s, ssd_s, re_s, im_s = _trunk(x_sample, mods_s, caches, wts_s, precise=True)
    return (y_p, y_s, k_p, v_p, conv_p, ssd_p, re_p, im_p, k_s, v_s, conv_s, ssd_s, re_s, im_s)
```

```python
import functools
import math

import numpy as np
import jax
import jax.numpy as jnp
from jax import lax
from jax.experimental import pallas as pl
from jax.experimental.pallas import tpu as pltpu

F32 = jnp.float32
BF16 = jnp.bfloat16

D_MODEL = 2048
DEPTH = 2
PAST_LEN = 2048
CHUNK = 64
A_HEADS = 16
A_HEAD_DIM = 64
A_WIDTH = A_HEADS * A_HEAD_DIM
A_PAST_CHUNKS = 8
A_WINDOW = A_PAST_CHUNKS * CHUNK
REL_CLIP = 128
B_HEADS = 16
B_HEAD_DIM = 64
B_WIDTH = B_HEADS * B_HEAD_DIM
B_GROUPS = 2
B_STATE = 128
B_CONV = 4
B_CONV_DIM = B_WIDTH + 2 * B_GROUPS * B_STATE
C_GROUP_CH = 16
C_GROUPS = D_MODEL // C_GROUP_CH
C_STATE = 64
D_FF = 5632
N_EXPERTS = 8
TOP_K = 2
D_FF_EXPERT = 2816
ALPHA = (2.0 * DEPTH) ** 0.25
LN_EPS = 1e-5
RMS_EPS = 1e-5
NEG_INF = -1e30
IN0_WIDTH = 3 * A_WIDTH + B_WIDTH + B_CONV_DIM + B_HEADS
IN0_MAIN = IN0_WIDTH - B_HEADS

LANES = 128
SUBLANES = 8
VMEM_LIMIT = 56 * 1024 * 1024

LN_SUBROWS = 256
ATT_QBLOCK = 512
ATT_QSUB = 256
ATT_KSUB = ATT_QSUB + A_WINDOW
ATT_PAIRS = 4
S5_JBLK = 16
S5_NJ = C_GROUPS // S5_JBLK
S5_JW = S5_JBLK * C_STATE


def _cparams(sem):
    return pltpu.CompilerParams(dimension_semantics=sem, vmem_limit_bytes=VMEM_LIMIT)


def _split_bf16(x):
    hi = x.astype(BF16)
    lo = (x - hi.astype(F32)).astype(BF16)
    return hi, lo


def _mxu(a, b, nt=False):
    if nt:
        return lax.dot_general(a, b, (((1,), (1,)), ((), ())), preferred_element_type=F32)
    return jnp.dot(a, b, preferred_element_type=F32)


def _dot1(a, b, nt=False):
    return _mxu(a.astype(BF16), b.astype(BF16), nt)


def _dot3(a, b, nt=False):
    ah, al = _split_bf16(a.astype(F32))
    bh, bl = _split_bf16(b.astype(F32))
    return _mxu(ah, bh, nt) + (_mxu(ah, bl, nt) + _mxu(al, bh, nt))


def _dot_sel(sel_bf16, x):
    x1 = x.astype(BF16)
    r1 = x - x1.astype(F32)
    x2 = r1.astype(BF16)
    x3 = (r1 - x2.astype(F32)).astype(BF16)
    return _mxu(sel_bf16, x1) + (_mxu(sel_bf16, x2) + _mxu(sel_bf16, x3))


def _dot_rsel(x, sel_bf16):
    x1 = x.astype(BF16)
    r1 = x - x1.astype(F32)
    x2 = r1.astype(BF16)
    x3 = (r1 - x2.astype(F32)).astype(BF16)
    return _mxu(x1, sel_bf16) + (_mxu(x2, sel_bf16) + _mxu(x3, sel_bf16))


def _silu(x):
    return x * jax.nn.sigmoid(x)


def _row_spec(arr, tm):
    d = arr.shape[-1]
    if arr.shape[0] == 1:
        return pl.BlockSpec((1, d), lambda i, *_: (0, 0))
    return pl.BlockSpec((tm, d), lambda i, *_: (i, 0))


def _adaln_kernel(c_ref, w_ref, b_ref, o_ref):
    c = c_ref[...]
    o_ref[0] = _dot3(_silu(c), w_ref[0]) + b_ref[0]


def adaln_all(c_rows, ada_w, ada_b):
    r, d = c_rows.shape
    n = ada_w.shape[-1]
    tn = 1024
    w = ada_w.reshape(2 * DEPTH, d, n)
    b = ada_b.reshape(2 * DEPTH, 1, n)
    return pl.pallas_call(
        _adaln_kernel,
        grid=(2 * DEPTH, n // tn),
        in_specs=[pl.BlockSpec((r, d), lambda i, j: (0, 0)),
                  pl.BlockSpec((1, d, tn), lambda i, j: (i, 0, j)),
                  pl.BlockSpec((1, 1, tn), lambda i, j: (i, 0, j))],
        out_specs=pl.BlockSpec((1, r, tn), lambda i, j: (i, 0, j)),
        out_shape=jax.ShapeDtypeStruct((2 * DEPTH, r, n), F32),
        compiler_params=_cparams(("arbitrary", "arbitrary")),
        name="adaln",
    )(c_rows, w, b)


def _mm_kernel(a_ref, w_ref, o_ref, *, dot):
    res = dot(a_ref[...], w_ref[...])
    for cb in range(o_ref.shape[0]):
        o_ref[cb] = res[:, cb * LANES:(cb + 1) * LANES]


def matmul_slabs(a, w, dot, tn=1024):
    t, k = a.shape
    n = w.shape[1]
    tm = min(t, 1024)
    return pl.pallas_call(
        functools.partial(_mm_kernel, dot=dot),
        grid=(t // tm, n // tn),
        in_specs=[pl.BlockSpec((tm, k), lambda i, j: (i, 0)), pl.BlockSpec((k, tn), lambda i, j: (0, j))],
        out_specs=pl.BlockSpec((tn // LANES, tm, LANES), lambda i, j: (j, i, 0)),
        out_shape=jax.ShapeDtypeStruct((n // LANES, t, LANES), F32),
        compiler_params=_cparams(("arbitrary", "arbitrary")),
        name="matmul",
    )(a, w)


def _in_proj_kernel(x_ref, sc_ref, sh_ref, w_ref, wdt_ref, o_ref, dt_ref, h_ref, *, dot):
    j = pl.program_id(1)

    @pl.when(j == 0)
    def _():
        h_ref[...] = (x_ref[...] * (1.0 + sc_ref[...]) + sh_ref[...]).astype(h_ref.dtype)

    h = h_ref[...]
    o_ref[...] = dot(h, w_ref[...])

    @pl.when(j == pl.num_programs(1) - 1)
    def _():
        dt_ref[...] = dot(h, wdt_ref[...])


def in_projection(x, scale, shift, w_in, dot, h_dtype):
    t, k = x.shape
    tm = min(t, 1024)
    tn = 512
    n_tail = w_in.shape[1] - IN0_MAIN
    w_dt = jnp.pad(w_in[:, IN0_MAIN:], ((0, 0), (0, LANES - n_tail)))
    return pl.pallas_call(
        functools.partial(_in_proj_kernel, dot=dot),
        grid=(t // tm, IN0_MAIN // tn),
        in_specs=[pl.BlockSpec((tm, k), lambda i, j: (i, 0)), _row_spec(scale, tm), _row_spec(shift, tm),
                  pl.BlockSpec((k, tn), lambda i, j: (0, j)),
                  pl.BlockSpec((k, LANES), lambda i, j: (0, 0))],
        out_specs=[pl.BlockSpec((tm, tn), lambda i, j: (i, j)), pl.BlockSpec((tm, LANES), lambda i, j: (i, 0))],
        out_shape=[jax.ShapeDtypeStruct((t, IN0_MAIN), F32), jax.ShapeDtypeStruct((t, LANES), F32)],
        scratch_shapes=[pltpu.VMEM((tm, k), h_dtype)],
        compiler_params=_cparams(("arbitrary", "arbitrary")),
        name="in_proj",
    )(x, scale, shift, w_in, w_dt)


def _ffn_up_kernel(*refs, dot, gated):
    if gated:
        a_ref, wg_ref, wu_ref, gates_ref, o_ref = refs
    else:
        a_ref, wg_ref, wu_ref, o_ref = refs
    a = a_ref[...]
    g = dot(a, wg_ref[0])
    u = dot(a, wu_ref[0])
    act = _silu(g) * u
    if gated:
        e = pl.program_id(1)
        gt = gates_ref[...]
        lane = lax.broadcasted_iota(jnp.int32, gt.shape, 1)
        act = act * jnp.sum(jnp.where(lane == e, gt, 0.0), axis=1, keepdims=True)
    o_ref[...] = act.astype(o_ref.dtype)


def ffn_up(a, w_up, dot, out_dtype, tm, tn, gates=None):
    t, d = a.shape
    e_n, _, f2 = w_up.shape
    f = f2 // 2
    nf = f // tn
    in_specs = [pl.BlockSpec((tm, d), lambda i, e, j: (i, 0)),
                pl.BlockSpec((1, d, tn), lambda i, e, j: (e, 0, j)),
                pl.BlockSpec((1, d, tn), lambda i, e, j: (e, 0, j + nf))]
    args = [a, w_up, w_up]
    if gates is not None:
        in_specs.append(pl.BlockSpec((tm, LANES), lambda i, e, j: (i, 0)))
        args.append(gates)
    return pl.pallas_call(
        functools.partial(_ffn_up_kernel, dot=dot, gated=gates is not None),
        grid=(t // tm, e_n, nf),
        in_specs=in_specs,
        out_specs=pl.BlockSpec((tm, tn), lambda i, e, j: (i, e * nf + j)),
        out_shape=jax.ShapeDtypeStruct((t, e_n * f), out_dtype),
        compiler_params=_cparams(("arbitrary", "arbitrary", "arbitrary")),
        name="ffn_up",
    )(*args)


def _mm_ln_kernel(*refs, n_a, nk, dot, glu, has_next):
    a_refs = refs[:n_a]
    w_refs = refs[n_a:2 * n_a]
    pos = 2 * n_a
    x_ref, gate_ref, g_ref, b_ref = refs[pos:pos + 4]
    pos += 4
    if has_next:
        sc_ref, sh_ref = refs[pos:pos + 2]
        pos += 2
    xo_ref = refs[pos]
    pos += 1
    if has_next:
        ho_ref = refs[pos]
        pos += 1
    acc_ref = refs[pos] if nk > 1 else None

    tm = x_ref.shape[0]
    sub = min(tm, max(LANES, min(LN_SUBROWS, tm // 2)))

    def load_a(a_ref, rows):
        if len(a_ref.shape) == 3:
            return jnp.concatenate([a_ref[cb, rows, :] for cb in range(a_ref.shape[0])], axis=1)
        return a_ref[rows, :]

    def product(rows):
        w0 = w_refs[0][0] if len(w_refs[0].shape) == 3 else w_refs[0][...]
        part = dot(load_a(a_refs[0], rows), w0)
        for a_ref, w_ref in zip(a_refs[1:], w_refs[1:]):
            part = part + dot(load_a(a_ref, rows), w_ref[...])
        return part

    def per_row(ref, rows):
        return ref[...] if ref.shape[0] == 1 else ref[rows, :]

    def epilogue(acc, rows):
        if glu:
            d = acc.shape[1] // 2
            out = acc[:, :d] * jax.nn.sigmoid(acc[:, d:])
        else:
            out = acc
        r = ALPHA * x_ref[rows, :] + (1.0 + per_row(gate_ref, rows)) * out
        mu = jnp.mean(r, axis=-1, keepdims=True)
        dev = r - mu
        var = jnp.mean(dev * dev, axis=-1, keepdims=True)
        xn = dev * lax.rsqrt(var + LN_EPS) * g_ref[...] + b_ref[...]
        xo_ref[rows, :] = xn
        if has_next:
            ho_ref[rows, :] = (xn * (1.0 + per_row(sc_ref, rows)) + per_row(sh_ref, rows)).astype(ho_ref.dtype)

    blocks = [slice(r0, r0 + sub) for r0 in range(0, tm, sub)]
    if nk == 1:
        for rows in blocks:
            epilogue(product(rows), rows)
    else:
        k = pl.program_id(1)
        part = product(slice(0, tm))

        @pl.when(k == 0)
        def _():
            acc_ref[...] = part

        @pl.when(k > 0)
        def _():
            acc_ref[...] += part

        @pl.when(k == nk - 1)
        def _():
            for rows in blocks:
                epilogue(acc_ref[rows, :], rows)


def matmul_ln(a_list, w, x, gate, ln_g, ln_b, dot, tm, tk=None, glu=False, nxt=None, next_dtype=None):
    t, d = x.shape
    n = w.shape[-1]
    n_a = len(a_list)
    if n_a > 1:
        nk = 1
        in_specs = [pl.BlockSpec((tm, a.shape[1]), lambda i, k: (i, 0)) for a in a_list]
        off = 0
        for a in a_list:
            ka = a.shape[1]
            assert off % ka == 0
            in_specs.append(pl.BlockSpec((ka, n), functools.partial(lambda i, k, o: (o, 0), o=off // ka),
                                         pipeline_mode=pl.Buffered(1)))
            off += ka
        w_args = [w] * n_a
    else:
        a0 = a_list[0]
        split_cols = a0.ndim == 3
        ktot = a0.shape[0] * LANES if split_cols else a0.shape[1]
        tk = ktot if tk is None else tk
        nk = ktot // tk
        if split_cols:
            a_spec = pl.BlockSpec((tk // LANES, tm, LANES), lambda i, k: (k, i, 0))
        else:
            a_spec = pl.BlockSpec((tm, tk), lambda i, k: (i, k))
        w_mode = dict(pipeline_mode=pl.Buffered(1)) if nk == 1 else {}
        if w.ndim == 3:
            assert w.shape[1] == tk
            w_spec = pl.BlockSpec((1, tk, n), lambda i, k: (k, 0, 0))
        else:
            w_spec = pl.BlockSpec((tk, n), lambda i, k: (k, 0), **w_mode)
        in_specs = [a_spec, w_spec]
        w_args = [w]
    in_specs += [pl.BlockSpec((tm, d), lambda i, k: (i, 0)), _row_spec(gate, tm),
                 pl.BlockSpec((1, d), lambda i, k: (0, 0)), pl.BlockSpec((1, d), lambda i, k: (0, 0))]
    args = list(a_list) + w_args + [x, gate, ln_g.reshape(1, d), ln_b.reshape(1, d)]
    out_specs = [pl.BlockSpec((tm, d), lambda i, k: (i, 0))]
    out_shape = [jax.ShapeDtypeStruct((t, d), F32)]
    if nxt is not None:
        in_specs += [_row_spec(nxt[0], tm), _row_spec(nxt[1], tm)]
        args += [nxt[0], nxt[1]]
        out_specs.append(pl.BlockSpec((tm, d), lambda i, k: (i, 0)))
        out_shape.append(jax.ShapeDtypeStruct((t, d), next_dtype))
    scratch = [pltpu.VMEM((tm, n), F32)] if nk > 1 else []
    res = pl.pallas_call(
        functools.partial(_mm_ln_kernel, n_a=n_a, nk=nk, dot=dot, glu=glu, has_next=nxt is not None),
        grid=(t // tm, nk),
        in_specs=in_specs,
        out_specs=out_specs,
        out_shape=out_shape,
        scratch_shapes=scratch,
        compiler_params=_cparams(("arbitrary", "arbitrary")),
        name="matmul_ln",
    )(*args)
    return res if nxt is not None else (res[0], None)


def _softmax_pv(scores, values, dot):
    m = scores[0].max(axis=-1, keepdims=True)
    for s in scores[1:]:
        m = jnp.maximum(m, s.max(axis=-1, keepdims=True))
    num = None
    den = None
    for s, v in zip(scores, values):
        p = jnp.exp(s - m)
        l = p.sum(axis=-1, keepdims=True)
        o = dot(p, v)
        num = o if num is None else num + o
        den = l if den is None else den + l
    return num / den


def _attn_prompt_kernel(q_ref, kp_ref, kc_ref, vp_ref, vc_ref, bias_ref, o_ref, *, dot):
    i = pl.program_id(0)
    lo = lax.broadcasted_iota(jnp.int32, (1, LANES), 1) < A_HEAD_DIM
    kidx = lax.broadcasted_iota(jnp.int32, (1, ATT_KSUB), 1)
    for pp in range(q_ref.shape[1] // LANES):
        cols = slice(pp * LANES, (pp + 1) * LANES)
        q = q_ref[:, cols] * (A_HEAD_DIM ** -0.5)
        k = jnp.concatenate([kp_ref[:, cols], kc_ref[:, cols]], axis=0)
        v = jnp.concatenate([vp_ref[:, cols], vc_ref[:, cols]], axis=0)
        for sub in range(ATT_QBLOCK // ATT_QSUB):
            r0 = sub * ATT_QSUB
            qs = q[r0:r0 + ATT_QSUB]
            ks = k[r0:r0 + ATT_KSUB]
            vs = v[r0:r0 + ATT_KSUB]
            before = jnp.where(jnp.logical_and(i == 0, kidx + r0 < ATT_QBLOCK), NEG_INF, 0.0)
            outs = []
            for hh in range(2):
                qm = jnp.where(lo if hh == 0 else jnp.logical_not(lo), qs, 0.0)
                s = dot(qm, ks, nt=True) + bias_ref[pp, hh] + before
                outs.append(_softmax_pv([s], [vs], dot))
            o_ref[r0:r0 + ATT_QSUB, cols] = jnp.where(lo, outs[0], outs[1]).astype(o_ref.dtype)


def _band_table_kernel(base_ref, allow_ref, o_ref):
    lq, lk = o_ref.shape[1], o_ref.shape[2]
    wide = jnp.broadcast_to(base_ref[0], (lq, base_ref.shape[2]))
    toep = pltpu.roll(wide, 0, 1, stride=1, stride_axis=0)
    o_ref[0] = jnp.where(allow_ref[...] > 0.0, toep[:, :lk], NEG_INF)


def _band_tables(rel_bias, lq, lk, q_pos0, k_pos0):
    n_heads = rel_bias.shape[0]
    width = pl.next_power_of_2(lq + lk)
    c = np.arange(width)
    m = np.where(c < lk, c, c - width)
    idx = np.clip((q_pos0 - k_pos0) - m, -REL_CLIP, REL_CLIP) + REL_CLIP
    base = rel_bias[:, idx].astype(F32).reshape(n_heads, 1, width)
    q_pos = q_pos0 + np.arange(lq)
    k_pos = k_pos0 + np.arange(lk)
    qc = q_pos[:, None] // CHUNK
    kc = k_pos[None, :] // CHUNK
    allowed = ((kc <= qc) & (kc >= qc - A_PAST_CHUNKS) & (k_pos[None, :] >= 0)).astype(np.float32)
    return pl.pallas_call(
        _band_table_kernel,
        grid=(n_heads,),
        in_specs=[pl.BlockSpec((1, 1, width), lambda h: (h, 0, 0)),
                  pl.BlockSpec((lq, lk), lambda h: (0, 0))],
        out_specs=pl.BlockSpec((1, lq, lk), lambda h: (h, 0, 0)),
        out_shape=jax.ShapeDtypeStruct((n_heads, lq, lk), F32),
        compiler_params=_cparams(("arbitrary",)),
        name="band_table",
    )(base, jnp.asarray(allowed))


def attention_prompt(proj, rel_bias, dot, out_dtype):
    t = proj.shape[0]
    nb = t // ATT_QBLOCK
    npair = A_HEADS // 2
    tab = _band_tables(rel_bias, ATT_QSUB, ATT_KSUB, A_WINDOW, 0).reshape(npair, 2, ATT_QSUB, ATT_KSUB)
    blk = (ATT_QBLOCK, ATT_PAIRS * LANES)
    ng = npair // ATT_PAIRS
    return pl.pallas_call(
        functools.partial(_attn_prompt_kernel, dot=dot),
        grid=(nb, ng),
        in_specs=[pl.BlockSpec(blk, lambda i, p: (i, p)),
                  pl.BlockSpec(blk, lambda i, p: (jnp.maximum(i - 1, 0), ng + p)),
                  pl.BlockSpec(blk, lambda i, p: (i, ng + p)),
                  pl.BlockSpec(blk, lambda i, p: (jnp.maximum(i - 1, 0), 2 * ng + p)),
                  pl.BlockSpec(blk, lambda i, p: (i, 2 * ng + p)),
                  pl.BlockSpec((ATT_PAIRS, 2, ATT_QSUB, ATT_KSUB), lambda i, p: (p, 0, 0, 0))],
        out_specs=pl.BlockSpec(blk, lambda i, p: (i, p)),
        out_shape=jax.ShapeDtypeStruct((t, A_WIDTH), out_dtype),
        compiler_params=_cparams(("arbitrary", "arbitrary")),
        name="attn_prompt",
    )(proj, proj, proj, proj, proj, tab)


def _attn_sample_kernel(q_ref, kn_ref, vn_ref, kc_ref, vc_ref, bias_ref, o_ref, *, dot, w):
    lo = lax.broadcasted_iota(jnp.int32, (1, LANES), 1) < A_HEAD_DIM
    lq = q_ref.shape[0]
    for p in range(A_HEADS // 2):
        cols = slice(p * LANES, (p + 1) * LANES)
        q = q_ref[:, cols] * (A_HEAD_DIM ** -0.5)
        kc = kc_ref[0, :, cols]
        vc = vc_ref[0, :, cols]
        kn = kn_ref[:, cols]
        vn = vn_ref[:, cols]
        outs = []
        for hh in range(2):
            qm = jnp.where(lo if hh == 0 else jnp.logical_not(lo), q, 0.0)
            bias = bias_ref[2 * p + hh]
            s_c = dot(qm, kc, nt=True) + bias[:, :w]
            s_n = dot(qm, kn, nt=True) + bias[:, w:w + lq]
            outs.append(_softmax_pv([s_c, s_n], [vc, vn], dot))
        o_ref[:, cols] = jnp.where(lo, outs[0], outs[1]).astype(o_ref.dtype)


def attention_sample(proj, k_cache, v_cache, rel_bias, bsz, lq, dot, out_dtype):
    w = k_cache.shape[1]
    lkp = ((w + lq + LANES - 1) // LANES) * LANES
    tab = _band_tables(rel_bias, lq, lkp, PAST_LEN, PAST_LEN - w)
    kc = k_cache.reshape(bsz, w, A_WIDTH)
    vc = v_cache.reshape(bsz, w, A_WIDTH)
    return pl.pallas_call(
        functools.partial(_attn_sample_kernel, dot=dot, w=w),
        grid=(bsz,),
        in_specs=[pl.BlockSpec((lq, A_WIDTH), lambda b: (b, 0)),
                  pl.BlockSpec((lq, A_WIDTH), lambda b: (b, 1)),
                  pl.BlockSpec((lq, A_WIDTH), lambda b: (b, 2)),
                  pl.BlockSpec((1, w, A_WIDTH), lambda b: (b, 0, 0)),
                  pl.BlockSpec((1, w, A_WIDTH), lambda b: (b, 0, 0)),
                  pl.BlockSpec((A_HEADS, lq, lkp), lambda b: (0, 0, 0))],
        out_specs=pl.BlockSpec((lq, A_WIDTH), lambda b: (b, 0)),
        out_shape=jax.ShapeDtypeStruct((bsz * lq, A_WIDTH), out_dtype),
        compiler_params=_cparams(("arbitrary",)),
        name="attn_sample",
    )(proj, proj, proj, kc, vc, tab)


def _ssd_kernel(z_ref, x0_ref, x1_ref, x2_ref, dt_ref, cbuf_ref, h0_ref,
                cw_ref, cb_ref, dtb_ref, alog_ref, dsk_ref, ng_ref, exp_ref,
                y_ref, cnew_ref, hout_ref, xp_ref, st_ref, *, dot, q):
    c = pl.program_id(1)
    half = B_WIDTH // B_GROUPS

    @pl.when(c == 0)
    def _():
        xp_ref[0:SUBLANES, :] = cbuf_ref[0]
        st_ref[...] = h0_ref[0]

    xp_ref[SUBLANES:SUBLANES + q, 0:512] = x0_ref[...]
    xp_ref[SUBLANES:SUBLANES + q, 512:1024] = x1_ref[...]
    xp_ref[SUBLANES:SUBLANES + q, 1024:1536] = x2_ref[...]
    base = SUBLANES - (B_CONV - 1)
    conv = cb_ref[...] + xp_ref[base:base + q, :] * cw_ref[0:1, :]
    for tap in range(1, B_CONV):
        conv = conv + xp_ref[base + tap:base + tap + q, :] * cw_ref[tap:tap + 1, :]
    tail = xp_ref[q:q + SUBLANES, :]
    xp_ref[0:SUBLANES, :] = tail
    cnew_ref[0] = tail

    xbc = _silu(conv)
    xs = xbc[:, :B_WIDTH]
    bm = xbc[:, B_WIDTH:B_WIDTH + B_GROUPS * B_STATE]
    cm = xbc[:, B_WIDTH + B_GROUPS * B_STATE:]

    lane = lax.broadcasted_iota(jnp.int32, (1, LANES), 1)
    head_ok = lane < B_HEADS
    dt = jnp.where(head_ok, jax.nn.softplus(dt_ref[...] + dtb_ref[...]), 0.0)
    a_neg = -jnp.exp(alog_ref[...])
    row = lax.broadcasted_iota(jnp.int32, (q, q), 0)
    col = lax.broadcasted_iota(jnp.int32, (q, q), 1)
    tril = row >= col
    acs = _dot_sel(tril.astype(BF16), dt * a_neg)
    acs_t = acs.T
    eacs = jnp.exp(acs)
    to_end = jnp.exp(acs[q - 1:q, :] - acs)
    wide = _dot_rsel(jnp.concatenate([dt, eacs, to_end], axis=0), exp_ref[...])
    dt_w = wide[0:q]
    eacs_w = wide[q:2 * q]
    toend_w = wide[2 * q:3 * q]
    xdt = xs * dt_w
    xend = xdt * toend_w
    lo = lane < B_HEAD_DIM

    y_parts = []
    for g in range(B_GROUPS):
        bg = bm[:, g * B_STATE:(g + 1) * B_STATE]
        cg = cm[:, g * B_STATE:(g + 1) * B_STATE]
        gmat = dot(cg, bg, nt=True)
        y_off = dot(cg, st_ref[g]) * eacs_w[:, g * half:(g + 1) * half]
        for pr in range(half // LANES):
            cols = slice(g * half + pr * LANES, g * half + (pr + 1) * LANES)
            x_pair = xdt[:, cols]
            outs = []
            for hh in range(2):
                h = (g * half + pr * LANES) // B_HEAD_DIM + hh
                diff = acs[:, h:h + 1] - acs_t[h:h + 1, :]
                decay = jnp.exp(jnp.where(tril, diff, -jnp.inf))
                outs.append(dot(gmat * decay, x_pair))
            y_parts.append(jnp.where(lo, outs[0], outs[1]) + y_off[:, pr * LANES:(pr + 1) * LANES])
        st_new = dot(bg.T, xend[:, g * half:(g + 1) * half])
        st_ref[g] = st_ref[g] * eacs_w[q - 1:q, g * half:(g + 1) * half] + st_new
    hout_ref[0] = st_ref[...]

    y = jnp.concatenate(y_parts, axis=1) + dsk_ref[...] * xs
    y = y * _silu(z_ref[...])
    y = y * lax.rsqrt(jnp.mean(y * y, axis=-1, keepdims=True) + RMS_EPS) * ng_ref[...]
    y_ref[...] = y.astype(y_ref.dtype)


def ssd_mixer(proj, dt_raw, conv_buf, h0, conv_w, conv_b, dt_bias, a_log, d_skip, norm_g, bsz, seq, q, dot,
              out_dtype):
    t = bsz * seq
    nc = seq // q
    half = B_WIDTH // B_GROUPS
    cbuf = jnp.pad(conv_buf, ((0, 0), (SUBLANES - (B_CONV - 1), 0), (0, 0)))
    h0_t = jnp.transpose(h0.reshape(bsz, B_GROUPS, half, B_STATE), (0, 1, 3, 2))
    pad = lambda v: jnp.pad(v.reshape(1, -1), ((0, 0), (0, LANES - v.shape[-1])))
    expand = jnp.asarray(np.repeat(np.eye(LANES, B_HEADS, dtype=np.float32).T, B_HEAD_DIM, axis=0).T, BF16)
    dsk_w = jnp.repeat(d_skip, B_HEAD_DIM).reshape(1, B_WIDTH)
    zb, xb = 3 * A_WIDTH // B_WIDTH, (3 * A_WIDTH + B_WIDTH) // 512
    const = lambda shp: pl.BlockSpec(shp, lambda b, c: (0,) * len(shp))
    y, cnew, hout = pl.pallas_call(
        functools.partial(_ssd_kernel, dot=dot, q=q),
        grid=(bsz, nc),
        in_specs=[pl.BlockSpec((q, B_WIDTH), lambda b, c: (b * nc + c, zb)),
                  pl.BlockSpec((q, 512), lambda b, c: (b * nc + c, xb)),
                  pl.BlockSpec((q, 512), lambda b, c: (b * nc + c, xb + 1)),
                  pl.BlockSpec((q, 512), lambda b, c: (b * nc + c, xb + 2)),
                  pl.BlockSpec((q, LANES), lambda b, c: (b * nc + c, 0)),
                  pl.BlockSpec((1, SUBLANES, B_CONV_DIM), lambda b, c: (b, 0, 0)),
                  pl.BlockSpec((1, B_GROUPS, B_STATE, half), lambda b, c: (b, 0, 0, 0)),
                  const((B_CONV, B_CONV_DIM)), const((1, B_CONV_DIM)), const((1, LANES)), const((1, LANES)),
                  const((1, B_WIDTH)), const((1, B_WIDTH)), const((LANES, B_WIDTH))],
        out_specs=[pl.BlockSpec((q, B_WIDTH), lambda b, c: (b * nc + c, 0)),
                   pl.BlockSpec((1, SUBLANES, B_CONV_DIM), lambda b, c: (b, 0, 0)),
                   pl.BlockSpec((1, B_GROUPS, B_STATE, half), lambda b, c: (b, 0, 0, 0))],
        out_shape=[jax.ShapeDtypeStruct((t, B_WIDTH), out_dtype),
                   jax.ShapeDtypeStruct((bsz, SUBLANES, B_CONV_DIM), F32),
                   jax.ShapeDtypeStruct((bsz, B_GROUPS, B_STATE, half), F32)],
        scratch_shapes=[pltpu.VMEM((q + SUBLANES, B_CONV_DIM), F32),
                        pltpu.VMEM((B_GROUPS, B_STATE, half), F32)],
        compiler_params=_cparams(("arbitrary", "arbitrary")),
        name="ssd",
    )(proj, proj, proj, proj, dt_raw, cbuf, h0_t,
      conv_w, conv_b.reshape(1, -1), pad(dt_bias), pad(a_log), dsk_w, norm_g.reshape(1, -1), expand)
    conv_new = cnew[:, SUBLANES - (B_CONV - 1):, :]
    h_final = jnp.transpose(hout, (0, 1, 3, 2)).reshape(bsz, B_HEADS, B_HEAD_DIM, B_STATE)
    return y, conv_new, h_final


def _s5_prep_kernel(lre_ref, lim_ref, ls_ref, bre_ref, bim_ref, are_ref, aim_ref, bbre_ref, bbim_ref):
    lre = lre_ref[...]
    lim = lim_ref[...]
    step = jnp.exp(ls_ref[...])
    mag = jnp.exp(lre * step)
    ang = lim * step
    ab_re = mag * jnp.cos(ang)
    ab_im = mag * jnp.sin(ang)
    den = lre * lre + lim * lim
    f_re = ((ab_re - 1.0) * lre + ab_im * lim) / den
    f_im = (ab_im * lre - (ab_re - 1.0) * lim) / den
    br = bre_ref[...]
    bi = bim_ref[...]
    are_ref[...] = ab_re
    aim_ref[...] = ab_im
    bbre_ref[...] = f_re * br - f_im * bi
    bbim_ref[...] = f_re * bi + f_im * br


def _cmul(ar, ai, br, bi):
    return ar * br - ai * bi, ar * bi + ai * br


def _gelu_tanh(x):
    return 0.5 * x * (1.0 + jnp.tanh(math.sqrt(2.0 / math.pi) * (x + 0.044715 * (x * x * x))))


def _s5_kernel(u_ref, bb_ref, ccre_ref, ccim_ref, a_ref, d_ref, h0_ref, y_ref, hout_ref,
               s_ref, sb_ref, pw_ref, pwb_ref, carry_ref, *, dot, seg, chain):
    t = pl.program_id(2)
    w = S5_JW
    sdt = sb_ref.dtype
    g = 2 if sdt == BF16 else 1
    grows = g * SUBLANES
    n_groups = seg // g
    unroll = min(n_groups, 4 // g)
    a_re = a_ref[0, 0:1, :]
    a_im = a_ref[0, 1:2, :]

    def group(k):
        return pl.ds(pl.multiple_of(k * grows, grows), grows)

    @pl.when(t == 0)
    def _():
        carry_ref[0:h0_ref.shape[2], :] = h0_ref[0, 0]
        pw_ref[0:1, 0:w] = a_re
        pw_ref[0:1, w:2 * w] = a_im

        def power(i, carry):
            pr, pi = _cmul(pw_ref[pl.ds(i - 1, 1), 0:w], pw_ref[pl.ds(i - 1, 1), w:2 * w], a_re, a_im)
            pw_ref[pl.ds(i, 1), 0:w] = pr
            pw_ref[pl.ds(i, 1), w:2 * w] = pi
            return carry

        lax.fori_loop(1, seg, power, 0)

        def spread(k, carry):
            rows = [jnp.broadcast_to(pw_ref[pl.ds(k * g + q, 1), :], (SUBLANES, 2 * w)) for q in range(g)]
            pwb_ref[group(k), :] = jnp.concatenate(rows, axis=0).astype(sdt)
            return carry

        lax.fori_loop(0, n_groups, spread, 0)

    ncb = u_ref.shape[0]
    up = jnp.concatenate(
        [jnp.concatenate([u_ref[cb, pl.ds(i, SUBLANES, stride=seg), :] for i in range(seg)], axis=0)
         for cb in range(ncb)], axis=1)
    s_ref[...] = dot(up, bb_ref[0])

    are_b = jnp.broadcast_to(a_re, (SUBLANES, w))
    aim_b = jnp.broadcast_to(a_im, (SUBLANES, w))

    def scan(k, st):
        sre, sim = st
        res, ims = [], []
        for q in range(g):
            rows = pl.ds(pl.multiple_of((k * g + q) * SUBLANES, SUBLANES), SUBLANES)
            pr, pi = _cmul(are_b, aim_b, sre, sim)
            sre = pr + s_ref[rows, 0:w]
            sim = pi + s_ref[rows, w:2 * w]
            res.append(sre)
            ims.append(sim)
        sb_ref[group(k), 0:w] = jnp.concatenate(res, axis=0).astype(sdt)
        sb_ref[group(k), w:2 * w] = jnp.concatenate(ims, axis=0).astype(sdt)
        return sre, sim

    zero = jnp.zeros((SUBLANES, w), F32)
    end_re, end_im = lax.fori_loop(0, n_groups, scan, (zero, zero), unroll=unroll)

    sg_re = pw_ref[seg - 1:seg, 0:w]
    sg_im = pw_ref[seg - 1:seg, w:2 * w]
    if chain:
        c_re = carry_ref[0:1, 0:w]
        c_im = carry_ref[0:1, w:2 * w]
        rows_re, rows_im = [], []
        for s in range(SUBLANES):
            rows_re.append(c_re)
            rows_im.append(c_im)
            pr, pi = _cmul(sg_re, sg_im, c_re, c_im)
            c_re = pr + end_re[s:s + 1, :]
            c_im = pi + end_im[s:s + 1, :]
        carry_ref[0:1, 0:w] = c_re
        carry_ref[0:1, w:2 * w] = c_im
        hout_ref[0, 0] = carry_ref[0:1, :]
        in_re = jnp.concatenate(rows_re, axis=0)
        in_im = jnp.concatenate(rows_im, axis=0)
    else:
        in_re = carry_ref[:, 0:w]
        in_im = carry_ref[:, w:2 * w]
        pr, pi = _cmul(sg_re, sg_im, in_re, in_im)
        carry_ref[:, 0:w] = pr + end_re
        carry_ref[:, w:2 * w] = pi + end_im
        hout_ref[0, 0] = carry_ref[...]

    in_re_g = jnp.concatenate([in_re] * g, axis=0).astype(sdt)
    in_im_g = jnp.concatenate([in_im] * g, axis=0).astype(sdt)

    def fixup(k, carry):
        rows = group(k)
        pr, pi = _cmul(pwb_ref[rows, 0:w], pwb_ref[rows, w:2 * w], in_re_g, in_im_g)
        sb_ref[rows, 0:w] = sb_ref[rows, 0:w] + pr
        sb_ref[rows, w:2 * w] = sb_ref[rows, w:2 * w] + pi
        return carry

    lax.fori_loop(0, n_groups, fixup, 0, unroll=unroll)

    y = dot(sb_ref[:, 0:w], ccre_ref[0]) - dot(sb_ref[:, w:2 * w], ccim_ref[0]) + d_ref[...] * up
    y = _gelu_tanh(y)
    for cb in range(ncb):
        for i in range(seg):
            y_ref[cb, pl.ds(i, SUBLANES, stride=seg), :] = (
                y[i * SUBLANES:(i + 1) * SUBLANES, cb * LANES:(cb + 1) * LANES])


def _block_diag(m):
    nj, j, r, c = m.shape
    eye = jnp.eye(j, dtype=m.dtype)
    return (m[:, :, :, None, :] * eye[None, :, None, :, None]).reshape(nj, j * r, j * c)


def s5_mixer(u, h0_re, h0_im, lam_re, lam_im, log_step, b_re, b_im, c_re, c_im, d_skip,
             bsz, seq, tm, dot, w_dtype, chain):
    t = bsz * seq
    g, p, ch = C_GROUPS, C_STATE, C_GROUP_CH
    rep = lambda v: jnp.repeat(v, ch, axis=-1)
    ls = jnp.broadcast_to(log_step[:, None], (g, p))
    pc = pl.BlockSpec((g, p * ch), lambda: (0, 0))
    shp = jax.ShapeDtypeStruct((g, p * ch), F32)
    a_re_x, a_im_x, bb_re, bb_im = pl.pallas_call(
        _s5_prep_kernel, in_specs=[pc] * 5, out_specs=[pc] * 4, out_shape=[shp] * 4, name="s5_prep",
    )(rep(lam_re), rep(lam_im), rep(ls), b_re.reshape(g, p * ch), b_im.reshape(g, p * ch))
    a_re = a_re_x[:, ::ch]
    a_im = a_im_x[:, ::ch]
    to_blk = lambda m: jnp.transpose(m.reshape(S5_NJ, S5_JBLK, p, ch), (0, 1, 3, 2))
    bb = jnp.concatenate([_block_diag(to_blk(bb_re)), _block_diag(to_blk(bb_im))], axis=-1).astype(w_dtype)
    cblk = lambda m: jnp.transpose(m.reshape(S5_NJ, S5_JBLK, ch, p), (0, 1, 3, 2))
    cc_re = _block_diag(cblk(c_re)).astype(w_dtype)
    cc_im = _block_diag(cblk(c_im)).astype(w_dtype)
    a_rows = jnp.stack([a_re.reshape(S5_NJ, S5_JW), a_im.reshape(S5_NJ, S5_JW)], axis=1)
    h0 = jnp.concatenate([h0_re.reshape(bsz, S5_NJ, S5_JW), h0_im.reshape(bsz, S5_NJ, S5_JW)], axis=-1)
    if chain:
        nb, srows, rows_per_b = bsz, 1, seq
        h0 = h0[:, :, None, :]
    else:
        assert bsz == SUBLANES and tm == bsz * seq
        nb, srows, rows_per_b = 1, SUBLANES, bsz * seq
        h0 = jnp.transpose(h0, (1, 0, 2))[None]
    nt = rows_per_b // tm
    seg = tm // SUBLANES
    uw = S5_JBLK * ch
    ncb = uw // LANES
    state_spec = pl.BlockSpec((1, 1, srows, 2 * S5_JW), lambda b, j, i: (b, j, 0, 0))
    y, hout = pl.pallas_call(
        functools.partial(_s5_kernel, dot=dot, seg=seg, chain=chain),
        grid=(nb, S5_NJ, nt),
        in_specs=[pl.BlockSpec((ncb, tm, LANES), lambda b, j, i: (j, b * nt + i, 0)),
                  pl.BlockSpec((1, uw, 2 * S5_JW), lambda b, j, i: (j, 0, 0)),
                  pl.BlockSpec((1, S5_JW, uw), lambda b, j, i: (j, 0, 0)),
                  pl.BlockSpec((1, S5_JW, uw), lambda b, j, i: (j, 0, 0)),
                  pl.BlockSpec((1, 2, S5_JW), lambda b, j, i: (j, 0, 0)),
                  pl.BlockSpec((1, uw), lambda b, j, i: (0, j)),
                  state_spec],
        out_specs=[pl.BlockSpec((ncb, tm, LANES), lambda b, j, i: (j, b * nt + i, 0)), state_spec],
        out_shape=[jax.ShapeDtypeStruct((g * ch // LANES, t, LANES), F32),
                   jax.ShapeDtypeStruct((nb, S5_NJ, srows, 2 * S5_JW), F32)],
        scratch_shapes=[pltpu.VMEM((tm, 2 * S5_JW), F32),
                        pltpu.VMEM((tm, 2 * S5_JW), w_dtype),
                        pltpu.VMEM((seg, 2 * S5_JW), F32),
                        pltpu.VMEM((tm, 2 * S5_JW), w_dtype),
                        pltpu.VMEM((SUBLANES, 2 * S5_JW), F32)],
        compiler_params=_cparams(("arbitrary", "arbitrary", "arbitrary")),
        name="s5_scan",
    )(u, bb, cc_re, cc_im, a_rows, d_skip.reshape(1, -1), h0)
    hout = hout[:, :, 0, :] if chain else jnp.transpose(hout[0], (1, 0, 2))
    s_re = hout[:, :, :S5_JW].reshape(bsz, g, p)
    s_im = hout[:, :, S5_JW:].reshape(bsz, g, p)
    return y, s_re, s_im


def _router_kernel(x_ref, sc_ref, sh_ref, w_ref, b_ref, o_ref, r_ref, cnt_ref, run_ref):
    h = x_ref[...] * (1.0 + sc_ref[...]) + sh_ref[...]
    logits = _dot3(h, w_ref[...]) + b_ref[...]
    lane = lax.broadcasted_iota(jnp.int32, logits.shape, 1).astype(F32)
    logits = jnp.where(lane < N_EXPERTS, logits, -jnp.inf)
    m1 = logits.max(axis=-1, keepdims=True)
    i1 = jnp.min(jnp.where(logits == m1, lane, float(LANES)), axis=-1, keepdims=True)
    rest = jnp.where(lane == i1, -jnp.inf, logits)
    m2 = rest.max(axis=-1, keepdims=True)
    i2 = jnp.min(jnp.where(rest == m2, lane, float(LANES)), axis=-1, keepdims=True)
    e2 = jnp.exp(m2 - m1)
    w1 = 1.0 / (1.0 + e2)
    w2 = e2 / (1.0 + e2)
    o_ref[...] = jnp.where(lane == i1, w1, 0.0) + jnp.where(lane == i2, w2, 0.0)

    step = pl.program_id(0)

    @pl.when(step == 0)
    def _():
        run_ref[...] = jnp.zeros_like(run_ref)

    tm = logits.shape[0]
    sel = jnp.where(jnp.logical_or(lane == i1, lane == i2), 1.0, 0.0)
    row = lax.broadcasted_iota(jnp.int32, (tm, tm), 0)
    col = lax.broadcasted_iota(jnp.int32, (tm, tm), 1)
    before = (row > col).astype(BF16)
    rank = _mxu(before, sel.astype(BF16)) + run_ref[0:1, :]
    pos1 = jnp.sum(jnp.where(lane == i1, rank, 0.0), axis=-1, keepdims=True)
    pos2 = jnp.sum(jnp.where(lane == i2, rank, 0.0), axis=-1, keepdims=True)
    run_ref[0:1, :] = run_ref[0:1, :] + jnp.sum(sel, axis=0, keepdims=True)
    cnt_ref[...] = run_ref[0:1, :]
    rec = jnp.zeros_like(logits)
    for k, val in enumerate((i1, i2, pos1, pos2, w1, w2)):
        rec = jnp.where(lane == float(k), val, rec)
    r_ref[...] = rec


ROUTE_E1, ROUTE_E2, ROUTE_POS1, ROUTE_POS2, ROUTE_W1, ROUTE_W2 = range(6)


def router(x, scale, shift, router_w, router_b):
    t, d = x.shape
    tm = min(t, 512)
    w = jnp.pad(router_w, ((0, 0), (0, LANES - N_EXPERTS)))
    b = jnp.pad(router_b.reshape(1, -1), ((0, 0), (0, LANES - N_EXPERTS)))
    return pl.pallas_call(
        _router_kernel,
        grid=(t // tm,),
        in_specs=[pl.BlockSpec((tm, d), lambda i: (i, 0)), _row_spec(scale, tm), _row_spec(shift, tm),
                  pl.BlockSpec((d, LANES), lambda i: (0, 0)), pl.BlockSpec((1, LANES), lambda i: (0, 0))],
        out_specs=[pl.BlockSpec((tm, LANES), lambda i: (i, 0)), pl.BlockSpec((tm, LANES), lambda i: (i, 0)),
                   pl.BlockSpec((1, LANES), lambda i: (0, 0))],
        out_shape=[jax.ShapeDtypeStruct((t, LANES), F32), jax.ShapeDtypeStruct((t, LANES), F32),
                   jax.ShapeDtypeStruct((1, LANES), F32)],
        scratch_shapes=[pltpu.VMEM((SUBLANES, LANES), F32)],
        compiler_params=_cparams(("arbitrary",)),
        name="router",
    )(x, scale, shift, w, b)


MOE_TILE = 256
MOE_UP_TN = D_FF_EXPERT // 2
DMA_ISSUE_UNROLL = 8


def _dispatch_kernel(fill_ref, d1_ref, d2_ref, h_ref, xs_ref, zero_ref, sem, zsem):
    tm = h_ref.shape[0]

    @pl.when(pl.program_id(0) == 0)
    def _():
        zero_ref[...] = jnp.zeros_like(zero_ref)

        def fill_copy(k):
            row0 = pl.multiple_of(jnp.maximum(fill_ref[k], 0), MOE_TILE)
            return pltpu.make_async_copy(zero_ref, xs_ref.at[pl.ds(row0, MOE_TILE)], zsem)

        for k in range(fill_ref.shape[0]):
            @pl.when(fill_ref[k] >= 0)
            def _():
                fill_copy(k).start()

        for k in range(fill_ref.shape[0]):
            @pl.when(fill_ref[k] >= 0)
            def _():
                fill_copy(k).wait()

    def issue(t, carry):
        src = h_ref.at[pl.ds(t, 1)]
        pltpu.make_async_copy(src, xs_ref.at[pl.ds(d1_ref[0, 0, t], 1)], sem).start()
        pltpu.make_async_copy(src, xs_ref.at[pl.ds(d2_ref[0, 0, t], 1)], sem).start()
        return carry

    lax.fori_loop(0, tm, issue, 0, unroll=DMA_ISSUE_UNROLL)
    pltpu.make_async_copy(xs_ref.at[pl.ds(0, 2 * tm)], xs_ref.at[pl.ds(0, 2 * tm)], sem).wait()


def moe_dispatch(h, dest1, dest2, fill_rows, n_rows):
    t, d = h.shape
    tm = 512
    idx_spec = pl.BlockSpec((1, 1, tm), lambda i, fr: (i, 0, 0), memory_space=pltpu.SMEM)
    grid_spec = pltpu.PrefetchScalarGridSpec(
        num_scalar_prefetch=1,
        grid=(t // tm,),
        in_specs=[idx_spec, idx_spec, pl.BlockSpec((tm, d), lambda i, fr: (i, 0))],
        out_specs=pl.BlockSpec(memory_space=pl.ANY),
        scratch_shapes=[pltpu.VMEM((MOE_TILE, d), h.dtype), pltpu.SemaphoreType.DMA(()),
                        pltpu.SemaphoreType.DMA(())])
    return pl.pallas_call(
        _dispatch_kernel,
        grid_spec=grid_spec,
        out_shape=jax.ShapeDtypeStruct((n_rows, d), h.dtype),
        compiler_params=_cparams(("arbitrary",)),
        name="moe_dispatch",
    )(fill_rows, dest1.reshape(t // tm, 1, tm), dest2.reshape(t // tm, 1, tm), h)


def _expert_up_kernel(te_ref, nv_ref, x_ref, wg_ref, wu_ref, o_ref, *, dot):
    del te_ref

    @pl.when(pl.program_id(1) < nv_ref[0])
    def _():
        gu = dot(x_ref[...], jnp.concatenate([wg_ref[0], wu_ref[0]], axis=1))
        tn = o_ref.shape[1]
        o_ref[...] = (_silu(gu[:, :tn]) * gu[:, tn:]).astype(o_ref.dtype)

    @pl.when(pl.program_id(1) >= nv_ref[0])
    def _():
        o_ref[...] = jnp.zeros_like(o_ref)


def expert_up(xs, w_up, tile_expert, n_valid, dot, tn):
    r, d = xs.shape
    f = w_up.shape[2] // 2
    nf = f // tn
    nt = r // MOE_TILE
    grid_spec = pltpu.PrefetchScalarGridSpec(
        num_scalar_prefetch=2,
        grid=(nf, nt),
        in_specs=[pl.BlockSpec((MOE_TILE, d), lambda j, i, te, nv: (i, 0)),
                  pl.BlockSpec((1, d, tn), lambda j, i, te, nv: (te[i], 0, j)),
                  pl.BlockSpec((1, d, tn), lambda j, i, te, nv: (te[i], 0, j + nf))],
        out_specs=pl.BlockSpec((MOE_TILE, tn), lambda j, i, te, nv: (i, j)))
    return pl.pallas_call(
        functools.partial(_expert_up_kernel, dot=dot),
        grid_spec=grid_spec,
        out_shape=jax.ShapeDtypeStruct((r, f), BF16),
        compiler_params=_cparams(("arbitrary", "arbitrary")),
        name="expert_up",
    )(tile_expert, n_valid, xs, w_up, w_up)


def _expert_down_kernel(te_ref, nv_ref, a_ref, w_ref, o_ref, *, dot):
    del te_ref

    @pl.when(pl.program_id(0) < nv_ref[0])
    def _():
        o_ref[...] = dot(a_ref[...], w_ref[0])

    @pl.when(pl.program_id(0) >= nv_ref[0])
    def _():
        o_ref[...] = jnp.zeros_like(o_ref)


def expert_down(act, w_down, tile_expert, n_valid, dot):
    r, f = act.shape
    d = w_down.shape[2]
    grid_spec = pltpu.PrefetchScalarGridSpec(
        num_scalar_prefetch=2,
        grid=(r // MOE_TILE,),
        in_specs=[pl.BlockSpec((MOE_TILE, f), lambda i, te, nv: (i, 0)),
                  pl.BlockSpec((1, f, d), lambda i, te, nv: (te[i], 0, 0))],
        out_specs=pl.BlockSpec((MOE_TILE, d), lambda i, te, nv: (i, 0)))
    return pl.pallas_call(
        functools.partial(_expert_down_kernel, dot=dot),
        grid_spec=grid_spec,
        out_shape=jax.ShapeDtypeStruct((r, d), F32),
        compiler_params=_cparams(("arbitrary",)),
        name="expert_down",
    )(tile_expert, n_valid, act, w_down)


def _combine_ln_kernel(d1_ref, d2_ref, d1n_ref, d2n_ref, y_ref, rec_ref, x_ref, gate_ref, g_ref, b_ref, xo_ref,
                       buf1, buf2, sem):
    tm = x_ref.shape[0]
    step = pl.program_id(0)
    slot = step % 2

    def gather(idx1, idx2, to):
        def issue(t, carry):
            pltpu.make_async_copy(y_ref.at[pl.ds(idx1[0, 0, t], 1)], buf1.at[to, pl.ds(t, 1)], sem.at[to]).start()
            pltpu.make_async_copy(y_ref.at[pl.ds(idx2[0, 0, t], 1)], buf2.at[to, pl.ds(t, 1)], sem.at[to]).start()
            return carry

        lax.fori_loop(0, tm, issue, 0, unroll=DMA_ISSUE_UNROLL)

    @pl.when(step == 0)
    def _():
        gather(d1_ref, d2_ref, 0)

    @pl.when(step + 1 < pl.num_programs(0))
    def _():
        gather(d1n_ref, d2n_ref, 1 - slot)

    pltpu.make_async_copy(y_ref.at[pl.ds(0, tm)], buf1.at[slot], sem.at[slot]).wait()
    pltpu.make_async_copy(y_ref.at[pl.ds(0, tm)], buf2.at[slot], sem.at[slot]).wait()

    rec = rec_ref[...]
    lane = lax.broadcasted_iota(jnp.int32, rec.shape, 1)
    w1 = jnp.sum(jnp.where(lane == ROUTE_W1, rec, 0.0), axis=-1, keepdims=True)
    w2 = jnp.sum(jnp.where(lane == ROUTE_W2, rec, 0.0), axis=-1, keepdims=True)
    out = w1 * buf1[slot] + w2 * buf2[slot]
    r = ALPHA * x_ref[...] + (1.0 + gate_ref[...]) * out
    mu = jnp.mean(r, axis=-1, keepdims=True)
    dev = r - mu
    var = jnp.mean(dev * dev, axis=-1, keepdims=True)
    xo_ref[...] = dev * lax.rsqrt(var + LN_EPS) * g_ref[...] + b_ref[...]


def moe_combine_ln(y_sorted, dest1, dest2, rec, x, gate, ln_g, ln_b):
    t, d = x.shape
    tm = 256
    nt = t // tm
    idx_spec = pl.BlockSpec((1, 1, tm), lambda i: (i, 0, 0), memory_space=pltpu.SMEM)
    nxt_spec = pl.BlockSpec((1, 1, tm), lambda i: (jnp.minimum(i + 1, nt - 1), 0, 0), memory_space=pltpu.SMEM)
    d1 = dest1.reshape(nt, 1, tm)
    d2 = dest2.reshape(nt, 1, tm)
    return pl.pallas_call(
        _combine_ln_kernel,
        grid=(nt,),
        in_specs=[idx_spec, idx_spec, nxt_spec, nxt_spec, pl.BlockSpec(memory_space=pl.ANY),
                  pl.BlockSpec((tm, LANES), lambda i: (i, 0)), pl.BlockSpec((tm, d), lambda i: (i, 0)),
                  _row_spec(gate, tm), pl.BlockSpec((1, d), lambda i: (0, 0)), pl.BlockSpec((1, d), lambda i: (0, 0))],
        out_specs=pl.BlockSpec((tm, d), lambda i: (i, 0)),
        out_shape=jax.ShapeDtypeStruct((t, d), F32),
        scratch_shapes=[pltpu.VMEM((2, tm, d), F32), pltpu.VMEM((2, tm, d), F32), pltpu.SemaphoreType.DMA((2,))],
        compiler_params=_cparams(("arbitrary",)),
        name="moe_combine_ln",
    )(d1, d2, d1, d2, y_sorted, rec, x, gate, ln_g.reshape(1, d), ln_b.reshape(1, d))


def moe_top2(h, x, rec, counts, gate, ln_g, ln_b, w_up, w_down, dot):
    t, d = h.shape
    n_tiles = (TOP_K * t) // MOE_TILE + N_EXPERTS
    n_rows = n_tiles * MOE_TILE
    cnt = counts[0, :N_EXPERTS].astype(jnp.int32)
    padded = ((cnt + MOE_TILE - 1) // MOE_TILE) * MOE_TILE
    ends = jnp.cumsum(padded)
    starts = ends - padded
    col = lambda k: rec[:, k].astype(jnp.int32)
    experts = jnp.arange(N_EXPERTS, dtype=jnp.int32)[None, :]
    start_of = lambda e: jnp.sum(jnp.where(e[:, None] == experts, starts[None, :], 0), axis=1)
    dest1 = start_of(col(ROUTE_E1)) + col(ROUTE_POS1)
    dest2 = start_of(col(ROUTE_E2)) + col(ROUTE_POS2)
    tile_start = jnp.arange(n_tiles, dtype=jnp.int32) * MOE_TILE
    tile_expert = jnp.minimum(jnp.sum(tile_start[:, None] >= ends[None, :], axis=1), N_EXPERTS - 1).astype(jnp.int32)
    n_valid = (ends[-1:] // MOE_TILE).astype(jnp.int32)
    last_tile = jnp.where(padded > 0, ends - MOE_TILE, -1)
    spare = ends[-1] + jnp.arange(N_EXPERTS, dtype=jnp.int32) * MOE_TILE
    fill_rows = jnp.concatenate([last_tile, jnp.where(spare < n_rows, spare, -1)]).astype(jnp.int32)
    xs = moe_dispatch(h, dest1, dest2, fill_rows, n_rows)
    act = expert_up(xs, w_up, tile_expert, n_valid, dot, tn=MOE_UP_TN)
    y_sorted = expert_down(act, w_down, tile_expert, n_valid, dot)
    return moe_combine_ln(y_sorted, dest1, dest2, rec, x, gate, ln_g, ln_b)


def _trunk(x, mods, caches, wts, precise):
    bsz, seq, d = x.shape
    t = bsz * seq
    prompt = caches is None
    dot = _dot3 if precise else _dot1
    act_dtype = F32 if precise else BF16
    tm = min(t, 512)
    x0 = x.reshape(t, d)

    sh, sc, gt = mods[0][0]
    proj, dt_raw = in_projection(x0, sc, sh, wts["w_in0"], dot, act_dtype)
    keep = min(A_WINDOW, seq) if prompt else seq
    assert keep == seq or bsz == 1
    kv_rows = proj[t - bsz * keep:, A_WIDTH:3 * A_WIDTH]
    k_new = kv_rows[:, :A_WIDTH].reshape(bsz, keep, A_HEADS, A_HEAD_DIM)
    v_new = kv_rows[:, A_WIDTH:].reshape(bsz, keep, A_HEADS, A_HEAD_DIM)
    if prompt:
        att = attention_prompt(proj, wts["rel_bias"], dot, act_dtype)
        conv_buf = jnp.zeros((bsz, B_CONV - 1, B_CONV_DIM), F32)
        ssm_h0 = jnp.zeros((bsz, B_HEADS, B_HEAD_DIM, B_STATE), F32)
        q_len = min(seq, 256)
    else:
        att = attention_sample(proj, caches["k"], caches["v"], wts["rel_bias"], bsz, seq, dot, act_dtype)
        conv_buf, ssm_h0 = caches["conv"], caches["ssm"]
        q_len = seq
    y_ssd, conv_new, ssm_new = ssd_mixer(proj, dt_raw, conv_buf, ssm_h0, wts["conv_w"], wts["conv_b"], wts["dt_bias"],
                                         wts["a_log"], wts["ssd_d"], wts["ssd_norm_g"], bsz, seq, q_len,
                                         dot, act_dtype)
    x1, h1 = matmul_ln([att, y_ssd], wts["w_out0"], x0, gt, wts["ln_g"][0, 0], wts["ln_b"][0, 0], dot, tm,
                       nxt=(mods[0][1][1], mods[0][1][0]), next_dtype=act_dtype)
    act = ffn_up(h1, wts["ffn_w_up"], dot, act_dtype, tm=min(t, 1024), tn=512)
    x2, h2 = matmul_ln([act], wts["ffn_w_down"], x1, mods[0][1][2], wts["ln_g"][0, 1], wts["ln_b"][0, 1], dot,
                       min(t, 256), tk=None if prompt else 1408, nxt=(mods[1][0][1], mods[1][0][0]), next_dtype=act_dtype)

    u = matmul_slabs(h2, wts["w_in1"], dot)
    if prompt:
        s5_re0 = jnp.zeros((bsz, C_GROUPS, C_STATE), F32)
        s5_im0 = jnp.zeros((bsz, C_GROUPS, C_STATE), F32)
    else:
        s5_re0, s5_im0 = caches["s5_re"], caches["s5_im"]
    y5, s5_re, s5_im = s5_mixer(u, s5_re0, s5_im0, wts["s5_lam_re"], wts["s5_lam_im"], wts["s5_log_step"],
                                wts["s5_b_re"], wts["s5_b_im"], wts["s5_c_re"], wts["s5_c_im"], wts["s5_d"],
                                bsz, seq, 512 if prompt else bsz * seq, dot, F32 if precise else BF16,
                                chain=prompt)
    x3, h3 = matmul_ln([y5], wts["glu_w"], x2, mods[1][0][2], wts["ln_g"][1, 0], wts["ln_b"][1, 0], dot,
                       min(t, 256), tk=None if prompt else 512, glu=True,
                       nxt=(mods[1][1][1], mods[1][1][0]), next_dtype=F32)
    gates, rec, counts = router(x3, mods[1][1][1], mods[1][1][0], wts["router_w"], wts["router_b"])
    if prompt:
        x4 = moe_top2(h3, x3, rec, counts, mods[1][1][2], wts["ln_g"][1, 1], wts["ln_b"][1, 1],
                      wts["moe_w_up"], wts["moe_w_down"], dot)
    else:
        act = ffn_up(h3, wts["moe_w_up"], _dot1, BF16, tm=min(t, 2048), tn=MOE_UP_TN, gates=gates)
        x4, _ = matmul_ln([act], wts["moe_w_down"], x3, mods[1][1][2], wts["ln_g"][1, 1], wts["ln_b"][1, 1],
                          _dot1, tm, tk=D_FF_EXPERT)
    return (x4.reshape(bsz, seq, d), k_new[None], v_new[None], conv_new[None], ssm_new[None],
            s5_re[None], s5_im[None])


def kernel(x_prompt, x_sample, cache_attn_k, cache_attn_v, state_ssd_conv, state_ssd, state_s5_re, state_s5_im, c_prompt, c_sample, ada_w, ada_b, ln_g, ln_b, w_in0, w_out0, rel_bias, conv_w, conv_b, dt_bias, a_log, ssd_d, ssd_norm_g, ffn_w_up, ffn_w_down, w_in1, s5_lam_re, s5_lam_im, s5_log_step, s5_b_re, s5_b_im, s5_c_re, s5_c_im, s5_d, glu_w, router_w, router_b, moe_w_up, moe_w_down):
    d = D_MODEL
    bp, lp, _ = x_prompt.shape
    bs, ls, _ = x_sample.shape

    n_c = bp + bs
    rows = ((n_c + SUBLANES - 1) // SUBLANES) * SUBLANES
    c_all = jnp.pad(jnp.concatenate([c_prompt, c_sample], axis=0), ((0, rows - n_c), (0, 0)))
    mod = adaln_all(c_all, ada_w, ada_b).reshape(DEPTH, 2, rows, 3, d)

    def mods_for(r0, nb, per_row):
        out = []
        for layer in range(DEPTH):
            out.append([])
            for j in range(2):
                trip = []
                for part in range(3):
                    m = mod[layer, j, r0:r0 + nb, part]
                    trip.append(jnp.repeat(m, per_row, axis=0) if nb > 1 else m)
                out[-1].append(tuple(trip))
        return out

    assert bp == 1
    mods_p = mods_for(0, bp, lp)
    mods_s = mods_for(bp, bs, ls)

    shared = dict(ln_g=ln_g, ln_b=ln_b, rel_bias=rel_bias[0], conv_w=conv_w[0], conv_b=conv_b[0],
                  dt_bias=dt_bias[0], a_log=a_log[0], ssd_d=ssd_d[0], ssd_norm_g=ssd_norm_g[0],
                  s5_lam_re=s5_lam_re[0], s5_lam_im=s5_lam_im[0], s5_log_step=s5_log_step[0],
                  s5_b_re=s5_b_re[0], s5_b_im=s5_b_im[0], s5_c_re=s5_c_re[0], s5_c_im=s5_c_im[0], s5_d=s5_d[0],
                  router_w=router_w[0], router_b=router_b[0])
    big = dict(w_in0=w_in0[0], w_out0=w_out0[0], ffn_w_up=ffn_w_up, ffn_w_down=ffn_w_down[0],
               w_in1=w_in1[0], glu_w=glu_w[0])
    moe = dict(moe_w_up=moe_w_up[0].astype(BF16), moe_w_down=moe_w_down[0].astype(BF16))
    wts_p = dict(shared, **moe, **{k: v.astype(BF16) for k, v in big.items()})
    wts_s = dict(shared, **moe, **big)

    y_p, k_p, v_p, conv_p, ssd_p, re_p, im_p = _trunk(x_prompt, mods_p, None, wts_p, precise=False)
    caches = dict(k=cache_attn_k[0], v=cache_attn_v[0], conv=state_ssd_conv[0], ssm=state_ssd[0],
                  s5_re=state_s5_re[0], s5_im=state_s5_im[0])
    y_s, k_s, v_s, conv_s, ssd_s, re_s, im_s = _trunk(x_sample, mods_s, caches, wts_s, precise=True)
    return (y_p, y_s, k_p, v_p, conv_p, ssd_p, re_p, im_p, k_s, v_s, conv_s, ssd_s, re_s, im_s)
```

```python
import functools
import math

import numpy as np
import jax
import jax.numpy as jnp
from jax import lax
from jax.experimental import pallas as pl
from jax.experimental.pallas import tpu as pltpu

F32 = jnp.float32
BF16 = jnp.bfloat16

D_MODEL = 2048
DEPTH = 2
PAST_LEN = 2048
CHUNK = 64
A_HEADS = 16
A_HEAD_DIM = 64
A_WIDTH = A_HEADS * A_HEAD_DIM
A_PAST_CHUNKS = 8
A_WINDOW = A_PAST_CHUNKS * CHUNK
REL_CLIP = 128
B_HEADS = 16
B_HEAD_DIM = 64
B_WIDTH = B_HEADS * B_HEAD_DIM
B_GROUPS = 2
B_STATE = 128
B_CONV = 4
B_CONV_DIM = B_WIDTH + 2 * B_GROUPS * B_STATE
C_GROUP_CH = 16
C_GROUPS = D_MODEL // C_GROUP_CH
C_STATE = 64
D_FF = 5632
N_EXPERTS = 8
TOP_K = 2
D_FF_EXPERT = 2816
ALPHA = (2.0 * DEPTH) ** 0.25
LN_EPS = 1e-5
RMS_EPS = 1e-5
NEG_INF = -1e30
IN0_WIDTH = 3 * A_WIDTH + B_WIDTH + B_CONV_DIM + B_HEADS
IN0_MAIN = IN0_WIDTH - B_HEADS

LANES = 128
SUBLANES = 8
VMEM_LIMIT = 56 * 1024 * 1024

LN_SUBROWS = 256
ATT_QBLOCK = 512
ATT_QSUB = 256
ATT_KSUB = ATT_QSUB + A_WINDOW
ATT_PAIRS = 4
S5_JBLK = 16
S5_NJ = C_GROUPS // S5_JBLK
S5_JW = S5_JBLK * C_STATE


def _cparams(sem):
    return pltpu.CompilerParams(dimension_semantics=sem, vmem_limit_bytes=VMEM_LIMIT)


def _split_bf16(x):
    hi = x.astype(BF16)
    lo = (x - hi.astype(F32)).astype(BF16)
    return hi, lo


def _mxu(a, b, nt=False):
    if nt:
        return lax.dot_general(a, b, (((1,), (1,)), ((), ())), preferred_element_type=F32)
    return jnp.dot(a, b, preferred_element_type=F32)


def _dot1(a, b, nt=False):
    return _mxu(a.astype(BF16), b.astype(BF16), nt)


def _dot3(a, b, nt=False):
    ah, al = _split_bf16(a.astype(F32))
    bh, bl = _split_bf16(b.astype(F32))
    return _mxu(ah, bh, nt) + (_mxu(ah, bl, nt) + _mxu(al, bh, nt))


def _dot_sel(sel_bf16, x):
    x1 = x.astype(BF16)
    r1 = x - x1.astype(F32)
    x2 = r1.astype(BF16)
    x3 = (r1 - x2.astype(F32)).astype(BF16)
    return _mxu(sel_bf16, x1) + (_mxu(sel_bf16, x2) + _mxu(sel_bf16, x3))


def _dot_rsel(x, sel_bf16):
    x1 = x.astype(BF16)
    r1 = x - x1.astype(F32)
    x2 = r1.astype(BF16)
    x3 = (r1 - x2.astype(F32)).astype(BF16)
    return _mxu(x1, sel_bf16) + (_mxu(x2, sel_bf16) + _mxu(x3, sel_bf16))


def _silu(x):
    return x * jax.nn.sigmoid(x)


def _row_spec(arr, tm):
    d = arr.shape[-1]
    if arr.shape[0] == 1:
        return pl.BlockSpec((1, d), lambda i, *_: (0, 0))
    return pl.BlockSpec((tm, d), lambda i, *_: (i, 0))


def _adaln_kernel(c_ref, w_ref, b_ref, o_ref):
    c = c_ref[...]
    o_ref[0] = _dot3(_silu(c), w_ref[0]) + b_ref[0]


def adaln_all(c_rows, ada_w, ada_b):
    r, d = c_rows.shape
    n = ada_w.shape[-1]
    tn = 1024
    w = ada_w.reshape(2 * DEPTH, d, n)
    b = ada_b.reshape(2 * DEPTH, 1, n)
    return pl.pallas_call(
        _adaln_kernel,
        grid=(2 * DEPTH, n // tn),
        in_specs=[pl.BlockSpec((r, d), lambda i, j: (0, 0)),
                  pl.BlockSpec((1, d, tn), lambda i, j: (i, 0, j)),
                  pl.BlockSpec((1, 1, tn), lambda i, j: (i, 0, j))],
        out_specs=pl.BlockSpec((1, r, tn), lambda i, j: (i, 0, j)),
        out_shape=jax.ShapeDtypeStruct((2 * DEPTH, r, n), F32),
        compiler_params=_cparams(("arbitrary", "arbitrary")),
        name="adaln",
    )(c_rows, w, b)


def _mm_kernel(a_ref, w_ref, o_ref, *, dot):
    res = dot(a_ref[...], w_ref[...])
    for cb in range(o_ref.shape[0]):
        o_ref[cb] = res[:, cb * LANES:(cb + 1) * LANES]


def matmul_slabs(a, w, dot, tn=1024):
    t, k = a.shape
    n = w.shape[1]
    tm = min(t, 1024)
    return pl.pallas_call(
        functools.partial(_mm_kernel, dot=dot),
        grid=(t // tm, n // tn),
        in_specs=[pl.BlockSpec((tm, k), lambda i, j: (i, 0)), pl.BlockSpec((k, tn), lambda i, j: (0, j))],
        out_specs=pl.BlockSpec((tn // LANES, tm, LANES), lambda i, j: (j, i, 0)),
        out_shape=jax.ShapeDtypeStruct((n // LANES, t, LANES), F32),
        compiler_params=_cparams(("arbitrary", "arbitrary")),
        name="matmul",
    )(a, w)


def _in_proj_kernel(x_ref, sc_ref, sh_ref, w_ref, wdt_ref, o_ref, dt_ref, h_ref, *, dot):
    j = pl.program_id(1)

    @pl.when(j == 0)
    def _():
        h_ref[...] = (x_ref[...] * (1.0 + sc_ref[...]) + sh_ref[...]).astype(h_ref.dtype)

    h = h_ref[...]
    o_ref[...] = dot(h, w_ref[...])

    @pl.when(j == pl.num_programs(1) - 1)
    def _():
        dt_ref[...] = dot(h, wdt_ref[...])


def in_projection(x, scale, shift, w_in, dot, h_dtype):
    t, k = x.shape
    tm = min(t, 1024)
    tn = 512
    n_tail = w_in.shape[1] - IN0_MAIN
    w_dt = jnp.pad(w_in[:, IN0_MAIN:], ((0, 0), (0, LANES - n_tail)))
    return pl.pallas_call(
        functools.partial(_in_proj_kernel, dot=dot),
        grid=(t // tm, IN0_MAIN // tn),
        in_specs=[pl.BlockSpec((tm, k), lambda i, j: (i, 0)), _row_spec(scale, tm), _row_spec(shift, tm),
                  pl.BlockSpec((k, tn), lambda i, j: (0, j)),
                  pl.BlockSpec((k, LANES), lambda i, j: (0, 0))],
        out_specs=[pl.BlockSpec((tm, tn), lambda i, j: (i, j)), pl.BlockSpec((tm, LANES), lambda i, j: (i, 0))],
        out_shape=[jax.ShapeDtypeStruct((t, IN0_MAIN), F32), jax.ShapeDtypeStruct((t, LANES), F32)],
        scratch_shapes=[pltpu.VMEM((tm, k), h_dtype)],
        compiler_params=_cparams(("arbitrary", "arbitrary")),
        name="in_proj",
    )(x, scale, shift, w_in, w_dt)


def _ffn_up_kernel(*refs, dot, gated):
    if gated:
        a_ref, wg_ref, wu_ref, gates_ref, o_ref = refs
    else:
        a_ref, wg_ref, wu_ref, o_ref = refs
    a = a_ref[...]
    g = dot(a, wg_ref[0])
    u = dot(a, wu_ref[0])
    act = _silu(g) * u
    if gated:
        e = pl.program_id(1)
        gt = gates_ref[...]
        lane = lax.broadcasted_iota(jnp.int32, gt.shape, 1)
        act = act * jnp.sum(jnp.where(lane == e, gt, 0.0), axis=1, keepdims=True)
    o_ref[...] = act.astype(o_ref.dtype)


def ffn_up(a, w_up, dot, out_dtype, tm, tn, gates=None):
    t, d = a.shape
    e_n, _, f2 = w_up.shape
    f = f2 // 2
    nf = f // tn
    in_specs = [pl.BlockSpec((tm, d), lambda i, e, j: (i, 0)),
                pl.BlockSpec((1, d, tn), lambda i, e, j: (e, 0, j)),
                pl.BlockSpec((1, d, tn), lambda i, e, j: (e, 0, j + nf))]
    args = [a, w_up, w_up]
    if gates is not None:
        in_specs.append(pl.BlockSpec((tm, LANES), lambda i, e, j: (i, 0)))
        args.append(gates)
    return pl.pallas_call(
        functools.partial(_ffn_up_kernel, dot=dot, gated=gates is not None),
        grid=(t // tm, e_n, nf),
        in_specs=in_specs,
        out_specs=pl.BlockSpec((tm, tn), lambda i, e, j: (i, e * nf + j)),
        out_shape=jax.ShapeDtypeStruct((t, e_n * f), out_dtype),
        compiler_params=_cparams(("arbitrary", "arbitrary", "arbitrary")),
        name="ffn_up",
    )(*args)


def _mm_ln_kernel(*refs, n_a, nk, dot, glu, has_next):
    a_refs = refs[:n_a]
    w_refs = refs[n_a:2 * n_a]
    pos = 2 * n_a
    x_ref, gate_ref, g_ref, b_ref = refs[pos:pos + 4]
    pos += 4
    if has_next:
        sc_ref, sh_ref = refs[pos:pos + 2]
        pos += 2
    xo_ref = refs[pos]
    pos += 1
    if has_next:
        ho_ref = refs[pos]
        pos += 1
    acc_ref = refs[pos] if nk > 1 else None

    tm = x_ref.shape[0]
    sub = min(tm, max(LANES, min(LN_SUBROWS, tm // 2)))

    def load_a(a_ref, rows):
        if len(a_ref.shape) == 3:
            return jnp.concatenate([a_ref[cb, rows, :] for cb in range(a_ref.shape[0])], axis=1)
        return a_ref[rows, :]

    def product(rows):
        w0 = w_refs[0][0] if len(w_refs[0].shape) == 3 else w_refs[0][...]
        part = dot(load_a(a_refs[0], rows), w0)
        for a_ref, w_ref in zip(a_refs[1:], w_refs[1:]):
            part = part + dot(load_a(a_ref, rows), w_ref[...])
        return part

    def per_row(ref, rows):
        return ref[...] if ref.shape[0] == 1 else ref[rows, :]

    def epilogue(acc, rows):
        if glu:
            d = acc.shape[1] // 2
            out = acc[:, :d] * jax.nn.sigmoid(acc[:, d:])
        else:
            out = acc
        r = ALPHA * x_ref[rows, :] + (1.0 + per_row(gate_ref, rows)) * out
        mu = jnp.mean(r, axis=-1, keepdims=True)
        dev = r - mu
        var = jnp.mean(dev * dev, axis=-1, keepdims=True)
        xn = dev * lax.rsqrt(var + LN_EPS) * g_ref[...] + b_ref[...]
        xo_ref[rows, :] = xn
        if has_next:
            ho_ref[rows, :] = (xn * (1.0 + per_row(sc_ref, rows)) + per_row(sh_ref, rows)).astype(ho_ref.dtype)

    blocks = [slice(r0, r0 + sub) for r0 in range(0, tm, sub)]
    if nk == 1:
        for rows in blocks:
            epilogue(product(rows), rows)
    else:
        k = pl.program_id(1)
        part = product(slice(0, tm))

        @pl.when(k == 0)
        def _():
            acc_ref[...] = part

        @pl.when(k > 0)
        def _():
            acc_ref[...] += part

        @pl.when(k == nk - 1)
        def _():
            for rows in blocks:
                epilogue(acc_ref[rows, :], rows)


def matmul_ln(a_list, w, x, gate, ln_g, ln_b, dot, tm, tk=None, glu=False, nxt=None, next_dtype=None):
    t, d = x.shape
    n = w.shape[-1]
    n_a = len(a_list)
    if n_a > 1:
        nk = 1
        in_specs = [pl.BlockSpec((tm, a.shape[1]), lambda i, k: (i, 0)) for a in a_list]
        off = 0
        for a in a_list:
            ka = a.shape[1]
            assert off % ka == 0
            in_specs.append(pl.BlockSpec((ka, n), functools.partial(lambda i, k, o: (o, 0), o=off // ka),
                                         pipeline_mode=pl.Buffered(1)))
            off += ka
        w_args = [w] * n_a
    else:
        a0 = a_list[0]
        split_cols = a0.ndim == 3
        ktot = a0.shape[0] * LANES if split_cols else a0.shape[1]
        tk = ktot if tk is None else tk
        nk = ktot // tk
        if split_cols:
            a_spec = pl.BlockSpec((tk // LANES, tm, LANES), lambda i, k: (k, i, 0))
        else:
            a_spec = pl.BlockSpec((tm, tk), lambda i, k: (i, k))
        w_mode = dict(pipeline_mode=pl.Buffered(1)) if nk == 1 else {}
        if w.ndim == 3:
            assert w.shape[1] == tk
            w_spec = pl.BlockSpec((1, tk, n), lambda i, k: (k, 0, 0))
        else:
            w_spec = pl.BlockSpec((tk, n), lambda i, k: (k, 0), **w_mode)
        in_specs = [a_spec, w_spec]
        w_args = [w]
    in_specs += [pl.BlockSpec((tm, d), lambda i, k: (i, 0)), _row_spec(gate, tm),
                 pl.BlockSpec((1, d), lambda i, k: (0, 0)), pl.BlockSpec((1, d), lambda i, k: (0, 0))]
    args = list(a_list) + w_args + [x, gate, ln_g.reshape(1, d), ln_b.reshape(1, d)]
    out_specs = [pl.BlockSpec((tm, d), lambda i, k: (i, 0))]
    out_shape = [jax.ShapeDtypeStruct((t, d), F32)]
    if nxt is not None:
        in_specs += [_row_spec(nxt[0], tm), _row_spec(nxt[1], tm)]
        args += [nxt[0], nxt[1]]
        out_specs.append(pl.BlockSpec((tm, d), lambda i, k: (i, 0)))
        out_shape.append(jax.ShapeDtypeStruct((t, d), next_dtype))
    scratch = [pltpu.VMEM((tm, n), F32)] if nk > 1 else []
    res = pl.pallas_call(
        functools.partial(_mm_ln_kernel, n_a=n_a, nk=nk, dot=dot, glu=glu, has_next=nxt is not None),
        grid=(t // tm, nk),
        in_specs=in_specs,
        out_specs=out_specs,
        out_shape=out_shape,
        scratch_shapes=scratch,
        compiler_params=_cparams(("arbitrary", "arbitrary")),
        name="matmul_ln",
    )(*args)
    return res if nxt is not None else (res[0], None)


def _softmax_pv(scores, values, dot):
    m = scores[0].max(axis=-1, keepdims=True)
    for s in scores[1:]:
        m = jnp.maximum(m, s.max(axis=-1, keepdims=True))
    num = None
    den = None
    for s, v in zip(scores, values):
        p = jnp.exp(s - m)
        l = p.sum(axis=-1, keepdims=True)
        o = dot(p, v)
        num = o if num is None else num + o
        den = l if den is None else den + l
    return num / den


def _attn_prompt_kernel(q_ref, kp_ref, kc_ref, vp_ref, vc_ref, bias_ref, o_ref, *, dot):
    i = pl.program_id(0)
    lo = lax.broadcasted_iota(jnp.int32, (1, LANES), 1) < A_HEAD_DIM
    kidx = lax.broadcasted_iota(jnp.int32, (1, ATT_KSUB), 1)
    for pp in range(q_ref.shape[1] // LANES):
        cols = slice(pp * LANES, (pp + 1) * LANES)
        q = q_ref[:, cols] * (A_HEAD_DIM ** -0.5)
        k = jnp.concatenate([kp_ref[:, cols], kc_ref[:, cols]], axis=0)
        v = jnp.concatenate([vp_ref[:, cols], vc_ref[:, cols]], axis=0)
        for sub in range(ATT_QBLOCK // ATT_QSUB):
            r0 = sub * ATT_QSUB
            qs = q[r0:r0 + ATT_QSUB]
            ks = k[r0:r0 + ATT_KSUB]
            vs = v[r0:r0 + ATT_KSUB]
            before = jnp.where(jnp.logical_and(i == 0, kidx + r0 < ATT_QBLOCK), NEG_INF, 0.0)
            outs = []
            for hh in range(2):
                qm = jnp.where(lo if hh == 0 else jnp.logical_not(lo), qs, 0.0)
                s = dot(qm, ks, nt=True) + bias_ref[pp, hh] + before
                outs.append(_softmax_pv([s], [vs], dot))
            o_ref[r0:r0 + ATT_QSUB, cols] = jnp.where(lo, outs[0], outs[1]).astype(o_ref.dtype)


def _band_table_kernel(base_ref, allow_ref, o_ref):
    lq, lk = o_ref.shape[1], o_ref.shape[2]
    wide = jnp.broadcast_to(base_ref[0], (lq, base_ref.shape[2]))
    toep = pltpu.roll(wide, 0, 1, stride=1, stride_axis=0)
    o_ref[0] = jnp.where(allow_ref[...] > 0.0, toep[:, :lk], NEG_INF)


def _band_tables(rel_bias, lq, lk, q_pos0, k_pos0):
    n_heads = rel_bias.shape[0]
    width = pl.next_power_of_2(lq + lk)
    c = np.arange(width)
    m = np.where(c < lk, c, c - width)
    idx = np.clip((q_pos0 - k_pos0) - m, -REL_CLIP, REL_CLIP) + REL_CLIP
    base = rel_bias[:, idx].astype(F32).reshape(n_heads, 1, width)
    q_pos = q_pos0 + np.arange(lq)
    k_pos = k_pos0 + np.arange(lk)
    qc = q_pos[:, None] // CHUNK
    kc = k_pos[None, :] // CHUNK
    allowed = ((kc <= qc) & (kc >= qc - A_PAST_CHUNKS) & (k_pos[None, :] >= 0)).astype(np.float32)
    return pl.pallas_call(
        _band_table_kernel,
        grid=(n_heads,),
        in_specs=[pl.BlockSpec((1, 1, width), lambda h: (h, 0, 0)),
                  pl.BlockSpec((lq, lk), lambda h: (0, 0))],
        out_specs=pl.BlockSpec((1, lq, lk), lambda h: (h, 0, 0)),
        out_shape=jax.ShapeDtypeStruct((n_heads, lq, lk), F32),
        compiler_params=_cparams(("arbitrary",)),
        name="band_table",
    )(base, jnp.asarray(allowed))


def attention_prompt(proj, rel_bias, dot, out_dtype):
    t = proj.shape[0]
    nb = t // ATT_QBLOCK
    npair = A_HEADS // 2
    tab = _band_tables(rel_bias, ATT_QSUB, ATT_KSUB, A_WINDOW, 0).reshape(npair, 2, ATT_QSUB, ATT_KSUB)
    blk = (ATT_QBLOCK, ATT_PAIRS * LANES)
    ng = npair // ATT_PAIRS
    return pl.pallas_call(
        functools.partial(_attn_prompt_kernel, dot=dot),
        grid=(nb, ng),
        in_specs=[pl.BlockSpec(blk, lambda i, p: (i, p)),
                  pl.BlockSpec(blk, lambda i, p: (jnp.maximum(i - 1, 0), ng + p)),
                  pl.BlockSpec(blk, lambda i, p: (i, ng + p)),
                  pl.BlockSpec(blk, lambda i, p: (jnp.maximum(i - 1, 0), 2 * ng + p)),
                  pl.BlockSpec(blk, lambda i, p: (i, 2 * ng + p)),
                  pl.BlockSpec((ATT_PAIRS, 2, ATT_QSUB, ATT_KSUB), lambda i, p: (p, 0, 0, 0))],
        out_specs=pl.BlockSpec(blk, lambda i, p: (i, p)),
        out_shape=jax.ShapeDtypeStruct((t, A_WIDTH), out_dtype),
        compiler_params=_cparams(("arbitrary", "arbitrary")),
        name="attn_prompt",
    )(proj, proj, proj, proj, proj, tab)


def _attn_sample_kernel(q_ref, kn_ref, vn_ref, kc_ref, vc_ref, bias_ref, o_ref, *, dot, w):
    lo = lax.broadcasted_iota(jnp.int32, (1, LANES), 1) < A_HEAD_DIM
    lq = q_ref.shape[0]
    for p in range(A_HEADS // 2):
        cols = slice(p * LANES, (p + 1) * LANES)
        q = q_ref[:, cols] * (A_HEAD_DIM ** -0.5)
        kc = kc_ref[0, :, cols]
        vc = vc_ref[0, :, cols]
        kn = kn_ref[:, cols]
        vn = vn_ref[:, cols]
        outs = []
        for hh in range(2):
            qm = jnp.where(lo if hh == 0 else jnp.logical_not(lo), q, 0.0)
            bias = bias_ref[2 * p + hh]
            s_c = dot(qm, kc, nt=True) + bias[:, :w]
            s_n = dot(qm, kn, nt=True) + bias[:, w:w + lq]
            outs.append(_softmax_pv([s_c, s_n], [vc, vn], dot))
        o_ref[:, cols] = jnp.where(lo, outs[0], outs[1]).astype(o_ref.dtype)


def attention_sample(proj, k_cache, v_cache, rel_bias, bsz, lq, dot, out_dtype):
    w = k_cache.shape[1]
    lkp = ((w + lq + LANES - 1) // LANES) * LANES
    tab = _band_tables(rel_bias, lq, lkp, PAST_LEN, PAST_LEN - w)
    kc = k_cache.reshape(bsz, w, A_WIDTH)
    vc = v_cache.reshape(bsz, w, A_WIDTH)
    return pl.pallas_call(
        functools.partial(_attn_sample_kernel, dot=dot, w=w),
        grid=(bsz,),
        in_specs=[pl.BlockSpec((lq, A_WIDTH), lambda b: (b, 0)),
                  pl.BlockSpec((lq, A_WIDTH), lambda b: (b, 1)),
                  pl.BlockSpec((lq, A_WIDTH), lambda b: (b, 2)),
                  pl.BlockSpec((1, w, A_WIDTH), lambda b: (b, 0, 0)),
                  pl.BlockSpec((1, w, A_WIDTH), lambda b: (b, 0, 0)),
                  pl.BlockSpec((A_HEADS, lq, lkp), lambda b: (0, 0, 0))],
        out_specs=pl.BlockSpec((lq, A_WIDTH), lambda b: (b, 0)),
        out_shape=jax.ShapeDtypeStruct((bsz * lq, A_WIDTH), out_dtype),
        compiler_params=_cparams(("arbitrary",)),
        name="attn_sample",
    )(proj, proj, proj, kc, vc, tab)


def _ssd_kernel(z_ref, x0_ref, x1_ref, x2_ref, dt_ref, cbuf_ref, h0_ref,
                cw_ref, cb_ref, dtb_ref, alog_ref, dsk_ref, ng_ref, exp_ref,
                y_ref, cnew_ref, hout_ref, xp_ref, st_ref, *, dot, q):
    c = pl.program_id(1)
    half = B_WIDTH // B_GROUPS

    @pl.when(c == 0)
    def _():
        xp_ref[0:SUBLANES, :] = cbuf_ref[0]
        st_ref[...] = h0_ref[0]

    xp_ref[SUBLANES:SUBLANES + q, 0:512] = x0_ref[...]
    xp_ref[SUBLANES:SUBLANES + q, 512:1024] = x1_ref[...]
    xp_ref[SUBLANES:SUBLANES + q, 1024:1536] = x2_ref[...]
    base = SUBLANES - (B_CONV - 1)
    conv = cb_ref[...] + xp_ref[base:base + q, :] * cw_ref[0:1, :]
    for tap in range(1, B_CONV):
        conv = conv + xp_ref[base + tap:base + tap + q, :] * cw_ref[tap:tap + 1, :]
    tail = xp_ref[q:q + SUBLANES, :]
    xp_ref[0:SUBLANES, :] = tail
    cnew_ref[0] = tail

    xbc = _silu(conv)
    xs = xbc[:, :B_WIDTH]
    bm = xbc[:, B_WIDTH:B_WIDTH + B_GROUPS * B_STATE]
    cm = xbc[:, B_WIDTH + B_GROUPS * B_STATE:]

    lane = lax.broadcasted_iota(jnp.int32, (1, LANES), 1)
    head_ok = lane < B_HEADS
    dt = jnp.where(head_ok, jax.nn.softplus(dt_ref[...] + dtb_ref[...]), 0.0)
    a_neg = -jnp.exp(alog_ref[...])
    row = lax.broadcasted_iota(jnp.int32, (q, q), 0)
    col = lax.broadcasted_iota(jnp.int32, (q, q), 1)
    tril = row >= col
    acs = _dot_sel(tril.astype(BF16), dt * a_neg)
    acs_t = acs.T
    eacs = jnp.exp(acs)
    to_end = jnp.exp(acs[q - 1:q, :] - acs)
    wide = _dot_rsel(jnp.concatenate([dt, eacs, to_end], axis=0), exp_ref[...])
    dt_w = wide[0:q]
    eacs_w = wide[q:2 * q]
    toend_w = wide[2 * q:3 * q]
    xdt = xs * dt_w
    xend = xdt * toend_w
    lo = lane < B_HEAD_DIM

    y_parts = []
    for g in range(B_GROUPS):
        bg = bm[:, g * B_STATE:(g + 1) * B_STATE]
        cg = cm[:, g * B_STATE:(g + 1) * B_STATE]
        gmat = dot(cg, bg, nt=True)
        y_off = dot(cg, st_ref[g]) * eacs_w[:, g * half:(g + 1) * half]
        for pr in range(half // LANES):
            cols = slice(g * half + pr * LANES, g * half + (pr + 1) * LANES)
            x_pair = xdt[:, cols]
            outs = []
            for hh in range(2):
                h = (g * half + pr * LANES) // B_HEAD_DIM + hh
                diff = acs[:, h:h + 1] - acs_t[h:h + 1, :]
                decay = jnp.exp(jnp.where(tril, diff, -jnp.inf))
                outs.append(dot(gmat * decay, x_pair))
            y_parts.append(jnp.where(lo, outs[0], outs[1]) + y_off[:, pr * LANES:(pr + 1) * LANES])
        st_new = dot(bg.T, xend[:, g * half:(g + 1) * half])
        st_ref[g] = st_ref[g] * eacs_w[q - 1:q, g * half:(g + 1) * half] + st_new
    hout_ref[0] = st_ref[...]

    y = jnp.concatenate(y_parts, axis=1) + dsk_ref[...] * xs
    y = y * _silu(z_ref[...])
    y = y * lax.rsqrt(jnp.mean(y * y, axis=-1, keepdims=True) + RMS_EPS) * ng_ref[...]
    y_ref[...] = y.astype(y_ref.dtype)


def ssd_mixer(proj, dt_raw, conv_buf, h0, conv_w, conv_b, dt_bias, a_log, d_skip, norm_g, bsz, seq, q, dot,
              out_dtype):
    t = bsz * seq
    nc = seq // q
    half = B_WIDTH // B_GROUPS
    cbuf = jnp.pad(conv_buf, ((0, 0), (SUBLANES - (B_CONV - 1), 0), (0, 0)))
    h0_t = jnp.transpose(h0.reshape(bsz, B_GROUPS, half, B_STATE), (0, 1, 3, 2))
    pad = lambda v: jnp.pad(v.reshape(1, -1), ((0, 0), (0, LANES - v.shape[-1])))
    expand = jnp.asarray(np.repeat(np.eye(LANES, B_HEADS, dtype=np.float32).T, B_HEAD_DIM, axis=0).T, BF16)
    dsk_w = jnp.repeat(d_skip, B_HEAD_DIM).reshape(1, B_WIDTH)
    zb, xb = 3 * A_WIDTH // B_WIDTH, (3 * A_WIDTH + B_WIDTH) // 512
    const = lambda shp: pl.BlockSpec(shp, lambda b, c: (0,) * len(shp))
    y, cnew, hout = pl.pallas_call(
        functools.partial(_ssd_kernel, dot=dot, q=q),
        grid=(bsz, nc),
        in_specs=[pl.BlockSpec((q, B_WIDTH), lambda b, c: (b * nc + c, zb)),
                  pl.BlockSpec((q, 512), lambda b, c: (b * nc + c, xb)),
                  pl.BlockSpec((q, 512), lambda b, c: (b * nc + c, xb + 1)),
                  pl.BlockSpec((q, 512), lambda b, c: (b * nc + c, xb + 2)),
                  pl.BlockSpec((q, LANES), lambda b, c: (b * nc + c, 0)),
                  pl.BlockSpec((1, SUBLANES, B_CONV_DIM), lambda b, c: (b, 0, 0)),
                  pl.BlockSpec((1, B_GROUPS, B_STATE, half), lambda b, c: (b, 0, 0, 0)),
                  const((B_CONV, B_CONV_DIM)), const((1, B_CONV_DIM)), const((1, LANES)), const((1, LANES)),
                  const((1, B_WIDTH)), const((1, B_WIDTH)), const((LANES, B_WIDTH))],
        out_specs=[pl.BlockSpec((q, B_WIDTH), lambda b, c: (b * nc + c, 0)),
                   pl.BlockSpec((1, SUBLANES, B_CONV_DIM), lambda b, c: (b, 0, 0)),
                   pl.BlockSpec((1, B_GROUPS, B_STATE, half), lambda b, c: (b, 0, 0, 0))],
        out_shape=[jax.ShapeDtypeStruct((t, B_WIDTH), out_dtype),
                   jax.ShapeDtypeStruct((bsz, SUBLANES, B_CONV_DIM), F32),
                   jax.ShapeDtypeStruct((bsz, B_GROUPS, B_STATE, half), F32)],
        scratch_shapes=[pltpu.VMEM((q + SUBLANES, B_CONV_DIM), F32),
                        pltpu.VMEM((B_GROUPS, B_STATE, half), F32)],
        compiler_params=_cparams(("arbitrary", "arbitrary")),
        name="ssd",
    )(proj, proj, proj, proj, dt_raw, cbuf, h0_t,
      conv_w, conv_b.reshape(1, -1), pad(dt_bias), pad(a_log), dsk_w, norm_g.reshape(1, -1), expand)
    conv_new = cnew[:, SUBLANES - (B_CONV - 1):, :]
    h_final = jnp.transpose(hout, (0, 1, 3, 2)).reshape(bsz, B_HEADS, B_HEAD_DIM, B_STATE)
    return y, conv_new, h_final


def _s5_prep_kernel(lre_ref, lim_ref, ls_ref, bre_ref, bim_ref, are_ref, aim_ref, bbre_ref, bbim_ref):
    lre = lre_ref[...]
    lim = lim_ref[...]
    step = jnp.exp(ls_ref[...])
    mag = jnp.exp(lre * step)
    ang = lim * step
    ab_re = mag * jnp.cos(ang)
    ab_im = mag * jnp.sin(ang)
    den = lre * lre + lim * lim
    f_re = ((ab_re - 1.0) * lre + ab_im * lim) / den
    f_im = (ab_im * lre - (ab_re - 1.0) * lim) / den
    br = bre_ref[...]
    bi = bim_ref[...]
    are_ref[...] = ab_re
    aim_ref[...] = ab_im
    bbre_ref[...] = f_re * br - f_im * bi
    bbim_ref[...] = f_re * bi + f_im * br


def _cmul(ar, ai, br, bi):
    return ar * br - ai * bi, ar * bi + ai * br


def _gelu_tanh(x):
    return 0.5 * x * (1.0 + jnp.tanh(math.sqrt(2.0 / math.pi) * (x + 0.044715 * (x * x * x))))


def _s5_kernel(u_ref, bb_ref, ccre_ref, ccim_ref, a_ref, d_ref, h0_ref, y_ref, hout_ref,
               s_ref, sb_ref, pw_ref, pwb_ref, carry_ref, *, dot, seg, chain):
    t = pl.program_id(2)
    w = S5_JW
    sdt = sb_ref.dtype
    g = 2 if sdt == BF16 else 1
    grows = g * SUBLANES
    n_groups = seg // g
    unroll = min(n_groups, 4 // g)
    a_re = a_ref[0, 0:1, :]
    a_im = a_ref[0, 1:2, :]

    def group(k):
        return pl.ds(pl.multiple_of(k * grows, grows), grows)

    @pl.when(t == 0)
    def _():
        carry_ref[0:h0_ref.shape[2], :] = h0_ref[0, 0]
        pw_ref[0:1, 0:w] = a_re
        pw_ref[0:1, w:2 * w] = a_im

        def power(i, carry):
            pr, pi = _cmul(pw_ref[pl.ds(i - 1, 1), 0:w], pw_ref[pl.ds(i - 1, 1), w:2 * w], a_re, a_im)
            pw_ref[pl.ds(i, 1), 0:w] = pr
            pw_ref[pl.ds(i, 1), w:2 * w] = pi
            return carry

        lax.fori_loop(1, seg, power, 0)

        def spread(k, carry):
            rows = [jnp.broadcast_to(pw_ref[pl.ds(k * g + q, 1), :], (SUBLANES, 2 * w)) for q in range(g)]
            pwb_ref[group(k), :] = jnp.concatenate(rows, axis=0).astype(sdt)
            return carry

        lax.fori_loop(0, n_groups, spread, 0)

    ncb = u_ref.shape[0]
    up = jnp.concatenate(
        [jnp.concatenate([u_ref[cb, pl.ds(i, SUBLANES, stride=seg), :] for i in range(seg)], axis=0)
         for cb in range(ncb)], axis=1)
    s_ref[...] = dot(up, bb_ref[0])

    are_b = jnp.broadcast_to(a_re, (SUBLANES, w))
    aim_b = jnp.broadcast_to(a_im, (SUBLANES, w))

    def scan(k, st):
        sre, sim = st
        res, ims = [], []
        for q in range(g):
            rows = pl.ds(pl.multiple_of((k * g + q) * SUBLANES, SUBLANES), SUBLANES)
            pr, pi = _cmul(are_b, aim_b, sre, sim)
            sre = pr + s_ref[rows, 0:w]
            sim = pi + s_ref[rows, w:2 * w]
            res.append(sre)
            ims.append(sim)
        sb_ref[group(k), 0:w] = jnp.concatenate(res, axis=0).astype(sdt)
        sb_ref[group(k), w:2 * w] = jnp.concatenate(ims, axis=0).astype(sdt)
        return sre, sim

    zero = jnp.zeros((SUBLANES, w), F32)
    end_re, end_im = lax.fori_loop(0, n_groups, scan, (zero, zero), unroll=unroll)

    sg_re = pw_ref[seg - 1:seg, 0:w]
    sg_im = pw_ref[seg - 1:seg, w:2 * w]
    if chain:
        c_re = carry_ref[0:1, 0:w]
        c_im = carry_ref[0:1, w:2 * w]
        rows_re, rows_im = [], []
        for s in range(SUBLANES):
            rows_re.append(c_re)
            rows_im.append(c_im)
            pr, pi = _cmul(sg_re, sg_im, c_re, c_im)
            c_re = pr + end_re[s:s + 1, :]
            c_im = pi + end_im[s:s + 1, :]
        carry_ref[0:1, 0:w] = c_re
        carry_ref[0:1, w:2 * w] = c_im
        hout_ref[0, 0] = carry_ref[0:1, :]
        in_re = jnp.concatenate(rows_re, axis=0)
        in_im = jnp.concatenate(rows_im, axis=0)
    else:
        in_re = carry_ref[:, 0:w]
        in_im = carry_ref[:, w:2 * w]
        pr, pi = _cmul(sg_re, sg_im, in_re, in_im)
        carry_ref[:, 0:w] = pr + end_re
        carry_ref[:, w:2 * w] = pi + end_im
        hout_ref[0, 0] = carry_ref[...]

    in_re_g = jnp.concatenate([in_re] * g, axis=0).astype(sdt)
    in_im_g = jnp.concatenate([in_im] * g, axis=0).astype(sdt)

    def fixup(k, carry):
        rows = group(k)
        pr, pi = _cmul(pwb_ref[rows, 0:w], pwb_ref[rows, w:2 * w], in_re_g, in_im_g)
        sb_ref[rows, 0:w] = sb_ref[rows, 0:w] + pr
        sb_ref[rows, w:2 * w] = sb_ref[rows, w:2 * w] + pi
        return carry

    lax.fori_loop(0, n_groups, fixup, 0, unroll=unroll)

    y = dot(sb_ref[:, 0:w], ccre_ref[0]) - dot(sb_ref[:, w:2 * w], ccim_ref[0]) + d_ref[...] * up
    y = _gelu_tanh(y)
    for cb in range(ncb):
        for i in range(seg):
            y_ref[cb, pl.ds(i, SUBLANES, stride=seg), :] = (
                y[i * SUBLANES:(i + 1) * SUBLANES, cb * LANES:(cb + 1) * LANES])


def _block_diag(m):
    nj, j, r, c = m.shape
    eye = jnp.eye(j, dtype=m.dtype)
    return (m[:, :, :, None, :] * eye[None, :, None, :, None]).reshape(nj, j * r, j * c)


def s5_mixer(u, h0_re, h0_im, lam_re, lam_im, log_step, b_re, b_im, c_re, c_im, d_skip,
             bsz, seq, tm, dot, w_dtype, chain):
    t = bsz * seq
    g, p, ch = C_GROUPS, C_STATE, C_GROUP_CH
    rep = lambda v: jnp.repeat(v, ch, axis=-1)
    ls = jnp.broadcast_to(log_step[:, None], (g, p))
    pc = pl.BlockSpec((g, p * ch), lambda: (0, 0))
    shp = jax.ShapeDtypeStruct((g, p * ch), F32)
    a_re_x, a_im_x, bb_re, bb_im = pl.pallas_call(
        _s5_prep_kernel, in_specs=[pc] * 5, out_specs=[pc] * 4, out_shape=[shp] * 4, name="s5_prep",
    )(rep(lam_re), rep(lam_im), rep(ls), b_re.reshape(g, p * ch), b_im.reshape(g, p * ch))
    a_re = a_re_x[:, ::ch]
    a_im = a_im_x[:, ::ch]
    to_blk = lambda m: jnp.transpose(m.reshape(S5_NJ, S5_JBLK, p, ch), (0, 1, 3, 2))
    bb = jnp.concatenate([_block_diag(to_blk(bb_re)), _block_diag(to_blk(bb_im))], axis=-1).astype(w_dtype)
    cblk = lambda m: jnp.transpose(m.reshape(S5_NJ, S5_JBLK, ch, p), (0, 1, 3, 2))
    cc_re = _block_diag(cblk(c_re)).astype(w_dtype)
    cc_im = _block_diag(cblk(c_im)).astype(w_dtype)
    a_rows = jnp.stack([a_re.reshape(S5_NJ, S5_JW), a_im.reshape(S5_NJ, S5_JW)], axis=1)
    h0 = jnp.concatenate([h0_re.reshape(bsz, S5_NJ, S5_JW), h0_im.reshape(bsz, S5_NJ, S5_JW)], axis=-1)
    if chain:
        nb, srows, rows_per_b = bsz, 1, seq
        h0 = h0[:, :, None, :]
    else:
        assert bsz == SUBLANES and tm == bsz * seq
        nb, srows, rows_per_b = 1, SUBLANES, bsz * seq
        h0 = jnp.transpose(h0, (1, 0, 2))[None]
    nt = rows_per_b // tm
    seg = tm // SUBLANES
    uw = S5_JBLK * ch
    ncb = uw // LANES
    state_spec = pl.BlockSpec((1, 1, srows, 2 * S5_JW), lambda b, j, i: (b, j, 0, 0))
    y, hout = pl.pallas_call(
        functools.partial(_s5_kernel, dot=dot, seg=seg, chain=chain),
        grid=(nb, S5_NJ, nt),
        in_specs=[pl.BlockSpec((ncb, tm, LANES), lambda b, j, i: (j, b * nt + i, 0)),
                  pl.BlockSpec((1, uw, 2 * S5_JW), lambda b, j, i: (j, 0, 0)),
                  pl.BlockSpec((1, S5_JW, uw), lambda b, j, i: (j, 0, 0)),
                  pl.BlockSpec((1, S5_JW, uw), lambda b, j, i: (j, 0, 0)),
                  pl.BlockSpec((1, 2, S5_JW), lambda b, j, i: (j, 0, 0)),
                  pl.BlockSpec((1, uw), lambda b, j, i: (0, j)),
                  state_spec],
        out_specs=[pl.BlockSpec((ncb, tm, LANES), lambda b, j, i: (j, b * nt + i, 0)), state_spec],
        out_shape=[jax.ShapeDtypeStruct((g * ch // LANES, t, LANES), F32),
                   jax.ShapeDtypeStruct((nb, S5_NJ, srows, 2 * S5_JW), F32)],
        scratch_shapes=[pltpu.VMEM((tm, 2 * S5_JW), F32),
                        pltpu.VMEM((tm, 2 * S5_JW), w_dtype),
                        pltpu.VMEM((seg, 2 * S5_JW), F32),
                        pltpu.VMEM((tm, 2 * S5_JW), w_dtype),
                        pltpu.VMEM((SUBLANES, 2 * S5_JW), F32)],
        compiler_params=_cparams(("arbitrary", "arbitrary", "arbitrary")),
        name="s5_scan",
    )(u, bb, cc_re, cc_im, a_rows, d_skip.reshape(1, -1), h0)
    hout = hout[:, :, 0, :] if chain else jnp.transpose(hout[0], (1, 0, 2))
    s_re = hout[:, :, :S5_JW].reshape(bsz, g, p)
    s_im = hout[:, :, S5_JW:].reshape(bsz, g, p)
    return y, s_re, s_im


def _router_kernel(x_ref, sc_ref, sh_ref, w_ref, b_ref, o_ref, r_ref, cnt_ref, run_ref):
    h = x_ref[...] * (1.0 + sc_ref[...]) + sh_ref[...]
    logits = _dot3(h, w_ref[...]) + b_ref[...]
    lane = lax.broadcasted_iota(jnp.int32, logits.shape, 1).astype(F32)
    logits = jnp.where(lane < N_EXPERTS, logits, -jnp.inf)
    m1 = logits.max(axis=-1, keepdims=True)
    i1 = jnp.min(jnp.where(logits == m1, lane, float(LANES)), axis=-1, keepdims=True)
    rest = jnp.where(lane == i1, -jnp.inf, logits)
    m2 = rest.max(axis=-1, keepdims=True)
    i2 = jnp.min(jnp.where(rest == m2, lane, float(LANES)), axis=-1, keepdims=True)
    e2 = jnp.exp(m2 - m1)
    w1 = 1.0 / (1.0 + e2)
    w2 = e2 / (1.0 + e2)
    o_ref[...] = jnp.where(lane == i1, w1, 0.0) + jnp.where(lane == i2, w2, 0.0)

    step = pl.program_id(0)

    @pl.when(step == 0)
    def _():
        run_ref[...] = jnp.zeros_like(run_ref)

    tm = logits.shape[0]
    sel = jnp.where(jnp.logical_or(lane == i1, lane == i2), 1.0, 0.0)
    row = lax.broadcasted_iota(jnp.int32, (tm, tm), 0)
    col = lax.broadcasted_iota(jnp.int32, (tm, tm), 1)
    before = (row > col).astype(BF16)
    rank = _mxu(before, sel.astype(BF16)) + run_ref[0:1, :]
    pos1 = jnp.sum(jnp.where(lane == i1, rank, 0.0), axis=-1, keepdims=True)
    pos2 = jnp.sum(jnp.where(lane == i2, rank, 0.0), axis=-1, keepdims=True)
    run_ref[0:1, :] = run_ref[0:1, :] + jnp.sum(sel, axis=0, keepdims=True)
    cnt_ref[...] = run_ref[0:1, :]
    rec = jnp.zeros_like(logits)
    for k, val in enumerate((i1, i2, pos1, pos2, w1, w2)):
        rec = jnp.where(lane == float(k), val, rec)
    r_ref[...] = rec


ROUTE_E1, ROUTE_E2, ROUTE_POS1, ROUTE_POS2, ROUTE_W1, ROUTE_W2 = range(6)


def router(x, scale, shift, router_w, router_b):
    t, d = x.shape
    tm = min(t, 512)
    w = jnp.pad(router_w, ((0, 0), (0, LANES - N_EXPERTS)))
    b = jnp.pad(router_b.reshape(1, -1), ((0, 0), (0, LANES - N_EXPERTS)))
    return pl.pallas_call(
        _router_kernel,
        grid=(t // tm,),
        in_specs=[pl.BlockSpec((tm, d), lambda i: (i, 0)), _row_spec(scale, tm), _row_spec(shift, tm),
                  pl.BlockSpec((d, LANES), lambda i: (0, 0)), pl.BlockSpec((1, LANES), lambda i: (0, 0))],
        out_specs=[pl.BlockSpec((tm, LANES), lambda i: (i, 0)), pl.BlockSpec((tm, LANES), lambda i: (i, 0)),
                   pl.BlockSpec((1, LANES), lambda i: (0, 0))],
        out_shape=[jax.ShapeDtypeStruct((t, LANES), F32), jax.ShapeDtypeStruct((t, LANES), F32),
                   jax.ShapeDtypeStruct((1, LANES), F32)],
        scratch_shapes=[pltpu.VMEM((SUBLANES, LANES), F32)],
        compiler_params=_cparams(("arbitrary",)),
        name="router",
    )(x, scale, shift, w, b)


MOE_TILE = 256
MOE_UP_TN = D_FF_EXPERT // 2
DMA_ISSUE_UNROLL = 8


def _dispatch_kernel(fill_ref, d1_ref, d2_ref, h_ref, xs_ref, zero_ref, sem, zsem):
    tm = h_ref.shape[0]

    @pl.when(pl.program_id(0) == 0)
    def _():
        zero_ref[...] = jnp.zeros_like(zero_ref)

        def fill_copy(k):
            row0 = pl.multiple_of(jnp.maximum(fill_ref[k], 0), MOE_TILE)
            return pltpu.make_async_copy(zero_ref, xs_ref.at[pl.ds(row0, MOE_TILE)], zsem)

        for k in range(fill_ref.shape[0]):
            @pl.when(fill_ref[k] >= 0)
            def _():
                fill_copy(k).start()

        for k in range(fill_ref.shape[0]):
            @pl.when(fill_ref[k] >= 0)
            def _():
                fill_copy(k).wait()

    def issue(t, carry):
        src = h_ref.at[pl.ds(t, 1)]
        pltpu.make_async_copy(src, xs_ref.at[pl.ds(d1_ref[0, 0, t], 1)], sem).start()
        pltpu.make_async_copy(src, xs_ref.at[pl.ds(d2_ref[0, 0, t], 1)], sem).start()
        return carry

    lax.fori_loop(0, tm, issue, 0, unroll=DMA_ISSUE_UNROLL)
    pltpu.make_async_copy(xs_ref.at[pl.ds(0, 2 * tm)], xs_ref.at[pl.ds(0, 2 * tm)], sem).wait()


def moe_dispatch(h, dest1, dest2, fill_rows, n_rows):
    t, d = h.shape
    tm = 512
    idx_spec = pl.BlockSpec((1, 1, tm), lambda i, fr: (i, 0, 0), memory_space=pltpu.SMEM)
    grid_spec = pltpu.PrefetchScalarGridSpec(
        num_scalar_prefetch=1,
        grid=(t // tm,),
        in_specs=[idx_spec, idx_spec, pl.BlockSpec((tm, d), lambda i, fr: (i, 0))],
        out_specs=pl.BlockSpec(memory_space=pl.ANY),
        scratch_shapes=[pltpu.VMEM((MOE_TILE, d), h.dtype), pltpu.SemaphoreType.DMA(()),
                        pltpu.SemaphoreType.DMA(())])
    return pl.pallas_call(
        _dispatch_kernel,
        grid_spec=grid_spec,
        out_shape=jax.ShapeDtypeStruct((n_rows, d), h.dtype),
        compiler_params=_cparams(("arbitrary",)),
        name="moe_dispatch",
    )(fill_rows, dest1.reshape(t // tm, 1, tm), dest2.reshape(t // tm, 1, tm), h)


def _expert_up_kernel(te_ref, nv_ref, x_ref, wg_ref, wu_ref, o_ref, *, dot):
    del te_ref

    @pl.when(pl.program_id(1) < nv_ref[0])
    def _():
        gu = dot(x_ref[...], jnp.concatenate([wg_ref[0], wu_ref[0]], axis=1))
        tn = o_ref.shape[1]
        o_ref[...] = (_silu(gu[:, :tn]) * gu[:, tn:]).astype(o_ref.dtype)

    @pl.when(pl.program_id(1) >= nv_ref[0])
    def _():
        o_ref[...] = jnp.zeros_like(o_ref)


def expert_up(xs, w_up, tile_expert, n_valid, dot, tn):
    r, d = xs.shape
    f = w_up.shape[2] // 2
    nf = f // tn
    nt = r // MOE_TILE
    grid_spec = pltpu.PrefetchScalarGridSpec(
        num_scalar_prefetch=2,
        grid=(nf, nt),
        in_specs=[pl.BlockSpec((MOE_TILE, d), lambda j, i, te, nv: (i, 0)),
                  pl.BlockSpec((1, d, tn), lambda j, i, te, nv: (te[i], 0, j)),
                  pl.BlockSpec((1, d, tn), lambda j, i, te, nv: (te[i], 0, j + nf))],
        out_specs=pl.BlockSpec((MOE_TILE, tn), lambda j, i, te, nv: (i, j)))
    return pl.pallas_call(
        functools.partial(_expert_up_kernel, dot=dot),
        grid_spec=grid_spec,
        out_shape=jax.ShapeDtypeStruct((r, f), BF16),
        compiler_params=_cparams(("arbitrary", "arbitrary")),
        name="expert_up",
    )(tile_expert, n_valid, xs, w_up, w_up)


def _expert_down_kernel(te_ref, nv_ref, a_ref, w_ref, o_ref, *, dot):
    del te_ref

    @pl.when(pl.program_id(0) < nv_ref[0])
    def _():
        o_ref[...] = dot(a_ref[...], w_ref[0])

    @pl.when(pl.program_id(0) >= nv_ref[0])
    def _():
        o_ref[...] = jnp.zeros_like(o_ref)


def expert_down(act, w_down, tile_expert, n_valid, dot):
    r, f = act.shape
    d = w_down.shape[2]
    grid_spec = pltpu.PrefetchScalarGridSpec(
        num_scalar_prefetch=2,
        grid=(r // MOE_TILE,),
        in_specs=[pl.BlockSpec((MOE_TILE, f), lambda i, te, nv: (i, 0)),
                  pl.BlockSpec((1, f, d), lambda i, te, nv: (te[i], 0, 0))],
        out_specs=pl.BlockSpec((MOE_TILE, d), lambda i, te, nv: (i, 0)))
    return pl.pallas_call(
        functools.partial(_expert_down_kernel, dot=dot),
        grid_spec=grid_spec,
        out_shape=jax.ShapeDtypeStruct((r, d), F32),
        compiler_params=_cparams(("arbitrary",)),
        name="expert_down",
    )(tile_expert, n_valid, act, w_down)


def _combine_ln_kernel(d1_ref, d2_ref, d1n_ref, d2n_ref, y_ref, rec_ref, x_ref, gate_ref, g_ref, b_ref, xo_ref,
                       buf1, buf2, sem):
    tm = x_ref.shape[0]
    step = pl.program_id(0)
    slot = step % 2

    def gather(idx1, idx2, to):
        def issue(t, carry):
            pltpu.make_async_copy(y_ref.at[pl.ds(idx1[0, 0, t], 1)], buf1.at[to, pl.ds(t, 1)], sem.at[to]).start()
            pltpu.make_async_copy(y_ref.at[pl.ds(idx2[0, 0, t], 1)], buf2.at[to, pl.ds(t, 1)], sem.at[to]).start()
            return carry

        lax.fori_loop(0, tm, issue, 0, unroll=DMA_ISSUE_UNROLL)

    @pl.when(step == 0)
    def _():
        gather(d1_ref, d2_ref, 0)

    @pl.when(step + 1 < pl.num_programs(0))
    def _():
        gather(d1n_ref, d2n_ref, 1 - slot)

    pltpu.make_async_copy(y_ref.at[pl.ds(0, tm)], buf1.at[slot], sem.at[slot]).wait()
    pltpu.make_async_copy(y_ref.at[pl.ds(0, tm)], buf2.at[slot], sem.at[slot]).wait()

    rec = rec_ref[...]
    lane = lax.broadcasted_iota(jnp.int32, rec.shape, 1)
    w1 = jnp.sum(jnp.where(lane == ROUTE_W1, rec, 0.0), axis=-1, keepdims=True)
    w2 = jnp.sum(jnp.where(lane == ROUTE_W2, rec, 0.0), axis=-1, keepdims=True)
    out = w1 * buf1[slot] + w2 * buf2[slot]
    r = ALPHA * x_ref[...] + (1.0 + gate_ref[...]) * out
    mu = jnp.mean(r, axis=-1, keepdims=True)
    dev = r - mu
    var = jnp.mean(dev * dev, axis=-1, keepdims=True)
    xo_ref[...] = dev * lax.rsqrt(var + LN_EPS) * g_ref[...] + b_ref[...]


def moe_combine_ln(y_sorted, dest1, dest2, rec, x, gate, ln_g, ln_b):
    t, d = x.shape
    tm = 256
    nt = t // tm
    idx_spec = pl.BlockSpec((1, 1, tm), lambda i: (i, 0, 0), memory_space=pltpu.SMEM)
    nxt_spec = pl.BlockSpec((1, 1, tm), lambda i: (jnp.minimum(i + 1, nt - 1), 0, 0), memory_space=pltpu.SMEM)
    d1 = dest1.reshape(nt, 1, tm)
    d2 = dest2.reshape(nt, 1, tm)
    return pl.pallas_call(
        _combine_ln_kernel,
        grid=(nt,),
        in_specs=[idx_spec, idx_spec, nxt_spec, nxt_spec, pl.BlockSpec(memory_space=pl.ANY),
                  pl.BlockSpec((tm, LANES), lambda i: (i, 0)), pl.BlockSpec((tm, d), lambda i: (i, 0)),
                  _row_spec(gate, tm), pl.BlockSpec((1, d), lambda i: (0, 0)), pl.BlockSpec((1, d), lambda i: (0, 0))],
        out_specs=pl.BlockSpec((tm, d), lambda i: (i, 0)),
        out_shape=jax.ShapeDtypeStruct((t, d), F32),
        scratch_shapes=[pltpu.VMEM((2, tm, d), F32), pltpu.VMEM((2, tm, d), F32), pltpu.SemaphoreType.DMA((2,))],
        compiler_params=_cparams(("arbitrary",)),
        name="moe_combine_ln",
    )(d1, d2, d1, d2, y_sorted, rec, x, gate, ln_g.reshape(1, d), ln_b.reshape(1, d))


def moe_top2(h, x, rec, counts, gate, ln_g, ln_b, w_up, w_down, dot):
    t, d = h.shape
    n_tiles = (TOP_K * t) // MOE_TILE + N_EXPERTS
    n_rows = n_tiles * MOE_TILE
    cnt = counts[0, :N_EXPERTS].astype(jnp.int32)
    padded = ((cnt + MOE_TILE - 1) // MOE_TILE) * MOE_TILE
    ends = jnp.cumsum(padded)
    starts = ends - padded
    col = lambda k: rec[:, k].astype(jnp.int32)
    experts = jnp.arange(N_EXPERTS, dtype=jnp.int32)[None, :]
    start_of = lambda e: jnp.sum(jnp.where(e[:, None] == experts, starts[None, :], 0), axis=1)
    dest1 = start_of(col(ROUTE_E1)) + col(ROUTE_POS1)
    dest2 = start_of(col(ROUTE_E2)) + col(ROUTE_POS2)
    tile_start = jnp.arange(n_tiles, dtype=jnp.int32) * MOE_TILE
    tile_expert = jnp.minimum(jnp.sum(tile_start[:, None] >= ends[None, :], axis=1), N_EXPERTS - 1).astype(jnp.int32)
    n_valid = (ends[-1:] // MOE_TILE).astype(jnp.int32)
    last_tile = jnp.where(padded > 0, ends - MOE_TILE, -1)
    spare = ends[-1] + jnp.arange(N_EXPERTS, dtype=jnp.int32) * MOE_TILE
    fill_rows = jnp.concatenate([last_tile, jnp.where(spare < n_rows, spare, -1)]).astype(jnp.int32)
    xs = moe_dispatch(h, dest1, dest2, fill_rows, n_rows)
    act = expert_up(xs, w_up, tile_expert, n_valid, dot, tn=MOE_UP_TN)
    y_sorted = expert_down(act, w_down, tile_expert, n_valid, dot)
    return moe_combine_ln(y_sorted, dest1, dest2, rec, x, gate, ln_g, ln_b)


def _trunk(x, mods, caches, wts, precise):
    bsz, seq, d = x.shape
    t = bsz * seq
    prompt = caches is None
    dot = _dot3 if precise else _dot1
    act_dtype = F32 if precise else BF16
    tm = min(t, 512)
    x0 = x.reshape(t, d)

    sh, sc, gt = mods[0][0]
    proj, dt_raw = in_projection(x0, sc, sh, wts["w_in0"], dot, act_dtype)
    keep = min(A_WINDOW, seq) if prompt else seq
    assert keep == seq or bsz == 1
    kv_rows = proj[t - bsz * keep:, A_WIDTH:3 * A_WIDTH]
    k_new = kv_rows[:, :A_WIDTH].reshape(bsz, keep, A_HEADS, A_HEAD_DIM)
    v_new = kv_rows[:, A_WIDTH:].reshape(bsz, keep, A_HEADS, A_HEAD_DIM)
    if prompt:
        att = attention_prompt(proj, wts["rel_bias"], dot, act_dtype)
        conv_buf = jnp.zeros((bsz, B_CONV - 1, B_CONV_DIM), F32)
        ssm_h0 = jnp.zeros((bsz, B_HEADS, B_HEAD_DIM, B_STATE), F32)
        q_len = min(seq, 256)
    else:
        att = attention_sample(proj, caches["k"], caches["v"], wts["rel_bias"], bsz, seq, dot, act_dtype)
        conv_buf, ssm_h0 = caches["conv"], caches["ssm"]
        q_len = seq
    y_ssd, conv_new, ssm_new = ssd_mixer(proj, dt_raw, conv_buf, ssm_h0, wts["conv_w"], wts["conv_b"], wts["dt_bias"],
                                         wts["a_log"], wts["ssd_d"], wts["ssd_norm_g"], bsz, seq, q_len,
                                         dot, act_dtype)
    x1, h1 = matmul_ln([att, y_ssd], wts["w_out0"], x0, gt, wts["ln_g"][0, 0], wts["ln_b"][0, 0], dot, tm,
                       nxt=(mods[0][1][1], mods[0][1][0]), next_dtype=act_dtype)
    act = ffn_up(h1, wts["ffn_w_up"], dot, act_dtype, tm=min(t, 1024), tn=512)
    x2, h2 = matmul_ln([act], wts["ffn_w_down"], x1, mods[0][1][2], wts["ln_g"][0, 1], wts["ln_b"][0, 1], dot,
                       min(t, 256), tk=None if prompt else 1408, nxt=(mods[1][0][1], mods[1][0][0]), next_dtype=act_dtype)

    u = matmul_slabs(h2, wts["w_in1"], dot)
    if prompt:
        s5_re0 = jnp.zeros((bsz, C_GROUPS, C_STATE), F32)
        s5_im0 = jnp.zeros((bsz, C_GROUPS, C_STATE), F32)
    else:
        s5_re0, s5_im0 = caches["s5_re"], caches["s5_im"]
    y5, s5_re, s5_im = s5_mixer(u, s5_re0, s5_im0, wts["s5_lam_re"], wts["s5_lam_im"], wts["s5_log_step"],
                                wts["s5_b_re"], wts["s5_b_im"], wts["s5_c_re"], wts["s5_c_im"], wts["s5_d"],
                                bsz, seq, 1024 if prompt else bsz * seq, dot, F32 if precise else BF16,
                                chain=prompt)
    x3, h3 = matmul_ln([y5], wts["glu_w"], x2, mods[1][0][2], wts["ln_g"][1, 0], wts["ln_b"][1, 0], dot,
                       min(t, 256), tk=None if prompt else 512, glu=True,
                       nxt=(mods[1][1][1], mods[1][1][0]), next_dtype=F32)
    gates, rec, counts = router(x3, mods[1][1][1], mods[1][1][0], wts["router_w"], wts["router_b"])
    if prompt:
        x4 = moe_top2(h3, x3, rec, counts, mods[1][1][2], wts["ln_g"][1, 1], wts["ln_b"][1, 1],
                      wts["moe_w_up"], wts["moe_w_down"], dot)
    else:
        act = ffn_up(h3, wts["moe_w_up"], _dot1, BF16, tm=min(t, 2048), tn=MOE_UP_TN, gates=gates)
        x4, _ = matmul_ln([act], wts["moe_w_down"], x3, mods[1][1][2], wts["ln_g"][1, 1], wts["ln_b"][1, 1],
                          _dot1, tm, tk=D_FF_EXPERT)
    return (x4.reshape(bsz, seq, d), k_new[None], v_new[None], conv_new[None], ssm_new[None],
            s5_re[None], s5_im[None])


def kernel(x_prompt, x_sample, cache_attn_k, cache_attn_v, state_ssd_conv, state_ssd, state_s5_re, state_s5_im, c_prompt, c_sample, ada_w, ada_b, ln_g, ln_b, w_in0, w_out0, rel_bias, conv_w, conv_b, dt_bias, a_log, ssd_d, ssd_norm_g, ffn_w_up, ffn_w_down, w_in1, s5_lam_re, s5_lam_im, s5_log_step, s5_b_re, s5_b_im, s5_c_re, s5_c_im, s5_d, glu_w, router_w, router_b, moe_w_up, moe_w_down):
    d = D_MODEL
    bp, lp, _ = x_prompt.shape
    bs, ls, _ = x_sample.shape

    n_c = bp + bs
    rows = ((n_c + SUBLANES - 1) // SUBLANES) * SUBLANES
    c_all = jnp.pad(jnp.concatenate([c_prompt, c_sample], axis=0), ((0, rows - n_c), (0, 0)))
    mod = adaln_all(c_all, ada_w, ada_b).reshape(DEPTH, 2, rows, 3, d)

    def mods_for(r0, nb, per_row):
        out = []
        for layer in range(DEPTH):
            out.append([])
            for j in range(2):
                trip = []
                for part in range(3):
                    m = mod[layer, j, r0:r0 + nb, part]
                    trip.append(jnp.repeat(m, per_row, axis=0) if nb > 1 else m)
                out[-1].append(tuple(trip))
        return out

    assert bp == 1
    mods_p = mods_for(0, bp, lp)
    mods_s = mods_for(bp, bs, ls)

    shared = dict(ln_g=ln_g, ln_b=ln_b, rel_bias=rel_bias[0], conv_w=conv_w[0], conv_b=conv_b[0],
                  dt_bias=dt_bias[0], a_log=a_log[0], ssd_d=ssd_d[0], ssd_norm_g=ssd_norm_g[0],
                  s5_lam_re=s5_lam_re[0], s5_lam_im=s5_lam_im[0], s5_log_step=s5_log_step[0],
                  s5_b_re=s5_b_re[0], s5_b_im=s5_b_im[0], s5_c_re=s5_c_re[0], s5_c_im=s5_c_im[0], s5_d=s5_d[0],
                  router_w=router_w[0], router_b=router_b[0])
    big = dict(w_in0=w_in0[0], w_out0=w_out0[0], ffn_w_up=ffn_w_up, ffn_w_down=ffn_w_down[0],
               w_in1=w_in1[0], glu_w=glu_w[0])
    moe = dict(moe_w_up=moe_w_up[0].astype(BF16), moe_w_down=moe_w_down[0].astype(BF16))
    wts_p = dict(shared, **moe, **{k: v.astype(BF16) for k, v in big.items()})
    wts_s = dict(shared, **moe, **big)

    y_p, k_p, v_p, conv_p, ssd_p, re_p, im_p = _trunk(x_prompt, mods_p, None, wts_p, precise=False)
    caches = dict(k=cache_attn_k[0], v=cache_attn_v[0], conv=state_ssd_conv[0], ssm=state_ssd[0],
                  s5_re=state_s5_re[0], s5_im=state_s5_im[0])
    y_s, k_s, v_s, conv_s, ssd_s, re_s, im_s = _trunk(x_sample, mods_s, caches, wts_s, precise=True)
    return (y_p, y_s, k_p, v_p, conv_p, ssd_p, re_p, im_p, k_s, v_s, conv_s, ssd_s, re_s, im_s)
```

```python
import functools
import math

import numpy as np
import jax
import jax.numpy as jnp
from jax import lax
from jax.experimental import pallas as pl
from jax.experimental.pallas import tpu as pltpu

F32 = jnp.float32
BF16 = jnp.bfloat16

D_MODEL = 2048
DEPTH = 2
PAST_LEN = 2048
CHUNK = 64
A_HEADS = 16
A_HEAD_DIM = 64
A_WIDTH = A_HEADS * A_HEAD_DIM
A_PAST_CHUNKS = 8
A_WINDOW = A_PAST_CHUNKS * CHUNK
REL_CLIP = 128
B_HEADS = 16
B_HEAD_DIM = 64
B_WIDTH = B_HEADS * B_HEAD_DIM
B_GROUPS = 2
B_STATE = 128
B_CONV = 4
B_CONV_DIM = B_WIDTH + 2 * B_GROUPS * B_STATE
C_GROUP_CH = 16
C_GROUPS = D_MODEL // C_GROUP_CH
C_STATE = 64
D_FF = 5632
N_EXPERTS = 8
TOP_K = 2
D_FF_EXPERT = 2816
ALPHA = (2.0 * DEPTH) ** 0.25
LN_EPS = 1e-5
RMS_EPS = 1e-5
NEG_INF = -1e30
IN0_WIDTH = 3 * A_WIDTH + B_WIDTH + B_CONV_DIM + B_HEADS
IN0_MAIN = IN0_WIDTH - B_HEADS

LANES = 128
SUBLANES = 8
VMEM_LIMIT = 56 * 1024 * 1024

LN_SUBROWS = 256
ATT_QBLOCK = 512
ATT_QSUB = 256
ATT_KSUB = ATT_QSUB + A_WINDOW
ATT_PAIRS = 4
S5_JBLK = 16
S5_NJ = C_GROUPS // S5_JBLK
S5_JW = S5_JBLK * C_STATE


def _cparams(sem):
    return pltpu.CompilerParams(dimension_semantics=sem, vmem_limit_bytes=VMEM_LIMIT)


def _split_bf16(x):
    hi = x.astype(BF16)
    lo = (x - hi.astype(F32)).astype(BF16)
    return hi, lo


def _mxu(a, b, nt=False):
    if nt:
        return lax.dot_general(a, b, (((1,), (1,)), ((), ())), preferred_element_type=F32)
    return jnp.dot(a, b, preferred_element_type=F32)


def _dot1(a, b, nt=False):
    return _mxu(a.astype(BF16), b.astype(BF16), nt)


def _dot3(a, b, nt=False):
    ah, al = _split_bf16(a.astype(F32))
    bh, bl = _split_bf16(b.astype(F32))
    return _mxu(ah, bh, nt) + (_mxu(ah, bl, nt) + _mxu(al, bh, nt))


def _dot_sel(sel_bf16, x):
    x1 = x.astype(BF16)
    r1 = x - x1.astype(F32)
    x2 = r1.astype(BF16)
    x3 = (r1 - x2.astype(F32)).astype(BF16)
    return _mxu(sel_bf16, x1) + (_mxu(sel_bf16, x2) + _mxu(sel_bf16, x3))


def _dot_rsel(x, sel_bf16):
    x1 = x.astype(BF16)
    r1 = x - x1.astype(F32)
    x2 = r1.astype(BF16)
    x3 = (r1 - x2.astype(F32)).astype(BF16)
    return _mxu(x1, sel_bf16) + (_mxu(x2, sel_bf16) + _mxu(x3, sel_bf16))


def _silu(x):
    return x * jax.nn.sigmoid(x)


def _row_spec(arr, tm):
    d = arr.shape[-1]
    if arr.shape[0] == 1:
        return pl.BlockSpec((1, d), lambda i, *_: (0, 0))
    return pl.BlockSpec((tm, d), lambda i, *_: (i, 0))


def _adaln_kernel(c_ref, w_ref, b_ref, o_ref):
    c = c_ref[...]
    o_ref[0] = _dot3(_silu(c), w_ref[0]) + b_ref[0]


def adaln_all(c_rows, ada_w, ada_b):
    r, d = c_rows.shape
    n = ada_w.shape[-1]
    tn = 1024
    w = ada_w.reshape(2 * DEPTH, d, n)
    b = ada_b.reshape(2 * DEPTH, 1, n)
    return pl.pallas_call(
        _adaln_kernel,
        grid=(2 * DEPTH, n // tn),
        in_specs=[pl.BlockSpec((r, d), lambda i, j: (0, 0)),
                  pl.BlockSpec((1, d, tn), lambda i, j: (i, 0, j)),
                  pl.BlockSpec((1, 1, tn), lambda i, j: (i, 0, j))],
        out_specs=pl.BlockSpec((1, r, tn), lambda i, j: (i, 0, j)),
        out_shape=jax.ShapeDtypeStruct((2 * DEPTH, r, n), F32),
        compiler_params=_cparams(("arbitrary", "arbitrary")),
        name="adaln",
    )(c_rows, w, b)


def _mm_kernel(a_ref, w_ref, o_ref, *, dot):
    res = dot(a_ref[...], w_ref[...])
    for cb in range(o_ref.shape[0]):
        o_ref[cb] = res[:, cb * LANES:(cb + 1) * LANES]


def matmul_slabs(a, w, dot, tn=1024):
    t, k = a.shape
    n = w.shape[1]
    tm = min(t, 1024)
    return pl.pallas_call(
        functools.partial(_mm_kernel, dot=dot),
        grid=(t // tm, n // tn),
        in_specs=[pl.BlockSpec((tm, k), lambda i, j: (i, 0)), pl.BlockSpec((k, tn), lambda i, j: (0, j))],
        out_specs=pl.BlockSpec((tn // LANES, tm, LANES), lambda i, j: (j, i, 0)),
        out_shape=jax.ShapeDtypeStruct((n // LANES, t, LANES), F32),
        compiler_params=_cparams(("arbitrary", "arbitrary")),
        name="matmul",
    )(a, w)


def _in_proj_kernel(x_ref, sc_ref, sh_ref, w_ref, wdt_ref, o_ref, dt_ref, h_ref, *, dot):
    j = pl.program_id(1)

    @pl.when(j == 0)
    def _():
        h_ref[...] = (x_ref[...] * (1.0 + sc_ref[...]) + sh_ref[...]).astype(h_ref.dtype)

    h = h_ref[...]
    o_ref[...] = dot(h, w_ref[...])

    @pl.when(j == pl.num_programs(1) - 1)
    def _():
        dt_ref[...] = dot(h, wdt_ref[...])


def in_projection(x, scale, shift, w_in, dot, h_dtype):
    t, k = x.shape
    tm = min(t, 1024)
    tn = 512
    n_tail = w_in.shape[1] - IN0_MAIN
    w_dt = jnp.pad(w_in[:, IN0_MAIN:], ((0, 0), (0, LANES - n_tail)))
    return pl.pallas_call(
        functools.partial(_in_proj_kernel, dot=dot),
        grid=(t // tm, IN0_MAIN // tn),
        in_specs=[pl.BlockSpec((tm, k), lambda i, j: (i, 0)), _row_spec(scale, tm), _row_spec(shift, tm),
                  pl.BlockSpec((k, tn), lambda i, j: (0, j)),
                  pl.BlockSpec((k, LANES), lambda i, j: (0, 0))],
        out_specs=[pl.BlockSpec((tm, tn), lambda i, j: (i, j)), pl.BlockSpec((tm, LANES), lambda i, j: (i, 0))],
        out_shape=[jax.ShapeDtypeStruct((t, IN0_MAIN), F32), jax.ShapeDtypeStruct((t, LANES), F32)],
        scratch_shapes=[pltpu.VMEM((tm, k), h_dtype)],
        compiler_params=_cparams(("arbitrary", "arbitrary")),
        name="in_proj",
    )(x, scale, shift, w_in, w_dt)


def _ffn_up_kernel(*refs, dot, gated):
    if gated:
        a_ref, wg_ref, wu_ref, gates_ref, o_ref = refs
    else:
        a_ref, wg_ref, wu_ref, o_ref = refs
    a = a_ref[...]
    g = dot(a, wg_ref[0])
    u = dot(a, wu_ref[0])
    act = _silu(g) * u
    if gated:
        e = pl.program_id(1)
        gt = gates_ref[...]
        lane = lax.broadcasted_iota(jnp.int32, gt.shape, 1)
        act = act * jnp.sum(jnp.where(lane == e, gt, 0.0), axis=1, keepdims=True)
    o_ref[...] = act.astype(o_ref.dtype)


def ffn_up(a, w_up, dot, out_dtype, tm, tn, gates=None):
    t, d = a.shape
    e_n, _, f2 = w_up.shape
    f = f2 // 2
    nf = f // tn
    in_specs = [pl.BlockSpec((tm, d), lambda i, e, j: (i, 0)),
                pl.BlockSpec((1, d, tn), lambda i, e, j: (e, 0, j)),
                pl.BlockSpec((1, d, tn), lambda i, e, j: (e, 0, j + nf))]
    args = [a, w_up, w_up]
    if gates is not None:
        in_specs.append(pl.BlockSpec((tm, LANES), lambda i, e, j: (i, 0)))
        args.append(gates)
    return pl.pallas_call(
        functools.partial(_ffn_up_kernel, dot=dot, gated=gates is not None),
        grid=(t // tm, e_n, nf),
        in_specs=in_specs,
        out_specs=pl.BlockSpec((tm, tn), lambda i, e, j: (i, e * nf + j)),
        out_shape=jax.ShapeDtypeStruct((t, e_n * f), out_dtype),
        compiler_params=_cparams(("arbitrary", "arbitrary", "arbitrary")),
        name="ffn_up",
    )(*args)


def _mm_ln_kernel(*refs, n_a, nk, dot, glu, has_next):
    a_refs = refs[:n_a]
    w_refs = refs[n_a:2 * n_a]
    pos = 2 * n_a
    x_ref, gate_ref, g_ref, b_ref = refs[pos:pos + 4]
    pos += 4
    if has_next:
        sc_ref, sh_ref = refs[pos:pos + 2]
        pos += 2
    xo_ref = refs[pos]
    pos += 1
    if has_next:
        ho_ref = refs[pos]
        pos += 1
    acc_ref = refs[pos] if nk > 1 else None

    tm = x_ref.shape[0]
    sub = min(tm, max(LANES, min(LN_SUBROWS, tm // 2)))

    def load_a(a_ref, rows):
        if len(a_ref.shape) == 3:
            return jnp.concatenate([a_ref[cb, rows, :] for cb in range(a_ref.shape[0])], axis=1)
        return a_ref[rows, :]

    def product(rows):
        w0 = w_refs[0][0] if len(w_refs[0].shape) == 3 else w_refs[0][...]
        part = dot(load_a(a_refs[0], rows), w0)
        for a_ref, w_ref in zip(a_refs[1:], w_refs[1:]):
            part = part + dot(load_a(a_ref, rows), w_ref[...])
        return part

    def per_row(ref, rows):
        return ref[...] if ref.shape[0] == 1 else ref[rows, :]

    def epilogue(acc, rows):
        if glu:
            d = acc.shape[1] // 2
            out = acc[:, :d] * jax.nn.sigmoid(acc[:, d:])
        else:
            out = acc
        r = ALPHA * x_ref[rows, :] + (1.0 + per_row(gate_ref, rows)) * out
        mu = jnp.mean(r, axis=-1, keepdims=True)
        dev = r - mu
        var = jnp.mean(dev * dev, axis=-1, keepdims=True)
        xn = dev * lax.rsqrt(var + LN_EPS) * g_ref[...] + b_ref[...]
        xo_ref[rows, :] = xn
        if has_next:
            ho_ref[rows, :] = (xn * (1.0 + per_row(sc_ref, rows)) + per_row(sh_ref, rows)).astype(ho_ref.dtype)

    blocks = [slice(r0, r0 + sub) for r0 in range(0, tm, sub)]
    if nk == 1:
        for rows in blocks:
            epilogue(product(rows), rows)
    else:
        k = pl.program_id(1)
        part = product(slice(0, tm))

        @pl.when(k == 0)
        def _():
            acc_ref[...] = part

        @pl.when(k > 0)
        def _():
            acc_ref[...] += part

        @pl.when(k == nk - 1)
        def _():
            for rows in blocks:
                epilogue(acc_ref[rows, :], rows)


def matmul_ln(a_list, w, x, gate, ln_g, ln_b, dot, tm, tk=None, glu=False, nxt=None, next_dtype=None):
    t, d = x.shape
    n = w.shape[-1]
    n_a = len(a_list)
    if n_a > 1:
        nk = 1
        in_specs = [pl.BlockSpec((tm, a.shape[1]), lambda i, k: (i, 0)) for a in a_list]
        off = 0
        for a in a_list:
            ka = a.shape[1]
            assert off % ka == 0
            in_specs.append(pl.BlockSpec((ka, n), functools.partial(lambda i, k, o: (o, 0), o=off // ka),
                                         pipeline_mode=pl.Buffered(1)))
            off += ka
        w_args = [w] * n_a
    else:
        a0 = a_list[0]
        split_cols = a0.ndim == 3
        ktot = a0.shape[0] * LANES if split_cols else a0.shape[1]
        tk = ktot if tk is None else tk
        nk = ktot // tk
        if split_cols:
            a_spec = pl.BlockSpec((tk // LANES, tm, LANES), lambda i, k: (k, i, 0))
        else:
            a_spec = pl.BlockSpec((tm, tk), lambda i, k: (i, k))
        w_mode = dict(pipeline_mode=pl.Buffered(1)) if nk == 1 else {}
        if w.ndim == 3:
            assert w.shape[1] == tk
            w_spec = pl.BlockSpec((1, tk, n), lambda i, k: (k, 0, 0))
        else:
            w_spec = pl.BlockSpec((tk, n), lambda i, k: (k, 0), **w_mode)
        in_specs = [a_spec, w_spec]
        w_args = [w]
    in_specs += [pl.BlockSpec((tm, d), lambda i, k: (i, 0)), _row_spec(gate, tm),
                 pl.BlockSpec((1, d), lambda i, k: (0, 0)), pl.BlockSpec((1, d), lambda i, k: (0, 0))]
    args = list(a_list) + w_args + [x, gate, ln_g.reshape(1, d), ln_b.reshape(1, d)]
    out_specs = [pl.BlockSpec((tm, d), lambda i, k: (i, 0))]
    out_shape = [jax.ShapeDtypeStruct((t, d), F32)]
    if nxt is not None:
        in_specs += [_row_spec(nxt[0], tm), _row_spec(nxt[1], tm)]
        args += [nxt[0], nxt[1]]
        out_specs.append(pl.BlockSpec((tm, d), lambda i, k: (i, 0)))
        out_shape.append(jax.ShapeDtypeStruct((t, d), next_dtype))
    scratch = [pltpu.VMEM((tm, n), F32)] if nk > 1 else []
    res = pl.pallas_call(
        functools.partial(_mm_ln_kernel, n_a=n_a, nk=nk, dot=dot, glu=glu, has_next=nxt is not None),
        grid=(t // tm, nk),
        in_specs=in_specs,
        out_specs=out_specs,
        out_shape=out_shape,
        scratch_shapes=scratch,
        compiler_params=_cparams(("arbitrary", "arbitrary")),
        name="matmul_ln",
    )(*args)
    return res if nxt is not None else (res[0], None)


def _softmax_pv(scores, values, dot):
    m = scores[0].max(axis=-1, keepdims=True)
    for s in scores[1:]:
        m = jnp.maximum(m, s.max(axis=-1, keepdims=True))
    num = None
    den = None
    for s, v in zip(scores, values):
        p = jnp.exp(s - m)
        l = p.sum(axis=-1, keepdims=True)
        o = dot(p, v)
        num = o if num is None else num + o
        den = l if den is None else den + l
    return num / den


def _attn_prompt_kernel(q_ref, kp_ref, kc_ref, vp_ref, vc_ref, bias_ref, o_ref, *, dot):
    i = pl.program_id(0)
    lo = lax.broadcasted_iota(jnp.int32, (1, LANES), 1) < A_HEAD_DIM
    kidx = lax.broadcasted_iota(jnp.int32, (1, ATT_KSUB), 1)
    for pp in range(q_ref.shape[1] // LANES):
        cols = slice(pp * LANES, (pp + 1) * LANES)
        q = q_ref[:, cols] * (A_HEAD_DIM ** -0.5)
        k = jnp.concatenate([kp_ref[:, cols], kc_ref[:, cols]], axis=0)
        v = jnp.concatenate([vp_ref[:, cols], vc_ref[:, cols]], axis=0)
        for sub in range(ATT_QBLOCK // ATT_QSUB):
            r0 = sub * ATT_QSUB
            qs = q[r0:r0 + ATT_QSUB]
            ks = k[r0:r0 + ATT_KSUB]
            vs = v[r0:r0 + ATT_KSUB]
            before = jnp.where(jnp.logical_and(i == 0, kidx + r0 < ATT_QBLOCK), NEG_INF, 0.0)
            outs = []
            for hh in range(2):
                qm = jnp.where(lo if hh == 0 else jnp.logical_not(lo), qs, 0.0)
                s = dot(qm, ks, nt=True) + bias_ref[pp, hh] + before
                outs.append(_softmax_pv([s], [vs], dot))
            o_ref[r0:r0 + ATT_QSUB, cols] = jnp.where(lo, outs[0], outs[1]).astype(o_ref.dtype)


def _band_table_kernel(base_ref, allow_ref, o_ref):
    lq, lk = o_ref.shape[1], o_ref.shape[2]
    wide = jnp.broadcast_to(base_ref[0], (lq, base_ref.shape[2]))
    toep = pltpu.roll(wide, 0, 1, stride=1, stride_axis=0)
    o_ref[0] = jnp.where(allow_ref[...] > 0.0, toep[:, :lk], NEG_INF)


def _band_tables(rel_bias, lq, lk, q_pos0, k_pos0):
    n_heads = rel_bias.shape[0]
    width = pl.next_power_of_2(lq + lk)
    c = np.arange(width)
    m = np.where(c < lk, c, c - width)
    idx = np.clip((q_pos0 - k_pos0) - m, -REL_CLIP, REL_CLIP) + REL_CLIP
    base = rel_bias[:, idx].astype(F32).reshape(n_heads, 1, width)
    q_pos = q_pos0 + np.arange(lq)
    k_pos = k_pos0 + np.arange(lk)
    qc = q_pos[:, None] // CHUNK
    kc = k_pos[None, :] // CHUNK
    allowed = ((kc <= qc) & (kc >= qc - A_PAST_CHUNKS) & (k_pos[None, :] >= 0)).astype(np.float32)
    return pl.pallas_call(
        _band_table_kernel,
        grid=(n_heads,),
        in_specs=[pl.BlockSpec((1, 1, width), lambda h: (h, 0, 0)),
                  pl.BlockSpec((lq, lk), lambda h: (0, 0))],
        out_specs=pl.BlockSpec((1, lq, lk), lambda h: (h, 0, 0)),
        out_shape=jax.ShapeDtypeStruct((n_heads, lq, lk), F32),
        compiler_params=_cparams(("arbitrary",)),
        name="band_table",
    )(base, jnp.asarray(allowed))


def attention_prompt(proj, rel_bias, dot, out_dtype):
    t = proj.shape[0]
    nb = t // ATT_QBLOCK
    npair = A_HEADS // 2
    tab = _band_tables(rel_bias, ATT_QSUB, ATT_KSUB, A_WINDOW, 0).reshape(npair, 2, ATT_QSUB, ATT_KSUB)
    blk = (ATT_QBLOCK, ATT_PAIRS * LANES)
    ng = npair // ATT_PAIRS
    return pl.pallas_call(
        functools.partial(_attn_prompt_kernel, dot=dot),
        grid=(nb, ng),
        in_specs=[pl.BlockSpec(blk, lambda i, p: (i, p)),
                  pl.BlockSpec(blk, lambda i, p: (jnp.maximum(i - 1, 0), ng + p)),
                  pl.BlockSpec(blk, lambda i, p: (i, ng + p)),
                  pl.BlockSpec(blk, lambda i, p: (jnp.maximum(i - 1, 0), 2 * ng + p)),
                  pl.BlockSpec(blk, lambda i, p: (i, 2 * ng + p)),
                  pl.BlockSpec((ATT_PAIRS, 2, ATT_QSUB, ATT_KSUB), lambda i, p: (p, 0, 0, 0))],
        out_specs=pl.BlockSpec(blk, lambda i, p: (i, p)),
        out_shape=jax.ShapeDtypeStruct((t, A_WIDTH), out_dtype),
        compiler_params=_cparams(("arbitrary", "arbitrary")),
        name="attn_prompt",
    )(proj, proj, proj, proj, proj, tab)


def _attn_sample_kernel(q_ref, kn_ref, vn_ref, kc_ref, vc_ref, bias_ref, o_ref, *, dot, w):
    lo = lax.broadcasted_iota(jnp.int32, (1, LANES), 1) < A_HEAD_DIM
    lq = q_ref.shape[0]
    for p in range(A_HEADS // 2):
        cols = slice(p * LANES, (p + 1) * LANES)
        q = q_ref[:, cols] * (A_HEAD_DIM ** -0.5)
        kc = kc_ref[0, :, cols]
        vc = vc_ref[0, :, cols]
        kn = kn_ref[:, cols]
        vn = vn_ref[:, cols]
        outs = []
        for hh in range(2):
            qm = jnp.where(lo if hh == 0 else jnp.logical_not(lo), q, 0.0)
            bias = bias_ref[2 * p + hh]
            s_c = dot(qm, kc, nt=True) + bias[:, :w]
            s_n = dot(qm, kn, nt=True) + bias[:, w:w + lq]
            outs.append(_softmax_pv([s_c, s_n], [vc, vn], dot))
        o_ref[:, cols] = jnp.where(lo, outs[0], outs[1]).astype(o_ref.dtype)


def attention_sample(proj, k_cache, v_cache, rel_bias, bsz, lq, dot, out_dtype):
    w = k_cache.shape[1]
    lkp = ((w + lq + LANES - 1) // LANES) * LANES
    tab = _band_tables(rel_bias, lq, lkp, PAST_LEN, PAST_LEN - w)
    kc = k_cache.reshape(bsz, w, A_WIDTH)
    vc = v_cache.reshape(bsz, w, A_WIDTH)
    return pl.pallas_call(
        functools.partial(_attn_sample_kernel, dot=dot, w=w),
        grid=(bsz,),
        in_specs=[pl.BlockSpec((lq, A_WIDTH), lambda b: (b, 0)),
                  pl.BlockSpec((lq, A_WIDTH), lambda b: (b, 1)),
                  pl.BlockSpec((lq, A_WIDTH), lambda b: (b, 2)),
                  pl.BlockSpec((1, w, A_WIDTH), lambda b: (b, 0, 0)),
                  pl.BlockSpec((1, w, A_WIDTH), lambda b: (b, 0, 0)),
                  pl.BlockSpec((A_HEADS, lq, lkp), lambda b: (0, 0, 0))],
        out_specs=pl.BlockSpec((lq, A_WIDTH), lambda b: (b, 0)),
        out_shape=jax.ShapeDtypeStruct((bsz * lq, A_WIDTH), out_dtype),
        compiler_params=_cparams(("arbitrary",)),
        name="attn_sample",
    )(proj, proj, proj, kc, vc, tab)


def _ssd_kernel(z_ref, x0_ref, x1_ref, x2_ref, dt_ref, cbuf_ref, h0_ref,
                cw_ref, cb_ref, dtb_ref, alog_ref, dsk_ref, ng_ref, exp_ref,
                y_ref, cnew_ref, hout_ref, xp_ref, st_ref, *, dot, q):
    c = pl.program_id(1)
    half = B_WIDTH // B_GROUPS

    @pl.when(c == 0)
    def _():
        xp_ref[0:SUBLANES, :] = cbuf_ref[0]
        st_ref[...] = h0_ref[0]

    xp_ref[SUBLANES:SUBLANES + q, 0:512] = x0_ref[...]
    xp_ref[SUBLANES:SUBLANES + q, 512:1024] = x1_ref[...]
    xp_ref[SUBLANES:SUBLANES + q, 1024:1536] = x2_ref[...]
    base = SUBLANES - (B_CONV - 1)
    conv = cb_ref[...] + xp_ref[base:base + q, :] * cw_ref[0:1, :]
    for tap in range(1, B_CONV):
        conv = conv + xp_ref[base + tap:base + tap + q, :] * cw_ref[tap:tap + 1, :]
    tail = xp_ref[q:q + SUBLANES, :]
    xp_ref[0:SUBLANES, :] = tail
    cnew_ref[0] = tail

    xbc = _silu(conv)
    xs = xbc[:, :B_WIDTH]
    bm = xbc[:, B_WIDTH:B_WIDTH + B_GROUPS * B_STATE]
    cm = xbc[:, B_WIDTH + B_GROUPS * B_STATE:]

    lane = lax.broadcasted_iota(jnp.int32, (1, LANES), 1)
    head_ok = lane < B_HEADS
    dt = jnp.where(head_ok, jax.nn.softplus(dt_ref[...] + dtb_ref[...]), 0.0)
    a_neg = -jnp.exp(alog_ref[...])
    row = lax.broadcasted_iota(jnp.int32, (q, q), 0)
    col = lax.broadcasted_iota(jnp.int32, (q, q), 1)
    tril = row >= col
    acs = _dot_sel(tril.astype(BF16), dt * a_neg)
    acs_t = acs.T
    eacs = jnp.exp(acs)
    to_end = jnp.exp(acs[q - 1:q, :] - acs)
    wide = _dot_rsel(jnp.concatenate([dt, eacs, to_end], axis=0), exp_ref[...])
    dt_w = wide[0:q]
    eacs_w = wide[q:2 * q]
    toend_w = wide[2 * q:3 * q]
    xdt = xs * dt_w
    xend = xdt * toend_w
    lo = lane < B_HEAD_DIM

    y_parts = []
    for g in range(B_GROUPS):
        bg = bm[:, g * B_STATE:(g + 1) * B_STATE]
        cg = cm[:, g * B_STATE:(g + 1) * B_STATE]
        gmat = dot(cg, bg, nt=True)
        y_off = dot(cg, st_ref[g]) * eacs_w[:, g * half:(g + 1) * half]
        for pr in range(half // LANES):
            cols = slice(g * half + pr * LANES, g * half + (pr + 1) * LANES)
            x_pair = xdt[:, cols]
            outs = []
            for hh in range(2):
                h = (g * half + pr * LANES) // B_HEAD_DIM + hh
                diff = acs[:, h:h + 1] - acs_t[h:h + 1, :]
                decay = jnp.exp(jnp.where(tril, diff, -jnp.inf))
                outs.append(dot(gmat * decay, x_pair))
            y_parts.append(jnp.where(lo, outs[0], outs[1]) + y_off[:, pr * LANES:(pr + 1) * LANES])
        st_new = dot(bg.T, xend[:, g * half:(g + 1) * half])
        st_ref[g] = st_ref[g] * eacs_w[q - 1:q, g * half:(g + 1) * half] + st_new
    hout_ref[0] = st_ref[...]

    y = jnp.concatenate(y_parts, axis=1) + dsk_ref[...] * xs
    y = y * _silu(z_ref[...])
    y = y * lax.rsqrt(jnp.mean(y * y, axis=-1, keepdims=True) + RMS_EPS) * ng_ref[...]
    y_ref[...] = y.astype(y_ref.dtype)


def ssd_mixer(proj, dt_raw, conv_buf, h0, conv_w, conv_b, dt_bias, a_log, d_skip, norm_g, bsz, seq, q, dot,
              out_dtype):
    t = bsz * seq
    nc = seq // q
    half = B_WIDTH // B_GROUPS
    cbuf = jnp.pad(conv_buf, ((0, 0), (SUBLANES - (B_CONV - 1), 0), (0, 0)))
    h0_t = jnp.transpose(h0.reshape(bsz, B_GROUPS, half, B_STATE), (0, 1, 3, 2))
    pad = lambda v: jnp.pad(v.reshape(1, -1), ((0, 0), (0, LANES - v.shape[-1])))
    expand = jnp.asarray(np.repeat(np.eye(LANES, B_HEADS, dtype=np.float32).T, B_HEAD_DIM, axis=0).T, BF16)
    dsk_w = jnp.repeat(d_skip, B_HEAD_DIM).reshape(1, B_WIDTH)
    zb, xb = 3 * A_WIDTH // B_WIDTH, (3 * A_WIDTH + B_WIDTH) // 512
    const = lambda shp: pl.BlockSpec(shp, lambda b, c: (0,) * len(shp))
    y, cnew, hout = pl.pallas_call(
        functools.partial(_ssd_kernel, dot=dot, q=q),
        grid=(bsz, nc),
        in_specs=[pl.BlockSpec((q, B_WIDTH), lambda b, c: (b * nc + c, zb)),
                  pl.BlockSpec((q, 512), lambda b, c: (b * nc + c, xb)),
                  pl.BlockSpec((q, 512), lambda b, c: (b * nc + c, xb + 1)),
                  pl.BlockSpec((q, 512), lambda b, c: (b * nc + c, xb + 2)),
                  pl.BlockSpec((q, LANES), lambda b, c: (b * nc + c, 0)),
                  pl.BlockSpec((1, SUBLANES, B_CONV_DIM), lambda b, c: (b, 0, 0)),
                  pl.BlockSpec((1, B_GROUPS, B_STATE, half), lambda b, c: (b, 0, 0, 0)),
                  const((B_CONV, B_CONV_DIM)), const((1, B_CONV_DIM)), const((1, LANES)), const((1, LANES)),
                  const((1, B_WIDTH)), const((1, B_WIDTH)), const((LANES, B_WIDTH))],
        out_specs=[pl.BlockSpec((q, B_WIDTH), lambda b, c: (b * nc + c, 0)),
                   pl.BlockSpec((1, SUBLANES, B_CONV_DIM), lambda b, c: (b, 0, 0)),
                   pl.BlockSpec((1, B_GROUPS, B_STATE, half), lambda b, c: (b, 0, 0, 0))],
        out_shape=[jax.ShapeDtypeStruct((t, B_WIDTH), out_dtype),
                   jax.ShapeDtypeStruct((bsz, SUBLANES, B_CONV_DIM), F32),
                   jax.ShapeDtypeStruct((bsz, B_GROUPS, B_STATE, half), F32)],
        scratch_shapes=[pltpu.VMEM((q + SUBLANES, B_CONV_DIM), F32),
                        pltpu.VMEM((B_GROUPS, B_STATE, half), F32)],
        compiler_params=_cparams(("arbitrary", "arbitrary")),
        name="ssd",
    )(proj, proj, proj, proj, dt_raw, cbuf, h0_t,
      conv_w, conv_b.reshape(1, -1), pad(dt_bias), pad(a_log), dsk_w, norm_g.reshape(1, -1), expand)
    conv_new = cnew[:, SUBLANES - (B_CONV - 1):, :]
    h_final = jnp.transpose(hout, (0, 1, 3, 2)).reshape(bsz, B_HEADS, B_HEAD_DIM, B_STATE)
    return y, conv_new, h_final


def _s5_prep_kernel(lre_ref, lim_ref, ls_ref, bre_ref, bim_ref, are_ref, aim_ref, bbre_ref, bbim_ref):
    lre = lre_ref[...]
    lim = lim_ref[...]
    step = jnp.exp(ls_ref[...])
    mag = jnp.exp(lre * step)
    ang = lim * step
    ab_re = mag * jnp.cos(ang)
    ab_im = mag * jnp.sin(ang)
    den = lre * lre + lim * lim
    f_re = ((ab_re - 1.0) * lre + ab_im * lim) / den
    f_im = (ab_im * lre - (ab_re - 1.0) * lim) / den
    br = bre_ref[...]
    bi = bim_ref[...]
    are_ref[...] = ab_re
    aim_ref[...] = ab_im
    bbre_ref[...] = f_re * br - f_im * bi
    bbim_ref[...] = f_re * bi + f_im * br


def _cmul(ar, ai, br, bi):
    return ar * br - ai * bi, ar * bi + ai * br


def _gelu_tanh(x):
    return 0.5 * x * (1.0 + jnp.tanh(math.sqrt(2.0 / math.pi) * (x + 0.044715 * (x * x * x))))


def _s5_kernel(u_ref, bb_ref, ccre_ref, ccim_ref, a_ref, d_ref, h0_ref, y_ref, hout_ref,
               s_ref, sb_ref, pw_ref, pwb_ref, carry_ref, *, dot, seg, chain):
    t = pl.program_id(2)
    w = S5_JW
    sdt = sb_ref.dtype
    g = 2 if sdt == BF16 else 1
    grows = g * SUBLANES
    n_groups = seg // g
    unroll = min(n_groups, 4 // g)
    a_re = a_ref[0, 0:1, :]
    a_im = a_ref[0, 1:2, :]

    def group(k):
        return pl.ds(pl.multiple_of(k * grows, grows), grows)

    @pl.when(t == 0)
    def _():
        carry_ref[0:h0_ref.shape[2], :] = h0_ref[0, 0]
        pw_ref[0:1, 0:w] = a_re
        pw_ref[0:1, w:2 * w] = a_im

        def power(i, carry):
            pr, pi = _cmul(pw_ref[pl.ds(i - 1, 1), 0:w], pw_ref[pl.ds(i - 1, 1), w:2 * w], a_re, a_im)
            pw_ref[pl.ds(i, 1), 0:w] = pr
            pw_ref[pl.ds(i, 1), w:2 * w] = pi
            return carry

        lax.fori_loop(1, seg, power, 0)

        def spread(k, carry):
            rows = [jnp.broadcast_to(pw_ref[pl.ds(k * g + q, 1), :], (SUBLANES, 2 * w)) for q in range(g)]
            pwb_ref[group(k), :] = jnp.concatenate(rows, axis=0).astype(sdt)
            return carry

        lax.fori_loop(0, n_groups, spread, 0)

    ncb = u_ref.shape[0]
    up = jnp.concatenate(
        [jnp.concatenate([u_ref[cb, pl.ds(i, SUBLANES, stride=seg), :] for i in range(seg)], axis=0)
         for cb in range(ncb)], axis=1)
    s_ref[...] = dot(up, bb_ref[0])

    are_b = jnp.broadcast_to(a_re, (SUBLANES, w))
    aim_b = jnp.broadcast_to(a_im, (SUBLANES, w))

    def scan(k, st):
        sre, sim = st
        res, ims = [], []
        for q in range(g):
            rows = pl.ds(pl.multiple_of((k * g + q) * SUBLANES, SUBLANES), SUBLANES)
            pr, pi = _cmul(are_b, aim_b, sre, sim)
            sre = pr + s_ref[rows, 0:w]
            sim = pi + s_ref[rows, w:2 * w]
            res.append(sre)
            ims.append(sim)
        sb_ref[group(k), 0:w] = jnp.concatenate(res, axis=0).astype(sdt)
        sb_ref[group(k), w:2 * w] = jnp.concatenate(ims, axis=0).astype(sdt)
        return sre, sim

    zero = jnp.zeros((SUBLANES, w), F32)
    end_re, end_im = lax.fori_loop(0, n_groups, scan, (zero, zero), unroll=unroll)

    sg_re = pw_ref[seg - 1:seg, 0:w]
    sg_im = pw_ref[seg - 1:seg, w:2 * w]
    if chain:
        c_re = carry_ref[0:1, 0:w]
        c_im = carry_ref[0:1, w:2 * w]
        rows_re, rows_im = [], []
        for s in range(SUBLANES):
            rows_re.append(c_re)
            rows_im.append(c_im)
            pr, pi = _cmul(sg_re, sg_im, c_re, c_im)
            c_re = pr + end_re[s:s + 1, :]
            c_im = pi + end_im[s:s + 1, :]
        carry_ref[0:1, 0:w] = c_re
        carry_ref[0:1, w:2 * w] = c_im
        hout_ref[0, 0] = carry_ref[0:1, :]
        in_re = jnp.concatenate(rows_re, axis=0)
        in_im = jnp.concatenate(rows_im, axis=0)
    else:
        in_re = carry_ref[:, 0:w]
        in_im = carry_ref[:, w:2 * w]
        pr, pi = _cmul(sg_re, sg_im, in_re, in_im)
        carry_ref[:, 0:w] = pr + end_re
        carry_ref[:, w:2 * w] = pi + end_im
        hout_ref[0, 0] = carry_ref[...]

    in_re_g = jnp.concatenate([in_re] * g, axis=0).astype(sdt)
    in_im_g = jnp.concatenate([in_im] * g, axis=0).astype(sdt)

    def fixup(k, carry):
        rows = group(k)
        pr, pi = _cmul(pwb_ref[rows, 0:w], pwb_ref[rows, w:2 * w], in_re_g, in_im_g)
        sb_ref[rows, 0:w] = sb_ref[rows, 0:w] + pr
        sb_ref[rows, w:2 * w] = sb_ref[rows, w:2 * w] + pi
        return carry

    lax.fori_loop(0, n_groups, fixup, 0, unroll=unroll)

    y = dot(sb_ref[:, 0:w], ccre_ref[0]) - dot(sb_ref[:, w:2 * w], ccim_ref[0]) + d_ref[...] * up
    y = _gelu_tanh(y)
    for cb in range(ncb):
        for i in range(seg):
            y_ref[cb, pl.ds(i, SUBLANES, stride=seg), :] = (
                y[i * SUBLANES:(i + 1) * SUBLANES, cb * LANES:(cb + 1) * LANES])


def _block_diag(m):
    nj, j, r, c = m.shape
    eye = jnp.eye(j, dtype=m.dtype)
    return (m[:, :, :, None, :] * eye[None, :, None, :, None]).reshape(nj, j * r, j * c)


def s5_mixer(u, h0_re, h0_im, lam_re, lam_im, log_step, b_re, b_im, c_re, c_im, d_skip,
             bsz, seq, tm, dot, w_dtype, chain):
    t = bsz * seq
    g, p, ch = C_GROUPS, C_STATE, C_GROUP_CH
    rep = lambda v: jnp.repeat(v, ch, axis=-1)
    ls = jnp.broadcast_to(log_step[:, None], (g, p))
    pc = pl.BlockSpec((g, p * ch), lambda: (0, 0))
    shp = jax.ShapeDtypeStruct((g, p * ch), F32)
    a_re_x, a_im_x, bb_re, bb_im = pl.pallas_call(
        _s5_prep_kernel, in_specs=[pc] * 5, out_specs=[pc] * 4, out_shape=[shp] * 4, name="s5_prep",
    )(rep(lam_re), rep(lam_im), rep(ls), b_re.reshape(g, p * ch), b_im.reshape(g, p * ch))
    a_re = a_re_x[:, ::ch]
    a_im = a_im_x[:, ::ch]
    to_blk = lambda m: jnp.transpose(m.reshape(S5_NJ, S5_JBLK, p, ch), (0, 1, 3, 2))
    bb = jnp.concatenate([_block_diag(to_blk(bb_re)), _block_diag(to_blk(bb_im))], axis=-1).astype(w_dtype)
    cblk = lambda m: jnp.transpose(m.reshape(S5_NJ, S5_JBLK, ch, p), (0, 1, 3, 2))
    cc_re = _block_diag(cblk(c_re)).astype(w_dtype)
    cc_im = _block_diag(cblk(c_im)).astype(w_dtype)
    a_rows = jnp.stack([a_re.reshape(S5_NJ, S5_JW), a_im.reshape(S5_NJ, S5_JW)], axis=1)
    h0 = jnp.concatenate([h0_re.reshape(bsz, S5_NJ, S5_JW), h0_im.reshape(bsz, S5_NJ, S5_JW)], axis=-1)
    if chain:
        nb, srows, rows_per_b = bsz, 1, seq
        h0 = h0[:, :, None, :]
    else:
        assert bsz == SUBLANES and tm == bsz * seq
        nb, srows, rows_per_b = 1, SUBLANES, bsz * seq
        h0 = jnp.transpose(h0, (1, 0, 2))[None]
    nt = rows_per_b // tm
    seg = tm // SUBLANES
    uw = S5_JBLK * ch
    ncb = uw // LANES
    state_spec = pl.BlockSpec((1, 1, srows, 2 * S5_JW), lambda b, j, i: (b, j, 0, 0))
    y, hout = pl.pallas_call(
        functools.partial(_s5_kernel, dot=dot, seg=seg, chain=chain),
        grid=(nb, S5_NJ, nt),
        in_specs=[pl.BlockSpec((ncb, tm, LANES), lambda b, j, i: (j, b * nt + i, 0)),
                  pl.BlockSpec((1, uw, 2 * S5_JW), lambda b, j, i: (j, 0, 0)),
                  pl.BlockSpec((1, S5_JW, uw), lambda b, j, i: (j, 0, 0)),
                  pl.BlockSpec((1, S5_JW, uw), lambda b, j, i: (j, 0, 0)),
                  pl.BlockSpec((1, 2, S5_JW), lambda b, j, i: (j, 0, 0)),
                  pl.BlockSpec((1, uw), lambda b, j, i: (0, j)),
                  state_spec],
        out_specs=[pl.BlockSpec((ncb, tm, LANES), lambda b, j, i: (j, b * nt + i, 0)), state_spec],
        out_shape=[jax.ShapeDtypeStruct((g * ch // LANES, t, LANES), F32),
                   jax.ShapeDtypeStruct((nb, S5_NJ, srows, 2 * S5_JW), F32)],
        scratch_shapes=[pltpu.VMEM((tm, 2 * S5_JW), F32),
                        pltpu.VMEM((tm, 2 * S5_JW), w_dtype),
                        pltpu.VMEM((seg, 2 * S5_JW), F32),
                        pltpu.VMEM((tm, 2 * S5_JW), w_dtype),
                        pltpu.VMEM((SUBLANES, 2 * S5_JW), F32)],
        compiler_params=_cparams(("arbitrary", "arbitrary", "arbitrary")),
        name="s5_scan",
    )(u, bb, cc_re, cc_im, a_rows, d_skip.reshape(1, -1), h0)
    hout = hout[:, :, 0, :] if chain else jnp.transpose(hout[0], (1, 0, 2))
    s_re = hout[:, :, :S5_JW].reshape(bsz, g, p)
    s_im = hout[:, :, S5_JW:].reshape(bsz, g, p)
    return y, s_re, s_im


def _router_kernel(x_ref, sc_ref, sh_ref, w_ref, b_ref, o_ref, r_ref, cnt_ref, run_ref):
    h = x_ref[...] * (1.0 + sc_ref[...]) + sh_ref[...]
    logits = _dot3(h, w_ref[...]) + b_ref[...]
    lane = lax.broadcasted_iota(jnp.int32, logits.shape, 1).astype(F32)
    logits = jnp.where(lane < N_EXPERTS, logits, -jnp.inf)
    m1 = logits.max(axis=-1, keepdims=True)
    i1 = jnp.min(jnp.where(logits == m1, lane, float(LANES)), axis=-1, keepdims=True)
    rest = jnp.where(lane == i1, -jnp.inf, logits)
    m2 = rest.max(axis=-1, keepdims=True)
    i2 = jnp.min(jnp.where(rest == m2, lane, float(LANES)), axis=-1, keepdims=True)
    e2 = jnp.exp(m2 - m1)
    w1 = 1.0 / (1.0 + e2)
    w2 = e2 / (1.0 + e2)
    o_ref[...] = jnp.where(lane == i1, w1, 0.0) + jnp.where(lane == i2, w2, 0.0)

    step = pl.program_id(0)

    @pl.when(step == 0)
    def _():
        run_ref[...] = jnp.zeros_like(run_ref)

    tm = logits.shape[0]
    sel = jnp.where(jnp.logical_or(lane == i1, lane == i2), 1.0, 0.0)
    row = lax.broadcasted_iota(jnp.int32, (tm, tm), 0)
    col = lax.broadcasted_iota(jnp.int32, (tm, tm), 1)
    before = (row > col).astype(BF16)
    rank = _mxu(before, sel.astype(BF16)) + run_ref[0:1, :]
    pos1 = jnp.sum(jnp.where(lane == i1, rank, 0.0), axis=-1, keepdims=True)
    pos2 = jnp.sum(jnp.where(lane == i2, rank, 0.0), axis=-1, keepdims=True)
    run_ref[0:1, :] = run_ref[0:1, :] + jnp.sum(sel, axis=0, keepdims=True)
    cnt_ref[...] = run_ref[0:1, :]
    rec = jnp.zeros_like(logits)
    for k, val in enumerate((i1, i2, pos1, pos2, w1, w2)):
        rec = jnp.where(lane == float(k), val, rec)
    r_ref[...] = rec


ROUTE_E1, ROUTE_E2, ROUTE_POS1, ROUTE_POS2, ROUTE_W1, ROUTE_W2 = range(6)


def router(x, scale, shift, router_w, router_b):
    t, d = x.shape
    tm = min(t, 512)
    w = jnp.pad(router_w, ((0, 0), (0, LANES - N_EXPERTS)))
    b = jnp.pad(router_b.reshape(1, -1), ((0, 0), (0, LANES - N_EXPERTS)))
    return pl.pallas_call(
        _router_kernel,
        grid=(t // tm,),
        in_specs=[pl.BlockSpec((tm, d), lambda i: (i, 0)), _row_spec(scale, tm), _row_spec(shift, tm),
                  pl.BlockSpec((d, LANES), lambda i: (0, 0)), pl.BlockSpec((1, LANES), lambda i: (0, 0))],
        out_specs=[pl.BlockSpec((tm, LANES), lambda i: (i, 0)), pl.BlockSpec((tm, LANES), lambda i: (i, 0)),
                   pl.BlockSpec((1, LANES), lambda i: (0, 0))],
        out_shape=[jax.ShapeDtypeStruct((t, LANES), F32), jax.ShapeDtypeStruct((t, LANES), F32),
                   jax.ShapeDtypeStruct((1, LANES), F32)],
        scratch_shapes=[pltpu.VMEM((SUBLANES, LANES), F32)],
        compiler_params=_cparams(("arbitrary",)),
        name="router",
    )(x, scale, shift, w, b)


MOE_TILE = 256
MOE_UP_TN = D_FF_EXPERT // 2
DMA_ISSUE_UNROLL = 8


def _dispatch_kernel(fill_ref, d1_ref, d2_ref, h_ref, xs_ref, zero_ref, sem, zsem):
    tm = h_ref.shape[0]

    @pl.when(pl.program_id(0) == 0)
    def _():
        zero_ref[...] = jnp.zeros_like(zero_ref)

        def fill_copy(k):
            row0 = pl.multiple_of(jnp.maximum(fill_ref[k], 0), MOE_TILE)
            return pltpu.make_async_copy(zero_ref, xs_ref.at[pl.ds(row0, MOE_TILE)], zsem)

        for k in range(fill_ref.shape[0]):
            @pl.when(fill_ref[k] >= 0)
            def _():
                fill_copy(k).start()

        for k in range(fill_ref.shape[0]):
            @pl.when(fill_ref[k] >= 0)
            def _():
                fill_copy(k).wait()

    def issue(t, carry):
        src = h_ref.at[pl.ds(t, 1)]
        pltpu.make_async_copy(src, xs_ref.at[pl.ds(d1_ref[0, 0, t], 1)], sem).start()
        pltpu.make_async_copy(src, xs_ref.at[pl.ds(d2_ref[0, 0, t], 1)], sem).start(priority=1)
        return carry

    lax.fori_loop(0, tm, issue, 0, unroll=DMA_ISSUE_UNROLL)
    pltpu.make_async_copy(xs_ref.at[pl.ds(0, 2 * tm)], xs_ref.at[pl.ds(0, 2 * tm)], sem).wait()


def moe_dispatch(h, dest1, dest2, fill_rows, n_rows):
    t, d = h.shape
    tm = 512
    idx_spec = pl.BlockSpec((1, 1, tm), lambda i, fr: (i, 0, 0), memory_space=pltpu.SMEM)
    grid_spec = pltpu.PrefetchScalarGridSpec(
        num_scalar_prefetch=1,
        grid=(t // tm,),
        in_specs=[idx_spec, idx_spec, pl.BlockSpec((tm, d), lambda i, fr: (i, 0))],
        out_specs=pl.BlockSpec(memory_space=pl.ANY),
        scratch_shapes=[pltpu.VMEM((MOE_TILE, d), h.dtype), pltpu.SemaphoreType.DMA(()),
                        pltpu.SemaphoreType.DMA(())])
    return pl.pallas_call(
        _dispatch_kernel,
        grid_spec=grid_spec,
        out_shape=jax.ShapeDtypeStruct((n_rows, d), h.dtype),
        compiler_params=_cparams(("arbitrary",)),
        name="moe_dispatch",
    )(fill_rows, dest1.reshape(t // tm, 1, tm), dest2.reshape(t // tm, 1, tm), h)


def _expert_up_kernel(te_ref, nv_ref, x_ref, wg_ref, wu_ref, o_ref, *, dot):
    del te_ref

    @pl.when(pl.program_id(1) < nv_ref[0])
    def _():
        gu = dot(x_ref[...], jnp.concatenate([wg_ref[0], wu_ref[0]], axis=1))
        tn = o_ref.shape[1]
        o_ref[...] = (_silu(gu[:, :tn]) * gu[:, tn:]).astype(o_ref.dtype)

    @pl.when(pl.program_id(1) >= nv_ref[0])
    def _():
        o_ref[...] = jnp.zeros_like(o_ref)


def expert_up(xs, w_up, tile_expert, n_valid, dot, tn):
    r, d = xs.shape
    f = w_up.shape[2] // 2
    nf = f // tn
    nt = r // MOE_TILE
    grid_spec = pltpu.PrefetchScalarGridSpec(
        num_scalar_prefetch=2,
        grid=(nf, nt),
        in_specs=[pl.BlockSpec((MOE_TILE, d), lambda j, i, te, nv: (i, 0)),
                  pl.BlockSpec((1, d, tn), lambda j, i, te, nv: (te[i], 0, j)),
                  pl.BlockSpec((1, d, tn), lambda j, i, te, nv: (te[i], 0, j + nf))],
        out_specs=pl.BlockSpec((MOE_TILE, tn), lambda j, i, te, nv: (i, j)))
    return pl.pallas_call(
        functools.partial(_expert_up_kernel, dot=dot),
        grid_spec=grid_spec,
        out_shape=jax.ShapeDtypeStruct((r, f), BF16),
        compiler_params=_cparams(("arbitrary", "arbitrary")),
        name="expert_up",
    )(tile_expert, n_valid, xs, w_up, w_up)


def _expert_down_kernel(te_ref, nv_ref, a_ref, w_ref, o_ref, *, dot):
    del te_ref

    @pl.when(pl.program_id(0) < nv_ref[0])
    def _():
        o_ref[...] = dot(a_ref[...], w_ref[0])

    @pl.when(pl.program_id(0) >= nv_ref[0])
    def _():
        o_ref[...] = jnp.zeros_like(o_ref)


def expert_down(act, w_down, tile_expert, n_valid, dot):
    r, f = act.shape
    d = w_down.shape[2]
    grid_spec = pltpu.PrefetchScalarGridSpec(
        num_scalar_prefetch=2,
        grid=(r // MOE_TILE,),
        in_specs=[pl.BlockSpec((MOE_TILE, f), lambda i, te, nv: (i, 0)),
                  pl.BlockSpec((1, f, d), lambda i, te, nv: (te[i], 0, 0))],
        out_specs=pl.BlockSpec((MOE_TILE, d), lambda i, te, nv: (i, 0)))
    return pl.pallas_call(
        functools.partial(_expert_down_kernel, dot=dot),
        grid_spec=grid_spec,
        out_shape=jax.ShapeDtypeStruct((r, d), F32),
        compiler_params=_cparams(("arbitrary",)),
        name="expert_down",
    )(tile_expert, n_valid, act, w_down)


def _combine_ln_kernel(d1_ref, d2_ref, d1n_ref, d2n_ref, y_ref, rec_ref, x_ref, gate_ref, g_ref, b_ref, xo_ref,
                       buf1, buf2, sem):
    tm = x_ref.shape[0]
    step = pl.program_id(0)
    slot = step % 2

    def gather(idx1, idx2, to):
        def issue(t, carry):
            pltpu.make_async_copy(y_ref.at[pl.ds(idx1[0, 0, t], 1)], buf1.at[to, pl.ds(t, 1)], sem.at[to]).start()
            pltpu.make_async_copy(y_ref.at[pl.ds(idx2[0, 0, t], 1)], buf2.at[to, pl.ds(t, 1)],
                                  sem.at[to]).start(priority=1)
            return carry

        lax.fori_loop(0, tm, issue, 0, unroll=DMA_ISSUE_UNROLL)

    @pl.when(step == 0)
    def _():
        gather(d1_ref, d2_ref, 0)

    @pl.when(step + 1 < pl.num_programs(0))
    def _():
        gather(d1n_ref, d2n_ref, 1 - slot)

    pltpu.make_async_copy(y_ref.at[pl.ds(0, tm)], buf1.at[slot], sem.at[slot]).wait()
    pltpu.make_async_copy(y_ref.at[pl.ds(0, tm)], buf2.at[slot], sem.at[slot]).wait()

    rec = rec_ref[...]
    lane = lax.broadcasted_iota(jnp.int32, rec.shape, 1)
    w1 = jnp.sum(jnp.where(lane == ROUTE_W1, rec, 0.0), axis=-1, keepdims=True)
    w2 = jnp.sum(jnp.where(lane == ROUTE_W2, rec, 0.0), axis=-1, keepdims=True)
    out = w1 * buf1[slot] + w2 * buf2[slot]
    r = ALPHA * x_ref[...] + (1.0 + gate_ref[...]) * out
    mu = jnp.mean(r, axis=-1, keepdims=True)
    dev = r - mu
    var = jnp.mean(dev * dev, axis=-1, keepdims=True)
    xo_ref[...] = dev * lax.rsqrt(var + LN_EPS) * g_ref[...] + b_ref[...]


def moe_combine_ln(y_sorted, dest1, dest2, rec, x, gate, ln_g, ln_b):
    t, d = x.shape
    tm = 256
    nt = t // tm
    idx_spec = pl.BlockSpec((1, 1, tm), lambda i: (i, 0, 0), memory_space=pltpu.SMEM)
    nxt_spec = pl.BlockSpec((1, 1, tm), lambda i: (jnp.minimum(i + 1, nt - 1), 0, 0), memory_space=pltpu.SMEM)
    d1 = dest1.reshape(nt, 1, tm)
    d2 = dest2.reshape(nt, 1, tm)
    return pl.pallas_call(
        _combine_ln_kernel,
        grid=(nt,),
        in_specs=[idx_spec, idx_spec, nxt_spec, nxt_spec, pl.BlockSpec(memory_space=pl.ANY),
                  pl.BlockSpec((tm, LANES), lambda i: (i, 0)), pl.BlockSpec((tm, d), lambda i: (i, 0)),
                  _row_spec(gate, tm), pl.BlockSpec((1, d), lambda i: (0, 0)), pl.BlockSpec((1, d), lambda i: (0, 0))],
        out_specs=pl.BlockSpec((tm, d), lambda i: (i, 0)),
        out_shape=jax.ShapeDtypeStruct((t, d), F32),
        scratch_shapes=[pltpu.VMEM((2, tm, d), F32), pltpu.VMEM((2, tm, d), F32), pltpu.SemaphoreType.DMA((2,))],
        compiler_params=_cparams(("arbitrary",)),
        name="moe_combine_ln",
    )(d1, d2, d1, d2, y_sorted, rec, x, gate, ln_g.reshape(1, d), ln_b.reshape(1, d))


def moe_top2(h, x, rec, counts, gate, ln_g, ln_b, w_up, w_down, dot):
    t, d = h.shape
    n_tiles = (TOP_K * t) // MOE_TILE + N_EXPERTS
    n_rows = n_tiles * MOE_TILE
    cnt = counts[0, :N_EXPERTS].astype(jnp.int32)
    padded = ((cnt + MOE_TILE - 1) // MOE_TILE) * MOE_TILE
    ends = jnp.cumsum(padded)
    starts = ends - padded
    col = lambda k: rec[:, k].astype(jnp.int32)
    experts = jnp.arange(N_EXPERTS, dtype=jnp.int32)[None, :]
    start_of = lambda e: jnp.sum(jnp.where(e[:, None] == experts, starts[None, :], 0), axis=1)
    dest1 = start_of(col(ROUTE_E1)) + col(ROUTE_POS1)
    dest2 = start_of(col(ROUTE_E2)) + col(ROUTE_POS2)
    tile_start = jnp.arange(n_tiles, dtype=jnp.int32) * MOE_TILE
    tile_expert = jnp.minimum(jnp.sum(tile_start[:, None] >= ends[None, :], axis=1), N_EXPERTS - 1).astype(jnp.int32)
    n_valid = (ends[-1:] // MOE_TILE).astype(jnp.int32)
    last_tile = jnp.where(padded > 0, ends - MOE_TILE, -1)
    spare = ends[-1] + jnp.arange(N_EXPERTS, dtype=jnp.int32) * MOE_TILE
    fill_rows = jnp.concatenate([last_tile, jnp.where(spare < n_rows, spare, -1)]).astype(jnp.int32)
    xs = moe_dispatch(h, dest1, dest2, fill_rows, n_rows)
    act = expert_up(xs, w_up, tile_expert, n_valid, dot, tn=MOE_UP_TN)
    y_sorted = expert_down(act, w_down, tile_expert, n_valid, dot)
    return moe_combine_ln(y_sorted, dest1, dest2, rec, x, gate, ln_g, ln_b)


def _trunk(x, mods, caches, wts, precise):
    bsz, seq, d = x.shape
    t = bsz * seq
    prompt = caches is None
    dot = _dot3 if precise else _dot1
    act_dtype = F32 if precise else BF16
    tm = min(t, 512)
    x0 = x.reshape(t, d)

    sh, sc, gt = mods[0][0]
    proj, dt_raw = in_projection(x0, sc, sh, wts["w_in0"], dot, act_dtype)
    keep = min(A_WINDOW, seq) if prompt else seq
    assert keep == seq or bsz == 1
    kv_rows = proj[t - bsz * keep:, A_WIDTH:3 * A_WIDTH]
    k_new = kv_rows[:, :A_WIDTH].reshape(bsz, keep, A_HEADS, A_HEAD_DIM)
    v_new = kv_rows[:, A_WIDTH:].reshape(bsz, keep, A_HEADS, A_HEAD_DIM)
    if prompt:
        att = attention_prompt(proj, wts["rel_bias"], dot, act_dtype)
        conv_buf = jnp.zeros((bsz, B_CONV - 1, B_CONV_DIM), F32)
        ssm_h0 = jnp.zeros((bsz, B_HEADS, B_HEAD_DIM, B_STATE), F32)
        q_len = min(seq, 256)
    else:
        att = attention_sample(proj, caches["k"], caches["v"], wts["rel_bias"], bsz, seq, dot, act_dtype)
        conv_buf, ssm_h0 = caches["conv"], caches["ssm"]
        q_len = seq
    y_ssd, conv_new, ssm_new = ssd_mixer(proj, dt_raw, conv_buf, ssm_h0, wts["conv_w"], wts["conv_b"], wts["dt_bias"],
                                         wts["a_log"], wts["ssd_d"], wts["ssd_norm_g"], bsz, seq, q_len,
                                         dot, act_dtype)
    x1, h1 = matmul_ln([att, y_ssd], wts["w_out0"], x0, gt, wts["ln_g"][0, 0], wts["ln_b"][0, 0], dot, tm,
                       nxt=(mods[0][1][1], mods[0][1][0]), next_dtype=act_dtype)
    act = ffn_up(h1, wts["ffn_w_up"], dot, act_dtype, tm=min(t, 1024), tn=512)
    x2, h2 = matmul_ln([act], wts["ffn_w_down"], x1, mods[0][1][2], wts["ln_g"][0, 1], wts["ln_b"][0, 1], dot,
                       min(t, 256), tk=None if prompt else 1408, nxt=(mods[1][0][1], mods[1][0][0]), next_dtype=act_dtype)

    u = matmul_slabs(h2, wts["w_in1"], dot)
    if prompt:
        s5_re0 = jnp.zeros((bsz, C_GROUPS, C_STATE), F32)
        s5_im0 = jnp.zeros((bsz, C_GROUPS, C_STATE), F32)
    else:
        s5_re0, s5_im0 = caches["s5_re"], caches["s5_im"]
    y5, s5_re, s5_im = s5_mixer(u, s5_re0, s5_im0, wts["s5_lam_re"], wts["s5_lam_im"], wts["s5_log_step"],
                                wts["s5_b_re"], wts["s5_b_im"], wts["s5_c_re"], wts["s5_c_im"], wts["s5_d"],
                                bsz, seq, 1024 if prompt else bsz * seq, dot, F32 if precise else BF16,
                                chain=prompt)
    x3, h3 = matmul_ln([y5], wts["glu_w"], x2, mods[1][0][2], wts["ln_g"][1, 0], wts["ln_b"][1, 0], dot,
                       min(t, 256), tk=None if prompt else 512, glu=True,
                       nxt=(mods[1][1][1], mods[1][1][0]), next_dtype=F32)
    gates, rec, counts = router(x3, mods[1][1][1], mods[1][1][0], wts["router_w"], wts["router_b"])
    if prompt:
        x4 = moe_top2(h3, x3, rec, counts, mods[1][1][2], wts["ln_g"][1, 1], wts["ln_b"][1, 1],
                      wts["moe_w_up"], wts["moe_w_down"], dot)
    else:
        act = ffn_up(h3, wts["moe_w_up"], _dot1, BF16, tm=min(t, 2048), tn=MOE_UP_TN, gates=gates)
        x4, _ = matmul_ln([act], wts["moe_w_down"], x3, mods[1][1][2], wts["ln_g"][1, 1], wts["ln_b"][1, 1],
                          _dot1, tm, tk=D_FF_EXPERT)
    return (x4.reshape(bsz, seq, d), k_new[None], v_new[None], conv_new[None], ssm_new[None],
            s5_re[None], s5_im[None])


def kernel(x_prompt, x_sample, cache_attn_k, cache_attn_v, state_ssd_conv, state_ssd, state_s5_re, state_s5_im, c_prompt, c_sample, ada_w, ada_b, ln_g, ln_b, w_in0, w_out0, rel_bias, conv_w, conv_b, dt_bias, a_log, ssd_d, ssd_norm_g, ffn_w_up, ffn_w_down, w_in1, s5_lam_re, s5_lam_im, s5_log_step, s5_b_re, s5_b_im, s5_c_re, s5_c_im, s5_d, glu_w, router_w, router_b, moe_w_up, moe_w_down):
    d = D_MODEL
    bp, lp, _ = x_prompt.shape
    bs, ls, _ = x_sample.shape

    n_c = bp + bs
    rows = ((n_c + SUBLANES - 1) // SUBLANES) * SUBLANES
    c_all = jnp.pad(jnp.concatenate([c_prompt, c_sample], axis=0), ((0, rows - n_c), (0, 0)))
    mod = adaln_all(c_all, ada_w, ada_b).reshape(DEPTH, 2, rows, 3, d)

    def mods_for(r0, nb, per_row):
        out = []
        for layer in range(DEPTH):
            out.append([])
            for j in range(2):
                trip = []
                for part in range(3):
                    m = mod[layer, j, r0:r0 + nb, part]
                    trip.append(jnp.repeat(m, per_row, axis=0) if nb > 1 else m)
                out[-1].append(tuple(trip))
        return out

    assert bp == 1
    mods_p = mods_for(0, bp, lp)
    mods_s = mods_for(bp, bs, ls)

    shared = dict(ln_g=ln_g, ln_b=ln_b, rel_bias=rel_bias[0], conv_w=conv_w[0], conv_b=conv_b[0],
                  dt_bias=dt_bias[0], a_log=a_log[0], ssd_d=ssd_d[0], ssd_norm_g=ssd_norm_g[0],
                  s5_lam_re=s5_lam_re[0], s5_lam_im=s5_lam_im[0], s5_log_step=s5_log_step[0],
                  s5_b_re=s5_b_re[0], s5_b_im=s5_b_im[0], s5_c_re=s5_c_re[0], s5_c_im=s5_c_im[0], s5_d=s5_d[0],
                  router_w=router_w[0], router_b=router_b[0])
    big = dict(w_in0=w_in0[0], w_out0=w_out0[0], ffn_w_up=ffn_w_up, ffn_w_down=ffn_w_down[0],
               w_in1=w_in1[0], glu_w=glu_w[0])
    moe = dict(moe_w_up=moe_w_up[0].astype(BF16), moe_w_down=moe_w_down[0].astype(BF16))
    wts_p = dict(shared, **moe, **{k: v.astype(BF16) for k, v in big.items()})
    wts_s = dict(shared, **moe, **big)

    y_p, k_p, v_p, conv_p, ssd_p, re_p, im_p = _trunk(x_prompt, mods_p, None, wts_p, precise=False)
    caches = dict(k=cache_attn_k[0], v=cache_attn_v[0], conv=state_ssd_conv[0], ssm=state_ssd[0],
                  s5_re=state_s5_re[0], s5_im=state_s5_im[0])
    y_s, k_s, v_s, conv_s, ssd_s, re_s, im_s = _trunk(x_sample, mods_s, caches, wts_s, precise=True)
    return (y_p, y_s, k_p, v_p, conv_p, ssd_p, re_p, im_p, k_s, v_s, conv_s, ssd_s, re_s, im_s)
```
